```python
import jax, jax.numpy as jnp
from jax import lax
import numpy as np


D_MODEL = 1024
BATCH = 8
SEQ = 8192
DEPTH = 1

D_MIX = D_MODEL
ATT_HEADS = 8
ATT_KV_HEADS = 2
ATT_HEAD_DIM = 64
ATT_GROUP = ATT_HEADS // ATT_KV_HEADS
WINDOW = 128
ATT_BLOCK = WINDOW
ROPE_THETA = 10000.0
MLSTM_HEADS = 4
MLSTM_V_DIM = D_MIX // 2 // MLSTM_HEADS
MLSTM_QK_DIM = MLSTM_V_DIM // 2
MLSTM_CHUNK = 64
CONV_WIDTH = 4
F_BIAS_LO = 3.0
F_BIAS_HI = 6.0
PEER_HEADS = 8
PEER_KEYS = 128
PEER_EXPERTS = PEER_KEYS * PEER_KEYS
PEER_QUERY_DIM = 256
PEER_HALF = PEER_QUERY_DIM // 2
PEER_TOPK = 16
PEER_BLOCK = 64
N_MOD = 6
NORM_EPS = 1e-6

ATT_WIDTH = ATT_HEADS * ATT_HEAD_DIM
ATT_KV_WIDTH = ATT_KV_HEADS * ATT_HEAD_DIM
MLSTM_QK_WIDTH = MLSTM_HEADS * MLSTM_QK_DIM
MLSTM_WIDTH = MLSTM_HEADS * MLSTM_V_DIM
IN_SIZES = (ATT_WIDTH, ATT_KV_WIDTH, ATT_KV_WIDTH, MLSTM_QK_WIDTH, MLSTM_QK_WIDTH, MLSTM_WIDTH, MLSTM_HEADS, MLSTM_HEADS, MLSTM_WIDTH)
IN_SPLITS = tuple(int(s) for s in np.cumsum(IN_SIZES)[:-1])
D_IN = int(sum(IN_SIZES))

kernel_name = 'hymba_swa_mlstm_peer_adaln'


def rmsnorm(x, g):
    xf = x.astype(jnp.float32)
    y = xf * lax.rsqrt(jnp.mean(xf * xf, axis=-1, keepdims=True) + NORM_EPS)
    return (y * g.astype(jnp.float32)).astype(x.dtype)


def rope(x, positions):
    dh = x.shape[-1]
    inv_freq = ROPE_THETA ** (-jnp.arange(0, dh, 2, dtype=jnp.float32) / dh)
    ang = positions.astype(jnp.float32)[..., None] * inv_freq
    cos = jnp.cos(ang)[:, :, None, :]
    sin = jnp.sin(ang)[:, :, None, :]
    xf = x.astype(jnp.float32)
    x1, x2 = jnp.split(xf, 2, axis=-1)
    return jnp.concatenate([x1 * cos - x2 * sin, x2 * cos + x1 * sin], axis=-1).astype(x.dtype)


def sliding_window_attention(q, k, v, sinks):
    B, S, H, Dh = q.shape
    nb = S // ATT_BLOCK
    qb = q.reshape(B, nb, ATT_BLOCK, ATT_KV_HEADS, ATT_GROUP, Dh)

    def band(t):
        tb = t.reshape(B, nb, ATT_BLOCK, ATT_KV_HEADS, Dh)
        prev = jnp.pad(tb, ((0, 0), (1, 0), (0, 0), (0, 0), (0, 0)))[:, :-1]
        return jnp.concatenate([prev, tb], axis=2)

    kb = band(k)
    vb = band(v)
    scores = jnp.einsum('bnqkgd,bnskd->bnkgqs', qb, kb).astype(jnp.float32) * (Dh ** -0.5)
    qi = jnp.arange(ATT_BLOCK)[:, None]
    si = jnp.arange(2 * ATT_BLOCK)[None, :]
    delta = qi + ATT_BLOCK - si
    key_pos = jnp.arange(nb)[:, None, None] * ATT_BLOCK + si - ATT_BLOCK
    valid = (delta >= 0) & (delta < WINDOW) & (key_pos >= 0)
    scores = jnp.where(valid[None, :, None, None], scores, -jnp.inf)
    sink = sinks.astype(jnp.float32).reshape(1, 1, ATT_KV_HEADS, ATT_GROUP, 1, 1)
    m = jnp.maximum(scores.max(axis=-1, keepdims=True), sink)
    p = jnp.exp(scores - m)
    probs = (p / (p.sum(axis=-1, keepdims=True) + jnp.exp(sink - m))).astype(v.dtype)
    out = jnp.einsum('bnkgqs,bnskd->bnqkgd', probs, vb)
    return out.reshape(B, S, H * Dh)


def causal_dwconv(x, w):
    S = x.shape[1]
    xp = jnp.pad(x, ((0, 0), (CONV_WIDTH - 1, 0), (0, 0)))
    acc = xp[:, 0:S] * w[0]
    for j in range(1, CONV_WIDTH):
        acc = acc + xp[:, j:j + S] * w[j]
    return acc


def mlstm_chunkwise(q, k, v, i_pre, f_pre):
    B, S, H, Dk = q.shape
    Dv = v.shape[-1]
    L = MLSTM_CHUNK
    nc = S // L
    f32 = jnp.float32

    def to_chunks(t):
        t = t.astype(f32).reshape((B, nc, L) + t.shape[2:])
        return jnp.moveaxis(t, 1, 0)

    qc = to_chunks(q) * (Dk ** -0.5)
    kc = to_chunks(k)
    vc = to_chunks(v)
    ic = to_chunks(i_pre)
    fc = to_chunks(jax.nn.log_sigmoid(f_pre.astype(f32)))
    causal = jnp.tril(jnp.ones((L, L), dtype=bool))

    def step(carry, xs):
        C, n, m_prev = carry
        qt, kt, vt, it, lf = xs
        b = jnp.cumsum(lf, axis=1)
        bh = b.transpose(0, 2, 1)
        ih = it.transpose(0, 2, 1)
        dlog = bh[..., :, None] - bh[..., None, :] + ih[..., None, :]
        dlog = jnp.where(causal, dlog, -jnp.inf)
        m_inter = bh + m_prev[..., None]
        m_t = jnp.maximum(m_inter, dlog.max(axis=-1))
        w_intra = jnp.exp(dlog - m_t[..., None])
        a_inter = jnp.exp(m_inter - m_t)
        s = jnp.einsum('blhd,bshd->bhls', qt, kt) * w_intra
        a_bl = a_inter.transpose(0, 2, 1)[..., None]
        num = jnp.einsum('bhls,bshv->blhv', s, vt) + a_bl * jnp.einsum('blhd,bhvd->blhv', qt, C)
        den = s.sum(axis=-1) + a_inter * jnp.einsum('blhd,bhd->bhl', qt, n)
        denom = jnp.maximum(jnp.abs(den), jnp.exp(-m_t)).transpose(0, 2, 1)[..., None]
        h = num / denom
        bL = b[:, -1, :]
        g = bL[:, None, :] - b + it
        m_new = jnp.maximum(bL + m_prev, g.max(axis=1))
        w = jnp.exp(g - m_new[:, None, :])
        decay = jnp.exp(bL + m_prev - m_new)
        C_new = decay[..., None, None] * C + jnp.einsum('bsh,bshv,bshd->bhvd', w, vt, kt)
        n_new = decay[..., None] * n + jnp.einsum('bsh,bshd->bhd', w, kt)
        return (C_new, n_new, m_new), h

    init = (jnp.zeros((B, H, Dv, Dk), f32), jnp.zeros((B, H, Dk), f32), jnp.zeros((B, H), f32))
    _, hs = lax.scan(step, init, (qc, kc, vc, ic, fc))
    return jnp.moveaxis(hs, 0, 1).reshape(B, S, H, Dv)


def peer_ffn(h, w_q, keys1, keys2, u, v):
    B, S, D = h.shape
    nblk = S // PEER_BLOCK
    hb = h.reshape(B, nblk, PEER_BLOCK, D).swapaxes(0, 1).reshape(nblk, B * PEER_BLOCK, D)

    def block(ht):
        T = ht.shape[0]
        q = (ht @ w_q).reshape(T, PEER_HEADS, 2, PEER_HALF)
        s1 = jnp.einsum('thd,hkd->thk', q[:, :, 0], keys1).astype(jnp.float32)
        s2 = jnp.einsum('thd,hkd->thk', q[:, :, 1], keys2).astype(jnp.float32)
        v1, i1 = lax.top_k(s1, PEER_TOPK)
        v2, i2 = lax.top_k(s2, PEER_TOPK)
        cand = (v1[..., :, None] + v2[..., None, :]).reshape(T, PEER_HEADS, PEER_TOPK * PEER_TOPK)
        cidx = (i1[..., :, None] * PEER_KEYS + i2[..., None, :]).reshape(T, PEER_HEADS, PEER_TOPK * PEER_TOPK)
        top, pos = lax.top_k(cand, PEER_TOPK)
        eidx = jnp.take_along_axis(cidx, pos, axis=-1)
        gate = jax.nn.softmax(top, axis=-1)
        ug = u[eidx]
        act = jax.nn.gelu(jnp.einsum('thkd,td->thk', ug, ht).astype(jnp.float32), approximate=False) * gate
        vg = v[eidx]
        return jnp.einsum('thk,thkd->td', act.astype(ht.dtype), vg)

    yb = lax.map(block, hb)
    return yb.reshape(nblk, B, PEER_BLOCK, D).swapaxes(0, 1).reshape(B, S, D)


def setup_inputs(seed: int = 0) -> dict:
    key = jax.random.key(seed)
    ks = jax.random.split(key, 22)
    f32 = jnp.float32

    def nrm(k, shape, scale):
        return jax.random.normal(k, shape, f32) * scale

    x = nrm(ks[0], (BATCH, SEQ, D_MODEL), 1.0)
    c = nrm(ks[1], (BATCH, D_MODEL), 1.0)
    positions = jnp.broadcast_to(jnp.arange(SEQ, dtype=jnp.int32), (BATCH, SEQ))
    w_mod = nrm(ks[2], (DEPTH, D_MODEL, N_MOD * D_MODEL), 0.3 * D_MODEL ** -0.5)
    b_mod = nrm(ks[3], (DEPTH, N_MOD * D_MODEL), 0.02)
    g_pre_mix = 1.0 + nrm(ks[4], (DEPTH, D_MODEL), 0.05)
    g_post_mix = 1.0 + nrm(ks[5], (DEPTH, D_MODEL), 0.05)
    w_in = nrm(ks[6], (DEPTH, D_MODEL, D_IN), D_MODEL ** -0.5)
    conv_w = nrm(ks[7], (DEPTH, CONV_WIDTH, 2 * MLSTM_QK_WIDTH), CONV_WIDTH ** -0.5)
    b_igate = nrm(ks[8], (DEPTH, MLSTM_HEADS), 0.1)
    b_fgate = jnp.linspace(F_BIAS_LO, F_BIAS_HI, MLSTM_HEADS, dtype=f32)[None, :] + nrm(ks[9], (DEPTH, MLSTM_HEADS), 0.1)
    mlstm_norm_g = 1.0 + nrm(ks[10], (DEPTH, MLSTM_HEADS, MLSTM_V_DIM), 0.05)
    att_sinks = nrm(ks[11], (DEPTH, ATT_HEADS), 0.5)
    w_out = nrm(ks[12], (DEPTH, D_MIX, D_MODEL), D_MIX ** -0.5)
    g_pre_ffn = 1.0 + nrm(ks[13], (DEPTH, D_MODEL), 0.05)
    g_post_ffn = 1.0 + nrm(ks[14], (DEPTH, D_MODEL), 0.05)
    peer_wq = nrm(ks[15], (DEPTH, D_MODEL, PEER_HEADS * PEER_QUERY_DIM), D_MODEL ** -0.5)
    peer_keys1 = nrm(ks[16], (DEPTH, PEER_HEADS, PEER_KEYS, PEER_HALF), PEER_HALF ** -0.5)
    peer_keys2 = nrm(ks[17], (DEPTH, PEER_HEADS, PEER_KEYS, PEER_HALF), PEER_HALF ** -0.5)
    peer_u = nrm(ks[18], (DEPTH, PEER_EXPERTS, D_MODEL), D_MODEL ** -0.5)
    peer_v = nrm(ks[19], (DEPTH, PEER_EXPERTS, D_MODEL), (PEER_HEADS * PEER_TOPK) ** -0.5)
    return {'x': x, 'c': c, 'positions': positions, 'w_mod': w_mod, 'b_mod': b_mod,
            'g_pre_mix': g_pre_mix, 'g_post_mix': g_post_mix, 'w_in': w_in, 'conv_w': conv_w,
            'b_igate': b_igate, 'b_fgate': b_fgate, 'mlstm_norm_g': mlstm_norm_g,
            'att_sinks': att_sinks, 'w_out': w_out, 'g_pre_ffn': g_pre_ffn, 'g_post_ffn': g_post_ffn,
            'peer_wq': peer_wq, 'peer_keys1': peer_keys1, 'peer_keys2': peer_keys2,
            'peer_u': peer_u, 'peer_v': peer_v}


def reference(x, c, positions, w_mod, b_mod, g_pre_mix, g_post_mix, w_in, conv_w, b_igate, b_fgate,
              mlstm_norm_g, att_sinks, w_out, g_pre_ffn, g_post_ffn, peer_wq, peer_keys1, peer_keys2,
              peer_u, peer_v):
    B, S, D = x.shape
    for l in range(DEPTH):
        mod = (c @ w_mod[l] + b_mod[l])[:, None, :]
        shift1, scale1, gate1, shift2, scale2, gate2 = jnp.split(mod, N_MOD, axis=-1)

        h = rmsnorm(x, g_pre_mix[l]) * (1.0 + scale1) + shift1
        proj = h @ w_in[l]
        aq, ak, av, mq, mk, mv, mi, mf, mo = jnp.split(proj, IN_SPLITS, axis=-1)

        aq = rope(aq.reshape(B, S, ATT_HEADS, ATT_HEAD_DIM), positions)
        ak = rope(ak.reshape(B, S, ATT_KV_HEADS, ATT_HEAD_DIM), positions)
        av = av.reshape(B, S, ATT_KV_HEADS, ATT_HEAD_DIM)
        att = sliding_window_attention(aq, ak, av, att_sinks[l])

        mqk = jax.nn.silu(causal_dwconv(jnp.concatenate([mq, mk], axis=-1), conv_w[l]))
        mq, mk = jnp.split(mqk, 2, axis=-1)
        mh = mlstm_chunkwise(mq.reshape(B, S, MLSTM_HEADS, MLSTM_QK_DIM),
                             mk.reshape(B, S, MLSTM_HEADS, MLSTM_QK_DIM),
                             mv.reshape(B, S, MLSTM_HEADS, MLSTM_V_DIM),
                             mi + b_igate[l], mf + b_fgate[l])
        mh = rmsnorm(mh, mlstm_norm_g[l]).astype(x.dtype)
        mh = (mh * jax.nn.sigmoid(mo).reshape(B, S, MLSTM_HEADS, MLSTM_V_DIM)).reshape(B, S, MLSTM_WIDTH)

        mix = jnp.concatenate([att, mh], axis=-1) @ w_out[l]
        x = x + gate1 * rmsnorm(mix, g_post_mix[l])

        h2 = rmsnorm(x, g_pre_ffn[l]) * (1.0 + scale2) + shift2
        y = peer_ffn(h2, peer_wq[l], peer_keys1[l], peer_keys2[l], peer_u[l], peer_v[l])
        x = x + gate2 * rmsnorm(y, g_post_ffn[l])
    return x
```

```python
import functools

import jax
import jax.numpy as jnp
from jax import lax
from jax.experimental import pallas as pl
from jax.experimental.pallas import tpu as pltpu

F32 = jnp.float32
BF16 = jnp.bfloat16

ATT_HEADS = 8
ATT_KV_HEADS = 2
ATT_HEAD_DIM = 64
ATT_BLOCK = 128
ROPE_THETA = 10000.0
MLSTM_HEADS = 4
MLSTM_V_DIM = 128
MLSTM_QK_DIM = 64
MLSTM_CHUNK = 64
CONV_WIDTH = 4
PEER_HEADS = 8
PEER_KEYS = 128
PEER_HALF = 128
PEER_TOPK = 16
NORM_EPS = 1e-6

LANES = 128
SUBLANES = 8
VMEM_LIMIT = 52 * 1024 * 1024

NEG_INF = float("-inf")
NT_DIMS = (((1,), (1,)), ((), ()))
TN_DIMS = (((0,), (0,)), ((), ()))


def _params(sem, vmem=None):
    return pltpu.CompilerParams(dimension_semantics=sem, vmem_limit_bytes=vmem)


def _rms(x, g):
    return x * lax.rsqrt(jnp.mean(x * x, axis=-1, keepdims=True) + NORM_EPS) * g


def _mod_kernel(c_ref, w_ref, b_ref, o_ref):
    o_ref[...] = jnp.dot(c_ref[...], w_ref[...], preferred_element_type=F32,
                         precision=lax.Precision.HIGHEST) + b_ref[...]


def _mod(c, w, b):
    bsz, d = c.shape
    nout = w.shape[1]
    return pl.pallas_call(
        _mod_kernel,
        grid=(nout // d,),
        in_specs=[pl.BlockSpec((bsz, d), lambda i: (0, 0)),
                  pl.BlockSpec((d, d), lambda i: (0, i)),
                  pl.BlockSpec((1, d), lambda i: (0, i))],
        out_specs=pl.BlockSpec((bsz, d), lambda i: (0, i)),
        out_shape=jax.ShapeDtypeStruct((bsz, nout), F32),
        compiler_params=_params(("arbitrary",)),
        name="mod",
    )(c, w, b.reshape(1, nout))


def _rope_tab_kernel(pos_ref, inv_ref, sign_ref, cos_ref, sin_ref):
    ang = pos_ref[...].astype(F32) * inv_ref[...]
    cos_ref[...] = jnp.cos(ang)
    sin_ref[...] = jnp.sin(ang) * sign_ref[...]


def _rope_tab(pos_col):
    n = pos_col.shape[0]
    tr = min(n, 1024)
    half = ATT_HEAD_DIM // 2
    inv = ROPE_THETA ** (-jnp.arange(0, ATT_HEAD_DIM, 2, dtype=F32) / ATT_HEAD_DIM)
    inv_row = jnp.tile(inv, LANES // half).reshape(1, LANES)
    lane = jnp.arange(LANES)
    sign_row = jnp.where((lane % ATT_HEAD_DIM) < half, -1.0, 1.0).astype(F32).reshape(1, LANES)
    return pl.pallas_call(
        _rope_tab_kernel,
        grid=(n // tr,),
        in_specs=[pl.BlockSpec((tr, 1), lambda i: (i, 0)),
                  pl.BlockSpec((1, LANES), lambda i: (0, 0)),
                  pl.BlockSpec((1, LANES), lambda i: (0, 0))],
        out_specs=[pl.BlockSpec((tr, LANES), lambda i: (i, 0))] * 2,
        out_shape=[jax.ShapeDtypeStruct((n, LANES), F32)] * 2,
        compiler_params=_params(("arbitrary",)),
        name="rope_tab",
    )(pos_col, inv_row, sign_row)


def _rope(v, cos, sin):
    half = ATT_HEAD_DIM // 2
    lane = lax.broadcasted_iota(jnp.int32, cos.shape, 1)
    first = (lane % ATT_HEAD_DIM) < half
    outs = []
    for j in range(v.shape[1] // LANES):
        c = v[:, j * LANES:(j + 1) * LANES]
        rot = jnp.where(first, pltpu.roll(c, LANES - half, 1), pltpu.roll(c, half, 1))
        outs.append(c * cos + rot * sin)
    return jnp.concatenate(outs, axis=1)


_C_Q, _C_K, _C_V, _C_MQK, _C_MV, _C_MO, _C_G, _C_END = 0, 512, 768, 1024, 1536, 2048, 2560, 2688


def _in_proj_kernel(x_ref, sc_ref, sh_ref, g_ref, cos_ref, sin_ref, w_ref,
                    q_ref, k_ref, v_ref, mqk_ref, mv_ref, mo_ref, gt_ref):
    x = x_ref[...]
    h = _rms(x, g_ref[...]) * (1.0 + sc_ref[0]) + sh_ref[0]
    hb = h.astype(BF16)

    def mm(a, b):
        return jnp.dot(hb, w_ref[:, a:b], preferred_element_type=F32)

    cos = cos_ref[...]
    sin = sin_ref[...]
    q_ref[...] = (_rope(mm(_C_Q, _C_K), cos, sin) * (ATT_HEAD_DIM ** -0.5)).astype(BF16)
    k_ref[...] = _rope(mm(_C_K, _C_V), cos, sin).astype(BF16)
    v_ref[...] = mm(_C_V, _C_MQK).astype(BF16)
    mqk_ref[...] = mm(_C_MQK, _C_MV)
    mv_ref[...] = mm(_C_MV, _C_MO).astype(BF16)
    mo_ref[...] = mm(_C_MO, _C_G)
    gt_ref[...] = mm(_C_G, _C_END)


def _in_proj(x2, scale1, shift1, g_pre, cos, sin, w_all, seq, tm):
    n, d = x2.shape
    per_b = seq // tm
    row = lambda i: (i, 0)
    bsel = lambda i: (i // per_b, 0, 0)
    widths = (512, 256, 256, 512, 512, 512, 128)
    dtypes = (BF16, BF16, BF16, F32, BF16, F32, F32)
    return pl.pallas_call(
        _in_proj_kernel,
        grid=(n // tm,),
        in_specs=[pl.BlockSpec((tm, d), row),
                  pl.BlockSpec((1, 1, d), bsel),
                  pl.BlockSpec((1, 1, d), bsel),
                  pl.BlockSpec((1, d), lambda i: (0, 0)),
                  pl.BlockSpec((tm, LANES), row),
                  pl.BlockSpec((tm, LANES), row),
                  pl.BlockSpec((d, _C_END), lambda i: (0, 0))],
        out_specs=[pl.BlockSpec((tm, w), row) for w in widths],
        out_shape=[jax.ShapeDtypeStruct((n, w), dt) for w, dt in zip(widths, dtypes)],
        compiler_params=_params(("arbitrary",), VMEM_LIMIT),
        name="in_proj",
    )(x2, scale1, shift1, g_pre, cos, sin, w_all)


def _swa_kernel(sink_ref, q_ref, kp_ref, kc_ref, vp_ref, vc_ref, o_ref):
    blk = ATT_BLOCK
    n = pl.program_id(1)
    qi = lax.broadcasted_iota(jnp.int32, (blk, 2 * blk), 0)
    si = lax.broadcasted_iota(jnp.int32, (blk, 2 * blk), 1)
    delta = qi + blk - si
    valid = (delta >= 0) & (delta < blk) & ((si >= blk) | (n > 0))
    lo = lax.broadcasted_iota(jnp.int32, (2 * blk, LANES), 1) < ATT_HEAD_DIM
    group = ATT_HEADS // ATT_KV_HEADS
    for g in range(ATT_KV_HEADS):
        cs = slice(g * LANES, (g + 1) * LANES)
        k = jnp.concatenate([kp_ref[:, cs], kc_ref[:, cs]], axis=0)
        v = jnp.concatenate([vp_ref[:, cs], vc_ref[:, cs]], axis=0)
        zero = jnp.zeros_like(k)
        halves = ((jnp.where(lo, k, zero), jnp.where(lo, v, zero)),
                  (jnp.where(lo, zero, k), jnp.where(lo, zero, v)))
        for jj in range(group // 2):
            p = g * (group // 2) + jj
            q2 = q_ref[:, p * LANES:(p + 1) * LANES]
            acc = jnp.zeros((blk, LANES), F32)
            for half, (kh, vh) in enumerate(halves):
                s = lax.dot_general(q2, kh, NT_DIMS, preferred_element_type=F32)
                s = jnp.where(valid, s, NEG_INF)
                sink = sink_ref[2 * p + half]
                m = jnp.maximum(jnp.max(s, axis=-1, keepdims=True), sink)
                e = jnp.exp(s - m)
                den = jnp.sum(e, axis=-1, keepdims=True) + jnp.exp(sink - m)
                acc = acc + jnp.dot((e / den).astype(BF16), vh, preferred_element_type=F32)
            o_ref[:, p * LANES:(p + 1) * LANES] = acc.astype(BF16)


def _swa(sinks, q, kd, vd, bsz, seq):
    n = q.shape[0]
    nb = seq // ATT_BLOCK
    cur = lambda b, i: (b * nb + i, 0)
    prev = lambda b, i: (b * nb + jnp.maximum(i - 1, 0), 0)
    return pl.pallas_call(
        _swa_kernel,
        grid=(bsz, nb),
        in_specs=[pl.BlockSpec(memory_space=pltpu.SMEM),
                  pl.BlockSpec((ATT_BLOCK, 512), cur),
                  pl.BlockSpec((ATT_BLOCK, 256), prev),
                  pl.BlockSpec((ATT_BLOCK, 256), cur),
                  pl.BlockSpec((ATT_BLOCK, 256), prev),
                  pl.BlockSpec((ATT_BLOCK, 256), cur)],
        out_specs=pl.BlockSpec((ATT_BLOCK, 512), cur),
        out_shape=jax.ShapeDtypeStruct((n, 512), BF16),
        compiler_params=_params(("arbitrary", "arbitrary")),
        name="swa",
    )(sinks, q, kd, kd, vd, vd)


def _mlstm_kernel(mqk_ref, mv_ref, mo_ref, gt_ref, cw_ref, gb_ref, ng_ref, o_ref,
                  tail_ref, qk_ref, xs_ref, ct_ref, n_ref, m_ref, *, chunks):
    L = MLSTM_CHUNK
    tm = chunks * L
    nqk = MLSTM_HEADS * MLSTM_QK_DIM

    @pl.when(pl.program_id(1) == 0)
    def _():
        tail_ref[...] = jnp.zeros_like(tail_ref)
        ct_ref[...] = jnp.zeros_like(ct_ref)
        n_ref[...] = jnp.zeros_like(n_ref)
        m_ref[...] = jnp.zeros_like(m_ref)

    cur = mqk_ref[...]
    full = jnp.concatenate([tail_ref[...], cur], axis=0)
    off = SUBLANES - (CONV_WIDTH - 1)
    acc = full[off:off + tm] * cw_ref[0:1, :]
    for j in range(1, CONV_WIDTH):
        acc = acc + full[off + j:off + j + tm] * cw_ref[j:j + 1, :]
    act = acc * jax.nn.sigmoid(acc)
    col = lax.broadcasted_iota(jnp.int32, (1, 2 * nqk), 1)
    act = act * jnp.where(col < nqk, MLSTM_QK_DIM ** -0.5, 1.0)
    qk_ref[...] = act.astype(BF16)
    tail_ref[...] = cur[tm - SUBLANES:tm]

    lane = lax.broadcasted_iota(jnp.int32, (tm, LANES), 1)
    gts = gt_ref[...] + gb_ref[...]
    logsig = jnp.minimum(gts, 0.0) - jnp.log(1.0 + jnp.exp(-jnp.abs(gts)))
    xs_ref[...] = jnp.where(lane < MLSTM_HEADS, gts, jnp.where(lane < 2 * MLSTM_HEADS, logsig, 0.0))

    ri = lax.broadcasted_iota(jnp.int32, (L, L), 0)
    ci = lax.broadcasted_iota(jnp.int32, (L, L), 1)
    causal = ci <= ri
    tril = causal.astype(F32)
    lane_l = lax.broadcasted_iota(jnp.int32, (L, LANES), 1)
    lo_l = lane_l < MLSTM_QK_DIM
    row_c = lax.broadcasted_iota(jnp.int32, (LANES, 1), 0) < MLSTM_QK_DIM
    lane_1 = lax.broadcasted_iota(jnp.int32, (1, LANES), 1) < MLSTM_QK_DIM

    def chunk(c, carry):
        r0 = pl.multiple_of(c * L, L)
        rows = pl.ds(r0, L)
        xc = xs_ref[rows, :]
        bc = jnp.dot(tril, xc, preferred_element_type=F32, precision=lax.Precision.HIGHEST)
        x2 = jnp.where(lane_l < MLSTM_HEADS, xc, bc)
        xt = x2.T
        for p in range(MLSTM_HEADS // 2):
            q2 = qk_ref[rows, p * LANES:(p + 1) * LANES]
            k2 = qk_ref[rows, nqk + p * LANES:nqk + (p + 1) * LANES]
            ct_old = ct_ref[p]
            ctb = ct_old.astype(BF16)
            n2 = n_ref[p:p + 1, :]
            decs, upds, kws = [], [], []
            for half in range(2):
                h = 2 * p + half
                hm = lo_l if half == 0 else jnp.logical_not(lo_l)
                zero = jnp.zeros_like(q2)
                qm = jnp.where(hm, q2, zero)
                km = jnp.where(hm, k2, zero)
                v = mv_ref[rows, h * LANES:(h + 1) * LANES]
                b_col = x2[:, MLSTM_HEADS + h:MLSTM_HEADS + h + 1]
                ig_col = x2[:, h:h + 1]
                b_row = xt[MLSTM_HEADS + h:MLSTM_HEADS + h + 1, :]
                ig_row = xt[h:h + 1, :]
                m_prev = m_ref[h:h + 1, 0:1]
                dlog = jnp.where(causal, b_col - b_row + ig_row, NEG_INF)
                m_inter = b_col + m_prev
                m_t = jnp.maximum(m_inter, jnp.max(dlog, axis=-1, keepdims=True))
                w_intra = jnp.exp(dlog - m_t)
                a_inter = jnp.exp(m_inter - m_t)
                s = lax.dot_general(q2, km, NT_DIMS, preferred_element_type=F32) * w_intra
                num = (jnp.dot(s.astype(BF16), v, preferred_element_type=F32)
                       + a_inter * jnp.dot(qm, ctb, preferred_element_type=F32))
                den = (jnp.sum(s, axis=-1, keepdims=True)
                       + a_inter * jnp.sum(qm.astype(F32) * n2, axis=-1, keepdims=True))
                hh = num / jnp.maximum(jnp.abs(den), jnp.exp(-m_t))
                y = _rms(hh, ng_ref[:, h * LANES:(h + 1) * LANES])
                y = y * jax.nn.sigmoid(mo_ref[rows, h * LANES:(h + 1) * LANES])
                o_ref[rows, h * LANES:(h + 1) * LANES] = y.astype(BF16)
                b_last = xt[MLSTM_HEADS + h:MLSTM_HEADS + h + 1, L - 1:L]
                g_col = b_last - b_col + ig_col
                m_new = jnp.maximum(b_last + m_prev, jnp.max(g_col, axis=0, keepdims=True))
                kw = km.astype(F32) * jnp.exp(g_col - m_new)
                decs.append(jnp.exp(b_last + m_prev - m_new))
                kws.append(kw)
                upds.append(lax.dot_general(kw.astype(BF16), v, TN_DIMS, preferred_element_type=F32))
                m_ref[h:h + 1, :] = jnp.broadcast_to(m_new, (1, LANES))
            ct_ref[p] = ct_old * jnp.where(row_c, decs[0], decs[1]) + upds[0] + upds[1]
            n_ref[p:p + 1, :] = (n2 * jnp.where(lane_1, decs[0], decs[1])
                                 + jnp.sum(kws[0] + kws[1], axis=0, keepdims=True))
        return carry

    lax.fori_loop(0, chunks, chunk, 0)


def _mlstm(mqk, mv, mo, gts, conv_w, gate_bias, norm_g, bsz, seq, chunks):
    n = mqk.shape[0]
    tm = chunks * MLSTM_CHUNK
    steps = seq // tm
    row = lambda b, i: (b * steps + i, 0)
    const = lambda b, i: (0, 0)
    width = MLSTM_HEADS * MLSTM_V_DIM
    return pl.pallas_call(
        functools.partial(_mlstm_kernel, chunks=chunks),
        grid=(bsz, steps),
        in_specs=[pl.BlockSpec((tm, width), row),
                  pl.BlockSpec((tm, width), row),
                  pl.BlockSpec((tm, width), row),
                  pl.BlockSpec((tm, LANES), row),
                  pl.BlockSpec((CONV_WIDTH, width), const),
                  pl.BlockSpec((1, LANES), const),
                  pl.BlockSpec((1, width), const)],
        out_specs=pl.BlockSpec((tm, width), row),
        out_shape=jax.ShapeDtypeStruct((n, width), BF16),
        scratch_shapes=[pltpu.VMEM((SUBLANES, width), F32),
                        pltpu.VMEM((tm, width), BF16),
                        pltpu.VMEM((tm, LANES), F32),
                        pltpu.VMEM((MLSTM_HEADS // 2, LANES, LANES), F32),
                        pltpu.VMEM((SUBLANES, LANES), F32),
                        pltpu.VMEM((SUBLANES, LANES), F32)],
        compiler_params=_params(("arbitrary", "arbitrary")),
        name="mlstm",
    )(mqk, mv, mo, gts, conv_w, gate_bias, norm_g)


def _out_proj_kernel(att_ref, mh_ref, x_ref, g1_ref, sc_ref, sh_ref, gpm_ref, gpf_ref, wa_ref, wb_ref,
                     x1_ref, h2_ref):
    mix = (jnp.dot(att_ref[...], wa_ref[...], preferred_element_type=F32)
           + jnp.dot(mh_ref[...], wb_ref[...], preferred_element_type=F32))
    x1 = x_ref[...] + g1_ref[0] * _rms(mix, gpm_ref[...])
    x1_ref[...] = x1
    h2_ref[...] = _rms(x1, gpf_ref[...]) * (1.0 + sc_ref[0]) + sh_ref[0]


def _out_proj(att, mh, x2, gate1, scale2, shift2, g_post_mix, g_pre_ffn, wa, wb, seq, tm):
    n, d = x2.shape
    per_b = seq // tm
    row = lambda i: (i, 0)
    bsel = lambda i: (i // per_b, 0, 0)
    const = lambda i: (0, 0)
    half = att.shape[1]
    return pl.pallas_call(
        _out_proj_kernel,
        grid=(n // tm,),
        in_specs=[pl.BlockSpec((tm, half), row), pl.BlockSpec((tm, half), row), pl.BlockSpec((tm, d), row),
                  pl.BlockSpec((1, 1, d), bsel), pl.BlockSpec((1, 1, d), bsel), pl.BlockSpec((1, 1, d), bsel),
                  pl.BlockSpec((1, d), const), pl.BlockSpec((1, d), const),
                  pl.BlockSpec((half, d), const), pl.BlockSpec((half, d), const)],
        out_specs=[pl.BlockSpec((tm, d), row)] * 2,
        out_shape=[jax.ShapeDtypeStruct((n, d), F32)] * 2,
        compiler_params=_params(("arbitrary",), VMEM_LIMIT),
        name="out_proj",
    )(att, mh, x2, gate1, scale2, shift2, g_post_mix, g_pre_ffn, wa, wb)


def _top_rows(s, ids, k):
    big = jnp.int32(2 ** 30)
    vals, sel = [], []
    for _ in range(k):
        m = jnp.max(s, axis=0, keepdims=True)
        i = jnp.min(jnp.where(s == m, ids, big), axis=0, keepdims=True)
        vals.append(m)
        sel.append(i)
        s = jnp.where(ids == i, NEG_INF, s)
    return jnp.concatenate(vals, axis=0), jnp.concatenate(sel, axis=0)


def _pick_rows(table, which):
    r = lax.broadcasted_iota(jnp.int32, table.shape, 0)
    rows = []
    for k in range(which.shape[0]):
        rows.append(jnp.sum(jnp.where(r == which[k:k + 1, :], table, 0), axis=0, keepdims=True))
    return jnp.concatenate(rows, axis=0)


def _peer_sel_kernel(h_ref, wq_ref, k1_ref, k2_ref, e_ref, g_ref):
    tq = h_ref.shape[0]
    K = PEER_TOPK
    q = jnp.dot(h_ref[...].astype(BF16), wq_ref[...], preferred_element_type=F32).astype(BF16)
    key_ids = lax.broadcasted_iota(jnp.int32, (PEER_KEYS, tq), 0)
    r16 = lax.broadcasted_iota(jnp.int32, (K, tq), 0)
    r8 = lax.broadcasted_iota(jnp.int32, (SUBLANES, tq), 0)
    flat = jnp.concatenate([r16 * K] + [r8 * K + b for b in range(1, SUBLANES)] + [r8 + SUBLANES], axis=0)
    for hd in range(PEER_HEADS):
        base = hd * 2 * PEER_HALF
        s1 = lax.dot_general(k1_ref[hd], q[:, base:base + PEER_HALF], NT_DIMS, preferred_element_type=F32)
        s2 = lax.dot_general(k2_ref[hd], q[:, base + PEER_HALF:base + 2 * PEER_HALF], NT_DIMS,
                             preferred_element_type=F32)
        v1, i1 = _top_rows(s1, key_ids, K)
        v2, i2 = _top_rows(s2, key_ids, K)
        cand = jnp.concatenate([v1 + v2[0:1, :]]
                               + [v1[0:SUBLANES] + v2[b:b + 1, :] for b in range(1, SUBLANES)]
                               + [v1[0:1, :] + v2[SUBLANES:K]], axis=0)
        top, pos = _top_rows(cand, flat, K)
        eid = _pick_rows(i1, pos >> 4) * PEER_KEYS + _pick_rows(i2, pos & (K - 1))
        ex = jnp.exp(top - top[0:1, :])
        e_ref[hd * K:(hd + 1) * K, :] = eid
        g_ref[hd * K:(hd + 1) * K, :] = ex / jnp.sum(ex, axis=0, keepdims=True)


def _peer_sel(h2, wq, k1, k2, tq):
    n, d = h2.shape
    rows = PEER_HEADS * PEER_TOPK
    return pl.pallas_call(
        _peer_sel_kernel,
        grid=(n // tq,),
        in_specs=[pl.BlockSpec((tq, d), lambda i: (i, 0)),
                  pl.BlockSpec(wq.shape, lambda i: (0, 0)),
                  pl.BlockSpec(k1.shape, lambda i: (0, 0, 0)),
                  pl.BlockSpec(k2.shape, lambda i: (0, 0, 0))],
        out_specs=[pl.BlockSpec((rows, tq), lambda i: (0, i))] * 2,
        out_shape=[jax.ShapeDtypeStruct((rows, n), jnp.int32), jax.ShapeDtypeStruct((rows, n), F32)],
        compiler_params=_params(("arbitrary",), VMEM_LIMIT),
        name="peer_sel",
    )(h2, wq, k1, k2)


def _fold_rows(prods):
    sub = lax.broadcasted_iota(jnp.int32, (SUBLANES, LANES), 0)
    level = [prods[int(format(i, "03b")[::-1], 2)] for i in range(SUBLANES)]
    width = SUBLANES // 2
    while len(level) > 1:
        nxt = []
        m = (sub % (2 * width)) < width
        for a, b in zip(level[0::2], level[1::2]):
            other = jnp.where(m, pltpu.roll(a, SUBLANES - width, axis=0), pltpu.roll(b, width, axis=0))
            nxt.append(jnp.where(m, a, b) + other)
        level = nxt
        width //= 2
    return level[0]


def _peer_u_kernel(idx_ref, h_ref, gate_ref, tab_ref, act_ref, pre_ref):
    tq, nsel = gate_ref.shape
    ones = jnp.ones((SUBLANES, LANES), BF16)

    def body(t, carry):
        hv = h_ref[t]
        rs = []
        for g in range(nsel // SUBLANES):
            prods = []
            for r in range(SUBLANES):
                e = idx_ref[t, g * SUBLANES + r]
                prods.append(tab_ref[e].astype(F32) * hv)
            rs.append(_fold_rows(prods))
        qm = jnp.concatenate(rs, axis=0)
        q_hi = qm.astype(BF16)
        q_lo = (qm - q_hi.astype(F32)).astype(BF16)
        row = (lax.dot_general(ones, q_hi, NT_DIMS, preferred_element_type=F32)
               + lax.dot_general(ones, q_lo, NT_DIMS, preferred_element_type=F32))
        pre_ref[pl.ds(t, 1), :] = row[0:1]
        return carry

    lax.fori_loop(0, tq, body, 0)
    pre = pre_ref[...]
    act_ref[...] = 0.5 * pre * (1.0 + lax.erf(pre * (2.0 ** -0.5))) * gate_ref[...]


def _peer_u(idx, h3, gate, tab, tq):
    n, nsel = idx.shape
    return pl.pallas_call(
        _peer_u_kernel,
        grid=(n // tq,),
        in_specs=[pl.BlockSpec((tq, nsel), lambda i: (i, 0), memory_space=pltpu.SMEM),
                  pl.BlockSpec((tq, SUBLANES, LANES), lambda i: (i, 0, 0)),
                  pl.BlockSpec((tq, nsel), lambda i: (i, 0)),
                  pl.BlockSpec(memory_space=pltpu.VMEM)],
        out_specs=pl.BlockSpec((tq, nsel), lambda i: (i, 0)),
        out_shape=jax.ShapeDtypeStruct((n, nsel), F32),
        scratch_shapes=[pltpu.VMEM((tq, nsel), F32)],
        compiler_params=_params(("arbitrary",), VMEM_LIMIT),
        name="peer_u",
    )(idx, h3, gate, tab)


def _peer_v_kernel(idx_ref, act_ref, tab_ref, y_ref):
    tq = y_ref.shape[0]
    nsel = idx_ref.shape[1]

    def body(t, carry):
        accs = [jnp.zeros((SUBLANES, LANES), F32) for _ in range(4)]
        for j in range(nsel):
            e = idx_ref[t, j]
            accs[j % 4] = accs[j % 4] + act_ref[t, j] * tab_ref[e].astype(F32)
        y_ref[t] = (accs[0] + accs[1]) + (accs[2] + accs[3])
        return carry

    lax.fori_loop(0, tq, body, 0)


def _peer_v(idx, act, tab, tq):
    n, nsel = idx.shape
    return pl.pallas_call(
        _peer_v_kernel,
        grid=(n // tq,),
        in_specs=[pl.BlockSpec((tq, nsel), lambda i: (i, 0), memory_space=pltpu.SMEM),
                  pl.BlockSpec((tq, nsel), lambda i: (i, 0), memory_space=pltpu.SMEM),
                  pl.BlockSpec(memory_space=pltpu.VMEM)],
        out_specs=pl.BlockSpec((tq, SUBLANES, LANES), lambda i: (i, 0, 0)),
        out_shape=jax.ShapeDtypeStruct((n, SUBLANES, LANES), F32),
        compiler_params=_params(("arbitrary",), VMEM_LIMIT),
        name="peer_v",
    )(idx, act, tab)


def _final_kernel(x1_ref, y_ref, g2_ref, gpf_ref, o_ref):
    o_ref[...] = x1_ref[...] + g2_ref[0] * _rms(y_ref[...], gpf_ref[...])


def _final(x1, y, gate2, g_post_ffn, seq, tm):
    n, d = x1.shape
    per_b = seq // tm
    row = lambda i: (i, 0)
    return pl.pallas_call(
        _final_kernel,
        grid=(n // tm,),
        in_specs=[pl.BlockSpec((tm, d), row), pl.BlockSpec((tm, d), row),
                  pl.BlockSpec((1, 1, d), lambda i: (i // per_b, 0, 0)),
                  pl.BlockSpec((1, d), lambda i: (0, 0))],
        out_specs=pl.BlockSpec((tm, d), row),
        out_shape=jax.ShapeDtypeStruct((n, d), F32),
        compiler_params=_params(("arbitrary",)),
        name="final",
    )(x1, y, gate2, g_post_ffn)


def _dup_heads(w, heads, dh):
    d = w.shape[0]
    return jnp.repeat(w.reshape(d, heads, 1, dh), 2, axis=2).reshape(d, heads * 2 * dh)


def _layer(x2, c, pos_col, bsz, seq, w_mod, b_mod, g_pre_mix, g_post_mix, w_in, conv_w, b_igate, b_fgate,
           mlstm_norm_g, att_sinks, w_out, g_pre_ffn, g_post_ffn, peer_wq, peer_keys1, peer_keys2, peer_u, peer_v):
    n, d = x2.shape
    tm = min(seq, 512)
    mod = _mod(c, w_mod, b_mod)
    shift1, scale1, gate1, shift2, scale2, gate2 = [m.reshape(bsz, 1, d) for m in jnp.split(mod, 6, axis=-1)]

    aw = ATT_HEADS * ATT_HEAD_DIM
    kvw = ATT_KV_HEADS * ATT_HEAD_DIM
    qkw = MLSTM_HEADS * MLSTM_QK_DIM
    mw = MLSTM_HEADS * MLSTM_V_DIM
    o = 0
    wq_a = w_in[:, o:o + aw]; o += aw
    wk_a = w_in[:, o:o + kvw]; o += kvw
    wv_a = w_in[:, o:o + kvw]; o += kvw
    w_mqk = w_in[:, o:o + 2 * qkw]; o += 2 * qkw
    w_mv = w_in[:, o:o + mw]; o += mw
    w_g = w_in[:, o:o + 2 * MLSTM_HEADS]; o += 2 * MLSTM_HEADS
    w_mo = w_in[:, o:o + mw]
    w_gp = jnp.pad(w_g, ((0, 0), (0, LANES - 2 * MLSTM_HEADS)))
    w_all = jnp.concatenate([wq_a, _dup_heads(wk_a, ATT_KV_HEADS, ATT_HEAD_DIM),
                             _dup_heads(wv_a, ATT_KV_HEADS, ATT_HEAD_DIM), w_mqk, w_mv, w_mo, w_gp],
                            axis=1).astype(BF16)

    cos, sin = _rope_tab(pos_col)
    q, kd, vd, mqk, mv, mo, gts = _in_proj(x2, scale1, shift1, g_pre_mix.reshape(1, d), cos, sin, w_all, seq, tm)
    att = _swa(att_sinks, q, kd, vd, bsz, seq)
    gate_bias = jnp.pad(jnp.concatenate([b_igate, b_fgate]), (0, LANES - 2 * MLSTM_HEADS)).reshape(1, LANES)
    chunks = min(seq // MLSTM_CHUNK, 8)
    mh = _mlstm(mqk, mv, mo, gts, conv_w, gate_bias, mlstm_norm_g.reshape(1, mw), bsz, seq, chunks)
    wo = w_out.astype(BF16)
    x1, h2 = _out_proj(att, mh, x2, gate1, scale2, shift2, g_post_mix.reshape(1, d), g_pre_ffn.reshape(1, d),
                       wo[:aw], wo[aw:], seq, tm)

    tq = min(n, 256)
    eid_t, gate_t = _peer_sel(h2, peer_wq.astype(BF16), peer_keys1.astype(BF16), peer_keys2.astype(BF16), tq)
    eid = eid_t.T
    gate = gate_t.T
    nexp = peer_u.shape[0]
    tab_u = peer_u.astype(BF16).reshape(nexp, SUBLANES, LANES)
    tab_v = peer_v.astype(BF16).reshape(nexp, SUBLANES, LANES)
    act = _peer_u(eid, h2.reshape(n, SUBLANES, LANES), gate, tab_u, tq)
    y = _peer_v(eid, act, tab_v, tq).reshape(n, d)
    return _final(x1, y, gate2, g_post_ffn.reshape(1, d), seq, tm)


def kernel(x, c, positions, w_mod, b_mod, g_pre_mix, g_post_mix, w_in, conv_w, b_igate, b_fgate, mlstm_norm_g, att_sinks, w_out, g_pre_ffn, g_post_ffn, peer_wq, peer_keys1, peer_keys2, peer_u, peer_v):
    bsz, seq, d = x.shape
    n = bsz * seq
    x2 = x.reshape(n, d)
    pos_col = positions.reshape(n, 1)
    for l in range(w_mod.shape[0]):
        x2 = _layer(x2, c, pos_col, bsz, seq, w_mod[l], b_mod[l], g_pre_mix[l], g_post_mix[l], w_in[l], conv_w[l],
                    b_igate[l], b_fgate[l], mlstm_norm_g[l], att_sinks[l], w_out[l], g_pre_ffn[l], g_post_ffn[l],
                    peer_wq[l], peer_keys1[l], peer_keys2[l], peer_u[l], peer_v[l])
    return x2.reshape(bsz, seq, d)
```

```python
import functools

import jax
import jax.numpy as jnp
from jax import lax
from jax.experimental import pallas as pl
from jax.experimental.pallas import tpu as pltpu

F32 = jnp.float32
BF16 = jnp.bfloat16

ATT_HEADS = 8
ATT_KV_HEADS = 2
ATT_HEAD_DIM = 64
ATT_BLOCK = 128
ROPE_THETA = 10000.0
MLSTM_HEADS = 4
MLSTM_V_DIM = 128
MLSTM_QK_DIM = 64
MLSTM_CHUNK = 64
CONV_WIDTH = 4
PEER_HEADS = 8
PEER_KEYS = 128
PEER_HALF = 128
PEER_TOPK = 16
NORM_EPS = 1e-6

LANES = 128
SUBLANES = 8
VMEM_LIMIT = 52 * 1024 * 1024
GATHER_STAGES = 4

NEG_INF = float("-inf")
NT_DIMS = (((1,), (1,)), ((), ()))
TN_DIMS = (((0,), (0,)), ((), ()))


def _params(sem, vmem=None):
    return pltpu.CompilerParams(dimension_semantics=sem, vmem_limit_bytes=vmem)


def _rms(x, g):
    return x * lax.rsqrt(jnp.mean(x * x, axis=-1, keepdims=True) + NORM_EPS) * g


def _mod_kernel(c_ref, w_ref, b_ref, o_ref):
    o_ref[...] = jnp.dot(c_ref[...], w_ref[...], preferred_element_type=F32,
                         precision=lax.Precision.HIGHEST) + b_ref[...]


def _mod(c, w, b):
    bsz, d = c.shape
    nout = w.shape[1]
    return pl.pallas_call(
        _mod_kernel,
        grid=(nout // d,),
        in_specs=[pl.BlockSpec((bsz, d), lambda i: (0, 0)),
                  pl.BlockSpec((d, d), lambda i: (0, i)),
                  pl.BlockSpec((1, d), lambda i: (0, i))],
        out_specs=pl.BlockSpec((bsz, d), lambda i: (0, i)),
        out_shape=jax.ShapeDtypeStruct((bsz, nout), F32),
        compiler_params=_params(("arbitrary",)),
        name="mod",
    )(c, w, b.reshape(1, nout))


def _rope_tab_kernel(pos_ref, inv_ref, sign_ref, cos_ref, sin_ref):
    ang = pos_ref[...].astype(F32) * inv_ref[...]
    cos_ref[...] = jnp.cos(ang)
    sin_ref[...] = jnp.sin(ang) * sign_ref[...]


def _rope_tab(pos_col):
    n = pos_col.shape[0]
    tr = min(n, 1024)
    half = ATT_HEAD_DIM // 2
    inv = ROPE_THETA ** (-jnp.arange(0, ATT_HEAD_DIM, 2, dtype=F32) / ATT_HEAD_DIM)
    inv_row = jnp.tile(inv, LANES // half).reshape(1, LANES)
    lane = jnp.arange(LANES)
    sign_row = jnp.where((lane % ATT_HEAD_DIM) < half, -1.0, 1.0).astype(F32).reshape(1, LANES)
    return pl.pallas_call(
        _rope_tab_kernel,
        grid=(n // tr,),
        in_specs=[pl.BlockSpec((tr, 1), lambda i: (i, 0)),
                  pl.BlockSpec((1, LANES), lambda i: (0, 0)),
                  pl.BlockSpec((1, LANES), lambda i: (0, 0))],
        out_specs=[pl.BlockSpec((tr, LANES), lambda i: (i, 0))] * 2,
        out_shape=[jax.ShapeDtypeStruct((n, LANES), F32)] * 2,
        compiler_params=_params(("arbitrary",)),
        name="rope_tab",
    )(pos_col, inv_row, sign_row)


def _rope(v, cos, sin):
    half = ATT_HEAD_DIM // 2
    lane = lax.broadcasted_iota(jnp.int32, cos.shape, 1)
    first = (lane % ATT_HEAD_DIM) < half
    outs = []
    for j in range(v.shape[1] // LANES):
        c = v[:, j * LANES:(j + 1) * LANES]
        rot = jnp.where(first, pltpu.roll(c, LANES - half, 1), pltpu.roll(c, half, 1))
        outs.append(c * cos + rot * sin)
    return jnp.concatenate(outs, axis=1)


_C_Q, _C_K, _C_V, _C_MQK, _C_MV, _C_MO, _C_G, _C_END = 0, 512, 768, 1024, 1536, 2048, 2560, 2688


def _in_proj_kernel(x_ref, sc_ref, sh_ref, g_ref, cos_ref, sin_ref, w_ref,
                    q_ref, k_ref, v_ref, mqk_ref, mv_ref, mo_ref, gt_ref):
    x = x_ref[...]
    h = _rms(x, g_ref[...]) * (1.0 + sc_ref[0]) + sh_ref[0]
    hb = h.astype(BF16)

    def mm(a, b):
        return jnp.dot(hb, w_ref[:, a:b], preferred_element_type=F32)

    cos = cos_ref[...]
    sin = sin_ref[...]
    q_ref[...] = (_rope(mm(_C_Q, _C_K), cos, sin) * (ATT_HEAD_DIM ** -0.5)).astype(BF16)
    k_ref[...] = _rope(mm(_C_K, _C_V), cos, sin).astype(BF16)
    v_ref[...] = mm(_C_V, _C_MQK).astype(BF16)
    mqk_ref[...] = mm(_C_MQK, _C_MV)
    mv_ref[...] = mm(_C_MV, _C_MO).astype(BF16)
    mo_ref[...] = mm(_C_MO, _C_G)
    gt_ref[...] = mm(_C_G, _C_END)


def _in_proj(x2, scale1, shift1, g_pre, cos, sin, w_all, seq, tm):
    n, d = x2.shape
    per_b = seq // tm
    row = lambda i: (i, 0)
    bsel = lambda i: (i // per_b, 0, 0)
    widths = (512, 256, 256, 512, 512, 512, 128)
    dtypes = (BF16, BF16, BF16, F32, BF16, F32, F32)
    return pl.pallas_call(
        _in_proj_kernel,
        grid=(n // tm,),
        in_specs=[pl.BlockSpec((tm, d), row),
                  pl.BlockSpec((1, 1, d), bsel),
                  pl.BlockSpec((1, 1, d), bsel),
                  pl.BlockSpec((1, d), lambda i: (0, 0)),
                  pl.BlockSpec((tm, LANES), row),
                  pl.BlockSpec((tm, LANES), row),
                  pl.BlockSpec((d, _C_END), lambda i: (0, 0))],
        out_specs=[pl.BlockSpec((tm, w), row) for w in widths],
        out_shape=[jax.ShapeDtypeStruct((n, w), dt) for w, dt in zip(widths, dtypes)],
        compiler_params=_params(("arbitrary",), VMEM_LIMIT),
        name="in_proj",
    )(x2, scale1, shift1, g_pre, cos, sin, w_all)


def _swa_kernel(sink_ref, q_ref, kp_ref, kc_ref, vp_ref, vc_ref, o_ref):
    blk = ATT_BLOCK
    n = pl.program_id(1)
    qi = lax.broadcasted_iota(jnp.int32, (blk, 2 * blk), 0)
    si = lax.broadcasted_iota(jnp.int32, (blk, 2 * blk), 1)
    delta = qi + blk - si
    valid = (delta >= 0) & (delta < blk) & ((si >= blk) | (n > 0))
    lo = lax.broadcasted_iota(jnp.int32, (2 * blk, LANES), 1) < ATT_HEAD_DIM
    group = ATT_HEADS // ATT_KV_HEADS
    for g in range(ATT_KV_HEADS):
        cs = slice(g * LANES, (g + 1) * LANES)
        k = jnp.concatenate([kp_ref[:, cs], kc_ref[:, cs]], axis=0)
        v = jnp.concatenate([vp_ref[:, cs], vc_ref[:, cs]], axis=0)
        zero = jnp.zeros_like(k)
        halves = ((jnp.where(lo, k, zero), jnp.where(lo, v, zero)),
                  (jnp.where(lo, zero, k), jnp.where(lo, zero, v)))
        for jj in range(group // 2):
            p = g * (group // 2) + jj
            q2 = q_ref[:, p * LANES:(p + 1) * LANES]
            acc = jnp.zeros((blk, LANES), F32)
            for half, (kh, vh) in enumerate(halves):
                s = lax.dot_general(q2, kh, NT_DIMS, preferred_element_type=F32)
                s = jnp.where(valid, s, NEG_INF)
                sink = sink_ref[2 * p + half]
                m = jnp.maximum(jnp.max(s, axis=-1, keepdims=True), sink)
                e = jnp.exp(s - m)
                den = jnp.sum(e, axis=-1, keepdims=True) + jnp.exp(sink - m)
                acc = acc + jnp.dot((e / den).astype(BF16), vh, preferred_element_type=F32)
            o_ref[:, p * LANES:(p + 1) * LANES] = acc.astype(BF16)


def _swa(sinks, q, kd, vd, bsz, seq):
    n = q.shape[0]
    nb = seq // ATT_BLOCK
    cur = lambda b, i: (b * nb + i, 0)
    prev = lambda b, i: (b * nb + jnp.maximum(i - 1, 0), 0)
    return pl.pallas_call(
        _swa_kernel,
        grid=(bsz, nb),
        in_specs=[pl.BlockSpec(memory_space=pltpu.SMEM),
                  pl.BlockSpec((ATT_BLOCK, 512), cur),
                  pl.BlockSpec((ATT_BLOCK, 256), prev),
                  pl.BlockSpec((ATT_BLOCK, 256), cur),
                  pl.BlockSpec((ATT_BLOCK, 256), prev),
                  pl.BlockSpec((ATT_BLOCK, 256), cur)],
        out_specs=pl.BlockSpec((ATT_BLOCK, 512), cur),
        out_shape=jax.ShapeDtypeStruct((n, 512), BF16),
        compiler_params=_params(("arbitrary", "arbitrary")),
        name="swa",
    )(sinks, q, kd, kd, vd, vd)


def _mlstm_kernel(mqk_ref, mv_ref, mo_ref, gt_ref, cw_ref, gb_ref, ng_ref, o_ref,
                  tail_ref, qk_ref, xs_ref, ct_ref, n_ref, m_ref, *, chunks):
    L = MLSTM_CHUNK
    tm = chunks * L
    nqk = MLSTM_HEADS * MLSTM_QK_DIM

    @pl.when(pl.program_id(1) == 0)
    def _():
        tail_ref[...] = jnp.zeros_like(tail_ref)
        ct_ref[...] = jnp.zeros_like(ct_ref)
        n_ref[...] = jnp.zeros_like(n_ref)
        m_ref[...] = jnp.zeros_like(m_ref)

    cur = mqk_ref[...]
    full = jnp.concatenate([tail_ref[...], cur], axis=0)
    off = SUBLANES - (CONV_WIDTH - 1)
    acc = full[off:off + tm] * cw_ref[0:1, :]
    for j in range(1, CONV_WIDTH):
        acc = acc + full[off + j:off + j + tm] * cw_ref[j:j + 1, :]
    act = acc * jax.nn.sigmoid(acc)
    col = lax.broadcasted_iota(jnp.int32, (1, 2 * nqk), 1)
    act = act * jnp.where(col < nqk, MLSTM_QK_DIM ** -0.5, 1.0)
    qk_ref[...] = act.astype(BF16)
    tail_ref[...] = cur[tm - SUBLANES:tm]

    lane = lax.broadcasted_iota(jnp.int32, (tm, LANES), 1)
    gts = gt_ref[...] + gb_ref[...]
    logsig = jnp.minimum(gts, 0.0) - jnp.log(1.0 + jnp.exp(-jnp.abs(gts)))
    xs_ref[...] = jnp.where(lane < MLSTM_HEADS, gts, jnp.where(lane < 2 * MLSTM_HEADS, logsig, 0.0))

    ri = lax.broadcasted_iota(jnp.int32, (L, L), 0)
    ci = lax.broadcasted_iota(jnp.int32, (L, L), 1)
    causal = ci <= ri
    tril = causal.astype(F32)
    lane_l = lax.broadcasted_iota(jnp.int32, (L, LANES), 1)
    lo_l = lane_l < MLSTM_QK_DIM
    row_c = lax.broadcasted_iota(jnp.int32, (LANES, 1), 0) < MLSTM_QK_DIM
    lane_1 = lax.broadcasted_iota(jnp.int32, (1, LANES), 1) < MLSTM_QK_DIM

    def chunk(c, carry):
        r0 = pl.multiple_of(c * L, L)
        rows = pl.ds(r0, L)
        xc = xs_ref[rows, :]
        bc = jnp.dot(tril, xc, preferred_element_type=F32, precision=lax.Precision.HIGHEST)
        x2 = jnp.where(lane_l < MLSTM_HEADS, xc, bc)
        xt = x2.T
        for p in range(MLSTM_HEADS // 2):
            q2 = qk_ref[rows, p * LANES:(p + 1) * LANES]
            k2 = qk_ref[rows, nqk + p * LANES:nqk + (p + 1) * LANES]
            ct_old = ct_ref[p]
            ctb = ct_old.astype(BF16)
            n2 = n_ref[p:p + 1, :]
            decs, upds, kws = [], [], []
            for half in range(2):
                h = 2 * p + half
                hm = lo_l if half == 0 else jnp.logical_not(lo_l)
                zero = jnp.zeros_like(q2)
                qm = jnp.where(hm, q2, zero)
                km = jnp.where(hm, k2, zero)
                v = mv_ref[rows, h * LANES:(h + 1) * LANES]
                b_col = x2[:, MLSTM_HEADS + h:MLSTM_HEADS + h + 1]
                ig_col = x2[:, h:h + 1]
                b_row = xt[MLSTM_HEADS + h:MLSTM_HEADS + h + 1, :]
                ig_row = xt[h:h + 1, :]
                m_prev = m_ref[h:h + 1, 0:1]
                dlog = jnp.where(causal, b_col - b_row + ig_row, NEG_INF)
                m_inter = b_col + m_prev
                m_t = jnp.maximum(m_inter, jnp.max(dlog, axis=-1, keepdims=True))
                w_intra = jnp.exp(dlog - m_t)
                a_inter = jnp.exp(m_inter - m_t)
                s = lax.dot_general(q2, km, NT_DIMS, preferred_element_type=F32) * w_intra
                num = (jnp.dot(s.astype(BF16), v, preferred_element_type=F32)
                       + a_inter * jnp.dot(qm, ctb, preferred_element_type=F32))
                den = (jnp.sum(s, axis=-1, keepdims=True)
                       + a_inter * jnp.sum(qm.astype(F32) * n2, axis=-1, keepdims=True))
                hh = num / jnp.maximum(jnp.abs(den), jnp.exp(-m_t))
                y = _rms(hh, ng_ref[:, h * LANES:(h + 1) * LANES])
                y = y * jax.nn.sigmoid(mo_ref[rows, h * LANES:(h + 1) * LANES])
                o_ref[rows, h * LANES:(h + 1) * LANES] = y.astype(BF16)
                b_last = xt[MLSTM_HEADS + h:MLSTM_HEADS + h + 1, L - 1:L]
                g_col = b_last - b_col + ig_col
                m_new = jnp.maximum(b_last + m_prev, jnp.max(g_col, axis=0, keepdims=True))
                kw = km.astype(F32) * jnp.exp(g_col - m_new)
                decs.append(jnp.exp(b_last + m_prev - m_new))
                kws.append(kw)
                upds.append(lax.dot_general(kw.astype(BF16), v, TN_DIMS, preferred_element_type=F32))
                m_ref[h:h + 1, :] = jnp.broadcast_to(m_new, (1, LANES))
            ct_ref[p] = ct_old * jnp.where(row_c, decs[0], decs[1]) + upds[0] + upds[1]
            n_ref[p:p + 1, :] = (n2 * jnp.where(lane_1, decs[0], decs[1])
                                 + jnp.sum(kws[0] + kws[1], axis=0, keepdims=True))
        return carry

    lax.fori_loop(0, chunks, chunk, 0)


def _mlstm(mqk, mv, mo, gts, conv_w, gate_bias, norm_g, bsz, seq, chunks):
    n = mqk.shape[0]
    tm = chunks * MLSTM_CHUNK
    steps = seq // tm
    row = lambda b, i: (b * steps + i, 0)
    const = lambda b, i: (0, 0)
    width = MLSTM_HEADS * MLSTM_V_DIM
    return pl.pallas_call(
        functools.partial(_mlstm_kernel, chunks=chunks),
        grid=(bsz, steps),
        in_specs=[pl.BlockSpec((tm, width), row),
                  pl.BlockSpec((tm, width), row),
                  pl.BlockSpec((tm, width), row),
                  pl.BlockSpec((tm, LANES), row),
                  pl.BlockSpec((CONV_WIDTH, width), const),
                  pl.BlockSpec((1, LANES), const),
                  pl.BlockSpec((1, width), const)],
        out_specs=pl.BlockSpec((tm, width), row),
        out_shape=jax.ShapeDtypeStruct((n, width), BF16),
        scratch_shapes=[pltpu.VMEM((SUBLANES, width), F32),
                        pltpu.VMEM((tm, width), BF16),
                        pltpu.VMEM((tm, LANES), F32),
                        pltpu.VMEM((MLSTM_HEADS // 2, LANES, LANES), F32),
                        pltpu.VMEM((SUBLANES, LANES), F32),
                        pltpu.VMEM((SUBLANES, LANES), F32)],
        compiler_params=_params(("arbitrary", "arbitrary")),
        name="mlstm",
    )(mqk, mv, mo, gts, conv_w, gate_bias, norm_g)


def _out_proj_kernel(att_ref, mh_ref, x_ref, g1_ref, sc_ref, sh_ref, gpm_ref, gpf_ref, wa_ref, wb_ref,
                     x1_ref, h2_ref):
    mix = (jnp.dot(att_ref[...], wa_ref[...], preferred_element_type=F32)
           + jnp.dot(mh_ref[...], wb_ref[...], preferred_element_type=F32))
    x1 = x_ref[...] + g1_ref[0] * _rms(mix, gpm_ref[...])
    x1_ref[...] = x1
    h2_ref[...] = _rms(x1, gpf_ref[...]) * (1.0 + sc_ref[0]) + sh_ref[0]


def _out_proj(att, mh, x2, gate1, scale2, shift2, g_post_mix, g_pre_ffn, wa, wb, seq, tm):
    n, d = x2.shape
    per_b = seq // tm
    row = lambda i: (i, 0)
    bsel = lambda i: (i // per_b, 0, 0)
    const = lambda i: (0, 0)
    half = att.shape[1]
    return pl.pallas_call(
        _out_proj_kernel,
        grid=(n // tm,),
        in_specs=[pl.BlockSpec((tm, half), row), pl.BlockSpec((tm, half), row), pl.BlockSpec((tm, d), row),
                  pl.BlockSpec((1, 1, d), bsel), pl.BlockSpec((1, 1, d), bsel), pl.BlockSpec((1, 1, d), bsel),
                  pl.BlockSpec((1, d), const), pl.BlockSpec((1, d), const),
                  pl.BlockSpec((half, d), const), pl.BlockSpec((half, d), const)],
        out_specs=[pl.BlockSpec((tm, d), row)] * 2,
        out_shape=[jax.ShapeDtypeStruct((n, d), F32)] * 2,
        compiler_params=_params(("arbitrary",), VMEM_LIMIT),
        name="out_proj",
    )(att, mh, x2, gate1, scale2, shift2, g_post_mix, g_pre_ffn, wa, wb)


def _top_rows(s, ids, k):
    big = jnp.int32(2 ** 30)
    vals, sel = [], []
    for _ in range(k):
        m = jnp.max(s, axis=0, keepdims=True)
        i = jnp.min(jnp.where(s == m, ids, big), axis=0, keepdims=True)
        vals.append(m)
        sel.append(i)
        s = jnp.where(ids == i, NEG_INF, s)
    return jnp.concatenate(vals, axis=0), jnp.concatenate(sel, axis=0)


def _pick_rows(table, which):
    r = lax.broadcasted_iota(jnp.int32, table.shape, 0)
    rows = []
    for k in range(which.shape[0]):
        rows.append(jnp.sum(jnp.where(r == which[k:k + 1, :], table, 0), axis=0, keepdims=True))
    return jnp.concatenate(rows, axis=0)


def _peer_sel_kernel(h_ref, wq_ref, k1_ref, k2_ref, e_ref, g_ref):
    tq = h_ref.shape[0]
    K = PEER_TOPK
    q = jnp.dot(h_ref[...].astype(BF16), wq_ref[...], preferred_element_type=F32).astype(BF16)
    key_ids = lax.broadcasted_iota(jnp.int32, (PEER_KEYS, tq), 0)
    r16 = lax.broadcasted_iota(jnp.int32, (K, tq), 0)
    r8 = lax.broadcasted_iota(jnp.int32, (SUBLANES, tq), 0)
    flat = jnp.concatenate([r16 * K] + [r8 * K + b for b in range(1, SUBLANES)] + [r8 + SUBLANES], axis=0)
    for hd in range(PEER_HEADS):
        base = hd * 2 * PEER_HALF
        s1 = lax.dot_general(k1_ref[hd], q[:, base:base + PEER_HALF], NT_DIMS, preferred_element_type=F32)
        s2 = lax.dot_general(k2_ref[hd], q[:, base + PEER_HALF:base + 2 * PEER_HALF], NT_DIMS,
                             preferred_element_type=F32)
        v1, i1 = _top_rows(s1, key_ids, K)
        v2, i2 = _top_rows(s2, key_ids, K)
        cand = jnp.concatenate([v1 + v2[0:1, :]]
                               + [v1[0:SUBLANES] + v2[b:b + 1, :] for b in range(1, SUBLANES)]
                               + [v1[0:1, :] + v2[SUBLANES:K]], axis=0)
        top, pos = _top_rows(cand, flat, K)
        eid = _pick_rows(i1, pos >> 4) * PEER_KEYS + _pick_rows(i2, pos & (K - 1))
        ex = jnp.exp(top - top[0:1, :])
        e_ref[hd * K:(hd + 1) * K, :] = eid
        g_ref[hd * K:(hd + 1) * K, :] = ex / jnp.sum(ex, axis=0, keepdims=True)


def _peer_sel(h2, wq, k1, k2, tq):
    n, d = h2.shape
    rows = PEER_HEADS * PEER_TOPK
    return pl.pallas_call(
        _peer_sel_kernel,
        grid=(n // tq,),
        in_specs=[pl.BlockSpec((tq, d), lambda i: (i, 0)),
                  pl.BlockSpec(wq.shape, lambda i: (0, 0)),
                  pl.BlockSpec(k1.shape, lambda i: (0, 0, 0)),
                  pl.BlockSpec(k2.shape, lambda i: (0, 0, 0))],
        out_specs=[pl.BlockSpec((rows, tq), lambda i: (0, i))] * 2,
        out_shape=[jax.ShapeDtypeStruct((rows, n), jnp.int32), jax.ShapeDtypeStruct((rows, n), F32)],
        compiler_params=_params(("arbitrary",), VMEM_LIMIT),
        name="peer_sel",
    )(h2, wq, k1, k2)


def _split_bf16(x):
    hi = x.astype(BF16)
    return hi, (x - hi.astype(F32)).astype(BF16)


def _gather_rows(idx_ref, tab_ref, t, nsel, stage_ref):
    for j in range(nsel):
        stage_ref[j * SUBLANES:(j + 1) * SUBLANES, :] = tab_ref[idx_ref[t, j]]


def _pipelined_tokens(tq, gather, compute, stages):
    nb = len(stages)
    for k in range(nb - 1):
        gather(k, stages[k])

    def trip(i, carry):
        t = nb * i
        for k in range(nb):
            ahead = t + k + nb - 1
            gather(jnp.minimum(ahead, tq - 1), stages[(k + nb - 1) % nb])
            compute(t + k, stages[k])
        return carry

    lax.fori_loop(0, tq // nb, trip, 0)


def _diag_mask(nsel):
    shape = (SUBLANES, nsel * SUBLANES)
    return (lax.broadcasted_iota(jnp.int32, shape, 1) % SUBLANES) == lax.broadcasted_iota(jnp.int32, shape, 0)


def _peer_u_kernel(idx_ref, h_ref, gate_ref, grp_ref, tab_ref, act_ref, part_ref, *stages):
    tq, nsel = gate_ref.shape
    diag = _diag_mask(nsel)

    def gather(t, stage_ref):
        _gather_rows(idx_ref, tab_ref, t, nsel, stage_ref)

    def compute(t, stage_ref):
        h_hi, h_lo = _split_bf16(h_ref[t])
        both = lax.dot_general(jnp.concatenate([h_hi, h_lo], axis=0), stage_ref[...], NT_DIMS,
                               preferred_element_type=F32)
        prod = both[0:SUBLANES] + both[SUBLANES:2 * SUBLANES]
        part_ref[pl.ds(pl.multiple_of(t * SUBLANES, SUBLANES), SUBLANES), :] = jnp.where(diag, prod, 0.0)

    _pipelined_tokens(tq, gather, compute, stages)
    p_hi, p_lo = _split_bf16(part_ref[...])
    grp = grp_ref[...]
    pre = (jnp.dot(p_hi, grp, preferred_element_type=F32)
           + jnp.dot(p_lo, grp, preferred_element_type=F32)).reshape(tq, SUBLANES, nsel).sum(axis=1)
    act_ref[...] = 0.5 * pre * (1.0 + lax.erf(pre * (2.0 ** -0.5))) * gate_ref[...]


def _group_matrix(nsel):
    r = jnp.arange(nsel * SUBLANES)[:, None] // SUBLANES
    return (r == jnp.arange(nsel)[None, :]).astype(BF16)


def _peer_u(idx, h3, gate, tab, tq):
    n, nsel = idx.shape
    wide = nsel * SUBLANES
    return pl.pallas_call(
        _peer_u_kernel,
        grid=(n // tq,),
        in_specs=[pl.BlockSpec((tq, nsel), lambda i: (i, 0), memory_space=pltpu.SMEM),
                  pl.BlockSpec((tq, SUBLANES, LANES), lambda i: (i, 0, 0)),
                  pl.BlockSpec((tq, nsel), lambda i: (i, 0)),
                  pl.BlockSpec((wide, nsel), lambda i: (0, 0)),
                  pl.BlockSpec(memory_space=pltpu.VMEM)],
        out_specs=pl.BlockSpec((tq, nsel), lambda i: (i, 0)),
        out_shape=jax.ShapeDtypeStruct((n, nsel), F32),
        scratch_shapes=[pltpu.VMEM((tq * SUBLANES, wide), F32),
                        ] + [pltpu.VMEM((wide, LANES), BF16)] * GATHER_STAGES,
        compiler_params=_params(("arbitrary",), VMEM_LIMIT),
        name="peer_u",
    )(idx, h3, gate, _group_matrix(nsel), tab)


def _peer_v_kernel(idx_ref, act_ref, rep_ref, tab_ref, y_ref, wide_ref, *stages):
    tq = y_ref.shape[0]
    nsel = idx_ref.shape[1]
    diag = _diag_mask(nsel)
    a_hi, a_lo = _split_bf16(act_ref[...])
    rep = rep_ref[...]
    wide_ref[...] = (jnp.dot(a_hi, rep, preferred_element_type=F32)
                     + jnp.dot(a_lo, rep, preferred_element_type=F32))

    def gather(t, stage_ref):
        _gather_rows(idx_ref, tab_ref, t, nsel, stage_ref)

    def compute(t, stage_ref):
        w = jnp.where(diag, jnp.broadcast_to(wide_ref[pl.ds(t, 1), :], diag.shape), 0.0)
        w_hi, w_lo = _split_bf16(w)
        both = jnp.dot(jnp.concatenate([w_hi, w_lo], axis=0), stage_ref[...], preferred_element_type=F32)
        y_ref[t] = both[0:SUBLANES] + both[SUBLANES:2 * SUBLANES]

    _pipelined_tokens(tq, gather, compute, stages)


def _peer_v(idx, act, tab, tq):
    n, nsel = idx.shape
    wide = nsel * SUBLANES
    return pl.pallas_call(
        _peer_v_kernel,
        grid=(n // tq,),
        in_specs=[pl.BlockSpec((tq, nsel), lambda i: (i, 0), memory_space=pltpu.SMEM),
                  pl.BlockSpec((tq, nsel), lambda i: (i, 0)),
                  pl.BlockSpec((nsel, wide), lambda i: (0, 0)),
                  pl.BlockSpec(memory_space=pltpu.VMEM)],
        out_specs=pl.BlockSpec((tq, SUBLANES, LANES), lambda i: (i, 0, 0)),
        out_shape=jax.ShapeDtypeStruct((n, SUBLANES, LANES), F32),
        scratch_shapes=[pltpu.VMEM((tq, wide), F32),
                        ] + [pltpu.VMEM((wide, LANES), BF16)] * GATHER_STAGES,
        compiler_params=_params(("arbitrary",), VMEM_LIMIT),
        name="peer_v",
    )(idx, act, _group_matrix(nsel).T, tab)


def _final_kernel(x1_ref, y_ref, g2_ref, gpf_ref, o_ref):
    o_ref[...] = x1_ref[...] + g2_ref[0] * _rms(y_ref[...], gpf_ref[...])


def _final(x1, y, gate2, g_post_ffn, seq, tm):
    n, d = x1.shape
    per_b = seq // tm
    row = lambda i: (i, 0)
    return pl.pallas_call(
        _final_kernel,
        grid=(n // tm,),
        in_specs=[pl.BlockSpec((tm, d), row), pl.BlockSpec((tm, d), row),
                  pl.BlockSpec((1, 1, d), lambda i: (i // per_b, 0, 0)),
                  pl.BlockSpec((1, d), lambda i: (0, 0))],
        out_specs=pl.BlockSpec((tm, d), row),
        out_shape=jax.ShapeDtypeStruct((n, d), F32),
        compiler_params=_params(("arbitrary",)),
        name="final",
    )(x1, y, gate2, g_post_ffn)


def _dup_heads(w, heads, dh):
    d = w.shape[0]
    return jnp.repeat(w.reshape(d, heads, 1, dh), 2, axis=2).reshape(d, heads * 2 * dh)


def _layer(x2, c, pos_col, bsz, seq, w_mod, b_mod, g_pre_mix, g_post_mix, w_in, conv_w, b_igate, b_fgate,
           mlstm_norm_g, att_sinks, w_out, g_pre_ffn, g_post_ffn, peer_wq, peer_keys1, peer_keys2, peer_u, peer_v):
    n, d = x2.shape
    tm = min(seq, 512)
    mod = _mod(c, w_mod, b_mod)
    shift1, scale1, gate1, shift2, scale2, gate2 = [m.reshape(bsz, 1, d) for m in jnp.split(mod, 6, axis=-1)]

    aw = ATT_HEADS * ATT_HEAD_DIM
    kvw = ATT_KV_HEADS * ATT_HEAD_DIM
    qkw = MLSTM_HEADS * MLSTM_QK_DIM
    mw = MLSTM_HEADS * MLSTM_V_DIM
    o = 0
    wq_a = w_in[:, o:o + aw]; o += aw
    wk_a = w_in[:, o:o + kvw]; o += kvw
    wv_a = w_in[:, o:o + kvw]; o += kvw
    w_mqk = w_in[:, o:o + 2 * qkw]; o += 2 * qkw
    w_mv = w_in[:, o:o + mw]; o += mw
    w_g = w_in[:, o:o + 2 * MLSTM_HEADS]; o += 2 * MLSTM_HEADS
    w_mo = w_in[:, o:o + mw]
    w_gp = jnp.pad(w_g, ((0, 0), (0, LANES - 2 * MLSTM_HEADS)))
    w_all = jnp.concatenate([wq_a, _dup_heads(wk_a, ATT_KV_HEADS, ATT_HEAD_DIM),
                             _dup_heads(wv_a, ATT_KV_HEADS, ATT_HEAD_DIM), w_mqk, w_mv, w_mo, w_gp],
                            axis=1).astype(BF16)

    cos, sin = _rope_tab(pos_col)
    q, kd, vd, mqk, mv, mo, gts = _in_proj(x2, scale1, shift1, g_pre_mix.reshape(1, d), cos, sin, w_all, seq, tm)
    att = _swa(att_sinks, q, kd, vd, bsz, seq)
    gate_bias = jnp.pad(jnp.concatenate([b_igate, b_fgate]), (0, LANES - 2 * MLSTM_HEADS)).reshape(1, LANES)
    chunks = min(seq // MLSTM_CHUNK, 8)
    mh = _mlstm(mqk, mv, mo, gts, conv_w, gate_bias, mlstm_norm_g.reshape(1, mw), bsz, seq, chunks)
    wo = w_out.astype(BF16)
    x1, h2 = _out_proj(att, mh, x2, gate1, scale2, shift2, g_post_mix.reshape(1, d), g_pre_ffn.reshape(1, d),
                       wo[:aw], wo[aw:], seq, tm)

    tq = min(n, 256)
    eid_t, gate_t = _peer_sel(h2, peer_wq.astype(BF16), peer_keys1.astype(BF16), peer_keys2.astype(BF16), tq)
    eid = eid_t.T
    gate = gate_t.T
    nexp = peer_u.shape[0]
    tab_u = peer_u.astype(BF16).reshape(nexp, SUBLANES, LANES)
    tab_v = peer_v.astype(BF16).reshape(nexp, SUBLANES, LANES)
    act = _peer_u(eid, h2.reshape(n, SUBLANES, LANES), gate, tab_u, tq)
    y = _peer_v(eid, act, tab_v, tq).reshape(n, d)
    return _final(x1, y, gate2, g_post_ffn.reshape(1, d), seq, tm)


def kernel(x, c, positions, w_mod, b_mod, g_pre_mix, g_post_mix, w_in, conv_w, b_igate, b_fgate, mlstm_norm_g, att_sinks, w_out, g_pre_ffn, g_post_ffn, peer_wq, peer_keys1, peer_keys2, peer_u, peer_v):
    bsz, seq, d = x.shape
    n = bsz * seq
    x2 = x.reshape(n, d)
    pos_col = positions.reshape(n, 1)
    for l in range(w_mod.shape[0]):
        x2 = _layer(x2, c, pos_col, bsz, seq, w_mod[l], b_mod[l], g_pre_mix[l], g_post_mix[l], w_in[l], conv_w[l],
                    b_igate[l], b_fgate[l], mlstm_norm_g[l], att_sinks[l], w_out[l], g_pre_ffn[l], g_post_ffn[l],
                    peer_wq[l], peer_keys1[l], peer_keys2[l], peer_u[l], peer_v[l])
    return x2.reshape(bsz, seq, d)
```

```python
import functools

import jax
import jax.numpy as jnp
from jax import lax
from jax.experimental import pallas as pl
from jax.experimental.pallas import tpu as pltpu

F32 = jnp.float32
BF16 = jnp.bfloat16

ATT_HEADS = 8
ATT_KV_HEADS = 2
ATT_HEAD_DIM = 64
ATT_BLOCK = 128
ROPE_THETA = 10000.0
MLSTM_HEADS = 4
MLSTM_V_DIM = 128
MLSTM_QK_DIM = 64
MLSTM_CHUNK = 64
CONV_WIDTH = 4
PEER_HEADS = 8
PEER_KEYS = 128
PEER_HALF = 128
PEER_TOPK = 16
NORM_EPS = 1e-6

LANES = 128
SUBLANES = 8
VMEM_LIMIT = 52 * 1024 * 1024
GATHER_STAGES = 8
ROW_WORDS = SUBLANES // 2

NEG_INF = float("-inf")
NT_DIMS = (((1,), (1,)), ((), ()))
TN_DIMS = (((0,), (0,)), ((), ()))


def _params(sem, vmem=None):
    return pltpu.CompilerParams(dimension_semantics=sem, vmem_limit_bytes=vmem)


def _rms(x, g):
    return x * lax.rsqrt(jnp.mean(x * x, axis=-1, keepdims=True) + NORM_EPS) * g


def _mod_kernel(c_ref, w_ref, b_ref, o_ref):
    o_ref[...] = jnp.dot(c_ref[...], w_ref[...], preferred_element_type=F32,
                         precision=lax.Precision.HIGHEST) + b_ref[...]


def _mod(c, w, b):
    bsz, d = c.shape
    nout = w.shape[1]
    return pl.pallas_call(
        _mod_kernel,
        grid=(nout // d,),
        in_specs=[pl.BlockSpec((bsz, d), lambda i: (0, 0)),
                  pl.BlockSpec((d, d), lambda i: (0, i)),
                  pl.BlockSpec((1, d), lambda i: (0, i))],
        out_specs=pl.BlockSpec((bsz, d), lambda i: (0, i)),
        out_shape=jax.ShapeDtypeStruct((bsz, nout), F32),
        compiler_params=_params(("arbitrary",)),
        name="mod",
    )(c, w, b.reshape(1, nout))


def _rope_tab_kernel(pos_ref, inv_ref, sign_ref, cos_ref, sin_ref):
    ang = pos_ref[...].astype(F32) * inv_ref[...]
    cos_ref[...] = jnp.cos(ang)
    sin_ref[...] = jnp.sin(ang) * sign_ref[...]


def _rope_tab(pos_col):
    n = pos_col.shape[0]
    tr = min(n, 1024)
    half = ATT_HEAD_DIM // 2
    inv = ROPE_THETA ** (-jnp.arange(0, ATT_HEAD_DIM, 2, dtype=F32) / ATT_HEAD_DIM)
    inv_row = jnp.tile(inv, LANES // half).reshape(1, LANES)
    lane = jnp.arange(LANES)
    sign_row = jnp.where((lane % ATT_HEAD_DIM) < half, -1.0, 1.0).astype(F32).reshape(1, LANES)
    return pl.pallas_call(
        _rope_tab_kernel,
        grid=(n // tr,),
        in_specs=[pl.BlockSpec((tr, 1), lambda i: (i, 0)),
                  pl.BlockSpec((1, LANES), lambda i: (0, 0)),
                  pl.BlockSpec((1, LANES), lambda i: (0, 0))],
        out_specs=[pl.BlockSpec((tr, LANES), lambda i: (i, 0))] * 2,
        out_shape=[jax.ShapeDtypeStruct((n, LANES), F32)] * 2,
        compiler_params=_params(("arbitrary",)),
        name="rope_tab",
    )(pos_col, inv_row, sign_row)


def _rope(v, cos, sin):
    half = ATT_HEAD_DIM // 2
    lane = lax.broadcasted_iota(jnp.int32, cos.shape, 1)
    first = (lane % ATT_HEAD_DIM) < half
    outs = []
    for j in range(v.shape[1] // LANES):
        c = v[:, j * LANES:(j + 1) * LANES]
        rot = jnp.where(first, pltpu.roll(c, LANES - half, 1), pltpu.roll(c, half, 1))
        outs.append(c * cos + rot * sin)
    return jnp.concatenate(outs, axis=1)


_C_Q, _C_K, _C_V, _C_MQK, _C_MV, _C_MO, _C_G, _C_END = 0, 512, 768, 1024, 1536, 2048, 2560, 2688


def _in_proj_kernel(x_ref, sc_ref, sh_ref, g_ref, cos_ref, sin_ref, w_ref,
                    q_ref, k_ref, v_ref, mqk_ref, mv_ref, mo_ref, gt_ref):
    x = x_ref[...]
    h = _rms(x, g_ref[...]) * (1.0 + sc_ref[0]) + sh_ref[0]
    hb = h.astype(BF16)

    def mm(a, b):
        return jnp.dot(hb, w_ref[:, a:b], preferred_element_type=F32)

    cos = cos_ref[...]
    sin = sin_ref[...]
    q_ref[...] = (_rope(mm(_C_Q, _C_K), cos, sin) * (ATT_HEAD_DIM ** -0.5)).astype(BF16)
    k_ref[...] = _rope(mm(_C_K, _C_V), cos, sin).astype(BF16)
    v_ref[...] = mm(_C_V, _C_MQK).astype(BF16)
    mqk_ref[...] = mm(_C_MQK, _C_MV)
    mv_ref[...] = mm(_C_MV, _C_MO).astype(BF16)
    mo_ref[...] = mm(_C_MO, _C_G)
    gt_ref[...] = mm(_C_G, _C_END)


def _in_proj(x2, scale1, shift1, g_pre, cos, sin, w_all, seq, tm):
    n, d = x2.shape
    per_b = seq // tm
    row = lambda i: (i, 0)
    bsel = lambda i: (i // per_b, 0, 0)
    widths = (512, 256, 256, 512, 512, 512, 128)
    dtypes = (BF16, BF16, BF16, F32, BF16, F32, F32)
    return pl.pallas_call(
        _in_proj_kernel,
        grid=(n // tm,),
        in_specs=[pl.BlockSpec((tm, d), row),
                  pl.BlockSpec((1, 1, d), bsel),
                  pl.BlockSpec((1, 1, d), bsel),
                  pl.BlockSpec((1, d), lambda i: (0, 0)),
                  pl.BlockSpec((tm, LANES), row),
                  pl.BlockSpec((tm, LANES), row),
                  pl.BlockSpec((d, _C_END), lambda i: (0, 0))],
        out_specs=[pl.BlockSpec((tm, w), row) for w in widths],
        out_shape=[jax.ShapeDtypeStruct((n, w), dt) for w, dt in zip(widths, dtypes)],
        compiler_params=_params(("arbitrary",), VMEM_LIMIT),
        name="in_proj",
    )(x2, scale1, shift1, g_pre, cos, sin, w_all)


def _swa_kernel(sink_ref, q_ref, kp_ref, kc_ref, vp_ref, vc_ref, o_ref):
    blk = ATT_BLOCK
    n = pl.program_id(1)
    qi = lax.broadcasted_iota(jnp.int32, (blk, 2 * blk), 0)
    si = lax.broadcasted_iota(jnp.int32, (blk, 2 * blk), 1)
    delta = qi + blk - si
    valid = (delta >= 0) & (delta < blk) & ((si >= blk) | (n > 0))
    lo = lax.broadcasted_iota(jnp.int32, (2 * blk, LANES), 1) < ATT_HEAD_DIM
    group = ATT_HEADS // ATT_KV_HEADS
    for g in range(ATT_KV_HEADS):
        cs = slice(g * LANES, (g + 1) * LANES)
        k = jnp.concatenate([kp_ref[:, cs], kc_ref[:, cs]], axis=0)
        v = jnp.concatenate([vp_ref[:, cs], vc_ref[:, cs]], axis=0)
        zero = jnp.zeros_like(k)
        halves = ((jnp.where(lo, k, zero), jnp.where(lo, v, zero)),
                  (jnp.where(lo, zero, k), jnp.where(lo, zero, v)))
        for jj in range(group // 2):
            p = g * (group // 2) + jj
            q2 = q_ref[:, p * LANES:(p + 1) * LANES]
            acc = jnp.zeros((blk, LANES), F32)
            for half, (kh, vh) in enumerate(halves):
                s = lax.dot_general(q2, kh, NT_DIMS, preferred_element_type=F32)
                s = jnp.where(valid, s, NEG_INF)
                sink = sink_ref[2 * p + half]
                m = jnp.maximum(jnp.max(s, axis=-1, keepdims=True), sink)
                e = jnp.exp(s - m)
                den = jnp.sum(e, axis=-1, keepdims=True) + jnp.exp(sink - m)
                acc = acc + jnp.dot((e / den).astype(BF16), vh, preferred_element_type=F32)
            o_ref[:, p * LANES:(p + 1) * LANES] = acc.astype(BF16)


def _swa(sinks, q, kd, vd, bsz, seq):
    n = q.shape[0]
    nb = seq // ATT_BLOCK
    cur = lambda b, i: (b * nb + i, 0)
    prev = lambda b, i: (b * nb + jnp.maximum(i - 1, 0), 0)
    return pl.pallas_call(
        _swa_kernel,
        grid=(bsz, nb),
        in_specs=[pl.BlockSpec(memory_space=pltpu.SMEM),
                  pl.BlockSpec((ATT_BLOCK, 512), cur),
                  pl.BlockSpec((ATT_BLOCK, 256), prev),
                  pl.BlockSpec((ATT_BLOCK, 256), cur),
                  pl.BlockSpec((ATT_BLOCK, 256), prev),
                  pl.BlockSpec((ATT_BLOCK, 256), cur)],
        out_specs=pl.BlockSpec((ATT_BLOCK, 512), cur),
        out_shape=jax.ShapeDtypeStruct((n, 512), BF16),
        compiler_params=_params(("arbitrary", "arbitrary")),
        name="swa",
    )(sinks, q, kd, kd, vd, vd)


def _mlstm_kernel(mqk_ref, mv_ref, mo_ref, gt_ref, cw_ref, gb_ref, ng_ref, o_ref,
                  tail_ref, qk_ref, xs_ref, ct_ref, n_ref, m_ref, *, chunks):
    L = MLSTM_CHUNK
    tm = chunks * L
    nqk = MLSTM_HEADS * MLSTM_QK_DIM

    @pl.when(pl.program_id(1) == 0)
    def _():
        tail_ref[...] = jnp.zeros_like(tail_ref)
        ct_ref[...] = jnp.zeros_like(ct_ref)
        n_ref[...] = jnp.zeros_like(n_ref)
        m_ref[...] = jnp.zeros_like(m_ref)

    cur = mqk_ref[...]
    full = jnp.concatenate([tail_ref[...], cur], axis=0)
    off = SUBLANES - (CONV_WIDTH - 1)
    acc = full[off:off + tm] * cw_ref[0:1, :]
    for j in range(1, CONV_WIDTH):
        acc = acc + full[off + j:off + j + tm] * cw_ref[j:j + 1, :]
    act = acc * jax.nn.sigmoid(acc)
    col = lax.broadcasted_iota(jnp.int32, (1, 2 * nqk), 1)
    act = act * jnp.where(col < nqk, MLSTM_QK_DIM ** -0.5, 1.0)
    qk_ref[...] = act.astype(BF16)
    tail_ref[...] = cur[tm - SUBLANES:tm]

    lane = lax.broadcasted_iota(jnp.int32, (tm, LANES), 1)
    gts = gt_ref[...] + gb_ref[...]
    logsig = jnp.minimum(gts, 0.0) - jnp.log(1.0 + jnp.exp(-jnp.abs(gts)))
    xs_ref[...] = jnp.where(lane < MLSTM_HEADS, gts, jnp.where(lane < 2 * MLSTM_HEADS, logsig, 0.0))

    ri = lax.broadcasted_iota(jnp.int32, (L, L), 0)
    ci = lax.broadcasted_iota(jnp.int32, (L, L), 1)
    causal = ci <= ri
    tril = causal.astype(F32)
    lane_l = lax.broadcasted_iota(jnp.int32, (L, LANES), 1)
    lo_l = lane_l < MLSTM_QK_DIM
    row_c = lax.broadcasted_iota(jnp.int32, (LANES, 1), 0) < MLSTM_QK_DIM
    lane_1 = lax.broadcasted_iota(jnp.int32, (1, LANES), 1) < MLSTM_QK_DIM

    def chunk(c, carry):
        r0 = pl.multiple_of(c * L, L)
        rows = pl.ds(r0, L)
        xc = xs_ref[rows, :]
        bc = jnp.dot(tril, xc, preferred_element_type=F32, precision=lax.Precision.HIGHEST)
        x2 = jnp.where(lane_l < MLSTM_HEADS, xc, bc)
        xt = x2.T
        for p in range(MLSTM_HEADS // 2):
            q2 = qk_ref[rows, p * LANES:(p + 1) * LANES]
            k2 = qk_ref[rows, nqk + p * LANES:nqk + (p + 1) * LANES]
            ct_old = ct_ref[p]
            ctb = ct_old.astype(BF16)
            n2 = n_ref[p:p + 1, :]
            decs, upds, kws = [], [], []
            for half in range(2):
                h = 2 * p + half
                hm = lo_l if half == 0 else jnp.logical_not(lo_l)
                zero = jnp.zeros_like(q2)
                qm = jnp.where(hm, q2, zero)
                km = jnp.where(hm, k2, zero)
                v = mv_ref[rows, h * LANES:(h + 1) * LANES]
                b_col = x2[:, MLSTM_HEADS + h:MLSTM_HEADS + h + 1]
                ig_col = x2[:, h:h + 1]
                b_row = xt[MLSTM_HEADS + h:MLSTM_HEADS + h + 1, :]
                ig_row = xt[h:h + 1, :]
                m_prev = m_ref[h:h + 1, 0:1]
                dlog = jnp.where(causal, b_col - b_row + ig_row, NEG_INF)
                m_inter = b_col + m_prev
                m_t = jnp.maximum(m_inter, jnp.max(dlog, axis=-1, keepdims=True))
                w_intra = jnp.exp(dlog - m_t)
                a_inter = jnp.exp(m_inter - m_t)
                s = lax.dot_general(q2, km, NT_DIMS, preferred_element_type=F32) * w_intra
                num = (jnp.dot(s.astype(BF16), v, preferred_element_type=F32)
                       + a_inter * jnp.dot(qm, ctb, preferred_element_type=F32))
                den = (jnp.sum(s, axis=-1, keepdims=True)
                       + a_inter * jnp.sum(qm.astype(F32) * n2, axis=-1, keepdims=True))
                hh = num / jnp.maximum(jnp.abs(den), jnp.exp(-m_t))
                y = _rms(hh, ng_ref[:, h * LANES:(h + 1) * LANES])
                y = y * jax.nn.sigmoid(mo_ref[rows, h * LANES:(h + 1) * LANES])
                o_ref[rows, h * LANES:(h + 1) * LANES] = y.astype(BF16)
                b_last = xt[MLSTM_HEADS + h:MLSTM_HEADS + h + 1, L - 1:L]
                g_col = b_last - b_col + ig_col
                m_new = jnp.maximum(b_last + m_prev, jnp.max(g_col, axis=0, keepdims=True))
                kw = km.astype(F32) * jnp.exp(g_col - m_new)
                decs.append(jnp.exp(b_last + m_prev - m_new))
                kws.append(kw)
                upds.append(lax.dot_general(kw.astype(BF16), v, TN_DIMS, preferred_element_type=F32))
                m_ref[h:h + 1, :] = jnp.broadcast_to(m_new, (1, LANES))
            ct_ref[p] = ct_old * jnp.where(row_c, decs[0], decs[1]) + upds[0] + upds[1]
            n_ref[p:p + 1, :] = (n2 * jnp.where(lane_1, decs[0], decs[1])
                                 + jnp.sum(kws[0] + kws[1], axis=0, keepdims=True))
        return carry

    lax.fori_loop(0, chunks, chunk, 0)


def _mlstm(mqk, mv, mo, gts, conv_w, gate_bias, norm_g, bsz, seq, chunks):
    n = mqk.shape[0]
    tm = chunks * MLSTM_CHUNK
    steps = seq // tm
    row = lambda b, i: (b * steps + i, 0)
    const = lambda b, i: (0, 0)
    width = MLSTM_HEADS * MLSTM_V_DIM
    return pl.pallas_call(
        functools.partial(_mlstm_kernel, chunks=chunks),
        grid=(bsz, steps),
        in_specs=[pl.BlockSpec((tm, width), row),
                  pl.BlockSpec((tm, width), row),
                  pl.BlockSpec((tm, width), row),
                  pl.BlockSpec((tm, LANES), row),
                  pl.BlockSpec((CONV_WIDTH, width), const),
                  pl.BlockSpec((1, LANES), const),
                  pl.BlockSpec((1, width), const)],
        out_specs=pl.BlockSpec((tm, width), row),
        out_shape=jax.ShapeDtypeStruct((n, width), BF16),
        scratch_shapes=[pltpu.VMEM((SUBLANES, width), F32),
                        pltpu.VMEM((tm, width), BF16),
                        pltpu.VMEM((tm, LANES), F32),
                        pltpu.VMEM((MLSTM_HEADS // 2, LANES, LANES), F32),
                        pltpu.VMEM((SUBLANES, LANES), F32),
                        pltpu.VMEM((SUBLANES, LANES), F32)],
        compiler_params=_params(("arbitrary", "arbitrary")),
        name="mlstm",
    )(mqk, mv, mo, gts, conv_w, gate_bias, norm_g)


def _out_proj_kernel(att_ref, mh_ref, x_ref, g1_ref, sc_ref, sh_ref, gpm_ref, gpf_ref, wa_ref, wb_ref,
                     x1_ref, h2_ref):
    mix = (jnp.dot(att_ref[...], wa_ref[...], preferred_element_type=F32)
           + jnp.dot(mh_ref[...], wb_ref[...], preferred_element_type=F32))
    x1 = x_ref[...] + g1_ref[0] * _rms(mix, gpm_ref[...])
    x1_ref[...] = x1
    h2_ref[...] = _rms(x1, gpf_ref[...]) * (1.0 + sc_ref[0]) + sh_ref[0]


def _out_proj(att, mh, x2, gate1, scale2, shift2, g_post_mix, g_pre_ffn, wa, wb, seq, tm):
    n, d = x2.shape
    per_b = seq // tm
    row = lambda i: (i, 0)
    bsel = lambda i: (i // per_b, 0, 0)
    const = lambda i: (0, 0)
    half = att.shape[1]
    return pl.pallas_call(
        _out_proj_kernel,
        grid=(n // tm,),
        in_specs=[pl.BlockSpec((tm, half), row), pl.BlockSpec((tm, half), row), pl.BlockSpec((tm, d), row),
                  pl.BlockSpec((1, 1, d), bsel), pl.BlockSpec((1, 1, d), bsel), pl.BlockSpec((1, 1, d), bsel),
                  pl.BlockSpec((1, d), const), pl.BlockSpec((1, d), const),
                  pl.BlockSpec((half, d), const), pl.BlockSpec((half, d), const)],
        out_specs=[pl.BlockSpec((tm, d), row)] * 2,
        out_shape=[jax.ShapeDtypeStruct((n, d), F32)] * 2,
        compiler_params=_params(("arbitrary",), VMEM_LIMIT),
        name="out_proj",
    )(att, mh, x2, gate1, scale2, shift2, g_post_mix, g_pre_ffn, wa, wb)


def _top_rows(s, ids, k):
    big = jnp.int32(2 ** 30)
    vals, sel = [], []
    for _ in range(k):
        m = jnp.max(s, axis=0, keepdims=True)
        i = jnp.min(jnp.where(s == m, ids, big), axis=0, keepdims=True)
        vals.append(m)
        sel.append(i)
        s = jnp.where(ids == i, NEG_INF, s)
    return jnp.concatenate(vals, axis=0), jnp.concatenate(sel, axis=0)


def _pick_rows(table, which):
    r = lax.broadcasted_iota(jnp.int32, table.shape, 0)
    rows = []
    for k in range(which.shape[0]):
        rows.append(jnp.sum(jnp.where(r == which[k:k + 1, :], table, 0), axis=0, keepdims=True))
    return jnp.concatenate(rows, axis=0)


def _peer_sel_kernel(h_ref, wq_ref, k1_ref, k2_ref, e_ref, g_ref):
    tq = h_ref.shape[0]
    K = PEER_TOPK
    q = jnp.dot(h_ref[...].astype(BF16), wq_ref[...], preferred_element_type=F32).astype(BF16)
    key_ids = lax.broadcasted_iota(jnp.int32, (PEER_KEYS, tq), 0)
    r16 = lax.broadcasted_iota(jnp.int32, (K, tq), 0)
    r8 = lax.broadcasted_iota(jnp.int32, (SUBLANES, tq), 0)
    flat = jnp.concatenate([r16 * K] + [r8 * K + b for b in range(1, SUBLANES)] + [r8 + SUBLANES], axis=0)
    for hd in range(PEER_HEADS):
        base = hd * 2 * PEER_HALF
        s1 = lax.dot_general(k1_ref[hd], q[:, base:base + PEER_HALF], NT_DIMS, preferred_element_type=F32)
        s2 = lax.dot_general(k2_ref[hd], q[:, base + PEER_HALF:base + 2 * PEER_HALF], NT_DIMS,
                             preferred_element_type=F32)
        v1, i1 = _top_rows(s1, key_ids, K)
        v2, i2 = _top_rows(s2, key_ids, K)
        cand = jnp.concatenate([v1 + v2[0:1, :]]
                               + [v1[0:SUBLANES] + v2[b:b + 1, :] for b in range(1, SUBLANES)]
                               + [v1[0:1, :] + v2[SUBLANES:K]], axis=0)
        top, pos = _top_rows(cand, flat, K)
        eid = _pick_rows(i1, pos >> 4) * PEER_KEYS + _pick_rows(i2, pos & (K - 1))
        ex = jnp.exp(top - top[0:1, :])
        e_ref[hd * K:(hd + 1) * K, :] = eid * ROW_WORDS
        g_ref[hd * K:(hd + 1) * K, :] = ex / jnp.sum(ex, axis=0, keepdims=True)


def _peer_sel(h2, wq, k1, k2, tq):
    n, d = h2.shape
    rows = PEER_HEADS * PEER_TOPK
    return pl.pallas_call(
        _peer_sel_kernel,
        grid=(n // tq,),
        in_specs=[pl.BlockSpec((tq, d), lambda i: (i, 0)),
                  pl.BlockSpec(wq.shape, lambda i: (0, 0)),
                  pl.BlockSpec(k1.shape, lambda i: (0, 0, 0)),
                  pl.BlockSpec(k2.shape, lambda i: (0, 0, 0))],
        out_specs=[pl.BlockSpec((rows, tq), lambda i: (0, i))] * 2,
        out_shape=[jax.ShapeDtypeStruct((rows, n), jnp.int32), jax.ShapeDtypeStruct((rows, n), F32)],
        compiler_params=_params(("arbitrary",), VMEM_LIMIT),
        name="peer_sel",
    )(h2, wq, k1, k2)


def _split_bf16(x):
    hi = x.astype(BF16)
    return hi, (x - hi.astype(F32)).astype(BF16)


def _pack_table(w):
    e = w.shape[0]
    t = w.astype(BF16).reshape(e, ROW_WORDS, 2, LANES).transpose(0, 1, 3, 2)
    return lax.bitcast_convert_type(t, jnp.int32).reshape(e * ROW_WORDS, LANES)


def _gather_rows(slots, tab_ref, t, stage_ref):
    for j, slot in enumerate(slots):
        src = pl.ds(pl.multiple_of(slot[t], ROW_WORDS), ROW_WORDS)
        stage_ref[j * ROW_WORDS:(j + 1) * ROW_WORDS, :] = tab_ref[src, :]


def _staged_bf16(stage_ref):
    return pltpu.bitcast(stage_ref[...], BF16)


def _pipelined_tokens(tq, gather, compute, stages):
    nb = len(stages)
    for k in range(nb):
        gather(k, stages[k])

    def trip(i, carry):
        t = nb * i
        for k in range(nb):
            compute(t + k, stages[k])
            ahead = t + k + nb
            gather(jnp.minimum(ahead, tq - 1), stages[k])
        return carry

    lax.fori_loop(0, tq // nb, trip, 0)


def _with_slot_indices(idx_hbm, sems, bufs, tq, run):
    nsel = len(bufs) // 2
    step = pl.program_id(0)
    last = pl.num_programs(0) - 1

    def copies(block, which):
        return [pltpu.make_async_copy(idx_hbm.at[j, pl.ds(block * tq, tq)], bufs[which * nsel + j], sems.at[which])
                for j in range(nsel)]

    @pl.when(step == 0)
    def _():
        for cp in copies(0, 0):
            cp.start()

    def phase(which):
        for cp in copies(step, which):
            cp.wait()

        @pl.when(step < last)
        def _():
            for cp in copies(step + 1, 1 - which):
                cp.start()

        run(bufs[which * nsel:(which + 1) * nsel])

    for which in range(2):
        pl.when(step % 2 == which)(functools.partial(phase, which))


def _diag_mask(nsel):
    shape = (SUBLANES, nsel * SUBLANES)
    return (lax.broadcasted_iota(jnp.int32, shape, 1) % SUBLANES) == lax.broadcasted_iota(jnp.int32, shape, 0)


def _peer_u_kernel(idx_hbm, h_ref, gate_ref, grp_ref, tab_ref, act_ref, part_ref, *scratch):
    tq, nsel = gate_ref.shape
    stages, sems, bufs = scratch[:GATHER_STAGES], scratch[GATHER_STAGES], scratch[GATHER_STAGES + 1:]
    diag = _diag_mask(nsel)

    def compute(t, stage_ref):
        h_hi, h_lo = _split_bf16(h_ref[t])
        both = lax.dot_general(jnp.concatenate([h_hi, h_lo], axis=0), _staged_bf16(stage_ref), NT_DIMS,
                               preferred_element_type=F32)
        prod = both[0:SUBLANES] + both[SUBLANES:2 * SUBLANES]
        part_ref[pl.ds(pl.multiple_of(t * SUBLANES, SUBLANES), SUBLANES), :] = jnp.where(diag, prod, 0.0)

    def run(slots):
        _pipelined_tokens(tq, functools.partial(_gather_rows, slots, tab_ref), compute, stages)

    _with_slot_indices(idx_hbm, sems, bufs, tq, run)
    p_hi, p_lo = _split_bf16(part_ref[...])
    grp = grp_ref[...]
    pre = (jnp.dot(p_hi, grp, preferred_element_type=F32)
           + jnp.dot(p_lo, grp, preferred_element_type=F32)).reshape(tq, SUBLANES, nsel).sum(axis=1)
    act_ref[...] = 0.5 * pre * (1.0 + lax.erf(pre * (2.0 ** -0.5))) * gate_ref[...]


def _group_matrix(nsel):
    r = jnp.arange(nsel * SUBLANES)[:, None] // SUBLANES
    return (r == jnp.arange(nsel)[None, :]).astype(BF16)


def _gather_scratch(nsel, tq):
    return ([pltpu.VMEM((nsel * ROW_WORDS, LANES), jnp.int32)] * GATHER_STAGES
            + [pltpu.SemaphoreType.DMA((2,))] + [pltpu.SMEM((tq,), jnp.int32)] * (2 * nsel))


def _peer_u(idx_t, h3, gate, tab, tq):
    nsel, n = idx_t.shape
    wide = nsel * SUBLANES
    scratch = _gather_scratch(nsel, tq)
    return pl.pallas_call(
        _peer_u_kernel,
        grid=(n // tq,),
        in_specs=[pl.BlockSpec(memory_space=pl.ANY),
                  pl.BlockSpec((tq, SUBLANES, LANES), lambda i: (i, 0, 0)),
                  pl.BlockSpec((tq, nsel), lambda i: (i, 0)),
                  pl.BlockSpec((wide, nsel), lambda i: (0, 0)),
                  pl.BlockSpec(memory_space=pltpu.VMEM)],
        out_specs=pl.BlockSpec((tq, nsel), lambda i: (i, 0)),
        out_shape=jax.ShapeDtypeStruct((n, nsel), F32),
        scratch_shapes=[pltpu.VMEM((tq * SUBLANES, wide), F32)] + scratch,
        compiler_params=_params(("arbitrary",), VMEM_LIMIT),
        name="peer_u",
    )(idx_t, h3, gate, _group_matrix(nsel), tab)


def _peer_v_kernel(idx_hbm, act_ref, rep_ref, tab_ref, y_ref, wide_ref, *scratch):
    tq, nsel = act_ref.shape
    stages, sems, bufs = scratch[:GATHER_STAGES], scratch[GATHER_STAGES], scratch[GATHER_STAGES + 1:]
    diag = _diag_mask(nsel)
    a_hi, a_lo = _split_bf16(act_ref[...])
    rep = rep_ref[...]
    wide_ref[...] = (jnp.dot(a_hi, rep, preferred_element_type=F32)
                     + jnp.dot(a_lo, rep, preferred_element_type=F32))

    def compute(t, stage_ref):
        w = jnp.where(diag, jnp.broadcast_to(wide_ref[pl.ds(t, 1), :], diag.shape), 0.0)
        w_hi, w_lo = _split_bf16(w)
        both = jnp.dot(jnp.concatenate([w_hi, w_lo], axis=0), _staged_bf16(stage_ref),
                       preferred_element_type=F32)
        y_ref[t] = both[0:SUBLANES] + both[SUBLANES:2 * SUBLANES]

    def run(slots):
        _pipelined_tokens(tq, functools.partial(_gather_rows, slots, tab_ref), compute, stages)

    _with_slot_indices(idx_hbm, sems, bufs, tq, run)


def _peer_v(idx_t, act, tab, tq):
    nsel, n = idx_t.shape
    wide = nsel * SUBLANES
    return pl.pallas_call(
        _peer_v_kernel,
        grid=(n // tq,),
        in_specs=[pl.BlockSpec(memory_space=pl.ANY),
                  pl.BlockSpec((tq, nsel), lambda i: (i, 0)),
                  pl.BlockSpec((nsel, wide), lambda i: (0, 0)),
                  pl.BlockSpec(memory_space=pltpu.VMEM)],
        out_specs=pl.BlockSpec((tq, SUBLANES, LANES), lambda i: (i, 0, 0)),
        out_shape=jax.ShapeDtypeStruct((n, SUBLANES, LANES), F32),
        scratch_shapes=[pltpu.VMEM((tq, wide), F32)] + _gather_scratch(nsel, tq),
        compiler_params=_params(("arbitrary",), VMEM_LIMIT),
        name="peer_v",
    )(idx_t, act, _group_matrix(nsel).T, tab)


def _final_kernel(x1_ref, y_ref, g2_ref, gpf_ref, o_ref):
    o_ref[...] = x1_ref[...] + g2_ref[0] * _rms(y_ref[...], gpf_ref[...])


def _final(x1, y, gate2, g_post_ffn, seq, tm):
    n, d = x1.shape
    per_b = seq // tm
    row = lambda i: (i, 0)
    return pl.pallas_call(
        _final_kernel,
        grid=(n // tm,),
        in_specs=[pl.BlockSpec((tm, d), row), pl.BlockSpec((tm, d), row),
                  pl.BlockSpec((1, 1, d), lambda i: (i // per_b, 0, 0)),
                  pl.BlockSpec((1, d), lambda i: (0, 0))],
        out_specs=pl.BlockSpec((tm, d), row),
        out_shape=jax.ShapeDtypeStruct((n, d), F32),
        compiler_params=_params(("arbitrary",)),
        name="final",
    )(x1, y, gate2, g_post_ffn)


def _dup_heads(w, heads, dh):
    d = w.shape[0]
    return jnp.repeat(w.reshape(d, heads, 1, dh), 2, axis=2).reshape(d, heads * 2 * dh)


def _layer(x2, c, pos_col, bsz, seq, w_mod, b_mod, g_pre_mix, g_post_mix, w_in, conv_w, b_igate, b_fgate,
           mlstm_norm_g, att_sinks, w_out, g_pre_ffn, g_post_ffn, peer_wq, peer_keys1, peer_keys2, peer_u, peer_v):
    n, d = x2.shape
    tm = min(seq, 512)
    mod = _mod(c, w_mod, b_mod)
    shift1, scale1, gate1, shift2, scale2, gate2 = [m.reshape(bsz, 1, d) for m in jnp.split(mod, 6, axis=-1)]

    aw = ATT_HEADS * ATT_HEAD_DIM
    kvw = ATT_KV_HEADS * ATT_HEAD_DIM
    qkw = MLSTM_HEADS * MLSTM_QK_DIM
    mw = MLSTM_HEADS * MLSTM_V_DIM
    o = 0
    wq_a = w_in[:, o:o + aw]; o += aw
    wk_a = w_in[:, o:o + kvw]; o += kvw
    wv_a = w_in[:, o:o + kvw]; o += kvw
    w_mqk = w_in[:, o:o + 2 * qkw]; o += 2 * qkw
    w_mv = w_in[:, o:o + mw]; o += mw
    w_g = w_in[:, o:o + 2 * MLSTM_HEADS]; o += 2 * MLSTM_HEADS
    w_mo = w_in[:, o:o + mw]
    w_gp = jnp.pad(w_g, ((0, 0), (0, LANES - 2 * MLSTM_HEADS)))
    w_all = jnp.concatenate([wq_a, _dup_heads(wk_a, ATT_KV_HEADS, ATT_HEAD_DIM),
                             _dup_heads(wv_a, ATT_KV_HEADS, ATT_HEAD_DIM), w_mqk, w_mv, w_mo, w_gp],
                            axis=1).astype(BF16)

    cos, sin = _rope_tab(pos_col)
    q, kd, vd, mqk, mv, mo, gts = _in_proj(x2, scale1, shift1, g_pre_mix.reshape(1, d), cos, sin, w_all, seq, tm)
    att = _swa(att_sinks, q, kd, vd, bsz, seq)
    gate_bias = jnp.pad(jnp.concatenate([b_igate, b_fgate]), (0, LANES - 2 * MLSTM_HEADS)).reshape(1, LANES)
    chunks = min(seq // MLSTM_CHUNK, 8)
    mh = _mlstm(mqk, mv, mo, gts, conv_w, gate_bias, mlstm_norm_g.reshape(1, mw), bsz, seq, chunks)
    wo = w_out.astype(BF16)
    x1, h2 = _out_proj(att, mh, x2, gate1, scale2, shift2, g_post_mix.reshape(1, d), g_pre_ffn.reshape(1, d),
                       wo[:aw], wo[aw:], seq, tm)

    tq = min(n, 256)
    eid_t, gate_t = _peer_sel(h2, peer_wq.astype(BF16), peer_keys1.astype(BF16), peer_keys2.astype(BF16), tq)
    act = _peer_u(eid_t, h2.reshape(n, SUBLANES, LANES), gate_t.T, _pack_table(peer_u), tq)
    y = _peer_v(eid_t, act, _pack_table(peer_v), tq).reshape(n, d)
    return _final(x1, y, gate2, g_post_ffn.reshape(1, d), seq, tm)


def kernel(x, c, positions, w_mod, b_mod, g_pre_mix, g_post_mix, w_in, conv_w, b_igate, b_fgate, mlstm_norm_g, att_sinks, w_out, g_pre_ffn, g_post_ffn, peer_wq, peer_keys1, peer_keys2, peer_u, peer_v):
    bsz, seq, d = x.shape
    n = bsz * seq
    x2 = x.reshape(n, d)
    pos_col = positions.reshape(n, 1)
    for l in range(w_mod.shape[0]):
        x2 = _layer(x2, c, pos_col, bsz, seq, w_mod[l], b_mod[l], g_pre_mix[l], g_post_mix[l], w_in[l], conv_w[l],
                    b_igate[l], b_fgate[l], mlstm_norm_g[l], att_sinks[l], w_out[l], g_pre_ffn[l], g_post_ffn[l],
                    peer_wq[l], peer_keys1[l], peer_keys2[l], peer_u[l], peer_v[l])
    return x2.reshape(bsz, seq, d)
```

```python
import functools

import jax
import jax.numpy as jnp
from jax import lax
from jax.experimental import pallas as pl
from jax.experimental.pallas import tpu as pltpu

F32 = jnp.float32
BF16 = jnp.bfloat16

ATT_HEADS = 8
ATT_KV_HEADS = 2
ATT_HEAD_DIM = 64
ATT_BLOCK = 128
ROPE_THETA = 10000.0
MLSTM_HEADS = 4
MLSTM_V_DIM = 128
MLSTM_QK_DIM = 64
MLSTM_CHUNK = 64
CONV_WIDTH = 4
PEER_HEADS = 8
PEER_KEYS = 128
PEER_HALF = 128
PEER_TOPK = 16
NORM_EPS = 1e-6

LANES = 128
SUBLANES = 8
VMEM_LIMIT = 52 * 1024 * 1024
GATHER_STAGES = 16
ROW_WORDS = SUBLANES // 2
MLSTM_GROUP = 1
GATHER_TOKENS = 512
GATHER_VMEM_LIMIT = 58 * 1024 * 1024

NEG_INF = float("-inf")
NT_DIMS = (((1,), (1,)), ((), ()))
TN_DIMS = (((0,), (0,)), ((), ()))


def _params(sem, vmem=None):
    return pltpu.CompilerParams(dimension_semantics=sem, vmem_limit_bytes=vmem)


def _rms(x, g):
    return x * lax.rsqrt(jnp.mean(x * x, axis=-1, keepdims=True) + NORM_EPS) * g


def _mod_kernel(c_ref, w_ref, b_ref, o_ref):
    o_ref[...] = jnp.dot(c_ref[...], w_ref[...], preferred_element_type=F32,
                         precision=lax.Precision.HIGHEST) + b_ref[...]


def _mod(c, w, b):
    bsz, d = c.shape
    nout = w.shape[1]
    return pl.pallas_call(
        _mod_kernel,
        grid=(nout // d,),
        in_specs=[pl.BlockSpec((bsz, d), lambda i: (0, 0)),
                  pl.BlockSpec((d, d), lambda i: (0, i)),
                  pl.BlockSpec((1, d), lambda i: (0, i))],
        out_specs=pl.BlockSpec((bsz, d), lambda i: (0, i)),
        out_shape=jax.ShapeDtypeStruct((bsz, nout), F32),
        compiler_params=_params(("arbitrary",)),
        name="mod",
    )(c, w, b.reshape(1, nout))


def _rope_tab_kernel(pos_ref, inv_ref, sign_ref, cos_ref, sin_ref):
    ang = pos_ref[...].astype(F32) * inv_ref[...]
    cos_ref[...] = jnp.cos(ang)
    sin_ref[...] = jnp.sin(ang) * sign_ref[...]


def _rope_tab(pos_col):
    n = pos_col.shape[0]
    tr = min(n, 1024)
    half = ATT_HEAD_DIM // 2
    inv = ROPE_THETA ** (-jnp.arange(0, ATT_HEAD_DIM, 2, dtype=F32) / ATT_HEAD_DIM)
    inv_row = jnp.tile(inv, LANES // half).reshape(1, LANES)
    lane = jnp.arange(LANES)
    sign_row = jnp.where((lane % ATT_HEAD_DIM) < half, -1.0, 1.0).astype(F32).reshape(1, LANES)
    return pl.pallas_call(
        _rope_tab_kernel,
        grid=(n // tr,),
        in_specs=[pl.BlockSpec((tr, 1), lambda i: (i, 0)),
                  pl.BlockSpec((1, LANES), lambda i: (0, 0)),
                  pl.BlockSpec((1, LANES), lambda i: (0, 0))],
        out_specs=[pl.BlockSpec((tr, LANES), lambda i: (i, 0))] * 2,
        out_shape=[jax.ShapeDtypeStruct((n, LANES), F32)] * 2,
        compiler_params=_params(("arbitrary",)),
        name="rope_tab",
    )(pos_col, inv_row, sign_row)


def _rope(v, cos, sin):
    half = ATT_HEAD_DIM // 2
    lane = lax.broadcasted_iota(jnp.int32, cos.shape, 1)
    first = (lane % ATT_HEAD_DIM) < half
    outs = []
    for j in range(v.shape[1] // LANES):
        c = v[:, j * LANES:(j + 1) * LANES]
        rot = jnp.where(first, pltpu.roll(c, LANES - half, 1), pltpu.roll(c, half, 1))
        outs.append(c * cos + rot * sin)
    return jnp.concatenate(outs, axis=1)


_C_Q, _C_K, _C_V, _C_MQK, _C_MV, _C_MO, _C_G, _C_END = 0, 512, 768, 1024, 1536, 2048, 2560, 2688


def _in_proj_kernel(x_ref, sc_ref, sh_ref, g_ref, cos_ref, sin_ref, w_ref,
                    q_ref, k_ref, v_ref, mqk_ref, mv_ref, mo_ref, gt_ref):
    x = x_ref[...]
    h = _rms(x, g_ref[...]) * (1.0 + sc_ref[0]) + sh_ref[0]
    hb = h.astype(BF16)

    def mm(a, b):
        return jnp.dot(hb, w_ref[:, a:b], preferred_element_type=F32)

    cos = cos_ref[...]
    sin = sin_ref[...]
    q_ref[...] = (_rope(mm(_C_Q, _C_K), cos, sin) * (ATT_HEAD_DIM ** -0.5)).astype(BF16)
    k_ref[...] = _rope(mm(_C_K, _C_V), cos, sin).astype(BF16)
    v_ref[...] = mm(_C_V, _C_MQK).astype(BF16)
    mqk_ref[...] = mm(_C_MQK, _C_MV)
    mv_ref[...] = mm(_C_MV, _C_MO).astype(BF16)
    mo_ref[...] = mm(_C_MO, _C_G)
    gt_ref[...] = mm(_C_G, _C_END)


def _in_proj(x2, scale1, shift1, g_pre, cos, sin, w_all, seq, tm):
    n, d = x2.shape
    per_b = seq // tm
    row = lambda i: (i, 0)
    bsel = lambda i: (i // per_b, 0, 0)
    widths = (512, 256, 256, 512, 512, 512, 128)
    dtypes = (BF16, BF16, BF16, F32, BF16, F32, F32)
    return pl.pallas_call(
        _in_proj_kernel,
        grid=(n // tm,),
        in_specs=[pl.BlockSpec((tm, d), row),
                  pl.BlockSpec((1, 1, d), bsel),
                  pl.BlockSpec((1, 1, d), bsel),
                  pl.BlockSpec((1, d), lambda i: (0, 0)),
                  pl.BlockSpec((tm, LANES), row),
                  pl.BlockSpec((tm, LANES), row),
                  pl.BlockSpec((d, _C_END), lambda i: (0, 0))],
        out_specs=[pl.BlockSpec((tm, w), row) for w in widths],
        out_shape=[jax.ShapeDtypeStruct((n, w), dt) for w, dt in zip(widths, dtypes)],
        compiler_params=_params(("arbitrary",), VMEM_LIMIT),
        name="in_proj",
    )(x2, scale1, shift1, g_pre, cos, sin, w_all)


def _swa_kernel(sink_ref, q_ref, kp_ref, kc_ref, vp_ref, vc_ref, o_ref):
    blk = ATT_BLOCK
    n = pl.program_id(1)
    qi = lax.broadcasted_iota(jnp.int32, (blk, 2 * blk), 0)
    si = lax.broadcasted_iota(jnp.int32, (blk, 2 * blk), 1)
    delta = qi + blk - si
    valid = (delta >= 0) & (delta < blk) & ((si >= blk) | (n > 0))
    lo = lax.broadcasted_iota(jnp.int32, (2 * blk, LANES), 1) < ATT_HEAD_DIM
    group = ATT_HEADS // ATT_KV_HEADS
    for g in range(ATT_KV_HEADS):
        cs = slice(g * LANES, (g + 1) * LANES)
        k = jnp.concatenate([kp_ref[:, cs], kc_ref[:, cs]], axis=0)
        v = jnp.concatenate([vp_ref[:, cs], vc_ref[:, cs]], axis=0)
        zero = jnp.zeros_like(k)
        halves = ((jnp.where(lo, k, zero), jnp.where(lo, v, zero)),
                  (jnp.where(lo, zero, k), jnp.where(lo, zero, v)))
        for jj in range(group // 2):
            p = g * (group // 2) + jj
            q2 = q_ref[:, p * LANES:(p + 1) * LANES]
            acc = jnp.zeros((blk, LANES), F32)
            for half, (kh, vh) in enumerate(halves):
                s = lax.dot_general(q2, kh, NT_DIMS, preferred_element_type=F32)
                s = jnp.where(valid, s, NEG_INF)
                sink = sink_ref[2 * p + half]
                m = jnp.maximum(jnp.max(s, axis=-1, keepdims=True), sink)
                e = jnp.exp(s - m)
                den = jnp.sum(e, axis=-1, keepdims=True) + jnp.exp(sink - m)
                acc = acc + jnp.dot((e / den).astype(BF16), vh, preferred_element_type=F32)
            o_ref[:, p * LANES:(p + 1) * LANES] = acc.astype(BF16)


def _swa(sinks, q, kd, vd, bsz, seq):
    n = q.shape[0]
    nb = seq // ATT_BLOCK
    cur = lambda b, i: (b * nb + i, 0)
    prev = lambda b, i: (b * nb + jnp.maximum(i - 1, 0), 0)
    return pl.pallas_call(
        _swa_kernel,
        grid=(bsz, nb),
        in_specs=[pl.BlockSpec(memory_space=pltpu.SMEM),
                  pl.BlockSpec((ATT_BLOCK, 512), cur),
                  pl.BlockSpec((ATT_BLOCK, 256), prev),
                  pl.BlockSpec((ATT_BLOCK, 256), cur),
                  pl.BlockSpec((ATT_BLOCK, 256), prev),
                  pl.BlockSpec((ATT_BLOCK, 256), cur)],
        out_specs=pl.BlockSpec((ATT_BLOCK, 512), cur),
        out_shape=jax.ShapeDtypeStruct((n, 512), BF16),
        compiler_params=_params(("arbitrary", "arbitrary")),
        name="swa",
    )(sinks, q, kd, kd, vd, vd)


def _mlstm_kernel(mqk_all, mv_all, mo_all, gt_all, cw_ref, gb_ref, ng_ref, o_all,
                  tail_all, qk_all, xs_all, ct_all, n_all, m_all, *, chunks, group):
    @pl.when(pl.program_id(1) == 0)
    def _():
        for ref in (tail_all, ct_all, n_all, m_all):
            ref[...] = jnp.zeros_like(ref)

    bodies = [_mlstm_sequence(*(r.at[g] for r in (mqk_all, mv_all, mo_all, gt_all, o_all, tail_all, qk_all,
                                                   xs_all, ct_all, n_all, m_all)),
                              cw_ref, gb_ref, ng_ref, chunks) for g in range(group)]

    def chunk(c, carry):
        for body in bodies:
            body(c, carry)
        return carry

    lax.fori_loop(0, chunks, chunk, 0)


def _mlstm_sequence(mqk_ref, mv_ref, mo_ref, gt_ref, o_ref, tail_ref, qk_ref, xs_ref, ct_ref, n_ref, m_ref,
                    cw_ref, gb_ref, ng_ref, chunks):
    L = MLSTM_CHUNK
    tm = chunks * L
    nqk = MLSTM_HEADS * MLSTM_QK_DIM

    cur = mqk_ref[...]
    full = jnp.concatenate([tail_ref[...], cur], axis=0)
    off = SUBLANES - (CONV_WIDTH - 1)
    acc = full[off:off + tm] * cw_ref[0:1, :]
    for j in range(1, CONV_WIDTH):
        acc = acc + full[off + j:off + j + tm] * cw_ref[j:j + 1, :]
    act = acc * jax.nn.sigmoid(acc)
    col = lax.broadcasted_iota(jnp.int32, (1, 2 * nqk), 1)
    act = act * jnp.where(col < nqk, MLSTM_QK_DIM ** -0.5, 1.0)
    qk_ref[...] = act.astype(BF16)
    tail_ref[...] = cur[tm - SUBLANES:tm]

    lane = lax.broadcasted_iota(jnp.int32, (tm, LANES), 1)
    gts = gt_ref[...] + gb_ref[...]
    logsig = jnp.minimum(gts, 0.0) - jnp.log(1.0 + jnp.exp(-jnp.abs(gts)))
    xs_ref[...] = jnp.where(lane < MLSTM_HEADS, gts, jnp.where(lane < 2 * MLSTM_HEADS, logsig, 0.0))

    ri = lax.broadcasted_iota(jnp.int32, (L, L), 0)
    ci = lax.broadcasted_iota(jnp.int32, (L, L), 1)
    causal = ci <= ri
    tril = causal.astype(F32)
    lane_l = lax.broadcasted_iota(jnp.int32, (L, LANES), 1)
    lo_l = lane_l < MLSTM_QK_DIM
    row_c = lax.broadcasted_iota(jnp.int32, (LANES, 1), 0) < MLSTM_QK_DIM
    lane_1 = lax.broadcasted_iota(jnp.int32, (1, LANES), 1) < MLSTM_QK_DIM

    def chunk(c, carry):
        r0 = pl.multiple_of(c * L, L)
        rows = pl.ds(r0, L)
        xc = xs_ref[rows, :]
        bc = jnp.dot(tril, xc, preferred_element_type=F32, precision=lax.Precision.HIGHEST)
        x2 = jnp.where(lane_l < MLSTM_HEADS, xc, bc)
        xt = x2.T
        for p in range(MLSTM_HEADS // 2):
            q2 = qk_ref[rows, p * LANES:(p + 1) * LANES]
            k2 = qk_ref[rows, nqk + p * LANES:nqk + (p + 1) * LANES]
            ct_old = ct_ref[p]
            ctb = ct_old.astype(BF16)
            n2 = n_ref[p:p + 1, :]
            decs, upds, kws = [], [], []
            for half in range(2):
                h = 2 * p + half
                hm = lo_l if half == 0 else jnp.logical_not(lo_l)
                zero = jnp.zeros_like(q2)
                qm = jnp.where(hm, q2, zero)
                km = jnp.where(hm, k2, zero)
                v = mv_ref[rows, h * LANES:(h + 1) * LANES]
                b_col = x2[:, MLSTM_HEADS + h:MLSTM_HEADS + h + 1]
                ig_col = x2[:, h:h + 1]
                b_row = xt[MLSTM_HEADS + h:MLSTM_HEADS + h + 1, :]
                ig_row = xt[h:h + 1, :]
                m_prev = m_ref[h:h + 1, 0:1]
                dlog = jnp.where(causal, b_col - b_row + ig_row, NEG_INF)
                m_inter = b_col + m_prev
                m_t = jnp.maximum(m_inter, jnp.max(dlog, axis=-1, keepdims=True))
                w_intra = jnp.exp(dlog - m_t)
                a_inter = jnp.exp(m_inter - m_t)
                s = lax.dot_general(q2, km, NT_DIMS, preferred_element_type=F32) * w_intra
                num = (jnp.dot(s.astype(BF16), v, preferred_element_type=F32)
                       + a_inter * jnp.dot(qm, ctb, preferred_element_type=F32))
                den = (jnp.sum(s, axis=-1, keepdims=True)
                       + a_inter * jnp.sum(qm.astype(F32) * n2, axis=-1, keepdims=True))
                hh = num / jnp.maximum(jnp.abs(den), jnp.exp(-m_t))
                y = _rms(hh, ng_ref[:, h * LANES:(h + 1) * LANES])
                y = y * jax.nn.sigmoid(mo_ref[rows, h * LANES:(h + 1) * LANES])
                o_ref[rows, h * LANES:(h + 1) * LANES] = y.astype(BF16)
                b_last = xt[MLSTM_HEADS + h:MLSTM_HEADS + h + 1, L - 1:L]
                g_col = b_last - b_col + ig_col
                m_new = jnp.maximum(b_last + m_prev, jnp.max(g_col, axis=0, keepdims=True))
                kw = km.astype(F32) * jnp.exp(g_col - m_new)
                decs.append(jnp.exp(b_last + m_prev - m_new))
                kws.append(kw)
                upds.append(lax.dot_general(kw.astype(BF16), v, TN_DIMS, preferred_element_type=F32))
                m_ref[h:h + 1, :] = jnp.broadcast_to(m_new, (1, LANES))
            ct_ref[p] = ct_old * jnp.where(row_c, decs[0], decs[1]) + upds[0] + upds[1]
            n_ref[p:p + 1, :] = (n2 * jnp.where(lane_1, decs[0], decs[1])
                                 + jnp.sum(kws[0] + kws[1], axis=0, keepdims=True))
        return carry

    return chunk


def _mlstm(mqk, mv, mo, gts, conv_w, gate_bias, norm_g, bsz, seq, chunks, group):
    n = mqk.shape[0]
    tm = chunks * MLSTM_CHUNK
    steps = seq // tm
    row = lambda b, i: (b, i, 0)
    const = lambda b, i: (0, 0)
    width = MLSTM_HEADS * MLSTM_V_DIM
    per_seq = lambda a: a.reshape(bsz, seq, a.shape[1])
    out = pl.pallas_call(
        functools.partial(_mlstm_kernel, chunks=chunks, group=group),
        grid=(bsz // group, steps),
        in_specs=[pl.BlockSpec((group, tm, width), row),
                  pl.BlockSpec((group, tm, width), row),
                  pl.BlockSpec((group, tm, width), row),
                  pl.BlockSpec((group, tm, LANES), row),
                  pl.BlockSpec((CONV_WIDTH, width), const),
                  pl.BlockSpec((1, LANES), const),
                  pl.BlockSpec((1, width), const)],
        out_specs=pl.BlockSpec((group, tm, width), row),
        out_shape=jax.ShapeDtypeStruct((bsz, seq, width), BF16),
        scratch_shapes=[pltpu.VMEM((group, SUBLANES, width), F32),
                        pltpu.VMEM((group, tm, width), BF16),
                        pltpu.VMEM((group, tm, LANES), F32),
                        pltpu.VMEM((group, MLSTM_HEADS // 2, LANES, LANES), F32),
                        pltpu.VMEM((group, SUBLANES, LANES), F32),
                        pltpu.VMEM((group, SUBLANES, LANES), F32)],
        compiler_params=_params(("arbitrary", "arbitrary")),
        name="mlstm",
    )(per_seq(mqk), per_seq(mv), per_seq(mo), per_seq(gts), conv_w, gate_bias, norm_g)
    return out.reshape(n, width)


def _out_proj_kernel(att_ref, mh_ref, x_ref, g1_ref, sc_ref, sh_ref, gpm_ref, gpf_ref, wa_ref, wb_ref,
                     x1_ref, h2_ref):
    mix = (jnp.dot(att_ref[...], wa_ref[...], preferred_element_type=F32)
           + jnp.dot(mh_ref[...], wb_ref[...], preferred_element_type=F32))
    x1 = x_ref[...] + g1_ref[0] * _rms(mix, gpm_ref[...])
    x1_ref[...] = x1
    h2_ref[...] = _rms(x1, gpf_ref[...]) * (1.0 + sc_ref[0]) + sh_ref[0]


def _out_proj(att, mh, x2, gate1, scale2, shift2, g_post_mix, g_pre_ffn, wa, wb, seq, tm):
    n, d = x2.shape
    per_b = seq // tm
    row = lambda i: (i, 0)
    bsel = lambda i: (i // per_b, 0, 0)
    const = lambda i: (0, 0)
    half = att.shape[1]
    return pl.pallas_call(
        _out_proj_kernel,
        grid=(n // tm,),
        in_specs=[pl.BlockSpec((tm, half), row), pl.BlockSpec((tm, half), row), pl.BlockSpec((tm, d), row),
                  pl.BlockSpec((1, 1, d), bsel), pl.BlockSpec((1, 1, d), bsel), pl.BlockSpec((1, 1, d), bsel),
                  pl.BlockSpec((1, d), const), pl.BlockSpec((1, d), const),
                  pl.BlockSpec((half, d), const), pl.BlockSpec((half, d), const)],
        out_specs=[pl.BlockSpec((tm, d), row)] * 2,
        out_shape=[jax.ShapeDtypeStruct((n, d), F32)] * 2,
        compiler_params=_params(("arbitrary",), VMEM_LIMIT),
        name="out_proj",
    )(att, mh, x2, gate1, scale2, shift2, g_post_mix, g_pre_ffn, wa, wb)


def _top_rows(s, ids, k):
    ids = ids.astype(F32)
    big = jnp.float32(2 ** 30)
    vals, sel = [], []
    for _ in range(k):
        m = jnp.max(s, axis=0, keepdims=True)
        i = jnp.min(jnp.where(s == m, ids, big), axis=0, keepdims=True)
        vals.append(m)
        sel.append(i)
        s = jnp.where(ids == i, NEG_INF, s)
    return jnp.concatenate(vals, axis=0), jnp.concatenate(sel, axis=0).astype(jnp.int32)


def _pick_rows(table, which):
    r = lax.broadcasted_iota(jnp.int32, table.shape, 0)
    rows = []
    for k in range(which.shape[0]):
        rows.append(jnp.sum(jnp.where(r == which[k:k + 1, :], table, 0), axis=0, keepdims=True))
    return jnp.concatenate(rows, axis=0)


def _peer_sel_kernel(h_ref, wq_ref, k1_ref, k2_ref, e_ref, g_ref):
    tq = h_ref.shape[0]
    K = PEER_TOPK
    q = jnp.dot(h_ref[...].astype(BF16), wq_ref[...], preferred_element_type=F32).astype(BF16)
    key_ids = lax.broadcasted_iota(jnp.int32, (PEER_KEYS, tq), 0)
    r16 = lax.broadcasted_iota(jnp.int32, (K, tq), 0)
    r8 = lax.broadcasted_iota(jnp.int32, (SUBLANES, tq), 0)
    flat = jnp.concatenate([r16 * K] + [r8 * K + b for b in range(1, SUBLANES)] + [r8 + SUBLANES], axis=0)
    for hd in range(PEER_HEADS):
        base = hd * 2 * PEER_HALF
        s1 = lax.dot_general(k1_ref[hd], q[:, base:base + PEER_HALF], NT_DIMS, preferred_element_type=F32)
        s2 = lax.dot_general(k2_ref[hd], q[:, base + PEER_HALF:base + 2 * PEER_HALF], NT_DIMS,
                             preferred_element_type=F32)
        v1, i1 = _top_rows(s1, key_ids, K)
        v2, i2 = _top_rows(s2, key_ids, K)
        cand = jnp.concatenate([v1 + v2[0:1, :]]
                               + [v1[0:SUBLANES] + v2[b:b + 1, :] for b in range(1, SUBLANES)]
                               + [v1[0:1, :] + v2[SUBLANES:K]], axis=0)
        top, pos = _top_rows(cand, flat, K)
        eid = _pick_rows(i1, pos >> 4) * PEER_KEYS + _pick_rows(i2, pos & (K - 1))
        ex = jnp.exp(top - top[0:1, :])
        e_ref[hd * K:(hd + 1) * K, :] = eid * ROW_WORDS
        g_ref[hd * K:(hd + 1) * K, :] = ex / jnp.sum(ex, axis=0, keepdims=True)


def _peer_sel(h2, wq, k1, k2, tq):
    n, d = h2.shape
    rows = PEER_HEADS * PEER_TOPK
    return pl.pallas_call(
        _peer_sel_kernel,
        grid=(n // tq,),
        in_specs=[pl.BlockSpec((tq, d), lambda i: (i, 0)),
                  pl.BlockSpec(wq.shape, lambda i: (0, 0)),
                  pl.BlockSpec(k1.shape, lambda i: (0, 0, 0)),
                  pl.BlockSpec(k2.shape, lambda i: (0, 0, 0))],
        out_specs=[pl.BlockSpec((rows, tq), lambda i: (0, i))] * 2,
        out_shape=[jax.ShapeDtypeStruct((rows, n), jnp.int32), jax.ShapeDtypeStruct((rows, n), F32)],
        compiler_params=_params(("arbitrary",), VMEM_LIMIT),
        name="peer_sel",
    )(h2, wq, k1, k2)


def _split_bf16(x):
    hi = x.astype(BF16)
    return hi, (x - hi.astype(F32)).astype(BF16)


def _pack_table(w):
    e = w.shape[0]
    t = w.astype(BF16).reshape(e, ROW_WORDS, 2, LANES).transpose(0, 1, 3, 2)
    return lax.bitcast_convert_type(t, jnp.int32).reshape(e * ROW_WORDS, LANES)


def _gather_rows(slots, tab_ref, t, stage_ref):
    for j, slot in enumerate(slots):
        src = pl.ds(pl.multiple_of(slot[t], ROW_WORDS), ROW_WORDS)
        stage_ref[j * ROW_WORDS:(j + 1) * ROW_WORDS, :] = tab_ref[src, :]


def _staged_bf16(stage_ref):
    return pltpu.bitcast(stage_ref[...], BF16)


def _pipelined_tokens(tq, gather, compute, stages):
    nb = len(stages)
    for k in range(nb):
        gather(k, stages[k])

    def trip(i, carry):
        t = nb * i
        for k in range(nb):
            compute(t + k, stages[k])
            ahead = t + k + nb
            gather(jnp.minimum(ahead, tq - 1), stages[k])
        return carry

    lax.fori_loop(0, tq // nb, trip, 0)


def _with_slot_indices(idx_hbm, sems, bufs, tq, run):
    nsel = len(bufs) // 2
    step = pl.program_id(0)
    last = pl.num_programs(0) - 1

    def copies(block, which):
        return [pltpu.make_async_copy(idx_hbm.at[j, pl.ds(block * tq, tq)], bufs[which * nsel + j], sems.at[which])
                for j in range(nsel)]

    @pl.when(step == 0)
    def _():
        for cp in copies(0, 0):
            cp.start()

    def phase(which):
        for cp in copies(step, which):
            cp.wait()

        @pl.when(step < last)
        def _():
            for cp in copies(step + 1, 1 - which):
                cp.start()

        run(bufs[which * nsel:(which + 1) * nsel])

    for which in range(2):
        pl.when(step % 2 == which)(functools.partial(phase, which))


def _diag_mask(nsel):
    shape = (SUBLANES, nsel * SUBLANES)
    return (lax.broadcasted_iota(jnp.int32, shape, 1) % SUBLANES) == lax.broadcasted_iota(jnp.int32, shape, 0)


def _token_tile(ref, t):
    row = ref[pl.ds(t, 1), :]
    return jnp.concatenate([row[:, s * LANES:(s + 1) * LANES] for s in range(SUBLANES)], axis=0)


def _peer_u_kernel(idx_hbm, h_ref, gate_ref, grp_ref, tab_ref, act_ref, part_ref, *scratch):
    nsel, tq = gate_ref.shape
    stages, sems, bufs = scratch[:GATHER_STAGES], scratch[GATHER_STAGES], scratch[GATHER_STAGES + 1:]
    diag = _diag_mask(nsel)

    def compute(t, stage_ref):
        h_hi, h_lo = _split_bf16(_token_tile(h_ref, t))
        both = lax.dot_general(jnp.concatenate([h_hi, h_lo], axis=0), _staged_bf16(stage_ref), NT_DIMS,
                               preferred_element_type=F32)
        prod = both[0:SUBLANES] + both[SUBLANES:2 * SUBLANES]
        part_ref[pl.ds(t, 1), :] = jnp.sum(jnp.where(diag, prod, 0.0), axis=0, keepdims=True)

    def run(slots):
        _pipelined_tokens(tq, functools.partial(_gather_rows, slots, tab_ref), compute, stages)

    _with_slot_indices(idx_hbm, sems, bufs, tq, run)
    p_hi, p_lo = _split_bf16(part_ref[...])
    grp = grp_ref[...]
    pre = jnp.dot(p_hi, grp, preferred_element_type=F32) + jnp.dot(p_lo, grp, preferred_element_type=F32)
    act_ref[...] = 0.5 * pre * (1.0 + lax.erf(pre * (2.0 ** -0.5))) * gate_ref[...].T


def _group_matrix(nsel):
    r = jnp.arange(nsel * SUBLANES)[:, None] // SUBLANES
    return (r == jnp.arange(nsel)[None, :]).astype(BF16)


def _gather_scratch(nsel, tq):
    return ([pltpu.VMEM((nsel * ROW_WORDS, LANES), jnp.int32)] * GATHER_STAGES
            + [pltpu.SemaphoreType.DMA((2,))] + [pltpu.SMEM((tq,), jnp.int32)] * (2 * nsel))


def _peer_u(idx_t, h2, gate_t, tab, tq):
    nsel, n = idx_t.shape
    d = h2.shape[1]
    wide = nsel * SUBLANES
    return pl.pallas_call(
        _peer_u_kernel,
        grid=(n // tq,),
        in_specs=[pl.BlockSpec(memory_space=pl.ANY),
                  pl.BlockSpec((tq, d), lambda i: (i, 0)),
                  pl.BlockSpec((nsel, tq), lambda i: (0, i)),
                  pl.BlockSpec((wide, nsel), lambda i: (0, 0)),
                  pl.BlockSpec(memory_space=pltpu.VMEM)],
        out_specs=pl.BlockSpec((tq, nsel), lambda i: (i, 0)),
        out_shape=jax.ShapeDtypeStruct((n, nsel), F32),
        scratch_shapes=[pltpu.VMEM((tq, wide), F32)] + _gather_scratch(nsel, tq),
        compiler_params=_params(("arbitrary",), GATHER_VMEM_LIMIT),
        name="peer_u",
    )(idx_t, h2, gate_t, _group_matrix(nsel), tab)


def _peer_v_kernel(idx_hbm, act_ref, rep_ref, x1_ref, g2_ref, gpf_ref, tab_ref, o_ref, wide_ref, y_ref, *scratch):
    tq, nsel = act_ref.shape
    stages, sems, bufs = scratch[:GATHER_STAGES], scratch[GATHER_STAGES], scratch[GATHER_STAGES + 1:]
    diag = _diag_mask(nsel)
    a_hi, a_lo = _split_bf16(act_ref[...])
    rep = rep_ref[...]
    wide_ref[...] = (jnp.dot(a_hi, rep, preferred_element_type=F32)
                     + jnp.dot(a_lo, rep, preferred_element_type=F32))

    def compute(t, stage_ref):
        w = jnp.where(diag, jnp.broadcast_to(wide_ref[pl.ds(t, 1), :], diag.shape), 0.0)
        w_hi, w_lo = _split_bf16(w)
        both = jnp.dot(jnp.concatenate([w_hi, w_lo], axis=0), _staged_bf16(stage_ref),
                       preferred_element_type=F32)
        tile = both[0:SUBLANES] + both[SUBLANES:2 * SUBLANES]
        y_ref[pl.ds(t, 1), :] = jnp.concatenate([tile[s:s + 1, :] for s in range(SUBLANES)], axis=1)

    def run(slots):
        _pipelined_tokens(tq, functools.partial(_gather_rows, slots, tab_ref), compute, stages)

    _with_slot_indices(idx_hbm, sems, bufs, tq, run)
    o_ref[...] = x1_ref[...] + g2_ref[0] * _rms(y_ref[...], gpf_ref[...])


def _peer_v(idx_t, act, x1, gate2, g_post_ffn, tab, seq, tq):
    nsel, n = idx_t.shape
    d = x1.shape[1]
    wide = nsel * SUBLANES
    per_b = seq // tq
    row = lambda i: (i, 0)
    return pl.pallas_call(
        _peer_v_kernel,
        grid=(n // tq,),
        in_specs=[pl.BlockSpec(memory_space=pl.ANY),
                  pl.BlockSpec((tq, nsel), row),
                  pl.BlockSpec((nsel, wide), lambda i: (0, 0)),
                  pl.BlockSpec((tq, d), row),
                  pl.BlockSpec((1, 1, d), lambda i: (i // per_b, 0, 0)),
                  pl.BlockSpec((1, d), lambda i: (0, 0)),
                  pl.BlockSpec(memory_space=pltpu.VMEM)],
        out_specs=pl.BlockSpec((tq, d), row),
        out_shape=jax.ShapeDtypeStruct((n, d), F32),
        scratch_shapes=[pltpu.VMEM((tq, wide), F32), pltpu.VMEM((tq, d), F32)] + _gather_scratch(nsel, tq),
        compiler_params=_params(("arbitrary",), GATHER_VMEM_LIMIT),
        name="peer_v",
    )(idx_t, act, _group_matrix(nsel).T, x1, gate2, g_post_ffn, tab)


def _dup_heads(w, heads, dh):
    d = w.shape[0]
    return jnp.repeat(w.reshape(d, heads, 1, dh), 2, axis=2).reshape(d, heads * 2 * dh)


def _layer(x2, c, pos_col, bsz, seq, w_mod, b_mod, g_pre_mix, g_post_mix, w_in, conv_w, b_igate, b_fgate,
           mlstm_norm_g, att_sinks, w_out, g_pre_ffn, g_post_ffn, peer_wq, peer_keys1, peer_keys2, peer_u, peer_v):
    n, d = x2.shape
    tm = min(seq, 512)
    mod = _mod(c, w_mod, b_mod)
    shift1, scale1, gate1, shift2, scale2, gate2 = [m.reshape(bsz, 1, d) for m in jnp.split(mod, 6, axis=-1)]

    aw = ATT_HEADS * ATT_HEAD_DIM
    kvw = ATT_KV_HEADS * ATT_HEAD_DIM
    qkw = MLSTM_HEADS * MLSTM_QK_DIM
    mw = MLSTM_HEADS * MLSTM_V_DIM
    o = 0
    wq_a = w_in[:, o:o + aw]; o += aw
    wk_a = w_in[:, o:o + kvw]; o += kvw
    wv_a = w_in[:, o:o + kvw]; o += kvw
    w_mqk = w_in[:, o:o + 2 * qkw]; o += 2 * qkw
    w_mv = w_in[:, o:o + mw]; o += mw
    w_g = w_in[:, o:o + 2 * MLSTM_HEADS]; o += 2 * MLSTM_HEADS
    w_mo = w_in[:, o:o + mw]
    w_gp = jnp.pad(w_g, ((0, 0), (0, LANES - 2 * MLSTM_HEADS)))
    w_all = jnp.concatenate([wq_a, _dup_heads(wk_a, ATT_KV_HEADS, ATT_HEAD_DIM),
                             _dup_heads(wv_a, ATT_KV_HEADS, ATT_HEAD_DIM), w_mqk, w_mv, w_mo, w_gp],
                            axis=1).astype(BF16)

    cos, sin = _rope_tab(pos_col)
    q, kd, vd, mqk, mv, mo, gts = _in_proj(x2, scale1, shift1, g_pre_mix.reshape(1, d), cos, sin, w_all, seq, tm)
    att = _swa(att_sinks, q, kd, vd, bsz, seq)
    gate_bias = jnp.pad(jnp.concatenate([b_igate, b_fgate]), (0, LANES - 2 * MLSTM_HEADS)).reshape(1, LANES)
    chunks = min(seq // MLSTM_CHUNK, 8)
    mh = _mlstm(mqk, mv, mo, gts, conv_w, gate_bias, mlstm_norm_g.reshape(1, mw), bsz, seq, chunks,
                MLSTM_GROUP if bsz % MLSTM_GROUP == 0 else 1)
    wo = w_out.astype(BF16)
    x1, h2 = _out_proj(att, mh, x2, gate1, scale2, shift2, g_post_mix.reshape(1, d), g_pre_ffn.reshape(1, d),
                       wo[:aw], wo[aw:], seq, tm)

    tq = min(n, 256)
    eid_t, gate_t = _peer_sel(h2, peer_wq.astype(BF16), peer_keys1.astype(BF16), peer_keys2.astype(BF16), tq)
    tg = min(seq, GATHER_TOKENS)
    act = _peer_u(eid_t, h2, gate_t, _pack_table(peer_u), tg)
    return _peer_v(eid_t, act, x1, gate2, g_post_ffn.reshape(1, d), _pack_table(peer_v), seq, tg)


def kernel(x, c, positions, w_mod, b_mod, g_pre_mix, g_post_mix, w_in, conv_w, b_igate, b_fgate, mlstm_norm_g, att_sinks, w_out, g_pre_ffn, g_post_ffn, peer_wq, peer_keys1, peer_keys2, peer_u, peer_v):
    bsz, seq, d = x.shape
    n = bsz * seq
    x2 = x.reshape(n, d)
    pos_col = positions.reshape(n, 1)
    for l in range(w_mod.shape[0]):
        x2 = _layer(x2, c, pos_col, bsz, seq, w_mod[l], b_mod[l], g_pre_mix[l], g_post_mix[l], w_in[l], conv_w[l],
                    b_igate[l], b_fgate[l], mlstm_norm_g[l], att_sinks[l], w_out[l], g_pre_ffn[l], g_post_ffn[l],
                    peer_wq[l], peer_keys1[l], peer_keys2[l], peer_u[l], peer_v[l])
    return x2.reshape(bsz, seq, d)
```

```python
import functools

import jax
import jax.numpy as jnp
from jax import lax
from jax.experimental import pallas as pl
from jax.experimental.pallas import tpu as pltpu

F32 = jnp.float32
BF16 = jnp.bfloat16

ATT_HEADS = 8
ATT_KV_HEADS = 2
ATT_HEAD_DIM = 64
ATT_BLOCK = 128
ROPE_THETA = 10000.0
MLSTM_HEADS = 4
MLSTM_V_DIM = 128
MLSTM_QK_DIM = 64
MLSTM_CHUNK = 64
CONV_WIDTH = 4
PEER_HEADS = 8
PEER_KEYS = 128
PEER_HALF = 128
PEER_TOPK = 16
NORM_EPS = 1e-6

LANES = 128
SUBLANES = 8
VMEM_LIMIT = 52 * 1024 * 1024
GATHER_STAGES = 16
ROW_WORDS = SUBLANES // 2
MLSTM_GROUP = 1
GATHER_TOKENS = 512
GATHER_VMEM_LIMIT = 58 * 1024 * 1024

NEG_INF = float("-inf")
NT_DIMS = (((1,), (1,)), ((), ()))
TN_DIMS = (((0,), (0,)), ((), ()))


def _params(sem, vmem=None):
    return pltpu.CompilerParams(dimension_semantics=sem, vmem_limit_bytes=vmem)


def _rms(x, g):
    return x * lax.rsqrt(jnp.mean(x * x, axis=-1, keepdims=True) + NORM_EPS) * g


def _mod_kernel(c_ref, w_ref, b_ref, o_ref):
    o_ref[...] = jnp.dot(c_ref[...], w_ref[...], preferred_element_type=F32,
                         precision=lax.Precision.HIGHEST) + b_ref[...]


def _mod(c, w, b):
    bsz, d = c.shape
    nout = w.shape[1]
    return pl.pallas_call(
        _mod_kernel,
        grid=(nout // d,),
        in_specs=[pl.BlockSpec((bsz, d), lambda i: (0, 0)),
                  pl.BlockSpec((d, d), lambda i: (0, i)),
                  pl.BlockSpec((1, d), lambda i: (0, i))],
        out_specs=pl.BlockSpec((bsz, d), lambda i: (0, i)),
        out_shape=jax.ShapeDtypeStruct((bsz, nout), F32),
        compiler_params=_params(("arbitrary",)),
        name="mod",
    )(c, w, b.reshape(1, nout))


def _rope_tab_kernel(pos_ref, inv_ref, sign_ref, cos_ref, sin_ref):
    ang = pos_ref[...].astype(F32) * inv_ref[...]
    cos_ref[...] = jnp.cos(ang)
    sin_ref[...] = jnp.sin(ang) * sign_ref[...]


def _rope_tab(pos_col):
    n = pos_col.shape[0]
    tr = min(n, 1024)
    half = ATT_HEAD_DIM // 2
    inv = ROPE_THETA ** (-jnp.arange(0, ATT_HEAD_DIM, 2, dtype=F32) / ATT_HEAD_DIM)
    inv_row = jnp.tile(inv, LANES // half).reshape(1, LANES)
    lane = jnp.arange(LANES)
    sign_row = jnp.where((lane % ATT_HEAD_DIM) < half, -1.0, 1.0).astype(F32).reshape(1, LANES)
    return pl.pallas_call(
        _rope_tab_kernel,
        grid=(n // tr,),
        in_specs=[pl.BlockSpec((tr, 1), lambda i: (i, 0)),
                  pl.BlockSpec((1, LANES), lambda i: (0, 0)),
                  pl.BlockSpec((1, LANES), lambda i: (0, 0))],
        out_specs=[pl.BlockSpec((tr, LANES), lambda i: (i, 0))] * 2,
        out_shape=[jax.ShapeDtypeStruct((n, LANES), F32)] * 2,
        compiler_params=_params(("arbitrary",)),
        name="rope_tab",
    )(pos_col, inv_row, sign_row)


def _rope(v, cos, sin):
    half = ATT_HEAD_DIM // 2
    lane = lax.broadcasted_iota(jnp.int32, cos.shape, 1)
    first = (lane % ATT_HEAD_DIM) < half
    outs = []
    for j in range(v.shape[1] // LANES):
        c = v[:, j * LANES:(j + 1) * LANES]
        rot = jnp.where(first, pltpu.roll(c, LANES - half, 1), pltpu.roll(c, half, 1))
        outs.append(c * cos + rot * sin)
    return jnp.concatenate(outs, axis=1)


_C_Q, _C_K, _C_V, _C_MQK, _C_MV, _C_MO, _C_G, _C_END = 0, 512, 768, 1024, 1536, 2048, 2560, 2688


def _in_proj_kernel(x_ref, sc_ref, sh_ref, g_ref, cos_ref, sin_ref, w_ref,
                    q_ref, k_ref, v_ref, mqk_ref, mv_ref, mo_ref, gt_ref):
    x = x_ref[...]
    h = _rms(x, g_ref[...]) * (1.0 + sc_ref[0]) + sh_ref[0]
    hb = h.astype(BF16)

    def mm(a, b):
        return jnp.dot(hb, w_ref[:, a:b], preferred_element_type=F32)

    cos = cos_ref[...]
    sin = sin_ref[...]
    q_ref[...] = (_rope(mm(_C_Q, _C_K), cos, sin) * (ATT_HEAD_DIM ** -0.5)).astype(BF16)
    k_ref[...] = _rope(mm(_C_K, _C_V), cos, sin).astype(BF16)
    v_ref[...] = mm(_C_V, _C_MQK).astype(BF16)
    mqk_ref[...] = mm(_C_MQK, _C_MV)
    mv_ref[...] = mm(_C_MV, _C_MO).astype(BF16)
    mo_ref[...] = mm(_C_MO, _C_G)
    gt_ref[...] = mm(_C_G, _C_END)


def _in_proj(x2, scale1, shift1, g_pre, cos, sin, w_all, seq, tm):
    n, d = x2.shape
    per_b = seq // tm
    row = lambda i: (i, 0)
    bsel = lambda i: (i // per_b, 0, 0)
    widths = (512, 256, 256, 512, 512, 512, 128)
    dtypes = (BF16, BF16, BF16, F32, BF16, F32, F32)
    return pl.pallas_call(
        _in_proj_kernel,
        grid=(n // tm,),
        in_specs=[pl.BlockSpec((tm, d), row),
                  pl.BlockSpec((1, 1, d), bsel),
                  pl.BlockSpec((1, 1, d), bsel),
                  pl.BlockSpec((1, d), lambda i: (0, 0)),
                  pl.BlockSpec((tm, LANES), row),
                  pl.BlockSpec((tm, LANES), row),
                  pl.BlockSpec((d, _C_END), lambda i: (0, 0))],
        out_specs=[pl.BlockSpec((tm, w), row) for w in widths],
        out_shape=[jax.ShapeDtypeStruct((n, w), dt) for w, dt in zip(widths, dtypes)],
        compiler_params=_params(("arbitrary",), VMEM_LIMIT),
        name="in_proj",
    )(x2, scale1, shift1, g_pre, cos, sin, w_all)


def _swa_kernel(sink_ref, q_ref, kp_ref, kc_ref, vp_ref, vc_ref, o_ref):
    blk = ATT_BLOCK
    n = pl.program_id(1)
    qi = lax.broadcasted_iota(jnp.int32, (blk, 2 * blk), 0)
    si = lax.broadcasted_iota(jnp.int32, (blk, 2 * blk), 1)
    delta = qi + blk - si
    valid = (delta >= 0) & (delta < blk) & ((si >= blk) | (n > 0))
    lo = lax.broadcasted_iota(jnp.int32, (2 * blk, LANES), 1) < ATT_HEAD_DIM
    group = ATT_HEADS // ATT_KV_HEADS
    for g in range(ATT_KV_HEADS):
        cs = slice(g * LANES, (g + 1) * LANES)
        k = jnp.concatenate([kp_ref[:, cs], kc_ref[:, cs]], axis=0)
        v = jnp.concatenate([vp_ref[:, cs], vc_ref[:, cs]], axis=0)
        zero = jnp.zeros_like(k)
        halves = ((jnp.where(lo, k, zero), jnp.where(lo, v, zero)),
                  (jnp.where(lo, zero, k), jnp.where(lo, zero, v)))
        for jj in range(group // 2):
            p = g * (group // 2) + jj
            q2 = q_ref[:, p * LANES:(p + 1) * LANES]
            acc = jnp.zeros((blk, LANES), F32)
            for half, (kh, vh) in enumerate(halves):
                s = lax.dot_general(q2, kh, NT_DIMS, preferred_element_type=F32)
                s = jnp.where(valid, s, NEG_INF)
                sink = sink_ref[2 * p + half]
                m = jnp.maximum(jnp.max(s, axis=-1, keepdims=True), sink)
                e = jnp.exp(s - m)
                den = jnp.sum(e, axis=-1, keepdims=True) + jnp.exp(sink - m)
                acc = acc + jnp.dot((e / den).astype(BF16), vh, preferred_element_type=F32)
            o_ref[:, p * LANES:(p + 1) * LANES] = acc.astype(BF16)


def _swa(sinks, q, kd, vd, bsz, seq):
    n = q.shape[0]
    nb = seq // ATT_BLOCK
    cur = lambda b, i: (b * nb + i, 0)
    prev = lambda b, i: (b * nb + jnp.maximum(i - 1, 0), 0)
    return pl.pallas_call(
        _swa_kernel,
        grid=(bsz, nb),
        in_specs=[pl.BlockSpec(memory_space=pltpu.SMEM),
                  pl.BlockSpec((ATT_BLOCK, 512), cur),
                  pl.BlockSpec((ATT_BLOCK, 256), prev),
                  pl.BlockSpec((ATT_BLOCK, 256), cur),
                  pl.BlockSpec((ATT_BLOCK, 256), prev),
                  pl.BlockSpec((ATT_BLOCK, 256), cur)],
        out_specs=pl.BlockSpec((ATT_BLOCK, 512), cur),
        out_shape=jax.ShapeDtypeStruct((n, 512), BF16),
        compiler_params=_params(("arbitrary", "arbitrary")),
        name="swa",
    )(sinks, q, kd, kd, vd, vd)


def _mlstm_kernel(mqk_all, mv_all, mo_all, gt_all, cw_ref, gb_ref, ng_ref, o_all,
                  tail_all, qk_all, xs_all, ct_all, n_all, m_all, *, chunks, group):
    @pl.when(pl.program_id(1) == 0)
    def _():
        for ref in (tail_all, ct_all, n_all, m_all):
            ref[...] = jnp.zeros_like(ref)

    bodies = [_mlstm_sequence(*(r.at[g] for r in (mqk_all, mv_all, mo_all, gt_all, o_all, tail_all, qk_all,
                                                   xs_all, ct_all, n_all, m_all)),
                              cw_ref, gb_ref, ng_ref, chunks) for g in range(group)]

    def chunk(c, carry):
        for body in bodies:
            body(c, carry)
        return carry

    lax.fori_loop(0, chunks, chunk, 0)


def _mlstm_sequence(mqk_ref, mv_ref, mo_ref, gt_ref, o_ref, tail_ref, qk_ref, xs_ref, ct_ref, n_ref, m_ref,
                    cw_ref, gb_ref, ng_ref, chunks):
    L = MLSTM_CHUNK
    tm = chunks * L
    nqk = MLSTM_HEADS * MLSTM_QK_DIM

    cur = mqk_ref[...]
    full = jnp.concatenate([tail_ref[...], cur], axis=0)
    off = SUBLANES - (CONV_WIDTH - 1)
    acc = full[off:off + tm] * cw_ref[0:1, :]
    for j in range(1, CONV_WIDTH):
        acc = acc + full[off + j:off + j + tm] * cw_ref[j:j + 1, :]
    act = acc * jax.nn.sigmoid(acc)
    col = lax.broadcasted_iota(jnp.int32, (1, 2 * nqk), 1)
    act = act * jnp.where(col < nqk, MLSTM_QK_DIM ** -0.5, 1.0)
    qk_ref[...] = act.astype(BF16)
    tail_ref[...] = cur[tm - SUBLANES:tm]

    lane = lax.broadcasted_iota(jnp.int32, (tm, LANES), 1)
    gts = gt_ref[...] + gb_ref[...]
    logsig = jnp.minimum(gts, 0.0) - jnp.log(1.0 + jnp.exp(-jnp.abs(gts)))
    xs_ref[...] = jnp.where(lane < MLSTM_HEADS, gts, jnp.where(lane < 2 * MLSTM_HEADS, logsig, 0.0))

    ri = lax.broadcasted_iota(jnp.int32, (L, L), 0)
    ci = lax.broadcasted_iota(jnp.int32, (L, L), 1)
    causal = ci <= ri
    tril = causal.astype(F32)
    lane_l = lax.broadcasted_iota(jnp.int32, (L, LANES), 1)
    lo_l = lane_l < MLSTM_QK_DIM
    row_c = lax.broadcasted_iota(jnp.int32, (LANES, 1), 0) < MLSTM_QK_DIM
    lane_1 = lax.broadcasted_iota(jnp.int32, (1, LANES), 1) < MLSTM_QK_DIM

    def chunk(c, carry):
        r0 = pl.multiple_of(c * L, L)
        rows = pl.ds(r0, L)
        xc = xs_ref[rows, :]
        bc = jnp.dot(tril, xc, preferred_element_type=F32, precision=lax.Precision.HIGHEST)
        x2 = jnp.where(lane_l < MLSTM_HEADS, xc, bc)
        xt = x2.T
        for p in range(MLSTM_HEADS // 2):
            q2 = qk_ref[rows, p * LANES:(p + 1) * LANES]
            k2 = qk_ref[rows, nqk + p * LANES:nqk + (p + 1) * LANES]
            ct_old = ct_ref[p]
            ctb = ct_old.astype(BF16)
            n2 = n_ref[p:p + 1, :]
            decs, upds, kws = [], [], []
            for half in range(2):
                h = 2 * p + half
                hm = lo_l if half == 0 else jnp.logical_not(lo_l)
                zero = jnp.zeros_like(q2)
                qm = jnp.where(hm, q2, zero)
                km = jnp.where(hm, k2, zero)
                v = mv_ref[rows, h * LANES:(h + 1) * LANES]
                b_col = x2[:, MLSTM_HEADS + h:MLSTM_HEADS + h + 1]
                ig_col = x2[:, h:h + 1]
                b_row = xt[MLSTM_HEADS + h:MLSTM_HEADS + h + 1, :]
                ig_row = xt[h:h + 1, :]
                m_prev = m_ref[h:h + 1, 0:1]
                dlog = jnp.where(causal, b_col - b_row + ig_row, NEG_INF)
                m_inter = b_col + m_prev
                m_t = jnp.maximum(m_inter, jnp.max(dlog, axis=-1, keepdims=True))
                w_intra = jnp.exp(dlog - m_t)
                a_inter = jnp.exp(m_inter - m_t)
                s = lax.dot_general(q2, km, NT_DIMS, preferred_element_type=F32) * w_intra
                num = (jnp.dot(s.astype(BF16), v, preferred_element_type=F32)
                       + a_inter * jnp.dot(qm, ctb, preferred_element_type=F32))
                den = (jnp.sum(s, axis=-1, keepdims=True)
                       + a_inter * jnp.sum(qm.astype(F32) * n2, axis=-1, keepdims=True))
                hh = num / jnp.maximum(jnp.abs(den), jnp.exp(-m_t))
                y = _rms(hh, ng_ref[:, h * LANES:(h + 1) * LANES])
                y = y * jax.nn.sigmoid(mo_ref[rows, h * LANES:(h + 1) * LANES])
                o_ref[rows, h * LANES:(h + 1) * LANES] = y.astype(BF16)
                b_last = xt[MLSTM_HEADS + h:MLSTM_HEADS + h + 1, L - 1:L]
                g_col = b_last - b_col + ig_col
                m_new = jnp.maximum(b_last + m_prev, jnp.max(g_col, axis=0, keepdims=True))
                kw = km.astype(F32) * jnp.exp(g_col - m_new)
                decs.append(jnp.exp(b_last + m_prev - m_new))
                kws.append(kw)
                upds.append(lax.dot_general(kw.astype(BF16), v, TN_DIMS, preferred_element_type=F32))
                m_ref[h:h + 1, :] = jnp.broadcast_to(m_new, (1, LANES))
            ct_ref[p] = ct_old * jnp.where(row_c, decs[0], decs[1]) + upds[0] + upds[1]
            n_ref[p:p + 1, :] = (n2 * jnp.where(lane_1, decs[0], decs[1])
                                 + jnp.sum(kws[0] + kws[1], axis=0, keepdims=True))
        return carry

    return chunk


def _mlstm(mqk, mv, mo, gts, conv_w, gate_bias, norm_g, bsz, seq, chunks, group):
    n = mqk.shape[0]
    tm = chunks * MLSTM_CHUNK
    steps = seq // tm
    row = lambda b, i: (b, i, 0)
    const = lambda b, i: (0, 0)
    width = MLSTM_HEADS * MLSTM_V_DIM
    per_seq = lambda a: a.reshape(bsz, seq, a.shape[1])
    out = pl.pallas_call(
        functools.partial(_mlstm_kernel, chunks=chunks, group=group),
        grid=(bsz // group, steps),
        in_specs=[pl.BlockSpec((group, tm, width), row),
                  pl.BlockSpec((group, tm, width), row),
                  pl.BlockSpec((group, tm, width), row),
                  pl.BlockSpec((group, tm, LANES), row),
                  pl.BlockSpec((CONV_WIDTH, width), const),
                  pl.BlockSpec((1, LANES), const),
                  pl.BlockSpec((1, width), const)],
        out_specs=pl.BlockSpec((group, tm, width), row),
        out_shape=jax.ShapeDtypeStruct((bsz, seq, width), BF16),
        scratch_shapes=[pltpu.VMEM((group, SUBLANES, width), F32),
                        pltpu.VMEM((group, tm, width), BF16),
                        pltpu.VMEM((group, tm, LANES), F32),
                        pltpu.VMEM((group, MLSTM_HEADS // 2, LANES, LANES), F32),
                        pltpu.VMEM((group, SUBLANES, LANES), F32),
                        pltpu.VMEM((group, SUBLANES, LANES), F32)],
        compiler_params=_params(("arbitrary", "arbitrary")),
        name="mlstm",
    )(per_seq(mqk), per_seq(mv), per_seq(mo), per_seq(gts), conv_w, gate_bias, norm_g)
    return out.reshape(n, width)


def _out_proj_kernel(att_ref, mh_ref, x_ref, g1_ref, sc_ref, sh_ref, gpm_ref, gpf_ref, wa_ref, wb_ref,
                     x1_ref, h2_ref):
    mix = (jnp.dot(att_ref[...], wa_ref[...], preferred_element_type=F32)
           + jnp.dot(mh_ref[...], wb_ref[...], preferred_element_type=F32))
    x1 = x_ref[...] + g1_ref[0] * _rms(mix, gpm_ref[...])
    x1_ref[...] = x1
    h2_ref[...] = _rms(x1, gpf_ref[...]) * (1.0 + sc_ref[0]) + sh_ref[0]


def _out_proj(att, mh, x2, gate1, scale2, shift2, g_post_mix, g_pre_ffn, wa, wb, seq, tm):
    n, d = x2.shape
    per_b = seq // tm
    row = lambda i: (i, 0)
    bsel = lambda i: (i // per_b, 0, 0)
    const = lambda i: (0, 0)
    half = att.shape[1]
    return pl.pallas_call(
        _out_proj_kernel,
        grid=(n // tm,),
        in_specs=[pl.BlockSpec((tm, half), row), pl.BlockSpec((tm, half), row), pl.BlockSpec((tm, d), row),
                  pl.BlockSpec((1, 1, d), bsel), pl.BlockSpec((1, 1, d), bsel), pl.BlockSpec((1, 1, d), bsel),
                  pl.BlockSpec((1, d), const), pl.BlockSpec((1, d), const),
                  pl.BlockSpec((half, d), const), pl.BlockSpec((half, d), const)],
        out_specs=[pl.BlockSpec((tm, d), row)] * 2,
        out_shape=[jax.ShapeDtypeStruct((n, d), F32)] * 2,
        compiler_params=_params(("arbitrary",), VMEM_LIMIT),
        name="out_proj",
    )(att, mh, x2, gate1, scale2, shift2, g_post_mix, g_pre_ffn, wa, wb)


_BIG_ID = float(2 ** 30)
SORT_LEVELS = 8


def _top_scores(s, k):
    rows, t = s.shape
    span = SORT_LEVELS * SUBLANES
    r = lax.broadcasted_iota(jnp.int32, (rows // SORT_LEVELS, t), 0)
    col_id = ((r // SUBLANES) * span + r % SUBLANES).astype(F32)
    lev = [jnp.concatenate([s[g * span + l * SUBLANES:g * span + (l + 1) * SUBLANES] for g in range(rows // span)],
                           axis=0) for l in range(SORT_LEVELS)]
    ids = [col_id + float(l * SUBLANES) for l in range(SORT_LEVELS)]
    for rnd in range(SORT_LEVELS):
        for a in range(rnd % 2, SORT_LEVELS - 1, 2):
            swap = lev[a + 1] > lev[a]
            lev[a], lev[a + 1] = jnp.where(swap, lev[a + 1], lev[a]), jnp.where(swap, lev[a], lev[a + 1])
            ids[a], ids[a + 1] = jnp.where(swap, ids[a + 1], ids[a]), jnp.where(swap, ids[a], ids[a + 1])
    vals, sel = [], []
    for _ in range(k):
        m = jnp.max(lev[0], axis=0, keepdims=True)
        i = jnp.min(jnp.where(lev[0] == m, ids[0], _BIG_ID), axis=0, keepdims=True)
        vals.append(m)
        sel.append(i)
        hit = ids[0] == i
        for l in range(SORT_LEVELS - 1):
            lev[l] = jnp.where(hit, lev[l + 1], lev[l])
            ids[l] = jnp.where(hit, ids[l + 1], ids[l])
        lev[-1] = jnp.where(hit, NEG_INF, lev[-1])
    return jnp.concatenate(vals, axis=0), jnp.concatenate(sel, axis=0).astype(jnp.int32)


def _top_pair_sums(v1, v2):
    k, t = v1.shape
    half = SUBLANES
    lev = [v1[0:half] + v2[b:b + 1, :] for b in range(k)]
    side = v1[half:k] + v2[0:1, :]
    a_low = lax.broadcasted_iota(jnp.int32, (half, t), 0).astype(F32) * float(k)
    side_id = a_low + float(half * k)
    depth = jnp.zeros((half, t), F32)
    vals, sel = [], []
    for it in range(k):
        top_id = a_low + depth
        m = jnp.max(jnp.maximum(lev[0], side), axis=0, keepdims=True)
        i = jnp.min(jnp.minimum(jnp.where(lev[0] == m, top_id, _BIG_ID), jnp.where(side == m, side_id, _BIG_ID)),
                    axis=0, keepdims=True)
        vals.append(m)
        sel.append(i)
        hit = top_id == i
        for l in range(k - 1 - it):
            lev[l] = jnp.where(hit, lev[l + 1], lev[l])
        side = jnp.where(side_id == i, NEG_INF, side)
        depth = depth + jnp.where(hit, 1.0, 0.0)
    return jnp.concatenate(vals, axis=0), jnp.concatenate(sel, axis=0).astype(jnp.int32)


def _pick_rows(table, which):
    r = lax.broadcasted_iota(jnp.int32, table.shape, 0)
    rows = []
    for k in range(which.shape[0]):
        rows.append(jnp.sum(jnp.where(r == which[k:k + 1, :], table, 0), axis=0, keepdims=True))
    return jnp.concatenate(rows, axis=0)


def _peer_sel_kernel(h_ref, wq_ref, k1_ref, k2_ref, e_ref, g_ref):
    tq = h_ref.shape[0]
    K = PEER_TOPK
    q = jnp.dot(h_ref[...].astype(BF16), wq_ref[...], preferred_element_type=F32).astype(BF16)
    for hd in range(PEER_HEADS):
        base = hd * 2 * PEER_HALF
        s1 = lax.dot_general(k1_ref[hd], q[:, base:base + PEER_HALF], NT_DIMS, preferred_element_type=F32)
        s2 = lax.dot_general(k2_ref[hd], q[:, base + PEER_HALF:base + 2 * PEER_HALF], NT_DIMS,
                             preferred_element_type=F32)
        v1, i1 = _top_scores(s1, K)
        v2, i2 = _top_scores(s2, K)
        top, pos = _top_pair_sums(v1, v2)
        eid = _pick_rows(i1, pos >> 4) * PEER_KEYS + _pick_rows(i2, pos & (K - 1))
        ex = jnp.exp(top - top[0:1, :])
        e_ref[hd * K:(hd + 1) * K, :] = eid * ROW_WORDS
        g_ref[hd * K:(hd + 1) * K, :] = ex / jnp.sum(ex, axis=0, keepdims=True)


def _peer_sel(h2, wq, k1, k2, tq):
    n, d = h2.shape
    rows = PEER_HEADS * PEER_TOPK
    return pl.pallas_call(
        _peer_sel_kernel,
        grid=(n // tq,),
        in_specs=[pl.BlockSpec((tq, d), lambda i: (i, 0)),
                  pl.BlockSpec(wq.shape, lambda i: (0, 0)),
                  pl.BlockSpec(k1.shape, lambda i: (0, 0, 0)),
                  pl.BlockSpec(k2.shape, lambda i: (0, 0, 0))],
        out_specs=[pl.BlockSpec((rows, tq), lambda i: (0, i))] * 2,
        out_shape=[jax.ShapeDtypeStruct((rows, n), jnp.int32), jax.ShapeDtypeStruct((rows, n), F32)],
        compiler_params=_params(("arbitrary",), VMEM_LIMIT),
        name="peer_sel",
    )(h2, wq, k1, k2)


def _split_bf16(x):
    hi = x.astype(BF16)
    return hi, (x - hi.astype(F32)).astype(BF16)


def _pack_kernel(w_ref, o_ref):
    x = w_ref[...]
    eb = x.shape[0]
    for r in range(ROW_WORDS):
        lo = x[:, 2 * r * LANES:(2 * r + 1) * LANES].astype(BF16).astype(F32)
        hi = x[:, (2 * r + 1) * LANES:(2 * r + 2) * LANES].astype(BF16).astype(F32)
        word = (lax.shift_right_logical(pltpu.bitcast(lo, jnp.int32), 16)
                | (pltpu.bitcast(hi, jnp.int32) & jnp.int32(-65536)))
        o_ref[pl.ds(r, eb, stride=ROW_WORDS), :] = word


def _pack_table(w):
    e, d = w.shape
    eb = 512
    return pl.pallas_call(
        _pack_kernel,
        grid=(e // eb,),
        in_specs=[pl.BlockSpec((eb, d), lambda i: (i, 0))],
        out_specs=pl.BlockSpec((eb * ROW_WORDS, LANES), lambda i: (i, 0)),
        out_shape=jax.ShapeDtypeStruct((e * ROW_WORDS, LANES), jnp.int32),
        compiler_params=_params(("arbitrary",)),
        name="pack_table",
    )(w)


def _gather_rows(slots, tab_ref, t, stage_ref):
    for j, slot in enumerate(slots):
        src = pl.ds(pl.multiple_of(slot[t], ROW_WORDS), ROW_WORDS)
        stage_ref[j * ROW_WORDS:(j + 1) * ROW_WORDS, :] = tab_ref[src, :]


def _staged_bf16(stage_ref):
    return pltpu.bitcast(stage_ref[...], BF16)


def _pipelined_tokens(tq, gather, compute, stages):
    nb = len(stages)
    for k in range(nb):
        gather(k, stages[k])

    def trip(i, carry):
        t = nb * i
        for k in range(nb):
            compute(t + k, stages[k])
            ahead = t + k + nb
            gather(jnp.minimum(ahead, tq - 1), stages[k])
        return carry

    lax.fori_loop(0, tq // nb, trip, 0)


def _with_slot_indices(idx_hbm, sems, bufs, tq, run):
    nsel = len(bufs) // 2
    step = pl.program_id(0)
    last = pl.num_programs(0) - 1

    def copies(block, which):
        return [pltpu.make_async_copy(idx_hbm.at[j, pl.ds(block * tq, tq)], bufs[which * nsel + j], sems.at[which])
                for j in range(nsel)]

    @pl.when(step == 0)
    def _():
        for cp in copies(0, 0):
            cp.start()

    def phase(which):
        for cp in copies(step, which):
            cp.wait()

        @pl.when(step < last)
        def _():
            for cp in copies(step + 1, 1 - which):
                cp.start()

        run(bufs[which * nsel:(which + 1) * nsel])

    for which in range(2):
        pl.when(step % 2 == which)(functools.partial(phase, which))


def _diag_mask(nsel):
    shape = (SUBLANES, nsel * SUBLANES)
    return (lax.broadcasted_iota(jnp.int32, shape, 1) % SUBLANES) == lax.broadcasted_iota(jnp.int32, shape, 0)


def _token_tile(ref, t):
    row = ref[pl.ds(t, 1), :]
    return jnp.concatenate([row[:, s * LANES:(s + 1) * LANES] for s in range(SUBLANES)], axis=0)


def _peer_u_kernel(idx_hbm, h_ref, gate_ref, grp_ref, tab_ref, act_ref, part_ref, *scratch):
    nsel, tq = gate_ref.shape
    stages, sems, bufs = scratch[:GATHER_STAGES], scratch[GATHER_STAGES], scratch[GATHER_STAGES + 1:]
    diag = _diag_mask(nsel)

    def compute(t, stage_ref):
        h_hi, h_lo = _split_bf16(_token_tile(h_ref, t))
        both = lax.dot_general(jnp.concatenate([h_hi, h_lo], axis=0), _staged_bf16(stage_ref), NT_DIMS,
                               preferred_element_type=F32)
        prod = both[0:SUBLANES] + both[SUBLANES:2 * SUBLANES]
        part_ref[pl.ds(t, 1), :] = jnp.sum(jnp.where(diag, prod, 0.0), axis=0, keepdims=True)

    def run(slots):
        _pipelined_tokens(tq, functools.partial(_gather_rows, slots, tab_ref), compute, stages)

    _with_slot_indices(idx_hbm, sems, bufs, tq, run)
    p_hi, p_lo = _split_bf16(part_ref[...])
    grp = grp_ref[...]
    pre = jnp.dot(p_hi, grp, preferred_element_type=F32) + jnp.dot(p_lo, grp, preferred_element_type=F32)
    act_ref[...] = 0.5 * pre * (1.0 + lax.erf(pre * (2.0 ** -0.5))) * gate_ref[...].T


def _group_matrix(nsel):
    r = jnp.arange(nsel * SUBLANES)[:, None] // SUBLANES
    return (r == jnp.arange(nsel)[None, :]).astype(BF16)


def _gather_scratch(nsel, tq):
    return ([pltpu.VMEM((nsel * ROW_WORDS, LANES), jnp.int32)] * GATHER_STAGES
            + [pltpu.SemaphoreType.DMA((2,))] + [pltpu.SMEM((tq,), jnp.int32)] * (2 * nsel))


def _peer_u(idx_t, h2, gate_t, tab, tq):
    nsel, n = idx_t.shape
    d = h2.shape[1]
    wide = nsel * SUBLANES
    return pl.pallas_call(
        _peer_u_kernel,
        grid=(n // tq,),
        in_specs=[pl.BlockSpec(memory_space=pl.ANY),
                  pl.BlockSpec((tq, d), lambda i: (i, 0)),
                  pl.BlockSpec((nsel, tq), lambda i: (0, i)),
                  pl.BlockSpec((wide, nsel), lambda i: (0, 0)),
                  pl.BlockSpec(memory_space=pltpu.VMEM)],
        out_specs=pl.BlockSpec((tq, nsel), lambda i: (i, 0)),
        out_shape=jax.ShapeDtypeStruct((n, nsel), F32),
        scratch_shapes=[pltpu.VMEM((tq, wide), F32)] + _gather_scratch(nsel, tq),
        compiler_params=_params(("arbitrary",), GATHER_VMEM_LIMIT),
        name="peer_u",
    )(idx_t, h2, gate_t, _group_matrix(nsel), tab)


def _peer_v_kernel(idx_hbm, act_ref, rep_ref, x1_ref, g2_ref, gpf_ref, tab_ref, o_ref, wide_ref, y_ref, *scratch):
    tq, nsel = act_ref.shape
    stages, sems, bufs = scratch[:GATHER_STAGES], scratch[GATHER_STAGES], scratch[GATHER_STAGES + 1:]
    diag = _diag_mask(nsel)
    a_hi, a_lo = _split_bf16(act_ref[...])
    rep = rep_ref[...]
    wide_ref[...] = (jnp.dot(a_hi, rep, preferred_element_type=F32)
                     + jnp.dot(a_lo, rep, preferred_element_type=F32))

    def compute(t, stage_ref):
        w = jnp.where(diag, jnp.broadcast_to(wide_ref[pl.ds(t, 1), :], diag.shape), 0.0)
        w_hi, w_lo = _split_bf16(w)
        both = jnp.dot(jnp.concatenate([w_hi, w_lo], axis=0), _staged_bf16(stage_ref),
                       preferred_element_type=F32)
        tile = both[0:SUBLANES] + both[SUBLANES:2 * SUBLANES]
        y_ref[pl.ds(t, 1), :] = jnp.concatenate([tile[s:s + 1, :] for s in range(SUBLANES)], axis=1)

    def run(slots):
        _pipelined_tokens(tq, functools.partial(_gather_rows, slots, tab_ref), compute, stages)

    _with_slot_indices(idx_hbm, sems, bufs, tq, run)
    o_ref[...] = x1_ref[...] + g2_ref[0] * _rms(y_ref[...], gpf_ref[...])


def _peer_v(idx_t, act, x1, gate2, g_post_ffn, tab, seq, tq):
    nsel, n = idx_t.shape
    d = x1.shape[1]
    wide = nsel * SUBLANES
    per_b = seq // tq
    row = lambda i: (i, 0)
    return pl.pallas_call(
        _peer_v_kernel,
        grid=(n // tq,),
        in_specs=[pl.BlockSpec(memory_space=pl.ANY),
                  pl.BlockSpec((tq, nsel), row),
                  pl.BlockSpec((nsel, wide), lambda i: (0, 0)),
                  pl.BlockSpec((tq, d), row),
                  pl.BlockSpec((1, 1, d), lambda i: (i // per_b, 0, 0)),
                  pl.BlockSpec((1, d), lambda i: (0, 0)),
                  pl.BlockSpec(memory_space=pltpu.VMEM)],
        out_specs=pl.BlockSpec((tq, d), row),
        out_shape=jax.ShapeDtypeStruct((n, d), F32),
        scratch_shapes=[pltpu.VMEM((tq, wide), F32), pltpu.VMEM((tq, d), F32)] + _gather_scratch(nsel, tq),
        compiler_params=_params(("arbitrary",), GATHER_VMEM_LIMIT),
        name="peer_v",
    )(idx_t, act, _group_matrix(nsel).T, x1, gate2, g_post_ffn, tab)


def _dup_heads(w, heads, dh):
    d = w.shape[0]
    return jnp.repeat(w.reshape(d, heads, 1, dh), 2, axis=2).reshape(d, heads * 2 * dh)


def _layer(x2, c, pos_col, bsz, seq, w_mod, b_mod, g_pre_mix, g_post_mix, w_in, conv_w, b_igate, b_fgate,
           mlstm_norm_g, att_sinks, w_out, g_pre_ffn, g_post_ffn, peer_wq, peer_keys1, peer_keys2, peer_u, peer_v):
    n, d = x2.shape
    tm = min(seq, 512)
    mod = _mod(c, w_mod, b_mod)
    shift1, scale1, gate1, shift2, scale2, gate2 = [m.reshape(bsz, 1, d) for m in jnp.split(mod, 6, axis=-1)]

    aw = ATT_HEADS * ATT_HEAD_DIM
    kvw = ATT_KV_HEADS * ATT_HEAD_DIM
    qkw = MLSTM_HEADS * MLSTM_QK_DIM
    mw = MLSTM_HEADS * MLSTM_V_DIM
    o = 0
    wq_a = w_in[:, o:o + aw]; o += aw
    wk_a = w_in[:, o:o + kvw]; o += kvw
    wv_a = w_in[:, o:o + kvw]; o += kvw
    w_mqk = w_in[:, o:o + 2 * qkw]; o += 2 * qkw
    w_mv = w_in[:, o:o + mw]; o += mw
    w_g = w_in[:, o:o + 2 * MLSTM_HEADS]; o += 2 * MLSTM_HEADS
    w_mo = w_in[:, o:o + mw]
    w_gp = jnp.pad(w_g, ((0, 0), (0, LANES - 2 * MLSTM_HEADS)))
    w_all = jnp.concatenate([wq_a, _dup_heads(wk_a, ATT_KV_HEADS, ATT_HEAD_DIM),
                             _dup_heads(wv_a, ATT_KV_HEADS, ATT_HEAD_DIM), w_mqk, w_mv, w_mo, w_gp],
                            axis=1).astype(BF16)

    cos, sin = _rope_tab(pos_col)
    q, kd, vd, mqk, mv, mo, gts = _in_proj(x2, scale1, shift1, g_pre_mix.reshape(1, d), cos, sin, w_all, seq, tm)
    att = _swa(att_sinks, q, kd, vd, bsz, seq)
    gate_bias = jnp.pad(jnp.concatenate([b_igate, b_fgate]), (0, LANES - 2 * MLSTM_HEADS)).reshape(1, LANES)
    chunks = min(seq // MLSTM_CHUNK, 8)
    mh = _mlstm(mqk, mv, mo, gts, conv_w, gate_bias, mlstm_norm_g.reshape(1, mw), bsz, seq, chunks,
                MLSTM_GROUP if bsz % MLSTM_GROUP == 0 else 1)
    wo = w_out.astype(BF16)
    x1, h2 = _out_proj(att, mh, x2, gate1, scale2, shift2, g_post_mix.reshape(1, d), g_pre_ffn.reshape(1, d),
                       wo[:aw], wo[aw:], seq, tm)

    tq = min(n, 256)
    eid_t, gate_t = _peer_sel(h2, peer_wq.astype(BF16), peer_keys1.astype(BF16), peer_keys2.astype(BF16), tq)
    tg = min(seq, GATHER_TOKENS)
    act = _peer_u(eid_t, h2, gate_t, _pack_table(peer_u), tg)
    return _peer_v(eid_t, act, x1, gate2, g_post_ffn.reshape(1, d), _pack_table(peer_v), seq, tg)


def kernel(x, c, positions, w_mod, b_mod, g_pre_mix, g_post_mix, w_in, conv_w, b_igate, b_fgate, mlstm_norm_g, att_sinks, w_out, g_pre_ffn, g_post_ffn, peer_wq, peer_keys1, peer_keys2, peer_u, peer_v):
    bsz, seq, d = x.shape
    n = bsz * seq
    x2 = x.reshape(n, d)
    pos_col = positions.reshape(n, 1)
    for l in range(w_mod.shape[0]):
        x2 = _layer(x2, c, pos_col, bsz, seq, w_mod[l], b_mod[l], g_pre_mix[l], g_post_mix[l], w_in[l], conv_w[l],
                    b_igate[l], b_fgate[l], mlstm_norm_g[l], att_sinks[l], w_out[l], g_pre_ffn[l], g_post_ffn[l],
                    peer_wq[l], peer_keys1[l], peer_keys2[l], peer_u[l], peer_v[l])
    return x2.reshape(bsz, seq, d)
```

```python
import functools

import jax
import jax.numpy as jnp
from jax import lax
from jax.experimental import pallas as pl
from jax.experimental.pallas import tpu as pltpu
from jax.experimental.pallas import tpu_sc as plsc

F32 = jnp.float32
BF16 = jnp.bfloat16

ATT_HEADS = 8
ATT_KV_HEADS = 2
ATT_HEAD_DIM = 64
ATT_BLOCK = 128
ROPE_THETA = 10000.0
MLSTM_HEADS = 4
MLSTM_V_DIM = 128
MLSTM_QK_DIM = 64
MLSTM_CHUNK = 64
CONV_WIDTH = 4
PEER_HEADS = 8
PEER_KEYS = 128
PEER_HALF = 128
PEER_TOPK = 16
NORM_EPS = 1e-6

LANES = 128
SUBLANES = 8
VMEM_LIMIT = 52 * 1024 * 1024
GATHER_STAGES = 16
ROW_WORDS = SUBLANES // 2
MLSTM_GROUP = 1
STREAM_SPLIT = (5, 3)
STREAM_RING = 4
STREAM_STAGES = 10
GATHER_TOKENS = 512
GATHER_VMEM_LIMIT = 58 * 1024 * 1024

NEG_INF = float("-inf")
NT_DIMS = (((1,), (1,)), ((), ()))
TN_DIMS = (((0,), (0,)), ((), ()))


def _params(sem, vmem=None):
    return pltpu.CompilerParams(dimension_semantics=sem, vmem_limit_bytes=vmem)


def _rms(x, g):
    return x * lax.rsqrt(jnp.mean(x * x, axis=-1, keepdims=True) + NORM_EPS) * g


def _mod_kernel(c_ref, w_ref, b_ref, o_ref):
    o_ref[...] = jnp.dot(c_ref[...], w_ref[...], preferred_element_type=F32,
                         precision=lax.Precision.HIGHEST) + b_ref[...]


def _mod(c, w, b):
    bsz, d = c.shape
    nout = w.shape[1]
    return pl.pallas_call(
        _mod_kernel,
        grid=(nout // d,),
        in_specs=[pl.BlockSpec((bsz, d), lambda i: (0, 0)),
                  pl.BlockSpec((d, d), lambda i: (0, i)),
                  pl.BlockSpec((1, d), lambda i: (0, i))],
        out_specs=pl.BlockSpec((bsz, d), lambda i: (0, i)),
        out_shape=jax.ShapeDtypeStruct((bsz, nout), F32),
        compiler_params=_params(("arbitrary",)),
        name="mod",
    )(c, w, b.reshape(1, nout))


def _rope_tab_kernel(pos_ref, inv_ref, sign_ref, cos_ref, sin_ref):
    ang = pos_ref[...].astype(F32) * inv_ref[...]
    cos_ref[...] = jnp.cos(ang)
    sin_ref[...] = jnp.sin(ang) * sign_ref[...]


def _rope_tab(pos_col):
    n = pos_col.shape[0]
    tr = min(n, 1024)
    half = ATT_HEAD_DIM // 2
    inv = ROPE_THETA ** (-jnp.arange(0, ATT_HEAD_DIM, 2, dtype=F32) / ATT_HEAD_DIM)
    inv_row = jnp.tile(inv, LANES // half).reshape(1, LANES)
    lane = jnp.arange(LANES)
    sign_row = jnp.where((lane % ATT_HEAD_DIM) < half, -1.0, 1.0).astype(F32).reshape(1, LANES)
    return pl.pallas_call(
        _rope_tab_kernel,
        grid=(n // tr,),
        in_specs=[pl.BlockSpec((tr, 1), lambda i: (i, 0)),
                  pl.BlockSpec((1, LANES), lambda i: (0, 0)),
                  pl.BlockSpec((1, LANES), lambda i: (0, 0))],
        out_specs=[pl.BlockSpec((tr, LANES), lambda i: (i, 0))] * 2,
        out_shape=[jax.ShapeDtypeStruct((n, LANES), F32)] * 2,
        compiler_params=_params(("arbitrary",)),
        name="rope_tab",
    )(pos_col, inv_row, sign_row)


def _rope(v, cos, sin):
    half = ATT_HEAD_DIM // 2
    lane = lax.broadcasted_iota(jnp.int32, cos.shape, 1)
    first = (lane % ATT_HEAD_DIM) < half
    outs = []
    for j in range(v.shape[1] // LANES):
        c = v[:, j * LANES:(j + 1) * LANES]
        rot = jnp.where(first, pltpu.roll(c, LANES - half, 1), pltpu.roll(c, half, 1))
        outs.append(c * cos + rot * sin)
    return jnp.concatenate(outs, axis=1)


_C_Q, _C_K, _C_V, _C_MQK, _C_MV, _C_MO, _C_G, _C_END = 0, 512, 768, 1024, 1536, 2048, 2560, 2688


def _in_proj_kernel(x_ref, sc_ref, sh_ref, g_ref, cos_ref, sin_ref, w_ref,
                    q_ref, k_ref, v_ref, mqk_ref, mv_ref, mo_ref, gt_ref):
    x = x_ref[...]
    h = _rms(x, g_ref[...]) * (1.0 + sc_ref[0]) + sh_ref[0]
    hb = h.astype(BF16)

    def mm(a, b):
        return jnp.dot(hb, w_ref[:, a:b], preferred_element_type=F32)

    cos = cos_ref[...]
    sin = sin_ref[...]
    q_ref[...] = (_rope(mm(_C_Q, _C_K), cos, sin) * (ATT_HEAD_DIM ** -0.5)).astype(BF16)
    k_ref[...] = _rope(mm(_C_K, _C_V), cos, sin).astype(BF16)
    v_ref[...] = mm(_C_V, _C_MQK).astype(BF16)
    mqk_ref[...] = mm(_C_MQK, _C_MV)
    mv_ref[...] = mm(_C_MV, _C_MO).astype(BF16)
    mo_ref[...] = mm(_C_MO, _C_G)
    gt_ref[...] = mm(_C_G, _C_END)


def _in_proj(x2, scale1, shift1, g_pre, cos, sin, w_all, seq, tm):
    n, d = x2.shape
    per_b = seq // tm
    row = lambda i: (i, 0)
    bsel = lambda i: (i // per_b, 0, 0)
    widths = (512, 256, 256, 512, 512, 512, 128)
    dtypes = (BF16, BF16, BF16, F32, BF16, F32, F32)
    return pl.pallas_call(
        _in_proj_kernel,
        grid=(n // tm,),
        in_specs=[pl.BlockSpec((tm, d), row),
                  pl.BlockSpec((1, 1, d), bsel),
                  pl.BlockSpec((1, 1, d), bsel),
                  pl.BlockSpec((1, d), lambda i: (0, 0)),
                  pl.BlockSpec((tm, LANES), row),
                  pl.BlockSpec((tm, LANES), row),
                  pl.BlockSpec((d, _C_END), lambda i: (0, 0))],
        out_specs=[pl.BlockSpec((tm, w), row) for w in widths],
        out_shape=[jax.ShapeDtypeStruct((n, w), dt) for w, dt in zip(widths, dtypes)],
        compiler_params=_params(("arbitrary",), VMEM_LIMIT),
        name="in_proj",
    )(x2, scale1, shift1, g_pre, cos, sin, w_all)


def _swa_kernel(sink_ref, q_ref, kp_ref, kc_ref, vp_ref, vc_ref, o_ref):
    blk = ATT_BLOCK
    n = pl.program_id(1)
    qi = lax.broadcasted_iota(jnp.int32, (blk, 2 * blk), 0)
    si = lax.broadcasted_iota(jnp.int32, (blk, 2 * blk), 1)
    delta = qi + blk - si
    valid = (delta >= 0) & (delta < blk) & ((si >= blk) | (n > 0))
    lo = lax.broadcasted_iota(jnp.int32, (2 * blk, LANES), 1) < ATT_HEAD_DIM
    group = ATT_HEADS // ATT_KV_HEADS
    for g in range(ATT_KV_HEADS):
        cs = slice(g * LANES, (g + 1) * LANES)
        k = jnp.concatenate([kp_ref[:, cs], kc_ref[:, cs]], axis=0)
        v = jnp.concatenate([vp_ref[:, cs], vc_ref[:, cs]], axis=0)
        zero = jnp.zeros_like(k)
        halves = ((jnp.where(lo, k, zero), jnp.where(lo, v, zero)),
                  (jnp.where(lo, zero, k), jnp.where(lo, zero, v)))
        for jj in range(group // 2):
            p = g * (group // 2) + jj
            q2 = q_ref[:, p * LANES:(p + 1) * LANES]
            acc = jnp.zeros((blk, LANES), F32)
            for half, (kh, vh) in enumerate(halves):
                s = lax.dot_general(q2, kh, NT_DIMS, preferred_element_type=F32)
                s = jnp.where(valid, s, NEG_INF)
                sink = sink_ref[2 * p + half]
                m = jnp.maximum(jnp.max(s, axis=-1, keepdims=True), sink)
                e = jnp.exp(s - m)
                den = jnp.sum(e, axis=-1, keepdims=True) + jnp.exp(sink - m)
                acc = acc + jnp.dot((e / den).astype(BF16), vh, preferred_element_type=F32)
            o_ref[:, p * LANES:(p + 1) * LANES] = acc.astype(BF16)


def _swa(sinks, q, kd, vd, bsz, seq):
    n = q.shape[0]
    nb = seq // ATT_BLOCK
    cur = lambda b, i: (b * nb + i, 0)
    prev = lambda b, i: (b * nb + jnp.maximum(i - 1, 0), 0)
    return pl.pallas_call(
        _swa_kernel,
        grid=(bsz, nb),
        in_specs=[pl.BlockSpec(memory_space=pltpu.SMEM),
                  pl.BlockSpec((ATT_BLOCK, 512), cur),
                  pl.BlockSpec((ATT_BLOCK, 256), prev),
                  pl.BlockSpec((ATT_BLOCK, 256), cur),
                  pl.BlockSpec((ATT_BLOCK, 256), prev),
                  pl.BlockSpec((ATT_BLOCK, 256), cur)],
        out_specs=pl.BlockSpec((ATT_BLOCK, 512), cur),
        out_shape=jax.ShapeDtypeStruct((n, 512), BF16),
        compiler_params=_params(("arbitrary", "arbitrary")),
        name="swa",
    )(sinks, q, kd, kd, vd, vd)


def _mlstm_kernel(mqk_all, mv_all, mo_all, gt_all, cw_ref, gb_ref, ng_ref, o_all,
                  tail_all, qk_all, xs_all, ct_all, n_all, m_all, *, chunks, group):
    @pl.when(pl.program_id(1) == 0)
    def _():
        for ref in (tail_all, ct_all, n_all, m_all):
            ref[...] = jnp.zeros_like(ref)

    bodies = [_mlstm_sequence(*(r.at[g] for r in (mqk_all, mv_all, mo_all, gt_all, o_all, tail_all, qk_all,
                                                   xs_all, ct_all, n_all, m_all)),
                              cw_ref, gb_ref, ng_ref, chunks) for g in range(group)]

    def chunk(c, carry):
        for body in bodies:
            body(c, carry)
        return carry

    lax.fori_loop(0, chunks, chunk, 0)


def _mlstm_sequence(mqk_ref, mv_ref, mo_ref, gt_ref, o_ref, tail_ref, qk_ref, xs_ref, ct_ref, n_ref, m_ref,
                    cw_ref, gb_ref, ng_ref, chunks):
    L = MLSTM_CHUNK
    tm = chunks * L
    nqk = MLSTM_HEADS * MLSTM_QK_DIM

    cur = mqk_ref[...]
    full = jnp.concatenate([tail_ref[...], cur], axis=0)
    off = SUBLANES - (CONV_WIDTH - 1)
    acc = full[off:off + tm] * cw_ref[0:1, :]
    for j in range(1, CONV_WIDTH):
        acc = acc + full[off + j:off + j + tm] * cw_ref[j:j + 1, :]
    act = acc * jax.nn.sigmoid(acc)
    col = lax.broadcasted_iota(jnp.int32, (1, 2 * nqk), 1)
    act = act * jnp.where(col < nqk, MLSTM_QK_DIM ** -0.5, 1.0)
    qk_ref[...] = act.astype(BF16)
    tail_ref[...] = cur[tm - SUBLANES:tm]

    lane = lax.broadcasted_iota(jnp.int32, (tm, LANES), 1)
    gts = gt_ref[...] + gb_ref[...]
    logsig = jnp.minimum(gts, 0.0) - jnp.log(1.0 + jnp.exp(-jnp.abs(gts)))
    xs_ref[...] = jnp.where(lane < MLSTM_HEADS, gts, jnp.where(lane < 2 * MLSTM_HEADS, logsig, 0.0))

    ri = lax.broadcasted_iota(jnp.int32, (L, L), 0)
    ci = lax.broadcasted_iota(jnp.int32, (L, L), 1)
    causal = ci <= ri
    tril = causal.astype(F32)
    lane_l = lax.broadcasted_iota(jnp.int32, (L, LANES), 1)
    lo_l = lane_l < MLSTM_QK_DIM
    row_c = lax.broadcasted_iota(jnp.int32, (LANES, 1), 0) < MLSTM_QK_DIM
    lane_1 = lax.broadcasted_iota(jnp.int32, (1, LANES), 1) < MLSTM_QK_DIM

    def chunk(c, carry):
        r0 = pl.multiple_of(c * L, L)
        rows = pl.ds(r0, L)
        xc = xs_ref[rows, :]
        bc = jnp.dot(tril, xc, preferred_element_type=F32, precision=lax.Precision.HIGHEST)
        x2 = jnp.where(lane_l < MLSTM_HEADS, xc, bc)
        xt = x2.T
        for p in range(MLSTM_HEADS // 2):
            q2 = qk_ref[rows, p * LANES:(p + 1) * LANES]
            k2 = qk_ref[rows, nqk + p * LANES:nqk + (p + 1) * LANES]
            ct_old = ct_ref[p]
            ctb = ct_old.astype(BF16)
            n2 = n_ref[p:p + 1, :]
            decs, upds, kws = [], [], []
            for half in range(2):
                h = 2 * p + half
                hm = lo_l if half == 0 else jnp.logical_not(lo_l)
                zero = jnp.zeros_like(q2)
                qm = jnp.where(hm, q2, zero)
                km = jnp.where(hm, k2, zero)
                v = mv_ref[rows, h * LANES:(h + 1) * LANES]
                b_col = x2[:, MLSTM_HEADS + h:MLSTM_HEADS + h + 1]
                ig_col = x2[:, h:h + 1]
                b_row = xt[MLSTM_HEADS + h:MLSTM_HEADS + h + 1, :]
                ig_row = xt[h:h + 1, :]
                m_prev = m_ref[h:h + 1, 0:1]
                dlog = jnp.where(causal, b_col - b_row + ig_row, NEG_INF)
                m_inter = b_col + m_prev
                m_t = jnp.maximum(m_inter, jnp.max(dlog, axis=-1, keepdims=True))
                w_intra = jnp.exp(dlog - m_t)
                a_inter = jnp.exp(m_inter - m_t)
                s = lax.dot_general(q2, km, NT_DIMS, preferred_element_type=F32) * w_intra
                num = (jnp.dot(s.astype(BF16), v, preferred_element_type=F32)
                       + a_inter * jnp.dot(qm, ctb, preferred_element_type=F32))
                den = (jnp.sum(s, axis=-1, keepdims=True)
                       + a_inter * jnp.sum(qm.astype(F32) * n2, axis=-1, keepdims=True))
                hh = num / jnp.maximum(jnp.abs(den), jnp.exp(-m_t))
                y = _rms(hh, ng_ref[:, h * LANES:(h + 1) * LANES])
                y = y * jax.nn.sigmoid(mo_ref[rows, h * LANES:(h + 1) * LANES])
                o_ref[rows, h * LANES:(h + 1) * LANES] = y.astype(BF16)
                b_last = xt[MLSTM_HEADS + h:MLSTM_HEADS + h + 1, L - 1:L]
                g_col = b_last - b_col + ig_col
                m_new = jnp.maximum(b_last + m_prev, jnp.max(g_col, axis=0, keepdims=True))
                kw = km.astype(F32) * jnp.exp(g_col - m_new)
                decs.append(jnp.exp(b_last + m_prev - m_new))
                kws.append(kw)
                upds.append(lax.dot_general(kw.astype(BF16), v, TN_DIMS, preferred_element_type=F32))
                m_ref[h:h + 1, :] = jnp.broadcast_to(m_new, (1, LANES))
            ct_ref[p] = ct_old * jnp.where(row_c, decs[0], decs[1]) + upds[0] + upds[1]
            n_ref[p:p + 1, :] = (n2 * jnp.where(lane_1, decs[0], decs[1])
                                 + jnp.sum(kws[0] + kws[1], axis=0, keepdims=True))
        return carry

    return chunk


def _mlstm(mqk, mv, mo, gts, conv_w, gate_bias, norm_g, bsz, seq, chunks, group):
    n = mqk.shape[0]
    tm = chunks * MLSTM_CHUNK
    steps = seq // tm
    row = lambda b, i: (b, i, 0)
    const = lambda b, i: (0, 0)
    width = MLSTM_HEADS * MLSTM_V_DIM
    per_seq = lambda a: a.reshape(bsz, seq, a.shape[1])
    out = pl.pallas_call(
        functools.partial(_mlstm_kernel, chunks=chunks, group=group),
        grid=(bsz // group, steps),
        in_specs=[pl.BlockSpec((group, tm, width), row),
                  pl.BlockSpec((group, tm, width), row),
                  pl.BlockSpec((group, tm, width), row),
                  pl.BlockSpec((group, tm, LANES), row),
                  pl.BlockSpec((CONV_WIDTH, width), const),
                  pl.BlockSpec((1, LANES), const),
                  pl.BlockSpec((1, width), const)],
        out_specs=pl.BlockSpec((group, tm, width), row),
        out_shape=jax.ShapeDtypeStruct((bsz, seq, width), BF16),
        scratch_shapes=[pltpu.VMEM((group, SUBLANES, width), F32),
                        pltpu.VMEM((group, tm, width), BF16),
                        pltpu.VMEM((group, tm, LANES), F32),
                        pltpu.VMEM((group, MLSTM_HEADS // 2, LANES, LANES), F32),
                        pltpu.VMEM((group, SUBLANES, LANES), F32),
                        pltpu.VMEM((group, SUBLANES, LANES), F32)],
        compiler_params=_params(("arbitrary", "arbitrary")),
        name="mlstm",
    )(per_seq(mqk), per_seq(mv), per_seq(mo), per_seq(gts), conv_w, gate_bias, norm_g)
    return out.reshape(n, width)


def _out_proj_kernel(att_ref, mh_ref, x_ref, g1_ref, sc_ref, sh_ref, gpm_ref, gpf_ref, wa_ref, wb_ref,
                     x1_ref, h2_ref):
    mix = (jnp.dot(att_ref[...], wa_ref[...], preferred_element_type=F32)
           + jnp.dot(mh_ref[...], wb_ref[...], preferred_element_type=F32))
    x1 = x_ref[...] + g1_ref[0] * _rms(mix, gpm_ref[...])
    x1_ref[...] = x1
    h2_ref[...] = _rms(x1, gpf_ref[...]) * (1.0 + sc_ref[0]) + sh_ref[0]


def _out_proj(att, mh, x2, gate1, scale2, shift2, g_post_mix, g_pre_ffn, wa, wb, seq, tm):
    n, d = x2.shape
    per_b = seq // tm
    row = lambda i: (i, 0)
    bsel = lambda i: (i // per_b, 0, 0)
    const = lambda i: (0, 0)
    half = att.shape[1]
    return pl.pallas_call(
        _out_proj_kernel,
        grid=(n // tm,),
        in_specs=[pl.BlockSpec((tm, half), row), pl.BlockSpec((tm, half), row), pl.BlockSpec((tm, d), row),
                  pl.BlockSpec((1, 1, d), bsel), pl.BlockSpec((1, 1, d), bsel), pl.BlockSpec((1, 1, d), bsel),
                  pl.BlockSpec((1, d), const), pl.BlockSpec((1, d), const),
                  pl.BlockSpec((half, d), const), pl.BlockSpec((half, d), const)],
        out_specs=[pl.BlockSpec((tm, d), row)] * 2,
        out_shape=[jax.ShapeDtypeStruct((n, d), F32)] * 2,
        compiler_params=_params(("arbitrary",), VMEM_LIMIT),
        name="out_proj",
    )(att, mh, x2, gate1, scale2, shift2, g_post_mix, g_pre_ffn, wa, wb)


_BIG_ID = float(2 ** 30)
SORT_LEVELS = 8


def _top_scores(s, k):
    rows, t = s.shape
    span = SORT_LEVELS * SUBLANES
    r = lax.broadcasted_iota(jnp.int32, (rows // SORT_LEVELS, t), 0)
    col_id = ((r // SUBLANES) * span + r % SUBLANES).astype(F32)
    lev = [jnp.concatenate([s[g * span + l * SUBLANES:g * span + (l + 1) * SUBLANES] for g in range(rows // span)],
                           axis=0) for l in range(SORT_LEVELS)]
    ids = [col_id + float(l * SUBLANES) for l in range(SORT_LEVELS)]
    for rnd in range(SORT_LEVELS):
        for a in range(rnd % 2, SORT_LEVELS - 1, 2):
            swap = lev[a + 1] > lev[a]
            lev[a], lev[a + 1] = jnp.where(swap, lev[a + 1], lev[a]), jnp.where(swap, lev[a], lev[a + 1])
            ids[a], ids[a + 1] = jnp.where(swap, ids[a + 1], ids[a]), jnp.where(swap, ids[a], ids[a + 1])
    vals, sel = [], []
    for _ in range(k):
        m = jnp.max(lev[0], axis=0, keepdims=True)
        i = jnp.min(jnp.where(lev[0] == m, ids[0], _BIG_ID), axis=0, keepdims=True)
        vals.append(m)
        sel.append(i)
        hit = ids[0] == i
        for l in range(SORT_LEVELS - 1):
            lev[l] = jnp.where(hit, lev[l + 1], lev[l])
            ids[l] = jnp.where(hit, ids[l + 1], ids[l])
        lev[-1] = jnp.where(hit, NEG_INF, lev[-1])
    return jnp.concatenate(vals, axis=0), jnp.concatenate(sel, axis=0).astype(jnp.int32)


def _top_pair_sums(v1, v2):
    k, t = v1.shape
    half = SUBLANES
    lev = [v1[0:half] + v2[b:b + 1, :] for b in range(k)]
    side = v1[half:k] + v2[0:1, :]
    a_low = lax.broadcasted_iota(jnp.int32, (half, t), 0).astype(F32) * float(k)
    side_id = a_low + float(half * k)
    depth = jnp.zeros((half, t), F32)
    vals, sel = [], []
    for it in range(k):
        top_id = a_low + depth
        m = jnp.max(jnp.maximum(lev[0], side), axis=0, keepdims=True)
        i = jnp.min(jnp.minimum(jnp.where(lev[0] == m, top_id, _BIG_ID), jnp.where(side == m, side_id, _BIG_ID)),
                    axis=0, keepdims=True)
        vals.append(m)
        sel.append(i)
        hit = top_id == i
        for l in range(k - 1 - it):
            lev[l] = jnp.where(hit, lev[l + 1], lev[l])
        side = jnp.where(side_id == i, NEG_INF, side)
        depth = depth + jnp.where(hit, 1.0, 0.0)
    return jnp.concatenate(vals, axis=0), jnp.concatenate(sel, axis=0).astype(jnp.int32)


def _pick_rows(table, which):
    r = lax.broadcasted_iota(jnp.int32, table.shape, 0)
    rows = []
    for k in range(which.shape[0]):
        rows.append(jnp.sum(jnp.where(r == which[k:k + 1, :], table, 0), axis=0, keepdims=True))
    return jnp.concatenate(rows, axis=0)


def _peer_sel_kernel(h_ref, wq_ref, k1_ref, k2_ref, e_ref, g_ref, r_ref):
    tq = h_ref.shape[0]
    K = PEER_TOPK
    q = jnp.dot(h_ref[...].astype(BF16), wq_ref[...], preferred_element_type=F32).astype(BF16)
    for hd in range(PEER_HEADS):
        base = hd * 2 * PEER_HALF
        s1 = lax.dot_general(k1_ref[hd], q[:, base:base + PEER_HALF], NT_DIMS, preferred_element_type=F32)
        s2 = lax.dot_general(k2_ref[hd], q[:, base + PEER_HALF:base + 2 * PEER_HALF], NT_DIMS,
                             preferred_element_type=F32)
        v1, i1 = _top_scores(s1, K)
        v2, i2 = _top_scores(s2, K)
        top, pos = _top_pair_sums(v1, v2)
        eid = _pick_rows(i1, pos >> 4) * PEER_KEYS + _pick_rows(i2, pos & (K - 1))
        ex = jnp.exp(top - top[0:1, :])
        e_ref[hd * K:(hd + 1) * K, :] = eid * ROW_WORDS
        g_ref[hd * K:(hd + 1) * K, :] = ex / jnp.sum(ex, axis=0, keepdims=True)
    first = e_ref[...].T
    nsel = first.shape[1]
    for w in range(ROW_WORDS):
        r_ref[:, w * nsel:(w + 1) * nsel] = first + w


def _peer_sel(h2, wq, k1, k2, tq):
    n, d = h2.shape
    rows = PEER_HEADS * PEER_TOPK
    return pl.pallas_call(
        _peer_sel_kernel,
        grid=(n // tq,),
        in_specs=[pl.BlockSpec((tq, d), lambda i: (i, 0)),
                  pl.BlockSpec(wq.shape, lambda i: (0, 0)),
                  pl.BlockSpec(k1.shape, lambda i: (0, 0, 0)),
                  pl.BlockSpec(k2.shape, lambda i: (0, 0, 0))],
        out_specs=[pl.BlockSpec((rows, tq), lambda i: (0, i))] * 2
        + [pl.BlockSpec((tq, rows * ROW_WORDS), lambda i: (i, 0))],
        out_shape=[jax.ShapeDtypeStruct((rows, n), jnp.int32), jax.ShapeDtypeStruct((rows, n), F32),
                   jax.ShapeDtypeStruct((n, rows * ROW_WORDS), jnp.int32)],
        compiler_params=_params(("arbitrary",), VMEM_LIMIT),
        name="peer_sel",
    )(h2, wq, k1, k2)


def _split_bf16(x):
    hi = x.astype(BF16)
    return hi, (x - hi.astype(F32)).astype(BF16)


def _pack_kernel(w_ref, o_ref):
    x = w_ref[...]
    eb = x.shape[0]
    for r in range(ROW_WORDS):
        lo = x[:, 2 * r * LANES:(2 * r + 1) * LANES].astype(BF16).astype(F32)
        hi = x[:, (2 * r + 1) * LANES:(2 * r + 2) * LANES].astype(BF16).astype(F32)
        word = (lax.shift_right_logical(pltpu.bitcast(lo, jnp.int32), 16)
                | (pltpu.bitcast(hi, jnp.int32) & jnp.int32(-65536)))
        o_ref[pl.ds(r, eb, stride=ROW_WORDS), :] = word


def _pack_table(w):
    e, d = w.shape
    eb = 512
    return pl.pallas_call(
        _pack_kernel,
        grid=(e // eb,),
        in_specs=[pl.BlockSpec((eb, d), lambda i: (i, 0))],
        out_specs=pl.BlockSpec((eb * ROW_WORDS, LANES), lambda i: (i, 0)),
        out_shape=jax.ShapeDtypeStruct((e * ROW_WORDS, LANES), jnp.int32),
        compiler_params=_params(("arbitrary",)),
        name="pack_table",
    )(w)


def _gather_rows(slots, tab_ref, t, stage_ref):
    for j, slot in enumerate(slots):
        src = pl.ds(pl.multiple_of(slot[t], ROW_WORDS), ROW_WORDS)
        stage_ref[j * ROW_WORDS:(j + 1) * ROW_WORDS, :] = tab_ref[src, :]


def _staged_bf16(stage_ref):
    return pltpu.bitcast(stage_ref[...], BF16)


def _pipelined_tokens(ntok, gather, compute, stages, on_trip=None, after=None):
    nb = len(stages)
    for k in range(nb):
        gather(k, stages[k])

    def trip(i, carry):
        t = nb * i
        if on_trip is not None:
            on_trip(i)
        for k in range(nb):
            compute(t + k, stages[k])
            ahead = t + k + nb
            gather(jnp.minimum(ahead, ntok - 1), stages[k])
            if after is not None:
                after(i, k)
        return carry

    lax.fori_loop(0, ntok // nb, trip, 0)


def _with_slot_indices(idx_hbm, sems, bufs, stride, count, run):
    nsel = len(bufs) // 2
    step = pl.program_id(0)
    last = pl.num_programs(0) - 1

    def copies(block, which):
        return [pltpu.make_async_copy(idx_hbm.at[j, pl.ds(block * stride, count)], bufs[which * nsel + j],
                                      sems.at[which])
                for j in range(nsel)]

    @pl.when(step == 0)
    def _():
        for cp in copies(0, 0):
            cp.start()

    def phase(which):
        for cp in copies(step, which):
            cp.wait()

        @pl.when(step < last)
        def _():
            for cp in copies(step + 1, 1 - which):
                cp.start()

        run(bufs[which * nsel:(which + 1) * nsel])

    for which in range(2):
        pl.when(step % 2 == which)(functools.partial(phase, which))


SC_WINDOW = 128


def _sc_gather_rows(tab, row_ids):
    n = row_ids.shape[0]
    mesh = plsc.VectorSubcoreMesh(core_axis_name="core", subcore_axis_name="subcore")

    @pl.kernel(out_type=jax.ShapeDtypeStruct((n, tab.shape[1]), tab.dtype), mesh=mesh)
    def gather(tab_hbm, ids_hbm, out_hbm):
        def body(ids_vmem, out_vmem):
            pltpu.sync_copy(tab_hbm.at[ids_vmem.at[0]], out_vmem)

        pltpu.emit_pipeline(
            body,
            grid=(n // SC_WINDOW,),
            in_specs=[pl.BlockSpec((1, SC_WINDOW), index_map=lambda i: (0, i))],
            out_specs=[pl.BlockSpec((SC_WINDOW, tab.shape[1]), index_map=lambda i: (i, 0))],
            core_axis_name=("core", "subcore"),
            dimension_semantics=(pltpu.PARALLEL,),
            trace_scopes=False,
        )(ids_hbm, out_hbm)

    return gather(tab, row_ids.reshape(1, n))


def _diag_mask(nsel):
    shape = (SUBLANES, nsel * SUBLANES)
    return (lax.broadcasted_iota(jnp.int32, shape, 1) % SUBLANES) == lax.broadcasted_iota(jnp.int32, shape, 0)


def _token_tile(ref, t):
    row = ref[pl.ds(t, 1), :]
    return jnp.concatenate([row[:, s * LANES:(s + 1) * LANES] for s in range(SUBLANES)], axis=0)


def _peer_u_kernel(idx_hbm, h_ref, gate_ref, grp_ref, tab_ref, act_ref, part_ref, *scratch):
    nsel, tq = gate_ref.shape
    stages, sems, bufs = scratch[:GATHER_STAGES], scratch[GATHER_STAGES], scratch[GATHER_STAGES + 1:]
    diag = _diag_mask(nsel)

    def compute(t, stage_ref):
        h_hi, h_lo = _split_bf16(_token_tile(h_ref, t))
        both = lax.dot_general(jnp.concatenate([h_hi, h_lo], axis=0), _staged_bf16(stage_ref), NT_DIMS,
                               preferred_element_type=F32)
        prod = both[0:SUBLANES] + both[SUBLANES:2 * SUBLANES]
        part_ref[pl.ds(t, 1), :] = jnp.sum(jnp.where(diag, prod, 0.0), axis=0, keepdims=True)

    def run(slots):
        _pipelined_tokens(tq, functools.partial(_gather_rows, slots, tab_ref), compute, stages)

    _with_slot_indices(idx_hbm, sems, bufs, tq, tq, run)
    p_hi, p_lo = _split_bf16(part_ref[...])
    grp = grp_ref[...]
    pre = jnp.dot(p_hi, grp, preferred_element_type=F32) + jnp.dot(p_lo, grp, preferred_element_type=F32)
    act_ref[...] = 0.5 * pre * (1.0 + lax.erf(pre * (2.0 ** -0.5))) * gate_ref[...].T


def _group_matrix(nsel):
    r = jnp.arange(nsel * SUBLANES)[:, None] // SUBLANES
    return (r == jnp.arange(nsel)[None, :]).astype(BF16)


def _gather_scratch(nsel, ntok, nstages):
    return ([pltpu.VMEM((nsel * ROW_WORDS, LANES), jnp.int32)] * nstages
            + [pltpu.SemaphoreType.DMA((2,))] + [pltpu.SMEM((ntok,), jnp.int32)] * (2 * nsel))


def _peer_u(idx_t, h2, gate_t, tab, tq):
    nsel, n = idx_t.shape
    d = h2.shape[1]
    wide = nsel * SUBLANES
    return pl.pallas_call(
        _peer_u_kernel,
        grid=(n // tq,),
        in_specs=[pl.BlockSpec(memory_space=pl.ANY),
                  pl.BlockSpec((tq, d), lambda i: (i, 0)),
                  pl.BlockSpec((nsel, tq), lambda i: (0, i)),
                  pl.BlockSpec((wide, nsel), lambda i: (0, 0)),
                  pl.BlockSpec(memory_space=pltpu.VMEM)],
        out_specs=pl.BlockSpec((tq, nsel), lambda i: (i, 0)),
        out_shape=jax.ShapeDtypeStruct((n, nsel), F32),
        scratch_shapes=[pltpu.VMEM((tq, wide), F32)] + _gather_scratch(nsel, tq, GATHER_STAGES),
        compiler_params=_params(("arbitrary",), GATHER_VMEM_LIMIT),
        name="peer_u",
    )(idx_t, h2, gate_t, _group_matrix(nsel), tab)


def _lane_aligned(count):
    return -(-count // LANES) * LANES


def _streamed_copy_matrix(nsel):
    r = jnp.arange(nsel * SUBLANES)
    return ((r[None, :] % (2 * nsel)) // 2 == jnp.arange(nsel)[:, None]).astype(BF16)


def _streamed_diag(nsel):
    shape = (SUBLANES, nsel * SUBLANES)
    r = lax.broadcasted_iota(jnp.int32, shape, 1)
    return lax.broadcasted_iota(jnp.int32, shape, 0) == 2 * (r // (2 * nsel)) + r % 2


def _peer_v_kernel(idx_hbm, act_ref, rep_ref, rep2_ref, x1_ref, g2_ref, gpf_ref, tab_ref, rows_hbm, o_ref,
                   wide_ref, y_ref, rowbuf, rsem, *scratch, nvld, nstages):
    tq, nsel = act_ref.shape
    stages, sems, bufs = scratch[:nstages], scratch[nstages], scratch[nstages + 1:]
    trips = nvld // nstages
    per_trip = (tq - nvld) // trips
    rows_tok = nsel * ROW_WORDS
    step = pl.program_id(0)
    chunks = pl.num_programs(0) * trips
    diag = _diag_mask(nsel)
    diag2 = _streamed_diag(nsel)
    a_hi, a_lo = _split_bf16(act_ref[...])
    for lo, hi, rep in ((0, nvld, rep_ref[...]), (nvld, tq, rep2_ref[...])):
        wide_ref[lo:hi, :] = (jnp.dot(a_hi[lo:hi], rep, preferred_element_type=F32)
                              + jnp.dot(a_lo[lo:hi], rep, preferred_element_type=F32))

    def finish(t, mask, rows_bf16):
        w = jnp.where(mask, jnp.broadcast_to(wide_ref[pl.ds(t, 1), :], mask.shape), 0.0)
        w_hi, w_lo = _split_bf16(w)
        both = jnp.dot(jnp.concatenate([w_hi, w_lo], axis=0), rows_bf16, preferred_element_type=F32)
        tile = both[0:SUBLANES] + both[SUBLANES:2 * SUBLANES]
        y_ref[pl.ds(t, 1), :] = jnp.concatenate([tile[s:s + 1, :] for s in range(SUBLANES)], axis=1)

    def compute(t, stage_ref):
        finish(t, diag, _staged_bf16(stage_ref))

    def chunk_copy(g, slot):
        return pltpu.make_async_copy(rows_hbm.at[pl.ds(g * (per_trip * rows_tok), per_trip * rows_tok), :],
                                     rowbuf.at[slot], rsem.at[slot])

    ring = STREAM_RING

    @pl.when(step == 0)
    def _():
        for g0 in range(ring - 1):
            chunk_copy(g0, g0).start()

    def on_trip(i):
        g = step * trips + i
        chunk_copy(g, i % ring).wait()

        @pl.when(g + ring - 1 < chunks)
        def _():
            chunk_copy(g + ring - 1, (i + ring - 1) % ring).start()

    places = [(d * nstages) // per_trip for d in range(per_trip)]

    def after(i, k):
        for d in range(per_trip):
            if places[d] == k:
                rows = rowbuf.at[i % ring, pl.ds(d * rows_tok, rows_tok), :]
                finish(nvld + i * per_trip + d, diag2, pltpu.bitcast(rows[...], BF16))

    def run(slots):
        _pipelined_tokens(nvld, functools.partial(_gather_rows, slots, tab_ref), compute, stages, on_trip, after)

    _with_slot_indices(idx_hbm, sems, bufs, tq, _lane_aligned(nvld), run)
    o_ref[...] = x1_ref[...] + g2_ref[0] * _rms(y_ref[...], gpf_ref[...])


def _peer_v(idx_t, act, x1, gate2, g_post_ffn, tab, rows, seq, tq, nvld, nstages):
    nsel, n = idx_t.shape
    d = x1.shape[1]
    wide = nsel * SUBLANES
    per_b = seq // tq
    row = lambda i: (i, 0)
    const = lambda i: (0, 0)
    trips = nvld // nstages
    per_trip = (tq - nvld) // trips
    assert trips % STREAM_RING == 0 and trips * nstages == nvld and trips * per_trip == tq - nvld
    return pl.pallas_call(
        functools.partial(_peer_v_kernel, nvld=nvld, nstages=nstages),
        grid=(n // tq,),
        in_specs=[pl.BlockSpec(memory_space=pl.ANY),
                  pl.BlockSpec((tq, nsel), row),
                  pl.BlockSpec((nsel, wide), const),
                  pl.BlockSpec((nsel, wide), const),
                  pl.BlockSpec((tq, d), row),
                  pl.BlockSpec((1, 1, d), lambda i: (i // per_b, 0, 0)),
                  pl.BlockSpec((1, d), const),
                  pl.BlockSpec(memory_space=pltpu.VMEM),
                  pl.BlockSpec(memory_space=pl.ANY)],
        out_specs=pl.BlockSpec((tq, d), row),
        out_shape=jax.ShapeDtypeStruct((n, d), F32),
        scratch_shapes=[pltpu.VMEM((tq, wide), F32), pltpu.VMEM((tq, d), F32),
                        pltpu.VMEM((STREAM_RING, per_trip * nsel * ROW_WORDS, LANES), jnp.int32),
                        pltpu.SemaphoreType.DMA((STREAM_RING,))]
        + _gather_scratch(nsel, _lane_aligned(nvld), nstages),
        compiler_params=_params(("arbitrary",), GATHER_VMEM_LIMIT),
        name="peer_v",
    )(idx_t, act, _group_matrix(nsel).T, _streamed_copy_matrix(nsel), x1, gate2, g_post_ffn, tab, rows)


def _dup_heads(w, heads, dh):
    d = w.shape[0]
    return jnp.repeat(w.reshape(d, heads, 1, dh), 2, axis=2).reshape(d, heads * 2 * dh)


def _layer(x2, c, pos_col, bsz, seq, w_mod, b_mod, g_pre_mix, g_post_mix, w_in, conv_w, b_igate, b_fgate,
           mlstm_norm_g, att_sinks, w_out, g_pre_ffn, g_post_ffn, peer_wq, peer_keys1, peer_keys2, peer_u, peer_v):
    n, d = x2.shape
    tm = min(seq, 512)
    mod = _mod(c, w_mod, b_mod)
    shift1, scale1, gate1, shift2, scale2, gate2 = [m.reshape(bsz, 1, d) for m in jnp.split(mod, 6, axis=-1)]

    aw = ATT_HEADS * ATT_HEAD_DIM
    kvw = ATT_KV_HEADS * ATT_HEAD_DIM
    qkw = MLSTM_HEADS * MLSTM_QK_DIM
    mw = MLSTM_HEADS * MLSTM_V_DIM
    o = 0
    wq_a = w_in[:, o:o + aw]; o += aw
    wk_a = w_in[:, o:o + kvw]; o += kvw
    wv_a = w_in[:, o:o + kvw]; o += kvw
    w_mqk = w_in[:, o:o + 2 * qkw]; o += 2 * qkw
    w_mv = w_in[:, o:o + mw]; o += mw
    w_g = w_in[:, o:o + 2 * MLSTM_HEADS]; o += 2 * MLSTM_HEADS
    w_mo = w_in[:, o:o + mw]
    w_gp = jnp.pad(w_g, ((0, 0), (0, LANES - 2 * MLSTM_HEADS)))
    w_all = jnp.concatenate([wq_a, _dup_heads(wk_a, ATT_KV_HEADS, ATT_HEAD_DIM),
                             _dup_heads(wv_a, ATT_KV_HEADS, ATT_HEAD_DIM), w_mqk, w_mv, w_mo, w_gp],
                            axis=1).astype(BF16)

    cos, sin = _rope_tab(pos_col)
    q, kd, vd, mqk, mv, mo, gts = _in_proj(x2, scale1, shift1, g_pre_mix.reshape(1, d), cos, sin, w_all, seq, tm)
    att = _swa(att_sinks, q, kd, vd, bsz, seq)
    gate_bias = jnp.pad(jnp.concatenate([b_igate, b_fgate]), (0, LANES - 2 * MLSTM_HEADS)).reshape(1, LANES)
    chunks = min(seq // MLSTM_CHUNK, 8)
    mh = _mlstm(mqk, mv, mo, gts, conv_w, gate_bias, mlstm_norm_g.reshape(1, mw), bsz, seq, chunks,
                MLSTM_GROUP if bsz % MLSTM_GROUP == 0 else 1)
    wo = w_out.astype(BF16)
    x1, h2 = _out_proj(att, mh, x2, gate1, scale2, shift2, g_post_mix.reshape(1, d), g_pre_ffn.reshape(1, d),
                       wo[:aw], wo[aw:], seq, tm)

    tq = min(n, 256)
    eid_t, gate_t, row_ids = _peer_sel(h2, peer_wq.astype(BF16), peer_keys1.astype(BF16),
                                       peer_keys2.astype(BF16), tq)
    tg = min(seq, GATHER_TOKENS)
    tab_v = _pack_table(peer_v)
    nvld = tg * STREAM_SPLIT[0] // sum(STREAM_SPLIT)
    streamed = row_ids.reshape(n // tg, tg, row_ids.shape[1])[:, nvld:, :]
    rows_v = _sc_gather_rows(tab_v, streamed.reshape(-1))
    act = _peer_u(eid_t, h2, gate_t, _pack_table(peer_u), tg)
    return _peer_v(eid_t, act, x1, gate2, g_post_ffn.reshape(1, d), tab_v, rows_v, seq, tg, nvld, STREAM_STAGES)


def kernel(x, c, positions, w_mod, b_mod, g_pre_mix, g_post_mix, w_in, conv_w, b_igate, b_fgate, mlstm_norm_g, att_sinks, w_out, g_pre_ffn, g_post_ffn, peer_wq, peer_keys1, peer_keys2, peer_u, peer_v):
    bsz, seq, d = x.shape
    n = bsz * seq
    x2 = x.reshape(n, d)
    pos_col = positions.reshape(n, 1)
    for l in range(w_mod.shape[0]):
        x2 = _layer(x2, c, pos_col, bsz, seq, w_mod[l], b_mod[l], g_pre_mix[l], g_post_mix[l], w_in[l], conv_w[l],
                    b_igate[l], b_fgate[l], mlstm_norm_g[l], att_sinks[l], w_out[l], g_pre_ffn[l], g_post_ffn[l],
                    peer_wq[l], peer_keys1[l], peer_keys2[l], peer_u[l], peer_v[l])
    return x2.reshape(bsz, seq, d)
```

```python
import functools

import jax
import jax.numpy as jnp
from jax import lax
from jax.experimental import pallas as pl
from jax.experimental.pallas import tpu as pltpu
from jax.experimental.pallas import tpu_sc as plsc

F32 = jnp.float32
BF16 = jnp.bfloat16

ATT_HEADS = 8
ATT_KV_HEADS = 2
ATT_HEAD_DIM = 64
ATT_BLOCK = 128
ROPE_THETA = 10000.0
MLSTM_HEADS = 4
MLSTM_V_DIM = 128
MLSTM_QK_DIM = 64
MLSTM_CHUNK = 64
CONV_WIDTH = 4
PEER_HEADS = 8
PEER_KEYS = 128
PEER_HALF = 128
PEER_TOPK = 16
NORM_EPS = 1e-6

LANES = 128
SUBLANES = 8
VMEM_LIMIT = 52 * 1024 * 1024
GATHER_STAGES = 16
ROW_WORDS = SUBLANES // 2
MLSTM_GROUP = 1
STREAM_SPLIT = (4, 4)
STREAM_RING = 4
STREAM_STAGES = 8
STREAM_HALVES = 2
STREAM_ROW = 2 * LANES
GATHER_TOKENS = 512
GATHER_VMEM_LIMIT = 58 * 1024 * 1024

NEG_INF = float("-inf")
NT_DIMS = (((1,), (1,)), ((), ()))
TN_DIMS = (((0,), (0,)), ((), ()))


def _params(sem, vmem=None):
    return pltpu.CompilerParams(dimension_semantics=sem, vmem_limit_bytes=vmem)


def _rms(x, g):
    return x * lax.rsqrt(jnp.mean(x * x, axis=-1, keepdims=True) + NORM_EPS) * g


def _mod_kernel(c_ref, w_ref, b_ref, o_ref):
    o_ref[...] = jnp.dot(c_ref[...], w_ref[...], preferred_element_type=F32,
                         precision=lax.Precision.HIGHEST) + b_ref[...]


def _mod(c, w, b):
    bsz, d = c.shape
    nout = w.shape[1]
    return pl.pallas_call(
        _mod_kernel,
        grid=(nout // d,),
        in_specs=[pl.BlockSpec((bsz, d), lambda i: (0, 0)),
                  pl.BlockSpec((d, d), lambda i: (0, i)),
                  pl.BlockSpec((1, d), lambda i: (0, i))],
        out_specs=pl.BlockSpec((bsz, d), lambda i: (0, i)),
        out_shape=jax.ShapeDtypeStruct((bsz, nout), F32),
        compiler_params=_params(("arbitrary",)),
        name="mod",
    )(c, w, b.reshape(1, nout))


def _rope_tab_kernel(pos_ref, inv_ref, sign_ref, cos_ref, sin_ref):
    ang = pos_ref[...].astype(F32) * inv_ref[...]
    cos_ref[...] = jnp.cos(ang)
    sin_ref[...] = jnp.sin(ang) * sign_ref[...]


def _rope_tab(pos_col):
    n = pos_col.shape[0]
    tr = min(n, 1024)
    half = ATT_HEAD_DIM // 2
    inv = ROPE_THETA ** (-jnp.arange(0, ATT_HEAD_DIM, 2, dtype=F32) / ATT_HEAD_DIM)
    inv_row = jnp.tile(inv, LANES // half).reshape(1, LANES)
    lane = jnp.arange(LANES)
    sign_row = jnp.where((lane % ATT_HEAD_DIM) < half, -1.0, 1.0).astype(F32).reshape(1, LANES)
    return pl.pallas_call(
        _rope_tab_kernel,
        grid=(n // tr,),
        in_specs=[pl.BlockSpec((tr, 1), lambda i: (i, 0)),
                  pl.BlockSpec((1, LANES), lambda i: (0, 0)),
                  pl.BlockSpec((1, LANES), lambda i: (0, 0))],
        out_specs=[pl.BlockSpec((tr, LANES), lambda i: (i, 0))] * 2,
        out_shape=[jax.ShapeDtypeStruct((n, LANES), F32)] * 2,
        compiler_params=_params(("arbitrary",)),
        name="rope_tab",
    )(pos_col, inv_row, sign_row)


def _rope(v, cos, sin):
    half = ATT_HEAD_DIM // 2
    lane = lax.broadcasted_iota(jnp.int32, cos.shape, 1)
    first = (lane % ATT_HEAD_DIM) < half
    outs = []
    for j in range(v.shape[1] // LANES):
        c = v[:, j * LANES:(j + 1) * LANES]
        rot = jnp.where(first, pltpu.roll(c, LANES - half, 1), pltpu.roll(c, half, 1))
        outs.append(c * cos + rot * sin)
    return jnp.concatenate(outs, axis=1)


_C_Q, _C_K, _C_V, _C_MQK, _C_MV, _C_MO, _C_G, _C_END = 0, 512, 768, 1024, 1536, 2048, 2560, 2688


def _in_proj_kernel(x_ref, sc_ref, sh_ref, g_ref, cos_ref, sin_ref, w_ref,
                    q_ref, k_ref, v_ref, mqk_ref, mv_ref, mo_ref, gt_ref):
    x = x_ref[...]
    h = _rms(x, g_ref[...]) * (1.0 + sc_ref[0]) + sh_ref[0]
    hb = h.astype(BF16)

    def mm(a, b):
        return jnp.dot(hb, w_ref[:, a:b], preferred_element_type=F32)

    cos = cos_ref[...]
    sin = sin_ref[...]
    q_ref[...] = (_rope(mm(_C_Q, _C_K), cos, sin) * (ATT_HEAD_DIM ** -0.5)).astype(BF16)
    k_ref[...] = _rope(mm(_C_K, _C_V), cos, sin).astype(BF16)
    v_ref[...] = mm(_C_V, _C_MQK).astype(BF16)
    mqk_ref[...] = mm(_C_MQK, _C_MV)
    mv_ref[...] = mm(_C_MV, _C_MO).astype(BF16)
    mo_ref[...] = mm(_C_MO, _C_G)
    gt_ref[...] = mm(_C_G, _C_END)


def _in_proj(x2, scale1, shift1, g_pre, cos, sin, w_all, seq, tm):
    n, d = x2.shape
    per_b = seq // tm
    row = lambda i: (i, 0)
    bsel = lambda i: (i // per_b, 0, 0)
    widths = (512, 256, 256, 512, 512, 512, 128)
    dtypes = (BF16, BF16, BF16, F32, BF16, F32, F32)
    return pl.pallas_call(
        _in_proj_kernel,
        grid=(n // tm,),
        in_specs=[pl.BlockSpec((tm, d), row),
                  pl.BlockSpec((1, 1, d), bsel),
                  pl.BlockSpec((1, 1, d), bsel),
                  pl.BlockSpec((1, d), lambda i: (0, 0)),
                  pl.BlockSpec((tm, LANES), row),
                  pl.BlockSpec((tm, LANES), row),
                  pl.BlockSpec((d, _C_END), lambda i: (0, 0))],
        out_specs=[pl.BlockSpec((tm, w), row) for w in widths],
        out_shape=[jax.ShapeDtypeStruct((n, w), dt) for w, dt in zip(widths, dtypes)],
        compiler_params=_params(("arbitrary",), VMEM_LIMIT),
        name="in_proj",
    )(x2, scale1, shift1, g_pre, cos, sin, w_all)


def _swa_kernel(sink_ref, q_ref, kp_ref, kc_ref, vp_ref, vc_ref, o_ref):
    blk = ATT_BLOCK
    n = pl.program_id(1)
    qi = lax.broadcasted_iota(jnp.int32, (blk, 2 * blk), 0)
    si = lax.broadcasted_iota(jnp.int32, (blk, 2 * blk), 1)
    delta = qi + blk - si
    valid = (delta >= 0) & (delta < blk) & ((si >= blk) | (n > 0))
    lo = lax.broadcasted_iota(jnp.int32, (2 * blk, LANES), 1) < ATT_HEAD_DIM
    group = ATT_HEADS // ATT_KV_HEADS
    for g in range(ATT_KV_HEADS):
        cs = slice(g * LANES, (g + 1) * LANES)
        k = jnp.concatenate([kp_ref[:, cs], kc_ref[:, cs]], axis=0)
        v = jnp.concatenate([vp_ref[:, cs], vc_ref[:, cs]], axis=0)
        zero = jnp.zeros_like(k)
        halves = ((jnp.where(lo, k, zero), jnp.where(lo, v, zero)),
                  (jnp.where(lo, zero, k), jnp.where(lo, zero, v)))
        for jj in range(group // 2):
            p = g * (group // 2) + jj
            q2 = q_ref[:, p * LANES:(p + 1) * LANES]
            acc = jnp.zeros((blk, LANES), F32)
            for half, (kh, vh) in enumerate(halves):
                s = lax.dot_general(q2, kh, NT_DIMS, preferred_element_type=F32)
                s = jnp.where(valid, s, NEG_INF)
                sink = sink_ref[2 * p + half]
                m = jnp.maximum(jnp.max(s, axis=-1, keepdims=True), sink)
                e = jnp.exp(s - m)
                den = jnp.sum(e, axis=-1, keepdims=True) + jnp.exp(sink - m)
                acc = acc + jnp.dot((e / den).astype(BF16), vh, preferred_element_type=F32)
            o_ref[:, p * LANES:(p + 1) * LANES] = acc.astype(BF16)


def _swa(sinks, q, kd, vd, bsz, seq):
    n = q.shape[0]
    nb = seq // ATT_BLOCK
    cur = lambda b, i: (b * nb + i, 0)
    prev = lambda b, i: (b * nb + jnp.maximum(i - 1, 0), 0)
    return pl.pallas_call(
        _swa_kernel,
        grid=(bsz, nb),
        in_specs=[pl.BlockSpec(memory_space=pltpu.SMEM),
                  pl.BlockSpec((ATT_BLOCK, 512), cur),
                  pl.BlockSpec((ATT_BLOCK, 256), prev),
                  pl.BlockSpec((ATT_BLOCK, 256), cur),
                  pl.BlockSpec((ATT_BLOCK, 256), prev),
                  pl.BlockSpec((ATT_BLOCK, 256), cur)],
        out_specs=pl.BlockSpec((ATT_BLOCK, 512), cur),
        out_shape=jax.ShapeDtypeStruct((n, 512), BF16),
        compiler_params=_params(("arbitrary", "arbitrary")),
        name="swa",
    )(sinks, q, kd, kd, vd, vd)


def _mlstm_kernel(mqk_all, mv_all, mo_all, gt_all, cw_ref, gb_ref, ng_ref, o_all,
                  tail_all, qk_all, xs_all, ct_all, n_all, m_all, *, chunks, group):
    @pl.when(pl.program_id(1) == 0)
    def _():
        for ref in (tail_all, ct_all, n_all, m_all):
            ref[...] = jnp.zeros_like(ref)

    bodies = [_mlstm_sequence(*(r.at[g] for r in (mqk_all, mv_all, mo_all, gt_all, o_all, tail_all, qk_all,
                                                   xs_all, ct_all, n_all, m_all)),
                              cw_ref, gb_ref, ng_ref, chunks) for g in range(group)]

    def chunk(c, carry):
        for body in bodies:
            body(c, carry)
        return carry

    lax.fori_loop(0, chunks, chunk, 0)


def _mlstm_sequence(mqk_ref, mv_ref, mo_ref, gt_ref, o_ref, tail_ref, qk_ref, xs_ref, ct_ref, n_ref, m_ref,
                    cw_ref, gb_ref, ng_ref, chunks):
    L = MLSTM_CHUNK
    tm = chunks * L
    nqk = MLSTM_HEADS * MLSTM_QK_DIM

    cur = mqk_ref[...]
    full = jnp.concatenate([tail_ref[...], cur], axis=0)
    off = SUBLANES - (CONV_WIDTH - 1)
    acc = full[off:off + tm] * cw_ref[0:1, :]
    for j in range(1, CONV_WIDTH):
        acc = acc + full[off + j:off + j + tm] * cw_ref[j:j + 1, :]
    act = acc * jax.nn.sigmoid(acc)
    col = lax.broadcasted_iota(jnp.int32, (1, 2 * nqk), 1)
    act = act * jnp.where(col < nqk, MLSTM_QK_DIM ** -0.5, 1.0)
    qk_ref[...] = act.astype(BF16)
    tail_ref[...] = cur[tm - SUBLANES:tm]

    lane = lax.broadcasted_iota(jnp.int32, (tm, LANES), 1)
    gts = gt_ref[...] + gb_ref[...]
    logsig = jnp.minimum(gts, 0.0) - jnp.log(1.0 + jnp.exp(-jnp.abs(gts)))
    xs_ref[...] = jnp.where(lane < MLSTM_HEADS, gts, jnp.where(lane < 2 * MLSTM_HEADS, logsig, 0.0))

    ri = lax.broadcasted_iota(jnp.int32, (L, L), 0)
    ci = lax.broadcasted_iota(jnp.int32, (L, L), 1)
    causal = ci <= ri
    tril = causal.astype(F32)
    lane_l = lax.broadcasted_iota(jnp.int32, (L, LANES), 1)
    lo_l = lane_l < MLSTM_QK_DIM
    row_c = lax.broadcasted_iota(jnp.int32, (LANES, 1), 0) < MLSTM_QK_DIM
    lane_1 = lax.broadcasted_iota(jnp.int32, (1, LANES), 1) < MLSTM_QK_DIM

    def chunk(c, carry):
        r0 = pl.multiple_of(c * L, L)
        rows = pl.ds(r0, L)
        xc = xs_ref[rows, :]
        bc = jnp.dot(tril, xc, preferred_element_type=F32, precision=lax.Precision.HIGHEST)
        x2 = jnp.where(lane_l < MLSTM_HEADS, xc, bc)
        xt = x2.T
        for p in range(MLSTM_HEADS // 2):
            q2 = qk_ref[rows, p * LANES:(p + 1) * LANES]
            k2 = qk_ref[rows, nqk + p * LANES:nqk + (p + 1) * LANES]
            ct_old = ct_ref[p]
            ctb = ct_old.astype(BF16)
            n2 = n_ref[p:p + 1, :]
            decs, upds, kws = [], [], []
            for half in range(2):
                h = 2 * p + half
                hm = lo_l if half == 0 else jnp.logical_not(lo_l)
                zero = jnp.zeros_like(q2)
                qm = jnp.where(hm, q2, zero)
                km = jnp.where(hm, k2, zero)
                v = mv_ref[rows, h * LANES:(h + 1) * LANES]
                b_col = x2[:, MLSTM_HEADS + h:MLSTM_HEADS + h + 1]
                ig_col = x2[:, h:h + 1]
                b_row = xt[MLSTM_HEADS + h:MLSTM_HEADS + h + 1, :]
                ig_row = xt[h:h + 1, :]
                m_prev = m_ref[h:h + 1, 0:1]
                dlog = jnp.where(causal, b_col - b_row + ig_row, NEG_INF)
                m_inter = b_col + m_prev
                m_t = jnp.maximum(m_inter, jnp.max(dlog, axis=-1, keepdims=True))
                w_intra = jnp.exp(dlog - m_t)
                a_inter = jnp.exp(m_inter - m_t)
                s = lax.dot_general(q2, km, NT_DIMS, preferred_element_type=F32) * w_intra
                num = (jnp.dot(s.astype(BF16), v, preferred_element_type=F32)
                       + a_inter * jnp.dot(qm, ctb, preferred_element_type=F32))
                den = (jnp.sum(s, axis=-1, keepdims=True)
                       + a_inter * jnp.sum(qm.astype(F32) * n2, axis=-1, keepdims=True))
                hh = num / jnp.maximum(jnp.abs(den), jnp.exp(-m_t))
                y = _rms(hh, ng_ref[:, h * LANES:(h + 1) * LANES])
                y = y * jax.nn.sigmoid(mo_ref[rows, h * LANES:(h + 1) * LANES])
                o_ref[rows, h * LANES:(h + 1) * LANES] = y.astype(BF16)
                b_last = xt[MLSTM_HEADS + h:MLSTM_HEADS + h + 1, L - 1:L]
                g_col = b_last - b_col + ig_col
                m_new = jnp.maximum(b_last + m_prev, jnp.max(g_col, axis=0, keepdims=True))
                kw = km.astype(F32) * jnp.exp(g_col - m_new)
                decs.append(jnp.exp(b_last + m_prev - m_new))
                kws.append(kw)
                upds.append(lax.dot_general(kw.astype(BF16), v, TN_DIMS, preferred_element_type=F32))
                m_ref[h:h + 1, :] = jnp.broadcast_to(m_new, (1, LANES))
            ct_ref[p] = ct_old * jnp.where(row_c, decs[0], decs[1]) + upds[0] + upds[1]
            n_ref[p:p + 1, :] = (n2 * jnp.where(lane_1, decs[0], decs[1])
                                 + jnp.sum(kws[0] + kws[1], axis=0, keepdims=True))
        return carry

    return chunk


def _mlstm(mqk, mv, mo, gts, conv_w, gate_bias, norm_g, bsz, seq, chunks, group):
    n = mqk.shape[0]
    tm = chunks * MLSTM_CHUNK
    steps = seq // tm
    row = lambda b, i: (b, i, 0)
    const = lambda b, i: (0, 0)
    width = MLSTM_HEADS * MLSTM_V_DIM
    per_seq = lambda a: a.reshape(bsz, seq, a.shape[1])
    out = pl.pallas_call(
        functools.partial(_mlstm_kernel, chunks=chunks, group=group),
        grid=(bsz // group, steps),
        in_specs=[pl.BlockSpec((group, tm, width), row),
                  pl.BlockSpec((group, tm, width), row),
                  pl.BlockSpec((group, tm, width), row),
                  pl.BlockSpec((group, tm, LANES), row),
                  pl.BlockSpec((CONV_WIDTH, width), const),
                  pl.BlockSpec((1, LANES), const),
                  pl.BlockSpec((1, width), const)],
        out_specs=pl.BlockSpec((group, tm, width), row),
        out_shape=jax.ShapeDtypeStruct((bsz, seq, width), BF16),
        scratch_shapes=[pltpu.VMEM((group, SUBLANES, width), F32),
                        pltpu.VMEM((group, tm, width), BF16),
                        pltpu.VMEM((group, tm, LANES), F32),
                        pltpu.VMEM((group, MLSTM_HEADS // 2, LANES, LANES), F32),
                        pltpu.VMEM((group, SUBLANES, LANES), F32),
                        pltpu.VMEM((group, SUBLANES, LANES), F32)],
        compiler_params=_params(("arbitrary", "arbitrary")),
        name="mlstm",
    )(per_seq(mqk), per_seq(mv), per_seq(mo), per_seq(gts), conv_w, gate_bias, norm_g)
    return out.reshape(n, width)


def _out_proj_kernel(att_ref, mh_ref, x_ref, g1_ref, sc_ref, sh_ref, gpm_ref, gpf_ref, wa_ref, wb_ref,
                     x1_ref, h2_ref):
    mix = (jnp.dot(att_ref[...], wa_ref[...], preferred_element_type=F32)
           + jnp.dot(mh_ref[...], wb_ref[...], preferred_element_type=F32))
    x1 = x_ref[...] + g1_ref[0] * _rms(mix, gpm_ref[...])
    x1_ref[...] = x1
    h2_ref[...] = _rms(x1, gpf_ref[...]) * (1.0 + sc_ref[0]) + sh_ref[0]


def _out_proj(att, mh, x2, gate1, scale2, shift2, g_post_mix, g_pre_ffn, wa, wb, seq, tm):
    n, d = x2.shape
    per_b = seq // tm
    row = lambda i: (i, 0)
    bsel = lambda i: (i // per_b, 0, 0)
    const = lambda i: (0, 0)
    half = att.shape[1]
    return pl.pallas_call(
        _out_proj_kernel,
        grid=(n // tm,),
        in_specs=[pl.BlockSpec((tm, half), row), pl.BlockSpec((tm, half), row), pl.BlockSpec((tm, d), row),
                  pl.BlockSpec((1, 1, d), bsel), pl.BlockSpec((1, 1, d), bsel), pl.BlockSpec((1, 1, d), bsel),
                  pl.BlockSpec((1, d), const), pl.BlockSpec((1, d), const),
                  pl.BlockSpec((half, d), const), pl.BlockSpec((half, d), const)],
        out_specs=[pl.BlockSpec((tm, d), row)] * 2,
        out_shape=[jax.ShapeDtypeStruct((n, d), F32)] * 2,
        compiler_params=_params(("arbitrary",), VMEM_LIMIT),
        name="out_proj",
    )(att, mh, x2, gate1, scale2, shift2, g_post_mix, g_pre_ffn, wa, wb)


_BIG_ID = float(2 ** 30)
SORT_LEVELS = 8


def _top_scores(s, k):
    rows, t = s.shape
    span = SORT_LEVELS * SUBLANES
    r = lax.broadcasted_iota(jnp.int32, (rows // SORT_LEVELS, t), 0)
    col_id = ((r // SUBLANES) * span + r % SUBLANES).astype(F32)
    lev = [jnp.concatenate([s[g * span + l * SUBLANES:g * span + (l + 1) * SUBLANES] for g in range(rows // span)],
                           axis=0) for l in range(SORT_LEVELS)]
    ids = [col_id + float(l * SUBLANES) for l in range(SORT_LEVELS)]
    for rnd in range(SORT_LEVELS):
        for a in range(rnd % 2, SORT_LEVELS - 1, 2):
            swap = lev[a + 1] > lev[a]
            lev[a], lev[a + 1] = jnp.where(swap, lev[a + 1], lev[a]), jnp.where(swap, lev[a], lev[a + 1])
            ids[a], ids[a + 1] = jnp.where(swap, ids[a + 1], ids[a]), jnp.where(swap, ids[a], ids[a + 1])
    vals, sel = [], []
    for _ in range(k):
        m = jnp.max(lev[0], axis=0, keepdims=True)
        i = jnp.min(jnp.where(lev[0] == m, ids[0], _BIG_ID), axis=0, keepdims=True)
        vals.append(m)
        sel.append(i)
        hit = ids[0] == i
        for l in range(SORT_LEVELS - 1):
            lev[l] = jnp.where(hit, lev[l + 1], lev[l])
            ids[l] = jnp.where(hit, ids[l + 1], ids[l])
        lev[-1] = jnp.where(hit, NEG_INF, lev[-1])
    return jnp.concatenate(vals, axis=0), jnp.concatenate(sel, axis=0).astype(jnp.int32)


def _top_pair_sums(v1, v2):
    k, t = v1.shape
    half = SUBLANES
    lev = [v1[0:half] + v2[b:b + 1, :] for b in range(k)]
    side = v1[half:k] + v2[0:1, :]
    a_low = lax.broadcasted_iota(jnp.int32, (half, t), 0).astype(F32) * float(k)
    side_id = a_low + float(half * k)
    depth = jnp.zeros((half, t), F32)
    vals, sel = [], []
    for it in range(k):
        top_id = a_low + depth
        m = jnp.max(jnp.maximum(lev[0], side), axis=0, keepdims=True)
        i = jnp.min(jnp.minimum(jnp.where(lev[0] == m, top_id, _BIG_ID), jnp.where(side == m, side_id, _BIG_ID)),
                    axis=0, keepdims=True)
        vals.append(m)
        sel.append(i)
        hit = top_id == i
        for l in range(k - 1 - it):
            lev[l] = jnp.where(hit, lev[l + 1], lev[l])
        side = jnp.where(side_id == i, NEG_INF, side)
        depth = depth + jnp.where(hit, 1.0, 0.0)
    return jnp.concatenate(vals, axis=0), jnp.concatenate(sel, axis=0).astype(jnp.int32)


def _pick_rows(table, which):
    r = lax.broadcasted_iota(jnp.int32, table.shape, 0)
    rows = []
    for k in range(which.shape[0]):
        rows.append(jnp.sum(jnp.where(r == which[k:k + 1, :], table, 0), axis=0, keepdims=True))
    return jnp.concatenate(rows, axis=0)


def _peer_sel_kernel(h_ref, wq_ref, k1_ref, k2_ref, e_ref, g_ref, r_ref):
    tq = h_ref.shape[0]
    K = PEER_TOPK
    q = jnp.dot(h_ref[...].astype(BF16), wq_ref[...], preferred_element_type=F32).astype(BF16)
    for hd in range(PEER_HEADS):
        base = hd * 2 * PEER_HALF
        s1 = lax.dot_general(k1_ref[hd], q[:, base:base + PEER_HALF], NT_DIMS, preferred_element_type=F32)
        s2 = lax.dot_general(k2_ref[hd], q[:, base + PEER_HALF:base + 2 * PEER_HALF], NT_DIMS,
                             preferred_element_type=F32)
        v1, i1 = _top_scores(s1, K)
        v2, i2 = _top_scores(s2, K)
        top, pos = _top_pair_sums(v1, v2)
        eid = _pick_rows(i1, pos >> 4) * PEER_KEYS + _pick_rows(i2, pos & (K - 1))
        ex = jnp.exp(top - top[0:1, :])
        e_ref[hd * K:(hd + 1) * K, :] = eid * ROW_WORDS
        g_ref[hd * K:(hd + 1) * K, :] = ex / jnp.sum(ex, axis=0, keepdims=True)
    pair = (e_ref[...] >> 1).T
    nsel = pair.shape[1]
    for h in range(STREAM_HALVES):
        r_ref[:, h * nsel:(h + 1) * nsel] = pair + h


def _peer_sel(h2, wq, k1, k2, tq):
    n, d = h2.shape
    rows = PEER_HEADS * PEER_TOPK
    return pl.pallas_call(
        _peer_sel_kernel,
        grid=(n // tq,),
        in_specs=[pl.BlockSpec((tq, d), lambda i: (i, 0)),
                  pl.BlockSpec(wq.shape, lambda i: (0, 0)),
                  pl.BlockSpec(k1.shape, lambda i: (0, 0, 0)),
                  pl.BlockSpec(k2.shape, lambda i: (0, 0, 0))],
        out_specs=[pl.BlockSpec((rows, tq), lambda i: (0, i))] * 2
        + [pl.BlockSpec((tq, rows * STREAM_HALVES), lambda i: (i, 0))],
        out_shape=[jax.ShapeDtypeStruct((rows, n), jnp.int32), jax.ShapeDtypeStruct((rows, n), F32),
                   jax.ShapeDtypeStruct((n, rows * STREAM_HALVES), jnp.int32)],
        compiler_params=_params(("arbitrary",), VMEM_LIMIT),
        name="peer_sel",
    )(h2, wq, k1, k2)


def _split_bf16(x):
    hi = x.astype(BF16)
    return hi, (x - hi.astype(F32)).astype(BF16)


def _pack_kernel(w_ref, o_ref):
    x = w_ref[...]
    eb = x.shape[0]
    for r in range(ROW_WORDS):
        lo = x[:, 2 * r * LANES:(2 * r + 1) * LANES].astype(BF16).astype(F32)
        hi = x[:, (2 * r + 1) * LANES:(2 * r + 2) * LANES].astype(BF16).astype(F32)
        word = (lax.shift_right_logical(pltpu.bitcast(lo, jnp.int32), 16)
                | (pltpu.bitcast(hi, jnp.int32) & jnp.int32(-65536)))
        o_ref[pl.ds(r, eb, stride=ROW_WORDS), :] = word


def _pack_table(w):
    e, d = w.shape
    eb = 512
    return pl.pallas_call(
        _pack_kernel,
        grid=(e // eb,),
        in_specs=[pl.BlockSpec((eb, d), lambda i: (i, 0))],
        out_specs=pl.BlockSpec((eb * ROW_WORDS, LANES), lambda i: (i, 0)),
        out_shape=jax.ShapeDtypeStruct((e * ROW_WORDS, LANES), jnp.int32),
        compiler_params=_params(("arbitrary",)),
        name="pack_table",
    )(w)


def _gather_rows(slots, tab_ref, t, stage_ref):
    for j, slot in enumerate(slots):
        src = pl.ds(pl.multiple_of(slot[t], ROW_WORDS), ROW_WORDS)
        stage_ref[j * ROW_WORDS:(j + 1) * ROW_WORDS, :] = tab_ref[src, :]


def _staged_bf16(stage_ref):
    return pltpu.bitcast(stage_ref[...], BF16)


def _pipelined_tokens(ntok, gather, compute, stages, on_trip=None, after=None):
    nb = len(stages)
    for k in range(nb):
        gather(k, stages[k])

    def trip(i, carry):
        t = nb * i
        if on_trip is not None:
            on_trip(i)
        for k in range(nb):
            compute(t + k, stages[k])
            ahead = t + k + nb
            gather(jnp.minimum(ahead, ntok - 1), stages[k])
            if after is not None:
                after(i, k)
        return carry

    lax.fori_loop(0, ntok // nb, trip, 0)


def _with_slot_indices(idx_hbm, sems, bufs, stride, count, run):
    nsel = len(bufs) // 2
    step = pl.program_id(0)
    last = pl.num_programs(0) - 1

    def copies(block, which):
        return [pltpu.make_async_copy(idx_hbm.at[j, pl.ds(block * stride, count)], bufs[which * nsel + j],
                                      sems.at[which])
                for j in range(nsel)]

    @pl.when(step == 0)
    def _():
        for cp in copies(0, 0):
            cp.start()

    def phase(which):
        for cp in copies(step, which):
            cp.wait()

        @pl.when(step < last)
        def _():
            for cp in copies(step + 1, 1 - which):
                cp.start()

        run(bufs[which * nsel:(which + 1) * nsel])

    for which in range(2):
        pl.when(step % 2 == which)(functools.partial(phase, which))


SC_WINDOW = 128


def _sc_gather_rows(tab, row_ids):
    n = row_ids.shape[0]
    mesh = plsc.VectorSubcoreMesh(core_axis_name="core", subcore_axis_name="subcore")

    @pl.kernel(out_type=jax.ShapeDtypeStruct((n, tab.shape[1]), tab.dtype), mesh=mesh)
    def gather(tab_hbm, ids_hbm, out_hbm):
        def body(ids_vmem, out_vmem):
            pltpu.sync_copy(tab_hbm.at[ids_vmem.at[0]], out_vmem)

        pltpu.emit_pipeline(
            body,
            grid=(n // SC_WINDOW,),
            in_specs=[pl.BlockSpec((1, SC_WINDOW), index_map=lambda i: (0, i))],
            out_specs=[pl.BlockSpec((SC_WINDOW, tab.shape[1]), index_map=lambda i: (i, 0))],
            core_axis_name=("core", "subcore"),
            dimension_semantics=(pltpu.PARALLEL,),
            trace_scopes=False,
        )(ids_hbm, out_hbm)

    return gather(tab, row_ids.reshape(1, n))


def _diag_mask(nsel):
    shape = (SUBLANES, nsel * SUBLANES)
    return (lax.broadcasted_iota(jnp.int32, shape, 1) % SUBLANES) == lax.broadcasted_iota(jnp.int32, shape, 0)


def _token_tile(ref, t):
    row = ref[pl.ds(t, 1), :]
    return jnp.concatenate([row[:, s * LANES:(s + 1) * LANES] for s in range(SUBLANES)], axis=0)


def _peer_u_kernel(idx_hbm, h_ref, gate_ref, grp_ref, tab_ref, act_ref, part_ref, *scratch):
    nsel, tq = gate_ref.shape
    stages, sems, bufs = scratch[:GATHER_STAGES], scratch[GATHER_STAGES], scratch[GATHER_STAGES + 1:]
    diag = _diag_mask(nsel)

    def compute(t, stage_ref):
        h_hi, h_lo = _split_bf16(_token_tile(h_ref, t))
        both = lax.dot_general(jnp.concatenate([h_hi, h_lo], axis=0), _staged_bf16(stage_ref), NT_DIMS,
                               preferred_element_type=F32)
        prod = both[0:SUBLANES] + both[SUBLANES:2 * SUBLANES]
        part_ref[pl.ds(t, 1), :] = jnp.sum(jnp.where(diag, prod, 0.0), axis=0, keepdims=True)

    def run(slots):
        _pipelined_tokens(tq, functools.partial(_gather_rows, slots, tab_ref), compute, stages)

    _with_slot_indices(idx_hbm, sems, bufs, tq, tq, run)
    p_hi, p_lo = _split_bf16(part_ref[...])
    grp = grp_ref[...]
    pre = jnp.dot(p_hi, grp, preferred_element_type=F32) + jnp.dot(p_lo, grp, preferred_element_type=F32)
    act_ref[...] = 0.5 * pre * (1.0 + lax.erf(pre * (2.0 ** -0.5))) * gate_ref[...].T


def _group_matrix(nsel):
    r = jnp.arange(nsel * SUBLANES)[:, None] // SUBLANES
    return (r == jnp.arange(nsel)[None, :]).astype(BF16)


def _gather_scratch(nsel, ntok, nstages):
    return ([pltpu.VMEM((nsel * ROW_WORDS, LANES), jnp.int32)] * nstages
            + [pltpu.SemaphoreType.DMA((2,))] + [pltpu.SMEM((ntok,), jnp.int32)] * (2 * nsel))


def _peer_u(idx_t, h2, gate_t, tab, tq):
    nsel, n = idx_t.shape
    d = h2.shape[1]
    wide = nsel * SUBLANES
    return pl.pallas_call(
        _peer_u_kernel,
        grid=(n // tq,),
        in_specs=[pl.BlockSpec(memory_space=pl.ANY),
                  pl.BlockSpec((tq, d), lambda i: (i, 0)),
                  pl.BlockSpec((nsel, tq), lambda i: (0, i)),
                  pl.BlockSpec((wide, nsel), lambda i: (0, 0)),
                  pl.BlockSpec(memory_space=pltpu.VMEM)],
        out_specs=pl.BlockSpec((tq, nsel), lambda i: (i, 0)),
        out_shape=jax.ShapeDtypeStruct((n, nsel), F32),
        scratch_shapes=[pltpu.VMEM((tq, wide), F32)] + _gather_scratch(nsel, tq, GATHER_STAGES),
        compiler_params=_params(("arbitrary",), GATHER_VMEM_LIMIT),
        name="peer_u",
    )(idx_t, h2, gate_t, _group_matrix(nsel), tab)


def _lane_aligned(count):
    return -(-count // LANES) * LANES


def _streamed_rows_bf16(words):
    x = pltpu.bitcast(words, BF16)
    return jnp.concatenate([x[:, 0:LANES], x[:, LANES:2 * LANES]], axis=0)


def _streamed_copy_matrix(nsel):
    r = jnp.arange(nsel * SUBLANES)
    return ((r[None, :] % (2 * nsel)) // 2 == jnp.arange(nsel)[:, None]).astype(BF16)


def _streamed_diag(nsel):
    shape = (SUBLANES, nsel * SUBLANES)
    r = lax.broadcasted_iota(jnp.int32, shape, 1)
    sub = 4 * ((r % (4 * nsel)) // (2 * nsel)) + 2 * (r // (4 * nsel)) + r % 2
    return lax.broadcasted_iota(jnp.int32, shape, 0) == sub


def _peer_v_kernel(idx_hbm, act_ref, rep_ref, rep2_ref, x1_ref, g2_ref, gpf_ref, tab_ref, rows_hbm, o_ref,
                   wide_ref, y_ref, rowbuf, rsem, *scratch, nvld, nstages):
    tq, nsel = act_ref.shape
    stages, sems, bufs = scratch[:nstages], scratch[nstages], scratch[nstages + 1:]
    trips = nvld // nstages
    per_trip = (tq - nvld) // trips
    rows_tok = nsel * STREAM_HALVES
    step = pl.program_id(0)
    chunks = pl.num_programs(0) * trips
    diag = _diag_mask(nsel)
    diag2 = _streamed_diag(nsel)
    a_hi, a_lo = _split_bf16(act_ref[...])
    for lo, hi, rep in ((0, nvld, rep_ref[...]), (nvld, tq, rep2_ref[...])):
        wide_ref[lo:hi, :] = (jnp.dot(a_hi[lo:hi], rep, preferred_element_type=F32)
                              + jnp.dot(a_lo[lo:hi], rep, preferred_element_type=F32))

    def finish(t, mask, rows_bf16):
        w = jnp.where(mask, jnp.broadcast_to(wide_ref[pl.ds(t, 1), :], mask.shape), 0.0)
        w_hi, w_lo = _split_bf16(w)
        both = jnp.dot(jnp.concatenate([w_hi, w_lo], axis=0), rows_bf16, preferred_element_type=F32)
        tile = both[0:SUBLANES] + both[SUBLANES:2 * SUBLANES]
        y_ref[pl.ds(t, 1), :] = jnp.concatenate([tile[s:s + 1, :] for s in range(SUBLANES)], axis=1)

    def compute(t, stage_ref):
        finish(t, diag, _staged_bf16(stage_ref))

    def chunk_copy(g, slot):
        return pltpu.make_async_copy(rows_hbm.at[pl.ds(g * (per_trip * rows_tok), per_trip * rows_tok), :],
                                     rowbuf.at[slot], rsem.at[slot])

    ring = STREAM_RING

    @pl.when(step == 0)
    def _():
        for g0 in range(ring - 1):
            chunk_copy(g0, g0).start()

    def on_trip(i):
        g = step * trips + i
        chunk_copy(g, i % ring).wait()

        @pl.when(g + ring - 1 < chunks)
        def _():
            chunk_copy(g + ring - 1, (i + ring - 1) % ring).start()

    places = [(d * nstages) // per_trip for d in range(per_trip)]

    def after(i, k):
        for d in range(per_trip):
            if places[d] == k:
                rows = rowbuf.at[i % ring, pl.ds(d * rows_tok, rows_tok), :]
                finish(nvld + i * per_trip + d, diag2, _streamed_rows_bf16(rows[...]))

    def run(slots):
        _pipelined_tokens(nvld, functools.partial(_gather_rows, slots, tab_ref), compute, stages, on_trip, after)

    _with_slot_indices(idx_hbm, sems, bufs, tq, _lane_aligned(nvld), run)
    o_ref[...] = x1_ref[...] + g2_ref[0] * _rms(y_ref[...], gpf_ref[...])


def _peer_v(idx_t, act, x1, gate2, g_post_ffn, tab, rows, seq, tq, nvld, nstages):
    nsel, n = idx_t.shape
    d = x1.shape[1]
    wide = nsel * SUBLANES
    per_b = seq // tq
    row = lambda i: (i, 0)
    const = lambda i: (0, 0)
    trips = nvld // nstages
    per_trip = (tq - nvld) // trips
    assert trips % STREAM_RING == 0 and trips * nstages == nvld and trips * per_trip == tq - nvld
    return pl.pallas_call(
        functools.partial(_peer_v_kernel, nvld=nvld, nstages=nstages),
        grid=(n // tq,),
        in_specs=[pl.BlockSpec(memory_space=pl.ANY),
                  pl.BlockSpec((tq, nsel), row),
                  pl.BlockSpec((nsel, wide), const),
                  pl.BlockSpec((nsel, wide), const),
                  pl.BlockSpec((tq, d), row),
                  pl.BlockSpec((1, 1, d), lambda i: (i // per_b, 0, 0)),
                  pl.BlockSpec((1, d), const),
                  pl.BlockSpec(memory_space=pltpu.VMEM),
                  pl.BlockSpec(memory_space=pl.ANY)],
        out_specs=pl.BlockSpec((tq, d), row),
        out_shape=jax.ShapeDtypeStruct((n, d), F32),
        scratch_shapes=[pltpu.VMEM((tq, wide), F32), pltpu.VMEM((tq, d), F32),
                        pltpu.VMEM((STREAM_RING, per_trip * nsel * STREAM_HALVES, STREAM_ROW), jnp.int32),
                        pltpu.SemaphoreType.DMA((STREAM_RING,))]
        + _gather_scratch(nsel, _lane_aligned(nvld), nstages),
        compiler_params=_params(("arbitrary",), GATHER_VMEM_LIMIT),
        name="peer_v",
    )(idx_t, act, _group_matrix(nsel).T, _streamed_copy_matrix(nsel), x1, gate2, g_post_ffn, tab, rows)


def _dup_heads(w, heads, dh):
    d = w.shape[0]
    return jnp.repeat(w.reshape(d, heads, 1, dh), 2, axis=2).reshape(d, heads * 2 * dh)


def _layer(x2, c, pos_col, bsz, seq, w_mod, b_mod, g_pre_mix, g_post_mix, w_in, conv_w, b_igate, b_fgate,
           mlstm_norm_g, att_sinks, w_out, g_pre_ffn, g_post_ffn, peer_wq, peer_keys1, peer_keys2, peer_u, peer_v):
    n, d = x2.shape
    tm = min(seq, 512)
    mod = _mod(c, w_mod, b_mod)
    shift1, scale1, gate1, shift2, scale2, gate2 = [m.reshape(bsz, 1, d) for m in jnp.split(mod, 6, axis=-1)]

    aw = ATT_HEADS * ATT_HEAD_DIM
    kvw = ATT_KV_HEADS * ATT_HEAD_DIM
    qkw = MLSTM_HEADS * MLSTM_QK_DIM
    mw = MLSTM_HEADS * MLSTM_V_DIM
    o = 0
    wq_a = w_in[:, o:o + aw]; o += aw
    wk_a = w_in[:, o:o + kvw]; o += kvw
    wv_a = w_in[:, o:o + kvw]; o += kvw
    w_mqk = w_in[:, o:o + 2 * qkw]; o += 2 * qkw
    w_mv = w_in[:, o:o + mw]; o += mw
    w_g = w_in[:, o:o + 2 * MLSTM_HEADS]; o += 2 * MLSTM_HEADS
    w_mo = w_in[:, o:o + mw]
    w_gp = jnp.pad(w_g, ((0, 0), (0, LANES - 2 * MLSTM_HEADS)))
    w_all = jnp.concatenate([wq_a, _dup_heads(wk_a, ATT_KV_HEADS, ATT_HEAD_DIM),
                             _dup_heads(wv_a, ATT_KV_HEADS, ATT_HEAD_DIM), w_mqk, w_mv, w_mo, w_gp],
                            axis=1).astype(BF16)

    cos, sin = _rope_tab(pos_col)
    q, kd, vd, mqk, mv, mo, gts = _in_proj(x2, scale1, shift1, g_pre_mix.reshape(1, d), cos, sin, w_all, seq, tm)
    att = _swa(att_sinks, q, kd, vd, bsz, seq)
    gate_bias = jnp.pad(jnp.concatenate([b_igate, b_fgate]), (0, LANES - 2 * MLSTM_HEADS)).reshape(1, LANES)
    chunks = min(seq // MLSTM_CHUNK, 8)
    mh = _mlstm(mqk, mv, mo, gts, conv_w, gate_bias, mlstm_norm_g.reshape(1, mw), bsz, seq, chunks,
                MLSTM_GROUP if bsz % MLSTM_GROUP == 0 else 1)
    wo = w_out.astype(BF16)
    x1, h2 = _out_proj(att, mh, x2, gate1, scale2, shift2, g_post_mix.reshape(1, d), g_pre_ffn.reshape(1, d),
                       wo[:aw], wo[aw:], seq, tm)

    tq = min(n, 256)
    eid_t, gate_t, row_ids = _peer_sel(h2, peer_wq.astype(BF16), peer_keys1.astype(BF16),
                                       peer_keys2.astype(BF16), tq)
    tg = min(seq, GATHER_TOKENS)
    tab_v = _pack_table(peer_v)
    nvld = tg * STREAM_SPLIT[0] // sum(STREAM_SPLIT)
    streamed = row_ids.reshape(n // tg, tg, row_ids.shape[1])[:, nvld:, :]
    rows_v = _sc_gather_rows(tab_v.reshape(-1, STREAM_ROW), streamed.reshape(-1))
    act = _peer_u(eid_t, h2, gate_t, _pack_table(peer_u), tg)
    return _peer_v(eid_t, act, x1, gate2, g_post_ffn.reshape(1, d), tab_v, rows_v, seq, tg, nvld, STREAM_STAGES)


def kernel(x, c, positions, w_mod, b_mod, g_pre_mix, g_post_mix, w_in, conv_w, b_igate, b_fgate, mlstm_norm_g, att_sinks, w_out, g_pre_ffn, g_post_ffn, peer_wq, peer_keys1, peer_keys2, peer_u, peer_v):
    bsz, seq, d = x.shape
    n = bsz * seq
    x2 = x.reshape(n, d)
    pos_col = positions.reshape(n, 1)
    for l in range(w_mod.shape[0]):
        x2 = _layer(x2, c, pos_col, bsz, seq, w_mod[l], b_mod[l], g_pre_mix[l], g_post_mix[l], w_in[l], conv_w[l],
                    b_igate[l], b_fgate[l], mlstm_norm_g[l], att_sinks[l], w_out[l], g_pre_ffn[l], g_post_ffn[l],
                    peer_wq[l], peer_keys1[l], peer_keys2[l], peer_u[l], peer_v[l])
    return x2.reshape(bsz, seq, d)
```

```python
import functools

import jax
import jax.numpy as jnp
from jax import lax
from jax.experimental import pallas as pl
from jax.experimental.pallas import tpu as pltpu
from jax.experimental.pallas import tpu_sc as plsc

F32 = jnp.float32
BF16 = jnp.bfloat16

ATT_HEADS = 8
ATT_KV_HEADS = 2
ATT_HEAD_DIM = 64
ATT_BLOCK = 128
ROPE_THETA = 10000.0
MLSTM_HEADS = 4
MLSTM_V_DIM = 128
MLSTM_QK_DIM = 64
MLSTM_CHUNK = 64
CONV_WIDTH = 4
PEER_HEADS = 8
PEER_KEYS = 128
PEER_HALF = 128
PEER_TOPK = 16
NORM_EPS = 1e-6

LANES = 128
SUBLANES = 8
VMEM_LIMIT = 52 * 1024 * 1024
GATHER_STAGES = 16
ROW_WORDS = SUBLANES // 2
MLSTM_GROUP = 1
PEER_SEGMENTS = 4
STREAM_FIRST = (256, 8)
STREAM_LATER = (320, 10)
STREAM_RING = 4
STREAM_HALVES = 2
STREAM_ROW = 2 * LANES
GATHER_TOKENS = 512
GATHER_VMEM_LIMIT = 58 * 1024 * 1024

NEG_INF = float("-inf")
NT_DIMS = (((1,), (1,)), ((), ()))
TN_DIMS = (((0,), (0,)), ((), ()))


def _params(sem, vmem=None):
    return pltpu.CompilerParams(dimension_semantics=sem, vmem_limit_bytes=vmem)


def _rms(x, g):
    return x * lax.rsqrt(jnp.mean(x * x, axis=-1, keepdims=True) + NORM_EPS) * g


def _mod_kernel(c_ref, w_ref, b_ref, o_ref):
    o_ref[...] = jnp.dot(c_ref[...], w_ref[...], preferred_element_type=F32,
                         precision=lax.Precision.HIGHEST) + b_ref[...]


def _mod(c, w, b):
    bsz, d = c.shape
    nout = w.shape[1]
    return pl.pallas_call(
        _mod_kernel,
        grid=(nout // d,),
        in_specs=[pl.BlockSpec((bsz, d), lambda i: (0, 0)),
                  pl.BlockSpec((d, d), lambda i: (0, i)),
                  pl.BlockSpec((1, d), lambda i: (0, i))],
        out_specs=pl.BlockSpec((bsz, d), lambda i: (0, i)),
        out_shape=jax.ShapeDtypeStruct((bsz, nout), F32),
        compiler_params=_params(("arbitrary",)),
        name="mod",
    )(c, w, b.reshape(1, nout))


def _rope_tab_kernel(pos_ref, inv_ref, sign_ref, cos_ref, sin_ref):
    ang = pos_ref[...].astype(F32) * inv_ref[...]
    cos_ref[...] = jnp.cos(ang)
    sin_ref[...] = jnp.sin(ang) * sign_ref[...]


def _rope_tab(pos_col):
    n = pos_col.shape[0]
    tr = min(n, 1024)
    half = ATT_HEAD_DIM // 2
    inv = ROPE_THETA ** (-jnp.arange(0, ATT_HEAD_DIM, 2, dtype=F32) / ATT_HEAD_DIM)
    inv_row = jnp.tile(inv, LANES // half).reshape(1, LANES)
    lane = jnp.arange(LANES)
    sign_row = jnp.where((lane % ATT_HEAD_DIM) < half, -1.0, 1.0).astype(F32).reshape(1, LANES)
    return pl.pallas_call(
        _rope_tab_kernel,
        grid=(n // tr,),
        in_specs=[pl.BlockSpec((tr, 1), lambda i: (i, 0)),
                  pl.BlockSpec((1, LANES), lambda i: (0, 0)),
                  pl.BlockSpec((1, LANES), lambda i: (0, 0))],
        out_specs=[pl.BlockSpec((tr, LANES), lambda i: (i, 0))] * 2,
        out_shape=[jax.ShapeDtypeStruct((n, LANES), F32)] * 2,
        compiler_params=_params(("arbitrary",)),
        name="rope_tab",
    )(pos_col, inv_row, sign_row)


def _rope(v, cos, sin):
    half = ATT_HEAD_DIM // 2
    lane = lax.broadcasted_iota(jnp.int32, cos.shape, 1)
    first = (lane % ATT_HEAD_DIM) < half
    outs = []
    for j in range(v.shape[1] // LANES):
        c = v[:, j * LANES:(j + 1) * LANES]
        rot = jnp.where(first, pltpu.roll(c, LANES - half, 1), pltpu.roll(c, half, 1))
        outs.append(c * cos + rot * sin)
    return jnp.concatenate(outs, axis=1)


_C_Q, _C_K, _C_V, _C_MQK, _C_MV, _C_MO, _C_G, _C_END = 0, 512, 768, 1024, 1536, 2048, 2560, 2688


def _in_proj_kernel(x_ref, sc_ref, sh_ref, g_ref, cos_ref, sin_ref, w_ref,
                    q_ref, k_ref, v_ref, mqk_ref, mv_ref, mo_ref, gt_ref):
    x = x_ref[...]
    h = _rms(x, g_ref[...]) * (1.0 + sc_ref[0]) + sh_ref[0]
    hb = h.astype(BF16)

    def mm(a, b):
        return jnp.dot(hb, w_ref[:, a:b], preferred_element_type=F32)

    cos = cos_ref[...]
    sin = sin_ref[...]
    q_ref[...] = (_rope(mm(_C_Q, _C_K), cos, sin) * (ATT_HEAD_DIM ** -0.5)).astype(BF16)
    k_ref[...] = _rope(mm(_C_K, _C_V), cos, sin).astype(BF16)
    v_ref[...] = mm(_C_V, _C_MQK).astype(BF16)
    mqk_ref[...] = mm(_C_MQK, _C_MV)
    mv_ref[...] = mm(_C_MV, _C_MO).astype(BF16)
    mo_ref[...] = mm(_C_MO, _C_G)
    gt_ref[...] = mm(_C_G, _C_END)


def _in_proj(x2, scale1, shift1, g_pre, cos, sin, w_all, seq, tm):
    n, d = x2.shape
    per_b = seq // tm
    row = lambda i: (i, 0)
    bsel = lambda i: (i // per_b, 0, 0)
    widths = (512, 256, 256, 512, 512, 512, 128)
    dtypes = (BF16, BF16, BF16, F32, BF16, F32, F32)
    return pl.pallas_call(
        _in_proj_kernel,
        grid=(n // tm,),
        in_specs=[pl.BlockSpec((tm, d), row),
                  pl.BlockSpec((1, 1, d), bsel),
                  pl.BlockSpec((1, 1, d), bsel),
                  pl.BlockSpec((1, d), lambda i: (0, 0)),
                  pl.BlockSpec((tm, LANES), row),
                  pl.BlockSpec((tm, LANES), row),
                  pl.BlockSpec((d, _C_END), lambda i: (0, 0))],
        out_specs=[pl.BlockSpec((tm, w), row) for w in widths],
        out_shape=[jax.ShapeDtypeStruct((n, w), dt) for w, dt in zip(widths, dtypes)],
        compiler_params=_params(("arbitrary",), VMEM_LIMIT),
        name="in_proj",
    )(x2, scale1, shift1, g_pre, cos, sin, w_all)


def _swa_kernel(sink_ref, q_ref, kp_ref, kc_ref, vp_ref, vc_ref, o_ref):
    blk = ATT_BLOCK
    n = pl.program_id(1)
    qi = lax.broadcasted_iota(jnp.int32, (blk, 2 * blk), 0)
    si = lax.broadcasted_iota(jnp.int32, (blk, 2 * blk), 1)
    delta = qi + blk - si
    valid = (delta >= 0) & (delta < blk) & ((si >= blk) | (n > 0))
    lo = lax.broadcasted_iota(jnp.int32, (2 * blk, LANES), 1) < ATT_HEAD_DIM
    group = ATT_HEADS // ATT_KV_HEADS
    for g in range(ATT_KV_HEADS):
        cs = slice(g * LANES, (g + 1) * LANES)
        k = jnp.concatenate([kp_ref[:, cs], kc_ref[:, cs]], axis=0)
        v = jnp.concatenate([vp_ref[:, cs], vc_ref[:, cs]], axis=0)
        zero = jnp.zeros_like(k)
        halves = ((jnp.where(lo, k, zero), jnp.where(lo, v, zero)),
                  (jnp.where(lo, zero, k), jnp.where(lo, zero, v)))
        for jj in range(group // 2):
            p = g * (group // 2) + jj
            q2 = q_ref[:, p * LANES:(p + 1) * LANES]
            acc = jnp.zeros((blk, LANES), F32)
            for half, (kh, vh) in enumerate(halves):
                s = lax.dot_general(q2, kh, NT_DIMS, preferred_element_type=F32)
                s = jnp.where(valid, s, NEG_INF)
                sink = sink_ref[2 * p + half]
                m = jnp.maximum(jnp.max(s, axis=-1, keepdims=True), sink)
                e = jnp.exp(s - m)
                den = jnp.sum(e, axis=-1, keepdims=True) + jnp.exp(sink - m)
                acc = acc + jnp.dot((e / den).astype(BF16), vh, preferred_element_type=F32)
            o_ref[:, p * LANES:(p + 1) * LANES] = acc.astype(BF16)


def _swa(sinks, q, kd, vd, bsz, seq):
    n = q.shape[0]
    nb = seq // ATT_BLOCK
    cur = lambda b, i: (b * nb + i, 0)
    prev = lambda b, i: (b * nb + jnp.maximum(i - 1, 0), 0)
    return pl.pallas_call(
        _swa_kernel,
        grid=(bsz, nb),
        in_specs=[pl.BlockSpec(memory_space=pltpu.SMEM),
                  pl.BlockSpec((ATT_BLOCK, 512), cur),
                  pl.BlockSpec((ATT_BLOCK, 256), prev),
                  pl.BlockSpec((ATT_BLOCK, 256), cur),
                  pl.BlockSpec((ATT_BLOCK, 256), prev),
                  pl.BlockSpec((ATT_BLOCK, 256), cur)],
        out_specs=pl.BlockSpec((ATT_BLOCK, 512), cur),
        out_shape=jax.ShapeDtypeStruct((n, 512), BF16),
        compiler_params=_params(("arbitrary", "arbitrary")),
        name="swa",
    )(sinks, q, kd, kd, vd, vd)


def _mlstm_kernel(mqk_all, mv_all, mo_all, gt_all, cw_ref, gb_ref, ng_ref, o_all,
                  tail_all, qk_all, xs_all, ct_all, n_all, m_all, *, chunks, group):
    @pl.when(pl.program_id(1) == 0)
    def _():
        for ref in (tail_all, ct_all, n_all, m_all):
            ref[...] = jnp.zeros_like(ref)

    bodies = [_mlstm_sequence(*(r.at[g] for r in (mqk_all, mv_all, mo_all, gt_all, o_all, tail_all, qk_all,
                                                   xs_all, ct_all, n_all, m_all)),
                              cw_ref, gb_ref, ng_ref, chunks) for g in range(group)]

    def chunk(c, carry):
        for body in bodies:
            body(c, carry)
        return carry

    lax.fori_loop(0, chunks, chunk, 0)


def _mlstm_sequence(mqk_ref, mv_ref, mo_ref, gt_ref, o_ref, tail_ref, qk_ref, xs_ref, ct_ref, n_ref, m_ref,
                    cw_ref, gb_ref, ng_ref, chunks):
    L = MLSTM_CHUNK
    tm = chunks * L
    nqk = MLSTM_HEADS * MLSTM_QK_DIM

    cur = mqk_ref[...]
    full = jnp.concatenate([tail_ref[...], cur], axis=0)
    off = SUBLANES - (CONV_WIDTH - 1)
    acc = full[off:off + tm] * cw_ref[0:1, :]
    for j in range(1, CONV_WIDTH):
        acc = acc + full[off + j:off + j + tm] * cw_ref[j:j + 1, :]
    act = acc * jax.nn.sigmoid(acc)
    col = lax.broadcasted_iota(jnp.int32, (1, 2 * nqk), 1)
    act = act * jnp.where(col < nqk, MLSTM_QK_DIM ** -0.5, 1.0)
    qk_ref[...] = act.astype(BF16)
    tail_ref[...] = cur[tm - SUBLANES:tm]

    lane = lax.broadcasted_iota(jnp.int32, (tm, LANES), 1)
    gts = gt_ref[...] + gb_ref[...]
    logsig = jnp.minimum(gts, 0.0) - jnp.log(1.0 + jnp.exp(-jnp.abs(gts)))
    xs_ref[...] = jnp.where(lane < MLSTM_HEADS, gts, jnp.where(lane < 2 * MLSTM_HEADS, logsig, 0.0))

    ri = lax.broadcasted_iota(jnp.int32, (L, L), 0)
    ci = lax.broadcasted_iota(jnp.int32, (L, L), 1)
    causal = ci <= ri
    tril = causal.astype(F32)
    lane_l = lax.broadcasted_iota(jnp.int32, (L, LANES), 1)
    lo_l = lane_l < MLSTM_QK_DIM
    row_c = lax.broadcasted_iota(jnp.int32, (LANES, 1), 0) < MLSTM_QK_DIM
    lane_1 = lax.broadcasted_iota(jnp.int32, (1, LANES), 1) < MLSTM_QK_DIM

    def chunk(c, carry):
        r0 = pl.multiple_of(c * L, L)
        rows = pl.ds(r0, L)
        xc = xs_ref[rows, :]
        bc = jnp.dot(tril, xc, preferred_element_type=F32, precision=lax.Precision.HIGHEST)
        x2 = jnp.where(lane_l < MLSTM_HEADS, xc, bc)
        xt = x2.T
        for p in range(MLSTM_HEADS // 2):
            q2 = qk_ref[rows, p * LANES:(p + 1) * LANES]
            k2 = qk_ref[rows, nqk + p * LANES:nqk + (p + 1) * LANES]
            ct_old = ct_ref[p]
            ctb = ct_old.astype(BF16)
            n2 = n_ref[p:p + 1, :]
            decs, upds, kws = [], [], []
            for half in range(2):
                h = 2 * p + half
                hm = lo_l if half == 0 else jnp.logical_not(lo_l)
                zero = jnp.zeros_like(q2)
                qm = jnp.where(hm, q2, zero)
                km = jnp.where(hm, k2, zero)
                v = mv_ref[rows, h * LANES:(h + 1) * LANES]
                b_col = x2[:, MLSTM_HEADS + h:MLSTM_HEADS + h + 1]
                ig_col = x2[:, h:h + 1]
                b_row = xt[MLSTM_HEADS + h:MLSTM_HEADS + h + 1, :]
                ig_row = xt[h:h + 1, :]
                m_prev = m_ref[h:h + 1, 0:1]
                dlog = jnp.where(causal, b_col - b_row + ig_row, NEG_INF)
                m_inter = b_col + m_prev
                m_t = jnp.maximum(m_inter, jnp.max(dlog, axis=-1, keepdims=True))
                w_intra = jnp.exp(dlog - m_t)
                a_inter = jnp.exp(m_inter - m_t)
                s = lax.dot_general(q2, km, NT_DIMS, preferred_element_type=F32) * w_intra
                num = (jnp.dot(s.astype(BF16), v, preferred_element_type=F32)
                       + a_inter * jnp.dot(qm, ctb, preferred_element_type=F32))
                den = (jnp.sum(s, axis=-1, keepdims=True)
                       + a_inter * jnp.sum(qm.astype(F32) * n2, axis=-1, keepdims=True))
                hh = num / jnp.maximum(jnp.abs(den), jnp.exp(-m_t))
                y = _rms(hh, ng_ref[:, h * LANES:(h + 1) * LANES])
                y = y * jax.nn.sigmoid(mo_ref[rows, h * LANES:(h + 1) * LANES])
                o_ref[rows, h * LANES:(h + 1) * LANES] = y.astype(BF16)
                b_last = xt[MLSTM_HEADS + h:MLSTM_HEADS + h + 1, L - 1:L]
                g_col = b_last - b_col + ig_col
                m_new = jnp.maximum(b_last + m_prev, jnp.max(g_col, axis=0, keepdims=True))
                kw = km.astype(F32) * jnp.exp(g_col - m_new)
                decs.append(jnp.exp(b_last + m_prev - m_new))
                kws.append(kw)
                upds.append(lax.dot_general(kw.astype(BF16), v, TN_DIMS, preferred_element_type=F32))
                m_ref[h:h + 1, :] = jnp.broadcast_to(m_new, (1, LANES))
            ct_ref[p] = ct_old * jnp.where(row_c, decs[0], decs[1]) + upds[0] + upds[1]
            n_ref[p:p + 1, :] = (n2 * jnp.where(lane_1, decs[0], decs[1])
                                 + jnp.sum(kws[0] + kws[1], axis=0, keepdims=True))
        return carry

    return chunk


def _mlstm(mqk, mv, mo, gts, conv_w, gate_bias, norm_g, bsz, seq, chunks, group):
    n = mqk.shape[0]
    tm = chunks * MLSTM_CHUNK
    steps = seq // tm
    row = lambda b, i: (b, i, 0)
    const = lambda b, i: (0, 0)
    width = MLSTM_HEADS * MLSTM_V_DIM
    per_seq = lambda a: a.reshape(bsz, seq, a.shape[1])
    out = pl.pallas_call(
        functools.partial(_mlstm_kernel, chunks=chunks, group=group),
        grid=(bsz // group, steps),
        in_specs=[pl.BlockSpec((group, tm, width), row),
                  pl.BlockSpec((group, tm, width), row),
                  pl.BlockSpec((group, tm, width), row),
                  pl.BlockSpec((group, tm, LANES), row),
                  pl.BlockSpec((CONV_WIDTH, width), const),
                  pl.BlockSpec((1, LANES), const),
                  pl.BlockSpec((1, width), const)],
        out_specs=pl.BlockSpec((group, tm, width), row),
        out_shape=jax.ShapeDtypeStruct((bsz, seq, width), BF16),
        scratch_shapes=[pltpu.VMEM((group, SUBLANES, width), F32),
                        pltpu.VMEM((group, tm, width), BF16),
                        pltpu.VMEM((group, tm, LANES), F32),
                        pltpu.VMEM((group, MLSTM_HEADS // 2, LANES, LANES), F32),
                        pltpu.VMEM((group, SUBLANES, LANES), F32),
                        pltpu.VMEM((group, SUBLANES, LANES), F32)],
        compiler_params=_params(("arbitrary", "arbitrary")),
        name="mlstm",
    )(per_seq(mqk), per_seq(mv), per_seq(mo), per_seq(gts), conv_w, gate_bias, norm_g)
    return out.reshape(n, width)


def _out_proj_kernel(att_ref, mh_ref, x_ref, g1_ref, sc_ref, sh_ref, gpm_ref, gpf_ref, wa_ref, wb_ref,
                     x1_ref, h2_ref):
    mix = (jnp.dot(att_ref[...], wa_ref[...], preferred_element_type=F32)
           + jnp.dot(mh_ref[...], wb_ref[...], preferred_element_type=F32))
    x1 = x_ref[...] + g1_ref[0] * _rms(mix, gpm_ref[...])
    x1_ref[...] = x1
    h2_ref[...] = _rms(x1, gpf_ref[...]) * (1.0 + sc_ref[0]) + sh_ref[0]


def _out_proj(att, mh, x2, gate1, scale2, shift2, g_post_mix, g_pre_ffn, wa, wb, seq, tm):
    n, d = x2.shape
    per_b = seq // tm
    row = lambda i: (i, 0)
    bsel = lambda i: (i // per_b, 0, 0)
    const = lambda i: (0, 0)
    half = att.shape[1]
    return pl.pallas_call(
        _out_proj_kernel,
        grid=(n // tm,),
        in_specs=[pl.BlockSpec((tm, half), row), pl.BlockSpec((tm, half), row), pl.BlockSpec((tm, d), row),
                  pl.BlockSpec((1, 1, d), bsel), pl.BlockSpec((1, 1, d), bsel), pl.BlockSpec((1, 1, d), bsel),
                  pl.BlockSpec((1, d), const), pl.BlockSpec((1, d), const),
                  pl.BlockSpec((half, d), const), pl.BlockSpec((half, d), const)],
        out_specs=[pl.BlockSpec((tm, d), row)] * 2,
        out_shape=[jax.ShapeDtypeStruct((n, d), F32)] * 2,
        compiler_params=_params(("arbitrary",), VMEM_LIMIT),
        name="out_proj",
    )(att, mh, x2, gate1, scale2, shift2, g_post_mix, g_pre_ffn, wa, wb)


_BIG_ID = float(2 ** 30)
SORT_LEVELS = 8


def _top_scores(s, k):
    rows, t = s.shape
    span = SORT_LEVELS * SUBLANES
    r = lax.broadcasted_iota(jnp.int32, (rows // SORT_LEVELS, t), 0)
    col_id = ((r // SUBLANES) * span + r % SUBLANES).astype(F32)
    lev = [jnp.concatenate([s[g * span + l * SUBLANES:g * span + (l + 1) * SUBLANES] for g in range(rows // span)],
                           axis=0) for l in range(SORT_LEVELS)]
    ids = [col_id + float(l * SUBLANES) for l in range(SORT_LEVELS)]
    for rnd in range(SORT_LEVELS):
        for a in range(rnd % 2, SORT_LEVELS - 1, 2):
            swap = lev[a + 1] > lev[a]
            lev[a], lev[a + 1] = jnp.where(swap, lev[a + 1], lev[a]), jnp.where(swap, lev[a], lev[a + 1])
            ids[a], ids[a + 1] = jnp.where(swap, ids[a + 1], ids[a]), jnp.where(swap, ids[a], ids[a + 1])
    vals, sel = [], []
    for _ in range(k):
        m = jnp.max(lev[0], axis=0, keepdims=True)
        i = jnp.min(jnp.where(lev[0] == m, ids[0], _BIG_ID), axis=0, keepdims=True)
        vals.append(m)
        sel.append(i)
        hit = ids[0] == i
        for l in range(SORT_LEVELS - 1):
            lev[l] = jnp.where(hit, lev[l + 1], lev[l])
            ids[l] = jnp.where(hit, ids[l + 1], ids[l])
        lev[-1] = jnp.where(hit, NEG_INF, lev[-1])
    return jnp.concatenate(vals, axis=0), jnp.concatenate(sel, axis=0).astype(jnp.int32)


def _top_pair_sums(v1, v2):
    k, t = v1.shape
    half = SUBLANES
    lev = [v1[0:half] + v2[b:b + 1, :] for b in range(k)]
    side = v1[half:k] + v2[0:1, :]
    a_low = lax.broadcasted_iota(jnp.int32, (half, t), 0).astype(F32) * float(k)
    side_id = a_low + float(half * k)
    depth = jnp.zeros((half, t), F32)
    vals, sel = [], []
    for it in range(k):
        top_id = a_low + depth
        m = jnp.max(jnp.maximum(lev[0], side), axis=0, keepdims=True)
        i = jnp.min(jnp.minimum(jnp.where(lev[0] == m, top_id, _BIG_ID), jnp.where(side == m, side_id, _BIG_ID)),
                    axis=0, keepdims=True)
        vals.append(m)
        sel.append(i)
        hit = top_id == i
        for l in range(k - 1 - it):
            lev[l] = jnp.where(hit, lev[l + 1], lev[l])
        side = jnp.where(side_id == i, NEG_INF, side)
        depth = depth + jnp.where(hit, 1.0, 0.0)
    return jnp.concatenate(vals, axis=0), jnp.concatenate(sel, axis=0).astype(jnp.int32)


def _pick_rows(table, which):
    r = lax.broadcasted_iota(jnp.int32, table.shape, 0)
    rows = []
    for k in range(which.shape[0]):
        rows.append(jnp.sum(jnp.where(r == which[k:k + 1, :], table, 0), axis=0, keepdims=True))
    return jnp.concatenate(rows, axis=0)


def _peer_sel_kernel(h_ref, wq_ref, k1_ref, k2_ref, e_ref, g_ref, r_ref):
    tq = h_ref.shape[0]
    K = PEER_TOPK
    q = jnp.dot(h_ref[...].astype(BF16), wq_ref[...], preferred_element_type=F32).astype(BF16)
    for hd in range(PEER_HEADS):
        base = hd * 2 * PEER_HALF
        s1 = lax.dot_general(k1_ref[hd], q[:, base:base + PEER_HALF], NT_DIMS, preferred_element_type=F32)
        s2 = lax.dot_general(k2_ref[hd], q[:, base + PEER_HALF:base + 2 * PEER_HALF], NT_DIMS,
                             preferred_element_type=F32)
        v1, i1 = _top_scores(s1, K)
        v2, i2 = _top_scores(s2, K)
        top, pos = _top_pair_sums(v1, v2)
        eid = _pick_rows(i1, pos >> 4) * PEER_KEYS + _pick_rows(i2, pos & (K - 1))
        ex = jnp.exp(top - top[0:1, :])
        e_ref[hd * K:(hd + 1) * K, :] = eid * ROW_WORDS
        g_ref[hd * K:(hd + 1) * K, :] = ex / jnp.sum(ex, axis=0, keepdims=True)
    pair = (e_ref[...] >> 1).T
    nsel = pair.shape[1]
    for h in range(STREAM_HALVES):
        r_ref[:, h * nsel:(h + 1) * nsel] = pair + h


def _peer_sel(h2, wq, k1, k2, tq):
    n, d = h2.shape
    rows = PEER_HEADS * PEER_TOPK
    return pl.pallas_call(
        _peer_sel_kernel,
        grid=(n // tq,),
        in_specs=[pl.BlockSpec((tq, d), lambda i: (i, 0)),
                  pl.BlockSpec(wq.shape, lambda i: (0, 0)),
                  pl.BlockSpec(k1.shape, lambda i: (0, 0, 0)),
                  pl.BlockSpec(k2.shape, lambda i: (0, 0, 0))],
        out_specs=[pl.BlockSpec((rows, tq), lambda i: (0, i))] * 2
        + [pl.BlockSpec((tq, rows * STREAM_HALVES), lambda i: (i, 0))],
        out_shape=[jax.ShapeDtypeStruct((rows, n), jnp.int32), jax.ShapeDtypeStruct((rows, n), F32),
                   jax.ShapeDtypeStruct((n, rows * STREAM_HALVES), jnp.int32)],
        compiler_params=_params(("arbitrary",), VMEM_LIMIT),
        name="peer_sel",
    )(h2, wq, k1, k2)


def _split_bf16(x):
    hi = x.astype(BF16)
    return hi, (x - hi.astype(F32)).astype(BF16)


def _pack_kernel(w_ref, o_ref):
    x = w_ref[...]
    eb = x.shape[0]
    for r in range(ROW_WORDS):
        lo = x[:, 2 * r * LANES:(2 * r + 1) * LANES].astype(BF16).astype(F32)
        hi = x[:, (2 * r + 1) * LANES:(2 * r + 2) * LANES].astype(BF16).astype(F32)
        word = (lax.shift_right_logical(pltpu.bitcast(lo, jnp.int32), 16)
                | (pltpu.bitcast(hi, jnp.int32) & jnp.int32(-65536)))
        o_ref[pl.ds(r, eb, stride=ROW_WORDS), :] = word


def _pack_table(w):
    e, d = w.shape
    eb = 512
    return pl.pallas_call(
        _pack_kernel,
        grid=(e // eb,),
        in_specs=[pl.BlockSpec((eb, d), lambda i: (i, 0))],
        out_specs=pl.BlockSpec((eb * ROW_WORDS, LANES), lambda i: (i, 0)),
        out_shape=jax.ShapeDtypeStruct((e * ROW_WORDS, LANES), jnp.int32),
        compiler_params=_params(("arbitrary",)),
        name="pack_table",
    )(w)


def _gather_rows(slots, tab_ref, t, stage_ref):
    for j, slot in enumerate(slots):
        src = pl.ds(pl.multiple_of(slot[t], ROW_WORDS), ROW_WORDS)
        stage_ref[j * ROW_WORDS:(j + 1) * ROW_WORDS, :] = tab_ref[src, :]


def _staged_bf16(stage_ref):
    return pltpu.bitcast(stage_ref[...], BF16)


def _pipelined_tokens(ntok, gather, compute, stages, on_trip=None, after=None):
    nb = len(stages)
    for k in range(nb):
        gather(k, stages[k])

    def trip(i, carry):
        t = nb * i
        if on_trip is not None:
            on_trip(i)
        for k in range(nb):
            compute(t + k, stages[k])
            ahead = t + k + nb
            gather(jnp.minimum(ahead, ntok - 1), stages[k])
            if after is not None:
                after(i, k)
        return carry

    lax.fori_loop(0, ntok // nb, trip, 0)


def _with_slot_indices(idx_hbm, sems, bufs, stride, count, first_block, run):
    nsel = len(bufs) // 2
    step = pl.program_id(0)
    last = pl.num_programs(0) - 1

    def copies(block, which):
        return [pltpu.make_async_copy(idx_hbm.at[j, pl.ds((first_block + block) * stride, count)],
                                      bufs[which * nsel + j], sems.at[which])
                for j in range(nsel)]

    @pl.when(step == 0)
    def _():
        for cp in copies(0, 0):
            cp.start()

    def phase(which):
        for cp in copies(step, which):
            cp.wait()

        @pl.when(step < last)
        def _():
            for cp in copies(step + 1, 1 - which):
                cp.start()

        run(bufs[which * nsel:(which + 1) * nsel])

    for which in range(2):
        pl.when(step % 2 == which)(functools.partial(phase, which))


SC_WINDOW = 128


def _sc_gather_rows(tab, row_ids):
    n = row_ids.shape[0]
    mesh = plsc.VectorSubcoreMesh(core_axis_name="core", subcore_axis_name="subcore")

    @pl.kernel(out_type=jax.ShapeDtypeStruct((n, tab.shape[1]), tab.dtype), mesh=mesh)
    def gather(tab_hbm, ids_hbm, out_hbm):
        def body(ids_vmem, out_vmem):
            pltpu.sync_copy(tab_hbm.at[ids_vmem.at[0]], out_vmem)

        pltpu.emit_pipeline(
            body,
            grid=(n // SC_WINDOW,),
            in_specs=[pl.BlockSpec((1, SC_WINDOW), index_map=lambda i: (0, i))],
            out_specs=[pl.BlockSpec((SC_WINDOW, tab.shape[1]), index_map=lambda i: (i, 0))],
            core_axis_name=("core", "subcore"),
            dimension_semantics=(pltpu.PARALLEL,),
            trace_scopes=False,
        )(ids_hbm, out_hbm)

    return gather(tab, row_ids.reshape(1, n))


def _diag_mask(nsel):
    shape = (SUBLANES, nsel * SUBLANES)
    return (lax.broadcasted_iota(jnp.int32, shape, 1) % SUBLANES) == lax.broadcasted_iota(jnp.int32, shape, 0)


def _token_tile(ref, t):
    row = ref[pl.ds(t, 1), :]
    return jnp.concatenate([row[:, s * LANES:(s + 1) * LANES] for s in range(SUBLANES)], axis=0)


def _group_matrix(nsel):
    r = jnp.arange(nsel * SUBLANES)[:, None] // SUBLANES
    return (r == jnp.arange(nsel)[None, :]).astype(BF16)


def _lane_aligned(count):
    return -(-count // LANES) * LANES


def _streamed_rows_bf16(words):
    x = pltpu.bitcast(words, BF16)
    return jnp.concatenate([x[:, 0:LANES], x[:, LANES:2 * LANES]], axis=0)


def _streamed_copy_matrix(nsel):
    r = jnp.arange(nsel * SUBLANES)
    return ((r[None, :] % (2 * nsel)) // 2 == jnp.arange(nsel)[:, None]).astype(BF16)


def _streamed_diag(nsel):
    shape = (SUBLANES, nsel * SUBLANES)
    r = lax.broadcasted_iota(jnp.int32, shape, 1)
    sub = 4 * ((r % (4 * nsel)) // (2 * nsel)) + 2 * (r // (4 * nsel)) + r % 2
    return lax.broadcasted_iota(jnp.int32, shape, 0) == sub


def _split_counts(tq, nvld, nstages):
    trips = nvld // nstages
    per_trip = (tq - nvld) // trips
    assert trips * nstages == nvld and trips * per_trip == tq - nvld
    assert per_trip == 0 or trips % STREAM_RING == 0
    return trips, per_trip


def _row_stream(rows_hbm, rowbuf, rsem, trips, per_trip, rows_tok):
    if per_trip == 0:
        return None, None
    step = pl.program_id(0)
    chunks = pl.num_programs(0) * trips
    ring = STREAM_RING

    def chunk_copy(g, slot):
        return pltpu.make_async_copy(rows_hbm.at[pl.ds(g * (per_trip * rows_tok), per_trip * rows_tok), :],
                                     rowbuf.at[slot], rsem.at[slot])

    @pl.when(step == 0)
    def _():
        for g0 in range(ring - 1):
            chunk_copy(g0, g0).start()

    def on_trip(i):
        g = step * trips + i
        chunk_copy(g, i % ring).wait()

        @pl.when(g + ring - 1 < chunks)
        def _():
            chunk_copy(g + ring - 1, (i + ring - 1) % ring).start()

    def rows_of(i, d):
        return _streamed_rows_bf16(rowbuf.at[i % ring, pl.ds(d * rows_tok, rows_tok), :][...])

    return on_trip, rows_of


def _mixed_tokens(nvld, per_trip, stages, gather, compute, on_trip, streamed):
    places = [(d * len(stages)) // per_trip for d in range(per_trip)]

    def after(i, k):
        for d in range(per_trip):
            if places[d] == k:
                streamed(i, d)

    _pipelined_tokens(nvld, gather, compute, stages, on_trip, after if per_trip else None)


def _gather_scratch(nsel, tq, nvld, nstages):
    _, per_trip = _split_counts(tq, nvld, nstages)
    ring_rows = per_trip * nsel * STREAM_HALVES if per_trip else SUBLANES
    return ([pltpu.VMEM((STREAM_RING if per_trip else 1, ring_rows, STREAM_ROW), jnp.int32),
             pltpu.SemaphoreType.DMA((STREAM_RING,))]
            + [pltpu.VMEM((nsel * ROW_WORDS, LANES), jnp.int32)] * nstages
            + [pltpu.SemaphoreType.DMA((2,))] + [pltpu.SMEM((_lane_aligned(nvld),), jnp.int32)] * (2 * nsel))


def _peer_u_kernel(idx_hbm, h_ref, gate_ref, grp_ref, grp2_ref, tab_ref, rows_hbm, act_ref, part_ref,
                   rowbuf, rsem, *scratch, nvld, nstages, first_block):
    nsel, tq = gate_ref.shape
    stages, sems, bufs = scratch[:nstages], scratch[nstages], scratch[nstages + 1:]
    trips, per_trip = _split_counts(tq, nvld, nstages)
    diag = _diag_mask(nsel)
    diag2 = _streamed_diag(nsel)

    def finish(t, mask, rows_bf16):
        h_hi, h_lo = _split_bf16(_token_tile(h_ref, t))
        both = lax.dot_general(jnp.concatenate([h_hi, h_lo], axis=0), rows_bf16, NT_DIMS,
                               preferred_element_type=F32)
        prod = both[0:SUBLANES] + both[SUBLANES:2 * SUBLANES]
        part_ref[pl.ds(t, 1), :] = jnp.sum(jnp.where(mask, prod, 0.0), axis=0, keepdims=True)

    on_trip, rows_of = _row_stream(rows_hbm, rowbuf, rsem, trips, per_trip, nsel * STREAM_HALVES)

    def run(slots):
        _mixed_tokens(nvld, per_trip, stages, functools.partial(_gather_rows, slots, tab_ref),
                      lambda t, stage_ref: finish(t, diag, _staged_bf16(stage_ref)), on_trip,
                      lambda i, d: finish(nvld + i * per_trip + d, diag2, rows_of(i, d)))

    _with_slot_indices(idx_hbm, sems, bufs, tq, _lane_aligned(nvld), first_block, run)
    p_hi, p_lo = _split_bf16(part_ref[...])
    pre = []
    for lo, hi, grp in ((0, nvld, grp_ref[...]), (nvld, tq, grp2_ref[...])):
        if hi > lo:
            pre.append(jnp.dot(p_hi[lo:hi], grp, preferred_element_type=F32)
                       + jnp.dot(p_lo[lo:hi], grp, preferred_element_type=F32))
    pre = jnp.concatenate(pre, axis=0)
    act_ref[...] = 0.5 * pre * (1.0 + lax.erf(pre * (2.0 ** -0.5))) * gate_ref[...].T


def _no_rows():
    return jnp.zeros((SUBLANES, STREAM_ROW), jnp.int32)


def _peer_u(idx_t, h2, gate_t, tab, rows, tq, nvld, nstages, first_block, nblocks):
    nsel, n = idx_t.shape
    d = h2.shape[1]
    wide = nsel * SUBLANES
    const = lambda i: (0, 0)
    return pl.pallas_call(
        functools.partial(_peer_u_kernel, nvld=nvld, nstages=nstages, first_block=first_block),
        grid=(nblocks,),
        in_specs=[pl.BlockSpec(memory_space=pl.ANY),
                  pl.BlockSpec((tq, d), lambda i: (i + first_block, 0)),
                  pl.BlockSpec((nsel, tq), lambda i: (0, i + first_block)),
                  pl.BlockSpec((wide, nsel), const),
                  pl.BlockSpec((wide, nsel), const),
                  pl.BlockSpec(memory_space=pltpu.VMEM),
                  pl.BlockSpec(memory_space=pl.ANY)],
        out_specs=pl.BlockSpec((tq, nsel), lambda i: (i, 0)),
        out_shape=jax.ShapeDtypeStruct((nblocks * tq, nsel), F32),
        scratch_shapes=[pltpu.VMEM((tq, wide), F32)] + _gather_scratch(nsel, tq, nvld, nstages),
        compiler_params=_params(("arbitrary",), GATHER_VMEM_LIMIT),
        name="peer_u",
    )(idx_t, h2, gate_t, _group_matrix(nsel), _streamed_copy_matrix(nsel).T, tab, rows)


def _peer_v_kernel(idx_hbm, act_ref, rep_ref, rep2_ref, x1_ref, g2_ref, gpf_ref, tab_ref, rows_hbm, *rest,
                   nvld, nstages, first_block, aliased):
    o_ref, wide_ref, y_ref, rowbuf, rsem = rest[1:6] if aliased else rest[0:5]
    scratch = rest[6:] if aliased else rest[5:]
    tq, nsel = act_ref.shape
    stages, sems, bufs = scratch[:nstages], scratch[nstages], scratch[nstages + 1:]
    trips, per_trip = _split_counts(tq, nvld, nstages)
    diag = _diag_mask(nsel)
    diag2 = _streamed_diag(nsel)
    a_hi, a_lo = _split_bf16(act_ref[...])
    for lo, hi, rep in ((0, nvld, rep_ref[...]), (nvld, tq, rep2_ref[...])):
        if hi > lo:
            wide_ref[lo:hi, :] = (jnp.dot(a_hi[lo:hi], rep, preferred_element_type=F32)
                                  + jnp.dot(a_lo[lo:hi], rep, preferred_element_type=F32))

    def finish(t, mask, rows_bf16):
        w = jnp.where(mask, jnp.broadcast_to(wide_ref[pl.ds(t, 1), :], mask.shape), 0.0)
        w_hi, w_lo = _split_bf16(w)
        both = jnp.dot(jnp.concatenate([w_hi, w_lo], axis=0), rows_bf16, preferred_element_type=F32)
        tile = both[0:SUBLANES] + both[SUBLANES:2 * SUBLANES]
        y_ref[pl.ds(t, 1), :] = jnp.concatenate([tile[s:s + 1, :] for s in range(SUBLANES)], axis=1)

    on_trip, rows_of = _row_stream(rows_hbm, rowbuf, rsem, trips, per_trip, nsel * STREAM_HALVES)

    def run(slots):
        _mixed_tokens(nvld, per_trip, stages, functools.partial(_gather_rows, slots, tab_ref),
                      lambda t, stage_ref: finish(t, diag, _staged_bf16(stage_ref)), on_trip,
                      lambda i, d: finish(nvld + i * per_trip + d, diag2, rows_of(i, d)))

    _with_slot_indices(idx_hbm, sems, bufs, tq, _lane_aligned(nvld), first_block, run)
    o_ref[...] = x1_ref[...] + g2_ref[0] * _rms(y_ref[...], gpf_ref[...])


def _peer_v(idx_t, act, x1, gate2, g_post_ffn, tab, rows, seq, tq, nvld, nstages, first_block, nblocks, prev):
    nsel, n = idx_t.shape
    d = x1.shape[1]
    wide = nsel * SUBLANES
    per_b = seq // tq
    row = lambda i: (i + first_block, 0)
    const = lambda i: (0, 0)
    aliased = prev is not None
    in_specs = [pl.BlockSpec(memory_space=pl.ANY),
                pl.BlockSpec((tq, nsel), lambda i: (i, 0)),
                pl.BlockSpec((nsel, wide), const),
                pl.BlockSpec((nsel, wide), const),
                pl.BlockSpec((tq, d), row),
                pl.BlockSpec((1, 1, d), lambda i: ((i + first_block) // per_b, 0, 0)),
                pl.BlockSpec((1, d), const),
                pl.BlockSpec(memory_space=pltpu.VMEM),
                pl.BlockSpec(memory_space=pl.ANY)]
    args = [idx_t, act, _group_matrix(nsel).T, _streamed_copy_matrix(nsel), x1, gate2, g_post_ffn, tab, rows]
    if aliased:
        in_specs.append(pl.BlockSpec(memory_space=pl.ANY))
        args.append(prev)
    return pl.pallas_call(
        functools.partial(_peer_v_kernel, nvld=nvld, nstages=nstages, first_block=first_block, aliased=aliased),
        grid=(nblocks,),
        in_specs=in_specs,
        out_specs=pl.BlockSpec((tq, d), row),
        out_shape=jax.ShapeDtypeStruct((n, d), F32),
        scratch_shapes=[pltpu.VMEM((tq, wide), F32), pltpu.VMEM((tq, d), F32)]
        + _gather_scratch(nsel, tq, nvld, nstages),
        input_output_aliases={len(args) - 1: 0} if aliased else {},
        compiler_params=_params(("arbitrary",), GATHER_VMEM_LIMIT),
        name="peer_v",
    )(*args)


def _dup_heads(w, heads, dh):
    d = w.shape[0]
    return jnp.repeat(w.reshape(d, heads, 1, dh), 2, axis=2).reshape(d, heads * 2 * dh)


def _layer(x2, c, pos_col, bsz, seq, w_mod, b_mod, g_pre_mix, g_post_mix, w_in, conv_w, b_igate, b_fgate,
           mlstm_norm_g, att_sinks, w_out, g_pre_ffn, g_post_ffn, peer_wq, peer_keys1, peer_keys2, peer_u, peer_v):
    n, d = x2.shape
    tm = min(seq, 512)
    mod = _mod(c, w_mod, b_mod)
    shift1, scale1, gate1, shift2, scale2, gate2 = [m.reshape(bsz, 1, d) for m in jnp.split(mod, 6, axis=-1)]

    aw = ATT_HEADS * ATT_HEAD_DIM
    kvw = ATT_KV_HEADS * ATT_HEAD_DIM
    qkw = MLSTM_HEADS * MLSTM_QK_DIM
    mw = MLSTM_HEADS * MLSTM_V_DIM
    o = 0
    wq_a = w_in[:, o:o + aw]; o += aw
    wk_a = w_in[:, o:o + kvw]; o += kvw
    wv_a = w_in[:, o:o + kvw]; o += kvw
    w_mqk = w_in[:, o:o + 2 * qkw]; o += 2 * qkw
    w_mv = w_in[:, o:o + mw]; o += mw
    w_g = w_in[:, o:o + 2 * MLSTM_HEADS]; o += 2 * MLSTM_HEADS
    w_mo = w_in[:, o:o + mw]
    w_gp = jnp.pad(w_g, ((0, 0), (0, LANES - 2 * MLSTM_HEADS)))
    w_all = jnp.concatenate([wq_a, _dup_heads(wk_a, ATT_KV_HEADS, ATT_HEAD_DIM),
                             _dup_heads(wv_a, ATT_KV_HEADS, ATT_HEAD_DIM), w_mqk, w_mv, w_mo, w_gp],
                            axis=1).astype(BF16)

    cos, sin = _rope_tab(pos_col)
    q, kd, vd, mqk, mv, mo, gts = _in_proj(x2, scale1, shift1, g_pre_mix.reshape(1, d), cos, sin, w_all, seq, tm)
    att = _swa(att_sinks, q, kd, vd, bsz, seq)
    gate_bias = jnp.pad(jnp.concatenate([b_igate, b_fgate]), (0, LANES - 2 * MLSTM_HEADS)).reshape(1, LANES)
    chunks = min(seq // MLSTM_CHUNK, 8)
    mh = _mlstm(mqk, mv, mo, gts, conv_w, gate_bias, mlstm_norm_g.reshape(1, mw), bsz, seq, chunks,
                MLSTM_GROUP if bsz % MLSTM_GROUP == 0 else 1)
    wo = w_out.astype(BF16)
    x1, h2 = _out_proj(att, mh, x2, gate1, scale2, shift2, g_post_mix.reshape(1, d), g_pre_ffn.reshape(1, d),
                       wo[:aw], wo[aw:], seq, tm)

    tq = min(n, 256)
    eid_t, gate_t, row_ids = _peer_sel(h2, peer_wq.astype(BF16), peer_keys1.astype(BF16),
                                       peer_keys2.astype(BF16), tq)
    tg = min(seq, GATHER_TOKENS)
    tab_u = _pack_table(peer_u)
    tab_v = _pack_table(peer_v)
    blocks = n // tg
    seg = blocks // PEER_SEGMENTS
    ids3 = row_ids.reshape(blocks, tg, row_ids.shape[1])
    wide_u, wide_v = tab_u.reshape(-1, STREAM_ROW), tab_v.reshape(-1, STREAM_ROW)
    out = None
    for k in range(PEER_SEGMENTS):
        b0 = k * seg
        nv_u, st_u = (tg, GATHER_STAGES) if k == 0 else STREAM_LATER
        nv_v, st_v = STREAM_FIRST if k == 0 else STREAM_LATER
        rows_u = _no_rows() if nv_u == tg else _sc_gather_rows(wide_u, ids3[b0:b0 + seg, nv_u:, :].reshape(-1))
        rows_v = _sc_gather_rows(wide_v, ids3[b0:b0 + seg, nv_v:, :].reshape(-1))
        act = _peer_u(eid_t, h2, gate_t, tab_u, rows_u, tg, nv_u, st_u, b0, seg)
        out = _peer_v(eid_t, act, x1, gate2, g_post_ffn.reshape(1, d), tab_v, rows_v, seq, tg, nv_v, st_v, b0, seg, out)
    return out


def kernel(x, c, positions, w_mod, b_mod, g_pre_mix, g_post_mix, w_in, conv_w, b_igate, b_fgate, mlstm_norm_g, att_sinks, w_out, g_pre_ffn, g_post_ffn, peer_wq, peer_keys1, peer_keys2, peer_u, peer_v):
    bsz, seq, d = x.shape
    n = bsz * seq
    x2 = x.reshape(n, d)
    pos_col = positions.reshape(n, 1)
    for l in range(w_mod.shape[0]):
        x2 = _layer(x2, c, pos_col, bsz, seq, w_mod[l], b_mod[l], g_pre_mix[l], g_post_mix[l], w_in[l], conv_w[l],
                    b_igate[l], b_fgate[l], mlstm_norm_g[l], att_sinks[l], w_out[l], g_pre_ffn[l], g_post_ffn[l],
                    peer_wq[l], peer_keys1[l], peer_keys2[l], peer_u[l], peer_v[l])
    return x2.reshape(bsz, seq, d)
```

```python
import functools

import jax
import jax.numpy as jnp
from jax import lax
from jax.experimental import pallas as pl
from jax.experimental.pallas import tpu as pltpu
from jax.experimental.pallas import tpu_sc as plsc

F32 = jnp.float32
BF16 = jnp.bfloat16

ATT_HEADS = 8
ATT_KV_HEADS = 2
ATT_HEAD_DIM = 64
ATT_BLOCK = 128
ROPE_THETA = 10000.0
MLSTM_HEADS = 4
MLSTM_V_DIM = 128
MLSTM_QK_DIM = 64
MLSTM_CHUNK = 64
CONV_WIDTH = 4
PEER_HEADS = 8
PEER_KEYS = 128
PEER_HALF = 128
PEER_TOPK = 16
NORM_EPS = 1e-6

LANES = 128
SUBLANES = 8
VMEM_LIMIT = 52 * 1024 * 1024
GATHER_STAGES = 16
ROW_WORDS = SUBLANES // 2
MLSTM_GROUP = 1
STREAM_SPLIT = (4, 4)
STREAM_RING = 4
STREAM_STAGES = 8
STREAM_HALVES = 2
STREAM_ROW = 2 * LANES
GATHER_TOKENS = 512
GATHER_VMEM_LIMIT = 58 * 1024 * 1024

NEG_INF = float("-inf")
NT_DIMS = (((1,), (1,)), ((), ()))
TN_DIMS = (((0,), (0,)), ((), ()))


def _params(sem, vmem=None):
    return pltpu.CompilerParams(dimension_semantics=sem, vmem_limit_bytes=vmem)


def _rms(x, g):
    return x * lax.rsqrt(jnp.mean(x * x, axis=-1, keepdims=True) + NORM_EPS) * g


def _mod_kernel(c_ref, w_ref, b_ref, o_ref):
    o_ref[...] = jnp.dot(c_ref[...], w_ref[...], preferred_element_type=F32,
                         precision=lax.Precision.HIGHEST) + b_ref[...]


def _mod(c, w, b):
    bsz, d = c.shape
    nout = w.shape[1]
    return pl.pallas_call(
        _mod_kernel,
        grid=(nout // d,),
        in_specs=[pl.BlockSpec((bsz, d), lambda i: (0, 0)),
                  pl.BlockSpec((d, d), lambda i: (0, i)),
                  pl.BlockSpec((1, d), lambda i: (0, i))],
        out_specs=pl.BlockSpec((bsz, d), lambda i: (0, i)),
        out_shape=jax.ShapeDtypeStruct((bsz, nout), F32),
        compiler_params=_params(("arbitrary",)),
        name="mod",
    )(c, w, b.reshape(1, nout))


def _rope_tab_kernel(pos_ref, inv_ref, sign_ref, cos_ref, sin_ref):
    ang = pos_ref[...].astype(F32) * inv_ref[...]
    cos_ref[...] = jnp.cos(ang)
    sin_ref[...] = jnp.sin(ang) * sign_ref[...]


def _rope_tab(pos_col):
    n = pos_col.shape[0]
    tr = min(n, 1024)
    half = ATT_HEAD_DIM // 2
    inv = ROPE_THETA ** (-jnp.arange(0, ATT_HEAD_DIM, 2, dtype=F32) / ATT_HEAD_DIM)
    inv_row = jnp.tile(inv, LANES // half).reshape(1, LANES)
    lane = jnp.arange(LANES)
    sign_row = jnp.where((lane % ATT_HEAD_DIM) < half, -1.0, 1.0).astype(F32).reshape(1, LANES)
    return pl.pallas_call(
        _rope_tab_kernel,
        grid=(n // tr,),
        in_specs=[pl.BlockSpec((tr, 1), lambda i: (i, 0)),
                  pl.BlockSpec((1, LANES), lambda i: (0, 0)),
                  pl.BlockSpec((1, LANES), lambda i: (0, 0))],
        out_specs=[pl.BlockSpec((tr, LANES), lambda i: (i, 0))] * 2,
        out_shape=[jax.ShapeDtypeStruct((n, LANES), F32)] * 2,
        compiler_params=_params(("arbitrary",)),
        name="rope_tab",
    )(pos_col, inv_row, sign_row)


def _rope(v, cos, sin):
    half = ATT_HEAD_DIM // 2
    lane = lax.broadcasted_iota(jnp.int32, cos.shape, 1)
    first = (lane % ATT_HEAD_DIM) < half
    outs = []
    for j in range(v.shape[1] // LANES):
        c = v[:, j * LANES:(j + 1) * LANES]
        rot = jnp.where(first, pltpu.roll(c, LANES - half, 1), pltpu.roll(c, half, 1))
        outs.append(c * cos + rot * sin)
    return jnp.concatenate(outs, axis=1)


_C_Q, _C_K, _C_V, _C_MQK, _C_MV, _C_MO, _C_G, _C_END = 0, 512, 768, 1024, 1536, 2048, 2560, 2688


def _in_proj_kernel(x_ref, sc_ref, sh_ref, g_ref, cos_ref, sin_ref, w_ref,
                    q_ref, k_ref, v_ref, mqk_ref, mv_ref, mo_ref, gt_ref):
    x = x_ref[...]
    h = _rms(x, g_ref[...]) * (1.0 + sc_ref[0]) + sh_ref[0]
    hb = h.astype(BF16)

    def mm(a, b):
        return jnp.dot(hb, w_ref[:, a:b], preferred_element_type=F32)

    cos = cos_ref[...]
    sin = sin_ref[...]
    q_ref[...] = (_rope(mm(_C_Q, _C_K), cos, sin) * (ATT_HEAD_DIM ** -0.5)).astype(BF16)
    k_ref[...] = _rope(mm(_C_K, _C_V), cos, sin).astype(BF16)
    v_ref[...] = mm(_C_V, _C_MQK).astype(BF16)
    mqk_ref[...] = mm(_C_MQK, _C_MV)
    mv_ref[...] = mm(_C_MV, _C_MO).astype(BF16)
    mo_ref[...] = mm(_C_MO, _C_G)
    gt_ref[...] = mm(_C_G, _C_END)


def _in_proj(x2, scale1, shift1, g_pre, cos, sin, w_all, seq, tm):
    n, d = x2.shape
    per_b = seq // tm
    row = lambda i: (i, 0)
    bsel = lambda i: (i // per_b, 0, 0)
    widths = (512, 256, 256, 512, 512, 512, 128)
    dtypes = (BF16, BF16, BF16, F32, BF16, F32, F32)
    return pl.pallas_call(
        _in_proj_kernel,
        grid=(n // tm,),
        in_specs=[pl.BlockSpec((tm, d), row),
                  pl.BlockSpec((1, 1, d), bsel),
                  pl.BlockSpec((1, 1, d), bsel),
                  pl.BlockSpec((1, d), lambda i: (0, 0)),
                  pl.BlockSpec((tm, LANES), row),
                  pl.BlockSpec((tm, LANES), row),
                  pl.BlockSpec((d, _C_END), lambda i: (0, 0))],
        out_specs=[pl.BlockSpec((tm, w), row) for w in widths],
        out_shape=[jax.ShapeDtypeStruct((n, w), dt) for w, dt in zip(widths, dtypes)],
        compiler_params=_params(("arbitrary",), VMEM_LIMIT),
        name="in_proj",
    )(x2, scale1, shift1, g_pre, cos, sin, w_all)


def _swa_kernel(sink_ref, q_ref, kp_ref, kc_ref, vp_ref, vc_ref, o_ref):
    blk = ATT_BLOCK
    n = pl.program_id(1)
    qi = lax.broadcasted_iota(jnp.int32, (blk, 2 * blk), 0)
    si = lax.broadcasted_iota(jnp.int32, (blk, 2 * blk), 1)
    delta = qi + blk - si
    valid = (delta >= 0) & (delta < blk) & ((si >= blk) | (n > 0))
    lo = lax.broadcasted_iota(jnp.int32, (2 * blk, LANES), 1) < ATT_HEAD_DIM
    group = ATT_HEADS // ATT_KV_HEADS
    kv = []
    for g in range(ATT_KV_HEADS):
        cs = slice(g * LANES, (g + 1) * LANES)
        k = jnp.concatenate([kp_ref[:, cs], kc_ref[:, cs]], axis=0)
        v = jnp.concatenate([vp_ref[:, cs], vc_ref[:, cs]], axis=0)
        zero = jnp.zeros_like(k)
        kv.append(((jnp.where(lo, k, zero), jnp.where(lo, v, zero)),
                   (jnp.where(lo, zero, k), jnp.where(lo, zero, v))))
    heads = [(h, kv[h // group][h % 2]) for h in range(ATT_HEADS)]
    sinks = [sink_ref[h] for h in range(ATT_HEADS)]
    scores = [jnp.where(valid, lax.dot_general(q_ref[:, (h // 2) * LANES:(h // 2 + 1) * LANES], kh, NT_DIMS,
                                               preferred_element_type=F32), NEG_INF) for h, (kh, _) in heads]
    tops = [jnp.maximum(jnp.max(s, axis=-1, keepdims=True), sinks[h]) for h, s in enumerate(scores)]
    exps = [jnp.exp(s - m) for s, m in zip(scores, tops)]
    dens = [jnp.sum(e, axis=-1, keepdims=True) + jnp.exp(sinks[h] - tops[h]) for h, e in enumerate(exps)]
    outs = [jnp.dot((e / d).astype(BF16), vh, preferred_element_type=F32)
            for e, d, (_, (_, vh)) in zip(exps, dens, heads)]
    for p in range(ATT_HEADS // 2):
        o_ref[:, p * LANES:(p + 1) * LANES] = (outs[2 * p] + outs[2 * p + 1]).astype(BF16)


def _swa(sinks, q, kd, vd, bsz, seq):
    n = q.shape[0]
    nb = seq // ATT_BLOCK
    cur = lambda b, i: (b * nb + i, 0)
    prev = lambda b, i: (b * nb + jnp.maximum(i - 1, 0), 0)
    return pl.pallas_call(
        _swa_kernel,
        grid=(bsz, nb),
        in_specs=[pl.BlockSpec(memory_space=pltpu.SMEM),
                  pl.BlockSpec((ATT_BLOCK, 512), cur),
                  pl.BlockSpec((ATT_BLOCK, 256), prev),
                  pl.BlockSpec((ATT_BLOCK, 256), cur),
                  pl.BlockSpec((ATT_BLOCK, 256), prev),
                  pl.BlockSpec((ATT_BLOCK, 256), cur)],
        out_specs=pl.BlockSpec((ATT_BLOCK, 512), cur),
        out_shape=jax.ShapeDtypeStruct((n, 512), BF16),
        compiler_params=_params(("arbitrary", "arbitrary")),
        name="swa",
    )(sinks, q, kd, kd, vd, vd)


def _mlstm_kernel(mqk_all, mv_all, mo_all, gt_all, cw_ref, gb_ref, ng_ref, o_all,
                  tail_all, qk_all, xs_all, ct_all, n_all, m_all, *, chunks, group):
    @pl.when(pl.program_id(1) == 0)
    def _():
        for ref in (tail_all, ct_all, n_all, m_all):
            ref[...] = jnp.zeros_like(ref)

    bodies = [_mlstm_sequence(*(r.at[g] for r in (mqk_all, mv_all, mo_all, gt_all, o_all, tail_all, qk_all,
                                                   xs_all, ct_all, n_all, m_all)),
                              cw_ref, gb_ref, ng_ref, chunks) for g in range(group)]

    def chunk(c, carry):
        for body in bodies:
            body(c, carry)
        return carry

    lax.fori_loop(0, chunks, chunk, 0)


def _mlstm_sequence(mqk_ref, mv_ref, mo_ref, gt_ref, o_ref, tail_ref, qk_ref, xs_ref, ct_ref, n_ref, m_ref,
                    cw_ref, gb_ref, ng_ref, chunks):
    L = MLSTM_CHUNK
    tm = chunks * L
    nqk = MLSTM_HEADS * MLSTM_QK_DIM

    cur = mqk_ref[...]
    full = jnp.concatenate([tail_ref[...], cur], axis=0)
    off = SUBLANES - (CONV_WIDTH - 1)
    acc = full[off:off + tm] * cw_ref[0:1, :]
    for j in range(1, CONV_WIDTH):
        acc = acc + full[off + j:off + j + tm] * cw_ref[j:j + 1, :]
    act = acc * jax.nn.sigmoid(acc)
    col = lax.broadcasted_iota(jnp.int32, (1, 2 * nqk), 1)
    act = act * jnp.where(col < nqk, MLSTM_QK_DIM ** -0.5, 1.0)
    qk_ref[...] = act.astype(BF16)
    tail_ref[...] = cur[tm - SUBLANES:tm]

    lane = lax.broadcasted_iota(jnp.int32, (tm, LANES), 1)
    gts = gt_ref[...] + gb_ref[...]
    logsig = jnp.minimum(gts, 0.0) - jnp.log(1.0 + jnp.exp(-jnp.abs(gts)))
    xs_ref[...] = jnp.where(lane < MLSTM_HEADS, gts, jnp.where(lane < 2 * MLSTM_HEADS, logsig, 0.0))

    ri = lax.broadcasted_iota(jnp.int32, (L, L), 0)
    ci = lax.broadcasted_iota(jnp.int32, (L, L), 1)
    causal = ci <= ri
    tril = causal.astype(F32)
    lane_l = lax.broadcasted_iota(jnp.int32, (L, LANES), 1)
    lo_l = lane_l < MLSTM_QK_DIM
    row_c = lax.broadcasted_iota(jnp.int32, (LANES, 1), 0) < MLSTM_QK_DIM
    lane_1 = lax.broadcasted_iota(jnp.int32, (1, LANES), 1) < MLSTM_QK_DIM

    def chunk(c, carry):
        r0 = pl.multiple_of(c * L, L)
        rows = pl.ds(r0, L)
        xc = xs_ref[rows, :]
        bc = jnp.dot(tril, xc, preferred_element_type=F32, precision=lax.Precision.HIGHEST)
        x2 = jnp.where(lane_l < MLSTM_HEADS, xc, bc)
        xt = x2.T
        H = range(MLSTM_HEADS)
        pairs = range(MLSTM_HEADS // 2)
        q2 = [qk_ref[rows, p * LANES:(p + 1) * LANES] for p in pairs]
        k2 = [qk_ref[rows, nqk + p * LANES:nqk + (p + 1) * LANES] for p in pairs]
        ct_old = [ct_ref[p] for p in pairs]
        ctb = [ct.astype(BF16) for ct in ct_old]
        n2 = [n_ref[p:p + 1, :] for p in pairs]
        hm = [lo_l if h % 2 == 0 else jnp.logical_not(lo_l) for h in H]
        qm = [jnp.where(hm[h], q2[h // 2], jnp.zeros_like(q2[0])) for h in H]
        km = [jnp.where(hm[h], k2[h // 2], jnp.zeros_like(k2[0])) for h in H]
        v = [mv_ref[rows, h * LANES:(h + 1) * LANES] for h in H]
        b_col = [x2[:, MLSTM_HEADS + h:MLSTM_HEADS + h + 1] for h in H]
        ig_col = [x2[:, h:h + 1] for h in H]
        b_row = [xt[MLSTM_HEADS + h:MLSTM_HEADS + h + 1, :] for h in H]
        ig_row = [xt[h:h + 1, :] for h in H]
        m_prev = [m_ref[h:h + 1, 0:1] for h in H]
        dlog = [jnp.where(causal, b_col[h] - b_row[h] + ig_row[h], NEG_INF) for h in H]
        m_inter = [b_col[h] + m_prev[h] for h in H]
        m_t = [jnp.maximum(m_inter[h], jnp.max(dlog[h], axis=-1, keepdims=True)) for h in H]
        w_intra = [jnp.exp(dlog[h] - m_t[h]) for h in H]
        a_inter = [jnp.exp(m_inter[h] - m_t[h]) for h in H]
        s = [lax.dot_general(q2[h // 2], km[h], NT_DIMS, preferred_element_type=F32) * w_intra[h] for h in H]
        num = [jnp.dot(s[h].astype(BF16), v[h], preferred_element_type=F32)
               + a_inter[h] * jnp.dot(qm[h], ctb[h // 2], preferred_element_type=F32) for h in H]
        den = [jnp.sum(s[h], axis=-1, keepdims=True)
               + a_inter[h] * jnp.sum(qm[h].astype(F32) * n2[h // 2], axis=-1, keepdims=True) for h in H]
        hh = [num[h] / jnp.maximum(jnp.abs(den[h]), jnp.exp(-m_t[h])) for h in H]
        y = [_rms(hh[h], ng_ref[:, h * LANES:(h + 1) * LANES]) for h in H]
        for h in H:
            o_ref[rows, h * LANES:(h + 1) * LANES] = (
                y[h] * jax.nn.sigmoid(mo_ref[rows, h * LANES:(h + 1) * LANES])).astype(BF16)
        b_last = [xt[MLSTM_HEADS + h:MLSTM_HEADS + h + 1, L - 1:L] for h in H]
        g_col = [b_last[h] - b_col[h] + ig_col[h] for h in H]
        m_new = [jnp.maximum(b_last[h] + m_prev[h], jnp.max(g_col[h], axis=0, keepdims=True)) for h in H]
        kw = [km[h].astype(F32) * jnp.exp(g_col[h] - m_new[h]) for h in H]
        dec = [jnp.exp(b_last[h] + m_prev[h] - m_new[h]) for h in H]
        upd = [lax.dot_general(kw[h].astype(BF16), v[h], TN_DIMS, preferred_element_type=F32) for h in H]
        for h in H:
            m_ref[h:h + 1, :] = jnp.broadcast_to(m_new[h], (1, LANES))
        for p in pairs:
            e, o = 2 * p, 2 * p + 1
            ct_ref[p] = ct_old[p] * jnp.where(row_c, dec[e], dec[o]) + upd[e] + upd[o]
            n_ref[p:p + 1, :] = (n2[p] * jnp.where(lane_1, dec[e], dec[o])
                                 + jnp.sum(kw[e] + kw[o], axis=0, keepdims=True))
        return carry

    return chunk


def _mlstm(mqk, mv, mo, gts, conv_w, gate_bias, norm_g, bsz, seq, chunks, group):
    n = mqk.shape[0]
    tm = chunks * MLSTM_CHUNK
    steps = seq // tm
    row = lambda b, i: (b, i, 0)
    const = lambda b, i: (0, 0)
    width = MLSTM_HEADS * MLSTM_V_DIM
    per_seq = lambda a: a.reshape(bsz, seq, a.shape[1])
    out = pl.pallas_call(
        functools.partial(_mlstm_kernel, chunks=chunks, group=group),
        grid=(bsz // group, steps),
        in_specs=[pl.BlockSpec((group, tm, width), row),
                  pl.BlockSpec((group, tm, width), row),
                  pl.BlockSpec((group, tm, width), row),
                  pl.BlockSpec((group, tm, LANES), row),
                  pl.BlockSpec((CONV_WIDTH, width), const),
                  pl.BlockSpec((1, LANES), const),
                  pl.BlockSpec((1, width), const)],
        out_specs=pl.BlockSpec((group, tm, width), row),
        out_shape=jax.ShapeDtypeStruct((bsz, seq, width), BF16),
        scratch_shapes=[pltpu.VMEM((group, SUBLANES, width), F32),
                        pltpu.VMEM((group, tm, width), BF16),
                        pltpu.VMEM((group, tm, LANES), F32),
                        pltpu.VMEM((group, MLSTM_HEADS // 2, LANES, LANES), F32),
                        pltpu.VMEM((group, SUBLANES, LANES), F32),
                        pltpu.VMEM((group, SUBLANES, LANES), F32)],
        compiler_params=_params(("arbitrary", "arbitrary")),
        name="mlstm",
    )(per_seq(mqk), per_seq(mv), per_seq(mo), per_seq(gts), conv_w, gate_bias, norm_g)
    return out.reshape(n, width)


def _out_proj_kernel(att_ref, mh_ref, x_ref, g1_ref, sc_ref, sh_ref, gpm_ref, gpf_ref, wa_ref, wb_ref,
                     x1_ref, h2_ref):
    mix = (jnp.dot(att_ref[...], wa_ref[...], preferred_element_type=F32)
           + jnp.dot(mh_ref[...], wb_ref[...], preferred_element_type=F32))
    x1 = x_ref[...] + g1_ref[0] * _rms(mix, gpm_ref[...])
    x1_ref[...] = x1
    h2_ref[...] = _rms(x1, gpf_ref[...]) * (1.0 + sc_ref[0]) + sh_ref[0]


def _out_proj(att, mh, x2, gate1, scale2, shift2, g_post_mix, g_pre_ffn, wa, wb, seq, tm):
    n, d = x2.shape
    per_b = seq // tm
    row = lambda i: (i, 0)
    bsel = lambda i: (i // per_b, 0, 0)
    const = lambda i: (0, 0)
    half = att.shape[1]
    return pl.pallas_call(
        _out_proj_kernel,
        grid=(n // tm,),
        in_specs=[pl.BlockSpec((tm, half), row), pl.BlockSpec((tm, half), row), pl.BlockSpec((tm, d), row),
                  pl.BlockSpec((1, 1, d), bsel), pl.BlockSpec((1, 1, d), bsel), pl.BlockSpec((1, 1, d), bsel),
                  pl.BlockSpec((1, d), const), pl.BlockSpec((1, d), const),
                  pl.BlockSpec((half, d), const), pl.BlockSpec((half, d), const)],
        out_specs=[pl.BlockSpec((tm, d), row)] * 2,
        out_shape=[jax.ShapeDtypeStruct((n, d), F32)] * 2,
        compiler_params=_params(("arbitrary",), VMEM_LIMIT),
        name="out_proj",
    )(att, mh, x2, gate1, scale2, shift2, g_post_mix, g_pre_ffn, wa, wb)


_BIG_ID = float(2 ** 30)
SORT_LEVELS = 8


def _top_scores(s, k):
    rows, t = s.shape
    span = SORT_LEVELS * SUBLANES
    r = lax.broadcasted_iota(jnp.int32, (rows // SORT_LEVELS, t), 0)
    col_id = ((r // SUBLANES) * span + r % SUBLANES).astype(F32)
    lev = [jnp.concatenate([s[g * span + l * SUBLANES:g * span + (l + 1) * SUBLANES] for g in range(rows // span)],
                           axis=0) for l in range(SORT_LEVELS)]
    ids = [col_id + float(l * SUBLANES) for l in range(SORT_LEVELS)]
    for rnd in range(SORT_LEVELS):
        for a in range(rnd % 2, SORT_LEVELS - 1, 2):
            swap = lev[a + 1] > lev[a]
            lev[a], lev[a + 1] = jnp.where(swap, lev[a + 1], lev[a]), jnp.where(swap, lev[a], lev[a + 1])
            ids[a], ids[a + 1] = jnp.where(swap, ids[a + 1], ids[a]), jnp.where(swap, ids[a], ids[a + 1])
    vals, sel = [], []
    for _ in range(k):
        m = jnp.max(lev[0], axis=0, keepdims=True)
        i = jnp.min(jnp.where(lev[0] == m, ids[0], _BIG_ID), axis=0, keepdims=True)
        vals.append(m)
        sel.append(i)
        hit = ids[0] == i
        for l in range(SORT_LEVELS - 1):
            lev[l] = jnp.where(hit, lev[l + 1], lev[l])
            ids[l] = jnp.where(hit, ids[l + 1], ids[l])
        lev[-1] = jnp.where(hit, NEG_INF, lev[-1])
    return jnp.concatenate(vals, axis=0), jnp.concatenate(sel, axis=0).astype(jnp.int32)


def _top_pair_sums(v1, v2):
    k, t = v1.shape
    half = SUBLANES
    lev = [v1[0:half] + v2[b:b + 1, :] for b in range(k)]
    side = v1[half:k] + v2[0:1, :]
    a_low = lax.broadcasted_iota(jnp.int32, (half, t), 0).astype(F32) * float(k)
    side_id = a_low + float(half * k)
    depth = jnp.zeros((half, t), F32)
    vals, sel = [], []
    for it in range(k):
        top_id = a_low + depth
        m = jnp.max(jnp.maximum(lev[0], side), axis=0, keepdims=True)
        i = jnp.min(jnp.minimum(jnp.where(lev[0] == m, top_id, _BIG_ID), jnp.where(side == m, side_id, _BIG_ID)),
                    axis=0, keepdims=True)
        vals.append(m)
        sel.append(i)
        hit = top_id == i
        for l in range(k - 1 - it):
            lev[l] = jnp.where(hit, lev[l + 1], lev[l])
        side = jnp.where(side_id == i, NEG_INF, side)
        depth = depth + jnp.where(hit, 1.0, 0.0)
    return jnp.concatenate(vals, axis=0), jnp.concatenate(sel, axis=0).astype(jnp.int32)


def _pick_rows(table, which):
    r = lax.broadcasted_iota(jnp.int32, table.shape, 0)
    rows = []
    for k in range(which.shape[0]):
        rows.append(jnp.sum(jnp.where(r == which[k:k + 1, :], table, 0), axis=0, keepdims=True))
    return jnp.concatenate(rows, axis=0)


def _peer_sel_kernel(h_ref, wq_ref, k1_ref, k2_ref, e_ref, g_ref, r_ref):
    tq = h_ref.shape[0]
    K = PEER_TOPK
    q = jnp.dot(h_ref[...].astype(BF16), wq_ref[...], preferred_element_type=F32).astype(BF16)
    for hd in range(PEER_HEADS):
        base = hd * 2 * PEER_HALF
        s1 = lax.dot_general(k1_ref[hd], q[:, base:base + PEER_HALF], NT_DIMS, preferred_element_type=F32)
        s2 = lax.dot_general(k2_ref[hd], q[:, base + PEER_HALF:base + 2 * PEER_HALF], NT_DIMS,
                             preferred_element_type=F32)
        v1, i1 = _top_scores(s1, K)
        v2, i2 = _top_scores(s2, K)
        top, pos = _top_pair_sums(v1, v2)
        eid = _pick_rows(i1, pos >> 4) * PEER_KEYS + _pick_rows(i2, pos & (K - 1))
        ex = jnp.exp(top - top[0:1, :])
        e_ref[hd * K:(hd + 1) * K, :] = eid * ROW_WORDS
        g_ref[hd * K:(hd + 1) * K, :] = ex / jnp.sum(ex, axis=0, keepdims=True)
    pair = (e_ref[...] >> 1).T
    nsel = pair.shape[1]
    for h in range(STREAM_HALVES):
        r_ref[:, h * nsel:(h + 1) * nsel] = pair + h


def _peer_sel(h2, wq, k1, k2, tq):
    n, d = h2.shape
    rows = PEER_HEADS * PEER_TOPK
    return pl.pallas_call(
        _peer_sel_kernel,
        grid=(n // tq,),
        in_specs=[pl.BlockSpec((tq, d), lambda i: (i, 0)),
                  pl.BlockSpec(wq.shape, lambda i: (0, 0)),
                  pl.BlockSpec(k1.shape, lambda i: (0, 0, 0)),
                  pl.BlockSpec(k2.shape, lambda i: (0, 0, 0))],
        out_specs=[pl.BlockSpec((rows, tq), lambda i: (0, i))] * 2
        + [pl.BlockSpec((tq, rows * STREAM_HALVES), lambda i: (i, 0))],
        out_shape=[jax.ShapeDtypeStruct((rows, n), jnp.int32), jax.ShapeDtypeStruct((rows, n), F32),
                   jax.ShapeDtypeStruct((n, rows * STREAM_HALVES), jnp.int32)],
        compiler_params=_params(("arbitrary",), VMEM_LIMIT),
        name="peer_sel",
    )(h2, wq, k1, k2)


def _split_bf16(x):
    hi = x.astype(BF16)
    return hi, (x - hi.astype(F32)).astype(BF16)


def _pack_kernel(w_ref, o_ref):
    x = w_ref[...]
    eb = x.shape[0]
    for r in range(ROW_WORDS):
        lo = x[:, 2 * r * LANES:(2 * r + 1) * LANES].astype(BF16).astype(F32)
        hi = x[:, (2 * r + 1) * LANES:(2 * r + 2) * LANES].astype(BF16).astype(F32)
        word = (lax.shift_right_logical(pltpu.bitcast(lo, jnp.int32), 16)
                | (pltpu.bitcast(hi, jnp.int32) & jnp.int32(-65536)))
        o_ref[pl.ds(r, eb, stride=ROW_WORDS), :] = word


def _pack_table(w):
    e, d = w.shape
    eb = 512
    return pl.pallas_call(
        _pack_kernel,
        grid=(e // eb,),
        in_specs=[pl.BlockSpec((eb, d), lambda i: (i, 0))],
        out_specs=pl.BlockSpec((eb * ROW_WORDS, LANES), lambda i: (i, 0)),
        out_shape=jax.ShapeDtypeStruct((e * ROW_WORDS, LANES), jnp.int32),
        compiler_params=_params(("arbitrary",)),
        name="pack_table",
    )(w)


def _gather_rows(slots, tab_ref, t, stage_ref):
    for j, slot in enumerate(slots):
        src = pl.ds(pl.multiple_of(slot[t], ROW_WORDS), ROW_WORDS)
        stage_ref[j * ROW_WORDS:(j + 1) * ROW_WORDS, :] = tab_ref[src, :]


def _staged_bf16(stage_ref):
    return pltpu.bitcast(stage_ref[...], BF16)


def _pipelined_tokens(ntok, gather, compute, stages, on_trip=None, after=None):
    nb = len(stages)
    for k in range(nb):
        gather(k, stages[k])

    def trip(i, carry):
        t = nb * i
        if on_trip is not None:
            on_trip(i)
        for k in range(nb):
            compute(t + k, stages[k])
            ahead = t + k + nb
            gather(jnp.minimum(ahead, ntok - 1), stages[k])
            if after is not None:
                after(i, k)
        return carry

    lax.fori_loop(0, ntok // nb, trip, 0)


def _with_slot_indices(idx_hbm, sems, bufs, stride, count, run):
    nsel = len(bufs) // 2
    step = pl.program_id(0)
    last = pl.num_programs(0) - 1

    def copies(block, which):
        return [pltpu.make_async_copy(idx_hbm.at[j, pl.ds(block * stride, count)], bufs[which * nsel + j],
                                      sems.at[which])
                for j in range(nsel)]

    @pl.when(step == 0)
    def _():
        for cp in copies(0, 0):
            cp.start()

    def phase(which):
        for cp in copies(step, which):
            cp.wait()

        @pl.when(step < last)
        def _():
            for cp in copies(step + 1, 1 - which):
                cp.start()

        run(bufs[which * nsel:(which + 1) * nsel])

    for which in range(2):
        pl.when(step % 2 == which)(functools.partial(phase, which))


SC_WINDOW = 128


def _sc_gather_rows(tab, row_ids):
    n = row_ids.shape[0]
    mesh = plsc.VectorSubcoreMesh(core_axis_name="core", subcore_axis_name="subcore")

    @pl.kernel(out_type=jax.ShapeDtypeStruct((n, tab.shape[1]), tab.dtype), mesh=mesh)
    def gather(tab_hbm, ids_hbm, out_hbm):
        def body(ids_vmem, out_vmem):
            pltpu.sync_copy(tab_hbm.at[ids_vmem.at[0]], out_vmem)

        pltpu.emit_pipeline(
            body,
            grid=(n // SC_WINDOW,),
            in_specs=[pl.BlockSpec((1, SC_WINDOW), index_map=lambda i: (0, i))],
            out_specs=[pl.BlockSpec((SC_WINDOW, tab.shape[1]), index_map=lambda i: (i, 0))],
            core_axis_name=("core", "subcore"),
            dimension_semantics=(pltpu.PARALLEL,),
            trace_scopes=False,
        )(ids_hbm, out_hbm)

    return gather(tab, row_ids.reshape(1, n))


def _diag_mask(nsel):
    shape = (SUBLANES, nsel * SUBLANES)
    return (lax.broadcasted_iota(jnp.int32, shape, 1) % SUBLANES) == lax.broadcasted_iota(jnp.int32, shape, 0)


def _token_tile(ref, t):
    row = ref[pl.ds(t, 1), :]
    return jnp.concatenate([row[:, s * LANES:(s + 1) * LANES] for s in range(SUBLANES)], axis=0)


def _peer_u_kernel(idx_hbm, h_ref, gate_ref, grp_ref, tab_ref, act_ref, part_ref, *scratch):
    nsel, tq = gate_ref.shape
    stages, sems, bufs = scratch[:GATHER_STAGES], scratch[GATHER_STAGES], scratch[GATHER_STAGES + 1:]
    diag = _diag_mask(nsel)

    def compute(t, stage_ref):
        h_hi, h_lo = _split_bf16(_token_tile(h_ref, t))
        both = lax.dot_general(jnp.concatenate([h_hi, h_lo], axis=0), _staged_bf16(stage_ref), NT_DIMS,
                               preferred_element_type=F32)
        prod = both[0:SUBLANES] + both[SUBLANES:2 * SUBLANES]
        part_ref[pl.ds(t, 1), :] = jnp.sum(jnp.where(diag, prod, 0.0), axis=0, keepdims=True)

    def run(slots):
        _pipelined_tokens(tq, functools.partial(_gather_rows, slots, tab_ref), compute, stages)

    _with_slot_indices(idx_hbm, sems, bufs, tq, tq, run)
    p_hi, p_lo = _split_bf16(part_ref[...])
    grp = grp_ref[...]
    pre = jnp.dot(p_hi, grp, preferred_element_type=F32) + jnp.dot(p_lo, grp, preferred_element_type=F32)
    act_ref[...] = 0.5 * pre * (1.0 + lax.erf(pre * (2.0 ** -0.5))) * gate_ref[...].T


def _group_matrix(nsel):
    r = jnp.arange(nsel * SUBLANES)[:, None] // SUBLANES
    return (r == jnp.arange(nsel)[None, :]).astype(BF16)


def _gather_scratch(nsel, ntok, nstages):
    return ([pltpu.VMEM((nsel * ROW_WORDS, LANES), jnp.int32)] * nstages
            + [pltpu.SemaphoreType.DMA((2,))] + [pltpu.SMEM((ntok,), jnp.int32)] * (2 * nsel))


def _peer_u(idx_t, h2, gate_t, tab, tq):
    nsel, n = idx_t.shape
    d = h2.shape[1]
    wide = nsel * SUBLANES
    return pl.pallas_call(
        _peer_u_kernel,
        grid=(n // tq,),
        in_specs=[pl.BlockSpec(memory_space=pl.ANY),
                  pl.BlockSpec((tq, d), lambda i: (i, 0)),
                  pl.BlockSpec((nsel, tq), lambda i: (0, i)),
                  pl.BlockSpec((wide, nsel), lambda i: (0, 0)),
                  pl.BlockSpec(memory_space=pltpu.VMEM)],
        out_specs=pl.BlockSpec((tq, nsel), lambda i: (i, 0)),
        out_shape=jax.ShapeDtypeStruct((n, nsel), F32),
        scratch_shapes=[pltpu.VMEM((tq, wide), F32)] + _gather_scratch(nsel, tq, GATHER_STAGES),
        compiler_params=_params(("arbitrary",), GATHER_VMEM_LIMIT),
        name="peer_u",
    )(idx_t, h2, gate_t, _group_matrix(nsel), tab)


def _lane_aligned(count):
    return -(-count // LANES) * LANES


def _streamed_rows_bf16(words):
    x = pltpu.bitcast(words, BF16)
    return jnp.concatenate([x[:, 0:LANES], x[:, LANES:2 * LANES]], axis=0)


def _streamed_copy_matrix(nsel):
    r = jnp.arange(nsel * SUBLANES)
    return ((r[None, :] % (2 * nsel)) // 2 == jnp.arange(nsel)[:, None]).astype(BF16)


def _streamed_diag(nsel):
    shape = (SUBLANES, nsel * SUBLANES)
    r = lax.broadcasted_iota(jnp.int32, shape, 1)
    sub = 4 * ((r % (4 * nsel)) // (2 * nsel)) + 2 * (r // (4 * nsel)) + r % 2
    return lax.broadcasted_iota(jnp.int32, shape, 0) == sub


def _peer_v_kernel(idx_hbm, act_ref, rep_ref, rep2_ref, x1_ref, g2_ref, gpf_ref, tab_ref, rows_hbm, o_ref,
                   wide_ref, y_ref, rowbuf, rsem, *scratch, nvld, nstages):
    tq, nsel = act_ref.shape
    stages, sems, bufs = scratch[:nstages], scratch[nstages], scratch[nstages + 1:]
    trips = nvld // nstages
    per_trip = (tq - nvld) // trips
    rows_tok = nsel * STREAM_HALVES
    step = pl.program_id(0)
    chunks = pl.num_programs(0) * trips
    diag = _diag_mask(nsel)
    diag2 = _streamed_diag(nsel)
    a_hi, a_lo = _split_bf16(act_ref[...])
    for lo, hi, rep in ((0, nvld, rep_ref[...]), (nvld, tq, rep2_ref[...])):
        wide_ref[lo:hi, :] = (jnp.dot(a_hi[lo:hi], rep, preferred_element_type=F32)
                              + jnp.dot(a_lo[lo:hi], rep, preferred_element_type=F32))

    def finish(t, mask, rows_bf16):
        w = jnp.where(mask, jnp.broadcast_to(wide_ref[pl.ds(t, 1), :], mask.shape), 0.0)
        w_hi, w_lo = _split_bf16(w)
        both = jnp.dot(jnp.concatenate([w_hi, w_lo], axis=0), rows_bf16, preferred_element_type=F32)
        tile = both[0:SUBLANES] + both[SUBLANES:2 * SUBLANES]
        y_ref[pl.ds(t, 1), :] = jnp.concatenate([tile[s:s + 1, :] for s in range(SUBLANES)], axis=1)

    def compute(t, stage_ref):
        finish(t, diag, _staged_bf16(stage_ref))

    def chunk_copy(g, slot):
        return pltpu.make_async_copy(rows_hbm.at[pl.ds(g * (per_trip * rows_tok), per_trip * rows_tok), :],
                                     rowbuf.at[slot], rsem.at[slot])

    ring = STREAM_RING

    @pl.when(step == 0)
    def _():
        for g0 in range(ring - 1):
            chunk_copy(g0, g0).start()

    def on_trip(i):
        g = step * trips + i
        chunk_copy(g, i % ring).wait()

        @pl.when(g + ring - 1 < chunks)
        def _():
            chunk_copy(g + ring - 1, (i + ring - 1) % ring).start()

    places = [(d * nstages) // per_trip for d in range(per_trip)]

    def after(i, k):
        for d in range(per_trip):
            if places[d] == k:
                rows = rowbuf.at[i % ring, pl.ds(d * rows_tok, rows_tok), :]
                finish(nvld + i * per_trip + d, diag2, _streamed_rows_bf16(rows[...]))

    def run(slots):
        _pipelined_tokens(nvld, functools.partial(_gather_rows, slots, tab_ref), compute, stages, on_trip, after)

    _with_slot_indices(idx_hbm, sems, bufs, tq, _lane_aligned(nvld), run)
    o_ref[...] = x1_ref[...] + g2_ref[0] * _rms(y_ref[...], gpf_ref[...])


def _peer_v(idx_t, act, x1, gate2, g_post_ffn, tab, rows, seq, tq, nvld, nstages):
    nsel, n = idx_t.shape
    d = x1.shape[1]
    wide = nsel * SUBLANES
    per_b = seq // tq
    row = lambda i: (i, 0)
    const = lambda i: (0, 0)
    trips = nvld // nstages
    per_trip = (tq - nvld) // trips
    assert trips % STREAM_RING == 0 and trips * nstages == nvld and trips * per_trip == tq - nvld
    return pl.pallas_call(
        functools.partial(_peer_v_kernel, nvld=nvld, nstages=nstages),
        grid=(n // tq,),
        in_specs=[pl.BlockSpec(memory_space=pl.ANY),
                  pl.BlockSpec((tq, nsel), row),
                  pl.BlockSpec((nsel, wide), const),
                  pl.BlockSpec((nsel, wide), const),
                  pl.BlockSpec((tq, d), row),
                  pl.BlockSpec((1, 1, d), lambda i: (i // per_b, 0, 0)),
                  pl.BlockSpec((1, d), const),
                  pl.BlockSpec(memory_space=pltpu.VMEM),
                  pl.BlockSpec(memory_space=pl.ANY)],
        out_specs=pl.BlockSpec((tq, d), row),
        out_shape=jax.ShapeDtypeStruct((n, d), F32),
        scratch_shapes=[pltpu.VMEM((tq, wide), F32), pltpu.VMEM((tq, d), F32),
                        pltpu.VMEM((STREAM_RING, per_trip * nsel * STREAM_HALVES, STREAM_ROW), jnp.int32),
                        pltpu.SemaphoreType.DMA((STREAM_RING,))]
        + _gather_scratch(nsel, _lane_aligned(nvld), nstages),
        compiler_params=_params(("arbitrary",), GATHER_VMEM_LIMIT),
        name="peer_v",
    )(idx_t, act, _group_matrix(nsel).T, _streamed_copy_matrix(nsel), x1, gate2, g_post_ffn, tab, rows)


def _dup_heads(w, heads, dh):
    d = w.shape[0]
    return jnp.repeat(w.reshape(d, heads, 1, dh), 2, axis=2).reshape(d, heads * 2 * dh)


def _layer(x2, c, pos_col, bsz, seq, w_mod, b_mod, g_pre_mix, g_post_mix, w_in, conv_w, b_igate, b_fgate,
           mlstm_norm_g, att_sinks, w_out, g_pre_ffn, g_post_ffn, peer_wq, peer_keys1, peer_keys2, peer_u, peer_v):
    n, d = x2.shape
    tm = min(seq, 512)
    mod = _mod(c, w_mod, b_mod)
    shift1, scale1, gate1, shift2, scale2, gate2 = [m.reshape(bsz, 1, d) for m in jnp.split(mod, 6, axis=-1)]

    aw = ATT_HEADS * ATT_HEAD_DIM
    kvw = ATT_KV_HEADS * ATT_HEAD_DIM
    qkw = MLSTM_HEADS * MLSTM_QK_DIM
    mw = MLSTM_HEADS * MLSTM_V_DIM
    o = 0
    wq_a = w_in[:, o:o + aw]; o += aw
    wk_a = w_in[:, o:o + kvw]; o += kvw
    wv_a = w_in[:, o:o + kvw]; o += kvw
    w_mqk = w_in[:, o:o + 2 * qkw]; o += 2 * qkw
    w_mv = w_in[:, o:o + mw]; o += mw
    w_g = w_in[:, o:o + 2 * MLSTM_HEADS]; o += 2 * MLSTM_HEADS
    w_mo = w_in[:, o:o + mw]
    w_gp = jnp.pad(w_g, ((0, 0), (0, LANES - 2 * MLSTM_HEADS)))
    w_all = jnp.concatenate([wq_a, _dup_heads(wk_a, ATT_KV_HEADS, ATT_HEAD_DIM),
                             _dup_heads(wv_a, ATT_KV_HEADS, ATT_HEAD_DIM), w_mqk, w_mv, w_mo, w_gp],
                            axis=1).astype(BF16)

    cos, sin = _rope_tab(pos_col)
    q, kd, vd, mqk, mv, mo, gts = _in_proj(x2, scale1, shift1, g_pre_mix.reshape(1, d), cos, sin, w_all, seq, tm)
    att = _swa(att_sinks, q, kd, vd, bsz, seq)
    gate_bias = jnp.pad(jnp.concatenate([b_igate, b_fgate]), (0, LANES - 2 * MLSTM_HEADS)).reshape(1, LANES)
    chunks = min(seq // MLSTM_CHUNK, 8)
    mh = _mlstm(mqk, mv, mo, gts, conv_w, gate_bias, mlstm_norm_g.reshape(1, mw), bsz, seq, chunks,
                MLSTM_GROUP if bsz % MLSTM_GROUP == 0 else 1)
    wo = w_out.astype(BF16)
    x1, h2 = _out_proj(att, mh, x2, gate1, scale2, shift2, g_post_mix.reshape(1, d), g_pre_ffn.reshape(1, d),
                       wo[:aw], wo[aw:], seq, tm)

    tq = min(n, 256)
    eid_t, gate_t, row_ids = _peer_sel(h2, peer_wq.astype(BF16), peer_keys1.astype(BF16),
                                       peer_keys2.astype(BF16), tq)
    tg = min(seq, GATHER_TOKENS)
    tab_v = _pack_table(peer_v)
    nvld = tg * STREAM_SPLIT[0] // sum(STREAM_SPLIT)
    streamed = row_ids.reshape(n // tg, tg, row_ids.shape[1])[:, nvld:, :]
    rows_v = _sc_gather_rows(tab_v.reshape(-1, STREAM_ROW), streamed.reshape(-1))
    act = _peer_u(eid_t, h2, gate_t, _pack_table(peer_u), tg)
    return _peer_v(eid_t, act, x1, gate2, g_post_ffn.reshape(1, d), tab_v, rows_v, seq, tg, nvld, STREAM_STAGES)


def kernel(x, c, positions, w_mod, b_mod, g_pre_mix, g_post_mix, w_in, conv_w, b_igate, b_fgate, mlstm_norm_g, att_sinks, w_out, g_pre_ffn, g_post_ffn, peer_wq, peer_keys1, peer_keys2, peer_u, peer_v):
    bsz, seq, d = x.shape
    n = bsz * seq
    x2 = x.reshape(n, d)
    pos_col = positions.reshape(n, 1)
    for l in range(w_mod.shape[0]):
        x2 = _layer(x2, c, pos_col, bsz, seq, w_mod[l], b_mod[l], g_pre_mix[l], g_post_mix[l], w_in[l], conv_w[l],
                    b_igate[l], b_fgate[l], mlstm_norm_g[l], att_sinks[l], w_out[l], g_pre_ffn[l], g_post_ffn[l],
                    peer_wq[l], peer_keys1[l], peer_keys2[l], peer_u[l], peer_v[l])
    return x2.reshape(bsz, seq, d)
```

```python
import functools

import jax
import jax.numpy as jnp
from jax import lax
from jax.experimental import pallas as pl
from jax.experimental.pallas import tpu as pltpu
from jax.experimental.pallas import tpu_sc as plsc

F32 = jnp.float32
BF16 = jnp.bfloat16

ATT_HEADS = 8
ATT_KV_HEADS = 2
ATT_HEAD_DIM = 64
ATT_BLOCK = 128
ROPE_THETA = 10000.0
MLSTM_HEADS = 4
MLSTM_V_DIM = 128
MLSTM_QK_DIM = 64
MLSTM_CHUNK = 64
CONV_WIDTH = 4
PEER_HEADS = 8
PEER_KEYS = 128
PEER_HALF = 128
PEER_TOPK = 16
NORM_EPS = 1e-6

LANES = 128
SUBLANES = 8
VMEM_LIMIT = 52 * 1024 * 1024
GATHER_STAGES = 16
ROW_WORDS = SUBLANES // 2
MLSTM_GROUP = 4
STREAM_SPLIT = (4, 4)
STREAM_RING = 4
STREAM_STAGES = 8
STREAM_HALVES = 2
STREAM_ROW = 2 * LANES
GATHER_TOKENS = 512
GATHER_VMEM_LIMIT = 58 * 1024 * 1024

NEG_INF = float("-inf")
NT_DIMS = (((1,), (1,)), ((), ()))
TN_DIMS = (((0,), (0,)), ((), ()))


def _params(sem, vmem=None):
    return pltpu.CompilerParams(dimension_semantics=sem, vmem_limit_bytes=vmem)


def _rms(x, g):
    return x * lax.rsqrt(jnp.mean(x * x, axis=-1, keepdims=True) + NORM_EPS) * g


def _mod_kernel(c_ref, w_ref, b_ref, o_ref):
    o_ref[...] = jnp.dot(c_ref[...], w_ref[...], preferred_element_type=F32,
                         precision=lax.Precision.HIGHEST) + b_ref[...]


def _mod(c, w, b):
    bsz, d = c.shape
    nout = w.shape[1]
    return pl.pallas_call(
        _mod_kernel,
        grid=(nout // d,),
        in_specs=[pl.BlockSpec((bsz, d), lambda i: (0, 0)),
                  pl.BlockSpec((d, d), lambda i: (0, i)),
                  pl.BlockSpec((1, d), lambda i: (0, i))],
        out_specs=pl.BlockSpec((bsz, d), lambda i: (0, i)),
        out_shape=jax.ShapeDtypeStruct((bsz, nout), F32),
        compiler_params=_params(("arbitrary",)),
        name="mod",
    )(c, w, b.reshape(1, nout))


def _rope_tab_kernel(pos_ref, inv_ref, sign_ref, cos_ref, sin_ref):
    ang = pos_ref[...].astype(F32) * inv_ref[...]
    cos_ref[...] = jnp.cos(ang)
    sin_ref[...] = jnp.sin(ang) * sign_ref[...]


def _rope_tab(pos_col):
    n = pos_col.shape[0]
    tr = min(n, 1024)
    half = ATT_HEAD_DIM // 2
    inv = ROPE_THETA ** (-jnp.arange(0, ATT_HEAD_DIM, 2, dtype=F32) / ATT_HEAD_DIM)
    inv_row = jnp.tile(inv, LANES // half).reshape(1, LANES)
    lane = jnp.arange(LANES)
    sign_row = jnp.where((lane % ATT_HEAD_DIM) < half, -1.0, 1.0).astype(F32).reshape(1, LANES)
    return pl.pallas_call(
        _rope_tab_kernel,
        grid=(n // tr,),
        in_specs=[pl.BlockSpec((tr, 1), lambda i: (i, 0)),
                  pl.BlockSpec((1, LANES), lambda i: (0, 0)),
                  pl.BlockSpec((1, LANES), lambda i: (0, 0))],
        out_specs=[pl.BlockSpec((tr, LANES), lambda i: (i, 0))] * 2,
        out_shape=[jax.ShapeDtypeStruct((n, LANES), F32)] * 2,
        compiler_params=_params(("arbitrary",)),
        name="rope_tab",
    )(pos_col, inv_row, sign_row)


def _rope(v, cos, sin):
    half = ATT_HEAD_DIM // 2
    lane = lax.broadcasted_iota(jnp.int32, cos.shape, 1)
    first = (lane % ATT_HEAD_DIM) < half
    outs = []
    for j in range(v.shape[1] // LANES):
        c = v[:, j * LANES:(j + 1) * LANES]
        rot = jnp.where(first, pltpu.roll(c, LANES - half, 1), pltpu.roll(c, half, 1))
        outs.append(c * cos + rot * sin)
    return jnp.concatenate(outs, axis=1)


_C_Q, _C_K, _C_V, _C_MQK, _C_MV, _C_MO, _C_G, _C_END = 0, 512, 768, 1024, 1536, 2048, 2560, 2688


def _in_proj_kernel(x_ref, sc_ref, sh_ref, g_ref, cos_ref, sin_ref, w_ref,
                    q_ref, k_ref, v_ref, mqk_ref, mv_ref, mo_ref, gt_ref):
    x = x_ref[...]
    h = _rms(x, g_ref[...]) * (1.0 + sc_ref[0]) + sh_ref[0]
    hb = h.astype(BF16)

    def mm(a, b):
        return jnp.dot(hb, w_ref[:, a:b], preferred_element_type=F32)

    cos = cos_ref[...]
    sin = sin_ref[...]
    q_ref[...] = (_rope(mm(_C_Q, _C_K), cos, sin) * (ATT_HEAD_DIM ** -0.5)).astype(BF16)
    k_ref[...] = _rope(mm(_C_K, _C_V), cos, sin).astype(BF16)
    v_ref[...] = mm(_C_V, _C_MQK).astype(BF16)
    mqk_ref[...] = mm(_C_MQK, _C_MV)
    mv_ref[...] = mm(_C_MV, _C_MO).astype(BF16)
    mo_ref[...] = mm(_C_MO, _C_G)
    gt_ref[...] = mm(_C_G, _C_END)


def _in_proj(x2, scale1, shift1, g_pre, cos, sin, w_all, seq, tm):
    n, d = x2.shape
    per_b = seq // tm
    row = lambda i: (i, 0)
    bsel = lambda i: (i // per_b, 0, 0)
    widths = (512, 256, 256, 512, 512, 512, 128)
    dtypes = (BF16, BF16, BF16, F32, BF16, F32, F32)
    return pl.pallas_call(
        _in_proj_kernel,
        grid=(n // tm,),
        in_specs=[pl.BlockSpec((tm, d), row),
                  pl.BlockSpec((1, 1, d), bsel),
                  pl.BlockSpec((1, 1, d), bsel),
                  pl.BlockSpec((1, d), lambda i: (0, 0)),
                  pl.BlockSpec((tm, LANES), row),
                  pl.BlockSpec((tm, LANES), row),
                  pl.BlockSpec((d, _C_END), lambda i: (0, 0))],
        out_specs=[pl.BlockSpec((tm, w), row) for w in widths],
        out_shape=[jax.ShapeDtypeStruct((n, w), dt) for w, dt in zip(widths, dtypes)],
        compiler_params=_params(("arbitrary",), VMEM_LIMIT),
        name="in_proj",
    )(x2, scale1, shift1, g_pre, cos, sin, w_all)


def _swa_kernel(sink_ref, q_ref, kp_ref, kc_ref, vp_ref, vc_ref, o_ref):
    blk = ATT_BLOCK
    n = pl.program_id(1)
    qi = lax.broadcasted_iota(jnp.int32, (blk, 2 * blk), 0)
    si = lax.broadcasted_iota(jnp.int32, (blk, 2 * blk), 1)
    delta = qi + blk - si
    valid = (delta >= 0) & (delta < blk) & ((si >= blk) | (n > 0))
    lo = lax.broadcasted_iota(jnp.int32, (2 * blk, LANES), 1) < ATT_HEAD_DIM
    group = ATT_HEADS // ATT_KV_HEADS
    kv = []
    for g in range(ATT_KV_HEADS):
        cs = slice(g * LANES, (g + 1) * LANES)
        k = jnp.concatenate([kp_ref[:, cs], kc_ref[:, cs]], axis=0)
        v = jnp.concatenate([vp_ref[:, cs], vc_ref[:, cs]], axis=0)
        zero = jnp.zeros_like(k)
        kv.append(((jnp.where(lo, k, zero), jnp.where(lo, v, zero)),
                   (jnp.where(lo, zero, k), jnp.where(lo, zero, v))))
    heads = [(h, kv[h // group][h % 2]) for h in range(ATT_HEADS)]
    sinks = [sink_ref[h] for h in range(ATT_HEADS)]
    scores = [jnp.where(valid, lax.dot_general(q_ref[:, (h // 2) * LANES:(h // 2 + 1) * LANES], kh, NT_DIMS,
                                               preferred_element_type=F32), NEG_INF) for h, (kh, _) in heads]
    tops = [jnp.maximum(jnp.max(s, axis=-1, keepdims=True), sinks[h]) for h, s in enumerate(scores)]
    exps = [jnp.exp(s - m) for s, m in zip(scores, tops)]
    dens = [jnp.sum(e, axis=-1, keepdims=True) + jnp.exp(sinks[h] - tops[h]) for h, e in enumerate(exps)]
    outs = [jnp.dot((e / d).astype(BF16), vh, preferred_element_type=F32)
            for e, d, (_, (_, vh)) in zip(exps, dens, heads)]
    for p in range(ATT_HEADS // 2):
        o_ref[:, p * LANES:(p + 1) * LANES] = (outs[2 * p] + outs[2 * p + 1]).astype(BF16)


def _swa(sinks, q, kd, vd, bsz, seq):
    n = q.shape[0]
    nb = seq // ATT_BLOCK
    cur = lambda b, i: (b * nb + i, 0)
    prev = lambda b, i: (b * nb + jnp.maximum(i - 1, 0), 0)
    return pl.pallas_call(
        _swa_kernel,
        grid=(bsz, nb),
        in_specs=[pl.BlockSpec(memory_space=pltpu.SMEM),
                  pl.BlockSpec((ATT_BLOCK, 512), cur),
                  pl.BlockSpec((ATT_BLOCK, 256), prev),
                  pl.BlockSpec((ATT_BLOCK, 256), cur),
                  pl.BlockSpec((ATT_BLOCK, 256), prev),
                  pl.BlockSpec((ATT_BLOCK, 256), cur)],
        out_specs=pl.BlockSpec((ATT_BLOCK, 512), cur),
        out_shape=jax.ShapeDtypeStruct((n, 512), BF16),
        compiler_params=_params(("arbitrary", "arbitrary")),
        name="swa",
    )(sinks, q, kd, kd, vd, vd)


def _mlstm_kernel(mqk_all, mv_all, mo_all, gt_all, cw_ref, gb_ref, ng_ref, o_all,
                  tail_all, qk_all, xs_all, ct_all, n_all, m_all, *, chunks, group):
    @pl.when(pl.program_id(1) == 0)
    def _():
        for ref in (tail_all, ct_all, n_all, m_all):
            ref[...] = jnp.zeros_like(ref)

    for g in range(group):
        _mlstm_prepare(mqk_all.at[g], gt_all.at[g], tail_all.at[g], qk_all.at[g], xs_all.at[g], cw_ref, gb_ref,
                       chunks)
    seqs = [tuple(r.at[g] for r in (mv_all, mo_all, o_all, qk_all, xs_all, ct_all, n_all, m_all))
            for g in range(group)]
    lax.fori_loop(0, chunks, lambda c, carry: _mlstm_chunk(c, seqs, ng_ref, carry), 0)


def _mlstm_prepare(mqk_ref, gt_ref, tail_ref, qk_ref, xs_ref, cw_ref, gb_ref, chunks):
    tm = chunks * MLSTM_CHUNK
    nqk = MLSTM_HEADS * MLSTM_QK_DIM

    cur = mqk_ref[...]
    full = jnp.concatenate([tail_ref[...], cur], axis=0)
    off = SUBLANES - (CONV_WIDTH - 1)
    acc = full[off:off + tm] * cw_ref[0:1, :]
    for j in range(1, CONV_WIDTH):
        acc = acc + full[off + j:off + j + tm] * cw_ref[j:j + 1, :]
    act = acc * jax.nn.sigmoid(acc)
    col = lax.broadcasted_iota(jnp.int32, (1, 2 * nqk), 1)
    act = act * jnp.where(col < nqk, MLSTM_QK_DIM ** -0.5, 1.0)
    qk_ref[...] = act.astype(BF16)
    tail_ref[...] = cur[tm - SUBLANES:tm]

    lane = lax.broadcasted_iota(jnp.int32, (tm, LANES), 1)
    gts = gt_ref[...] + gb_ref[...]
    logsig = jnp.minimum(gts, 0.0) - jnp.log(1.0 + jnp.exp(-jnp.abs(gts)))
    xs_ref[...] = jnp.where(lane < MLSTM_HEADS, gts, jnp.where(lane < 2 * MLSTM_HEADS, logsig, 0.0))


def _mlstm_chunk(c, seqs, ng_ref, carry):
    L = MLSTM_CHUNK
    nqk = MLSTM_HEADS * MLSTM_QK_DIM
    ri = lax.broadcasted_iota(jnp.int32, (L, L), 0)
    ci = lax.broadcasted_iota(jnp.int32, (L, L), 1)
    causal = ci <= ri
    tril = causal.astype(F32)
    lane_l = lax.broadcasted_iota(jnp.int32, (L, LANES), 1)
    lo_l = lane_l < MLSTM_QK_DIM
    row_c = lax.broadcasted_iota(jnp.int32, (LANES, 1), 0) < MLSTM_QK_DIM
    lane_1 = lax.broadcasted_iota(jnp.int32, (1, LANES), 1) < MLSTM_QK_DIM
    rows = pl.ds(pl.multiple_of(c * L, L), L)
    G = range(len(seqs))
    U = [(g, h) for g in G for h in range(MLSTM_HEADS)]
    P = [(g, p) for g in G for p in range(MLSTM_HEADS // 2)]
    mv_r, mo_r, o_r, qk_r, xs_r, ct_r, n_r, m_r = (dict(enumerate(col)) for col in zip(*seqs))
    pair = lambda u: (u[0], u[1] // 2)

    xc = {g: xs_r[g][rows, :] for g in G}
    bc = {g: jnp.dot(tril, xc[g], preferred_element_type=F32, precision=lax.Precision.HIGHEST) for g in G}
    x2 = {g: jnp.where(lane_l < MLSTM_HEADS, xc[g], bc[g]) for g in G}
    xt = {g: x2[g].T for g in G}
    q2 = {(g, p): qk_r[g][rows, p * LANES:(p + 1) * LANES] for g, p in P}
    k2 = {(g, p): qk_r[g][rows, nqk + p * LANES:nqk + (p + 1) * LANES] for g, p in P}
    ct_old = {(g, p): ct_r[g][p] for g, p in P}
    ctb = {gp: ct_old[gp].astype(BF16) for gp in P}
    n2 = {(g, p): n_r[g][p:p + 1, :] for g, p in P}
    hm = {u: lo_l if u[1] % 2 == 0 else jnp.logical_not(lo_l) for u in U}
    qm = {u: jnp.where(hm[u], q2[pair(u)], jnp.zeros_like(q2[pair(u)])) for u in U}
    km = {u: jnp.where(hm[u], k2[pair(u)], jnp.zeros_like(k2[pair(u)])) for u in U}
    v = {(g, h): mv_r[g][rows, h * LANES:(h + 1) * LANES] for g, h in U}
    b_col = {(g, h): x2[g][:, MLSTM_HEADS + h:MLSTM_HEADS + h + 1] for g, h in U}
    ig_col = {(g, h): x2[g][:, h:h + 1] for g, h in U}
    b_row = {(g, h): xt[g][MLSTM_HEADS + h:MLSTM_HEADS + h + 1, :] for g, h in U}
    ig_row = {(g, h): xt[g][h:h + 1, :] for g, h in U}
    m_prev = {(g, h): m_r[g][h:h + 1, 0:1] for g, h in U}
    dlog = {u: jnp.where(causal, b_col[u] - b_row[u] + ig_row[u], NEG_INF) for u in U}
    m_inter = {u: b_col[u] + m_prev[u] for u in U}
    m_t = {u: jnp.maximum(m_inter[u], jnp.max(dlog[u], axis=-1, keepdims=True)) for u in U}
    w_intra = {u: jnp.exp(dlog[u] - m_t[u]) for u in U}
    a_inter = {u: jnp.exp(m_inter[u] - m_t[u]) for u in U}
    s = {u: lax.dot_general(q2[pair(u)], km[u], NT_DIMS, preferred_element_type=F32) * w_intra[u] for u in U}
    num = {u: jnp.dot(s[u].astype(BF16), v[u], preferred_element_type=F32)
           + a_inter[u] * jnp.dot(qm[u], ctb[pair(u)], preferred_element_type=F32) for u in U}
    den = {u: jnp.sum(s[u], axis=-1, keepdims=True)
           + a_inter[u] * jnp.sum(qm[u].astype(F32) * n2[pair(u)], axis=-1, keepdims=True) for u in U}
    hh = {u: num[u] / jnp.maximum(jnp.abs(den[u]), jnp.exp(-m_t[u])) for u in U}
    y = {(g, h): _rms(hh[g, h], ng_ref[:, h * LANES:(h + 1) * LANES]) for g, h in U}
    for g, h in U:
        o_r[g][rows, h * LANES:(h + 1) * LANES] = (
            y[g, h] * jax.nn.sigmoid(mo_r[g][rows, h * LANES:(h + 1) * LANES])).astype(BF16)
    b_last = {(g, h): xt[g][MLSTM_HEADS + h:MLSTM_HEADS + h + 1, L - 1:L] for g, h in U}
    g_col = {u: b_last[u] - b_col[u] + ig_col[u] for u in U}
    m_new = {u: jnp.maximum(b_last[u] + m_prev[u], jnp.max(g_col[u], axis=0, keepdims=True)) for u in U}
    kw = {u: km[u].astype(F32) * jnp.exp(g_col[u] - m_new[u]) for u in U}
    dec = {u: jnp.exp(b_last[u] + m_prev[u] - m_new[u]) for u in U}
    upd = {u: lax.dot_general(kw[u].astype(BF16), v[u], TN_DIMS, preferred_element_type=F32) for u in U}
    for g, h in U:
        m_r[g][h:h + 1, :] = jnp.broadcast_to(m_new[g, h], (1, LANES))
    for g, p in P:
        e, o = (g, 2 * p), (g, 2 * p + 1)
        ct_r[g][p] = ct_old[g, p] * jnp.where(row_c, dec[e], dec[o]) + upd[e] + upd[o]
        n_r[g][p:p + 1, :] = (n2[g, p] * jnp.where(lane_1, dec[e], dec[o])
                              + jnp.sum(kw[e] + kw[o], axis=0, keepdims=True))
    return carry


def _mlstm(mqk, mv, mo, gts, conv_w, gate_bias, norm_g, bsz, seq, chunks, group):
    n = mqk.shape[0]
    tm = chunks * MLSTM_CHUNK
    steps = seq // tm
    row = lambda b, i: (b, i, 0)
    const = lambda b, i: (0, 0)
    width = MLSTM_HEADS * MLSTM_V_DIM
    per_seq = lambda a: a.reshape(bsz, seq, a.shape[1])
    out = pl.pallas_call(
        functools.partial(_mlstm_kernel, chunks=chunks, group=group),
        grid=(bsz // group, steps),
        in_specs=[pl.BlockSpec((group, tm, width), row),
                  pl.BlockSpec((group, tm, width), row),
                  pl.BlockSpec((group, tm, width), row),
                  pl.BlockSpec((group, tm, LANES), row),
                  pl.BlockSpec((CONV_WIDTH, width), const),
                  pl.BlockSpec((1, LANES), const),
                  pl.BlockSpec((1, width), const)],
        out_specs=pl.BlockSpec((group, tm, width), row),
        out_shape=jax.ShapeDtypeStruct((bsz, seq, width), BF16),
        scratch_shapes=[pltpu.VMEM((group, SUBLANES, width), F32),
                        pltpu.VMEM((group, tm, width), BF16),
                        pltpu.VMEM((group, tm, LANES), F32),
                        pltpu.VMEM((group, MLSTM_HEADS // 2, LANES, LANES), F32),
                        pltpu.VMEM((group, SUBLANES, LANES), F32),
                        pltpu.VMEM((group, SUBLANES, LANES), F32)],
        compiler_params=_params(("arbitrary", "arbitrary")),
        name="mlstm",
    )(per_seq(mqk), per_seq(mv), per_seq(mo), per_seq(gts), conv_w, gate_bias, norm_g)
    return out.reshape(n, width)


def _out_proj_kernel(att_ref, mh_ref, x_ref, g1_ref, sc_ref, sh_ref, gpm_ref, gpf_ref, wa_ref, wb_ref,
                     x1_ref, h2_ref):
    mix = (jnp.dot(att_ref[...], wa_ref[...], preferred_element_type=F32)
           + jnp.dot(mh_ref[...], wb_ref[...], preferred_element_type=F32))
    x1 = x_ref[...] + g1_ref[0] * _rms(mix, gpm_ref[...])
    x1_ref[...] = x1
    h2_ref[...] = _rms(x1, gpf_ref[...]) * (1.0 + sc_ref[0]) + sh_ref[0]


def _out_proj(att, mh, x2, gate1, scale2, shift2, g_post_mix, g_pre_ffn, wa, wb, seq, tm):
    n, d = x2.shape
    per_b = seq // tm
    row = lambda i: (i, 0)
    bsel = lambda i: (i // per_b, 0, 0)
    const = lambda i: (0, 0)
    half = att.shape[1]
    return pl.pallas_call(
        _out_proj_kernel,
        grid=(n // tm,),
        in_specs=[pl.BlockSpec((tm, half), row), pl.BlockSpec((tm, half), row), pl.BlockSpec((tm, d), row),
                  pl.BlockSpec((1, 1, d), bsel), pl.BlockSpec((1, 1, d), bsel), pl.BlockSpec((1, 1, d), bsel),
                  pl.BlockSpec((1, d), const), pl.BlockSpec((1, d), const),
                  pl.BlockSpec((half, d), const), pl.BlockSpec((half, d), const)],
        out_specs=[pl.BlockSpec((tm, d), row)] * 2,
        out_shape=[jax.ShapeDtypeStruct((n, d), F32)] * 2,
        compiler_params=_params(("arbitrary",), VMEM_LIMIT),
        name="out_proj",
    )(att, mh, x2, gate1, scale2, shift2, g_post_mix, g_pre_ffn, wa, wb)


_BIG_ID = float(2 ** 30)
SORT_LEVELS = 8


def _top_scores(s, k):
    rows, t = s.shape
    span = SORT_LEVELS * SUBLANES
    r = lax.broadcasted_iota(jnp.int32, (rows // SORT_LEVELS, t), 0)
    col_id = ((r // SUBLANES) * span + r % SUBLANES).astype(F32)
    lev = [jnp.concatenate([s[g * span + l * SUBLANES:g * span + (l + 1) * SUBLANES] for g in range(rows // span)],
                           axis=0) for l in range(SORT_LEVELS)]
    ids = [col_id + float(l * SUBLANES) for l in range(SORT_LEVELS)]
    for rnd in range(SORT_LEVELS):
        for a in range(rnd % 2, SORT_LEVELS - 1, 2):
            swap = lev[a + 1] > lev[a]
            lev[a], lev[a + 1] = jnp.where(swap, lev[a + 1], lev[a]), jnp.where(swap, lev[a], lev[a + 1])
            ids[a], ids[a + 1] = jnp.where(swap, ids[a + 1], ids[a]), jnp.where(swap, ids[a], ids[a + 1])
    vals, sel = [], []
    for _ in range(k):
        m = jnp.max(lev[0], axis=0, keepdims=True)
        i = jnp.min(jnp.where(lev[0] == m, ids[0], _BIG_ID), axis=0, keepdims=True)
        vals.append(m)
        sel.append(i)
        hit = ids[0] == i
        for l in range(SORT_LEVELS - 1):
            lev[l] = jnp.where(hit, lev[l + 1], lev[l])
            ids[l] = jnp.where(hit, ids[l + 1], ids[l])
        lev[-1] = jnp.where(hit, NEG_INF, lev[-1])
    return jnp.concatenate(vals, axis=0), jnp.concatenate(sel, axis=0).astype(jnp.int32)


def _top_pair_sums(v1, v2):
    k, t = v1.shape
    half = SUBLANES
    lev = [v1[0:half] + v2[b:b + 1, :] for b in range(k)]
    side = v1[half:k] + v2[0:1, :]
    a_low = lax.broadcasted_iota(jnp.int32, (half, t), 0).astype(F32) * float(k)
    side_id = a_low + float(half * k)
    depth = jnp.zeros((half, t), F32)
    vals, sel = [], []
    for it in range(k):
        top_id = a_low + depth
        m = jnp.max(jnp.maximum(lev[0], side), axis=0, keepdims=True)
        i = jnp.min(jnp.minimum(jnp.where(lev[0] == m, top_id, _BIG_ID), jnp.where(side == m, side_id, _BIG_ID)),
                    axis=0, keepdims=True)
        vals.append(m)
        sel.append(i)
        hit = top_id == i
        for l in range(k - 1 - it):
            lev[l] = jnp.where(hit, lev[l + 1], lev[l])
        side = jnp.where(side_id == i, NEG_INF, side)
        depth = depth + jnp.where(hit, 1.0, 0.0)
    return jnp.concatenate(vals, axis=0), jnp.concatenate(sel, axis=0).astype(jnp.int32)


def _pick_rows(table, which):
    r = lax.broadcasted_iota(jnp.int32, table.shape, 0)
    rows = []
    for k in range(which.shape[0]):
        rows.append(jnp.sum(jnp.where(r == which[k:k + 1, :], table, 0), axis=0, keepdims=True))
    return jnp.concatenate(rows, axis=0)


def _peer_sel_kernel(h_ref, wq_ref, k1_ref, k2_ref, e_ref, g_ref, r_ref):
    tq = h_ref.shape[0]
    K = PEER_TOPK
    q = jnp.dot(h_ref[...].astype(BF16), wq_ref[...], preferred_element_type=F32).astype(BF16)
    for hd in range(PEER_HEADS):
        base = hd * 2 * PEER_HALF
        s1 = lax.dot_general(k1_ref[hd], q[:, base:base + PEER_HALF], NT_DIMS, preferred_element_type=F32)
        s2 = lax.dot_general(k2_ref[hd], q[:, base + PEER_HALF:base + 2 * PEER_HALF], NT_DIMS,
                             preferred_element_type=F32)
        v1, i1 = _top_scores(s1, K)
        v2, i2 = _top_scores(s2, K)
        top, pos = _top_pair_sums(v1, v2)
        eid = _pick_rows(i1, pos >> 4) * PEER_KEYS + _pick_rows(i2, pos & (K - 1))
        ex = jnp.exp(top - top[0:1, :])
        e_ref[hd * K:(hd + 1) * K, :] = eid * ROW_WORDS
        g_ref[hd * K:(hd + 1) * K, :] = ex / jnp.sum(ex, axis=0, keepdims=True)
    pair = (e_ref[...] >> 1).T
    nsel = pair.shape[1]
    for h in range(STREAM_HALVES):
        r_ref[:, h * nsel:(h + 1) * nsel] = pair + h


def _peer_sel(h2, wq, k1, k2, tq):
    n, d = h2.shape
    rows = PEER_HEADS * PEER_TOPK
    return pl.pallas_call(
        _peer_sel_kernel,
        grid=(n // tq,),
        in_specs=[pl.BlockSpec((tq, d), lambda i: (i, 0)),
                  pl.BlockSpec(wq.shape, lambda i: (0, 0)),
                  pl.BlockSpec(k1.shape, lambda i: (0, 0, 0)),
                  pl.BlockSpec(k2.shape, lambda i: (0, 0, 0))],
        out_specs=[pl.BlockSpec((rows, tq), lambda i: (0, i))] * 2
        + [pl.BlockSpec((tq, rows * STREAM_HALVES), lambda i: (i, 0))],
        out_shape=[jax.ShapeDtypeStruct((rows, n), jnp.int32), jax.ShapeDtypeStruct((rows, n), F32),
                   jax.ShapeDtypeStruct((n, rows * STREAM_HALVES), jnp.int32)],
        compiler_params=_params(("arbitrary",), VMEM_LIMIT),
        name="peer_sel",
    )(h2, wq, k1, k2)


def _split_bf16(x):
    hi = x.astype(BF16)
    return hi, (x - hi.astype(F32)).astype(BF16)


def _pack_kernel(w_ref, o_ref):
    x = w_ref[...]
    eb = x.shape[0]
    for r in range(ROW_WORDS):
        lo = x[:, 2 * r * LANES:(2 * r + 1) * LANES].astype(BF16).astype(F32)
        hi = x[:, (2 * r + 1) * LANES:(2 * r + 2) * LANES].astype(BF16).astype(F32)
        word = (lax.shift_right_logical(pltpu.bitcast(lo, jnp.int32), 16)
                | (pltpu.bitcast(hi, jnp.int32) & jnp.int32(-65536)))
        o_ref[pl.ds(r, eb, stride=ROW_WORDS), :] = word


def _pack_table(w):
    e, d = w.shape
    eb = 512
    return pl.pallas_call(
        _pack_kernel,
        grid=(e // eb,),
        in_specs=[pl.BlockSpec((eb, d), lambda i: (i, 0))],
        out_specs=pl.BlockSpec((eb * ROW_WORDS, LANES), lambda i: (i, 0)),
        out_shape=jax.ShapeDtypeStruct((e * ROW_WORDS, LANES), jnp.int32),
        compiler_params=_params(("arbitrary",)),
        name="pack_table",
    )(w)


def _gather_rows(slots, tab_ref, t, stage_ref):
    for j, slot in enumerate(slots):
        src = pl.ds(pl.multiple_of(slot[t], ROW_WORDS), ROW_WORDS)
        stage_ref[j * ROW_WORDS:(j + 1) * ROW_WORDS, :] = tab_ref[src, :]


def _staged_bf16(stage_ref):
    return pltpu.bitcast(stage_ref[...], BF16)


def _pipelined_tokens(ntok, gather, compute, stages, on_trip=None, after=None):
    nb = len(stages)
    for k in range(nb):
        gather(k, stages[k])

    def trip(i, carry):
        t = nb * i
        if on_trip is not None:
            on_trip(i)
        for k in range(nb):
            compute(t + k, stages[k])
            ahead = t + k + nb
            gather(jnp.minimum(ahead, ntok - 1), stages[k])
            if after is not None:
                after(i, k)
        return carry

    lax.fori_loop(0, ntok // nb, trip, 0)


def _with_slot_indices(idx_hbm, sems, bufs, stride, count, run):
    nsel = len(bufs) // 2
    step = pl.program_id(0)
    last = pl.num_programs(0) - 1

    def copies(block, which):
        return [pltpu.make_async_copy(idx_hbm.at[j, pl.ds(block * stride, count)], bufs[which * nsel + j],
                                      sems.at[which])
                for j in range(nsel)]

    @pl.when(step == 0)
    def _():
        for cp in copies(0, 0):
            cp.start()

    def phase(which):
        for cp in copies(step, which):
            cp.wait()

        @pl.when(step < last)
        def _():
            for cp in copies(step + 1, 1 - which):
                cp.start()

        run(bufs[which * nsel:(which + 1) * nsel])

    for which in range(2):
        pl.when(step % 2 == which)(functools.partial(phase, which))


SC_WINDOW = 128


def _sc_gather_rows(tab, row_ids):
    n = row_ids.shape[0]
    mesh = plsc.VectorSubcoreMesh(core_axis_name="core", subcore_axis_name="subcore")

    @pl.kernel(out_type=jax.ShapeDtypeStruct((n, tab.shape[1]), tab.dtype), mesh=mesh)
    def gather(tab_hbm, ids_hbm, out_hbm):
        def body(ids_vmem, out_vmem):
            pltpu.sync_copy(tab_hbm.at[ids_vmem.at[0]], out_vmem)

        pltpu.emit_pipeline(
            body,
            grid=(n // SC_WINDOW,),
            in_specs=[pl.BlockSpec((1, SC_WINDOW), index_map=lambda i: (0, i))],
            out_specs=[pl.BlockSpec((SC_WINDOW, tab.shape[1]), index_map=lambda i: (i, 0))],
            core_axis_name=("core", "subcore"),
            dimension_semantics=(pltpu.PARALLEL,),
            trace_scopes=False,
        )(ids_hbm, out_hbm)

    return gather(tab, row_ids.reshape(1, n))


def _diag_mask(nsel):
    shape = (SUBLANES, nsel * SUBLANES)
    return (lax.broadcasted_iota(jnp.int32, shape, 1) % SUBLANES) == lax.broadcasted_iota(jnp.int32, shape, 0)


def _token_tile(ref, t):
    row = ref[pl.ds(t, 1), :]
    return jnp.concatenate([row[:, s * LANES:(s + 1) * LANES] for s in range(SUBLANES)], axis=0)


def _peer_u_kernel(idx_hbm, h_ref, gate_ref, grp_ref, tab_ref, act_ref, part_ref, *scratch):
    nsel, tq = gate_ref.shape
    stages, sems, bufs = scratch[:GATHER_STAGES], scratch[GATHER_STAGES], scratch[GATHER_STAGES + 1:]
    diag = _diag_mask(nsel)

    def compute(t, stage_ref):
        h_hi, h_lo = _split_bf16(_token_tile(h_ref, t))
        both = lax.dot_general(jnp.concatenate([h_hi, h_lo], axis=0), _staged_bf16(stage_ref), NT_DIMS,
                               preferred_element_type=F32)
        prod = both[0:SUBLANES] + both[SUBLANES:2 * SUBLANES]
        part_ref[pl.ds(t, 1), :] = jnp.sum(jnp.where(diag, prod, 0.0), axis=0, keepdims=True)

    def run(slots):
        _pipelined_tokens(tq, functools.partial(_gather_rows, slots, tab_ref), compute, stages)

    _with_slot_indices(idx_hbm, sems, bufs, tq, tq, run)
    p_hi, p_lo = _split_bf16(part_ref[...])
    grp = grp_ref[...]
    pre = jnp.dot(p_hi, grp, preferred_element_type=F32) + jnp.dot(p_lo, grp, preferred_element_type=F32)
    act_ref[...] = 0.5 * pre * (1.0 + lax.erf(pre * (2.0 ** -0.5))) * gate_ref[...].T


def _group_matrix(nsel):
    r = jnp.arange(nsel * SUBLANES)[:, None] // SUBLANES
    return (r == jnp.arange(nsel)[None, :]).astype(BF16)


def _gather_scratch(nsel, ntok, nstages):
    return ([pltpu.VMEM((nsel * ROW_WORDS, LANES), jnp.int32)] * nstages
            + [pltpu.SemaphoreType.DMA((2,))] + [pltpu.SMEM((ntok,), jnp.int32)] * (2 * nsel))


def _peer_u(idx_t, h2, gate_t, tab, tq):
    nsel, n = idx_t.shape
    d = h2.shape[1]
    wide = nsel * SUBLANES
    return pl.pallas_call(
        _peer_u_kernel,
        grid=(n // tq,),
        in_specs=[pl.BlockSpec(memory_space=pl.ANY),
                  pl.BlockSpec((tq, d), lambda i: (i, 0)),
                  pl.BlockSpec((nsel, tq), lambda i: (0, i)),
                  pl.BlockSpec((wide, nsel), lambda i: (0, 0)),
                  pl.BlockSpec(memory_space=pltpu.VMEM)],
        out_specs=pl.BlockSpec((tq, nsel), lambda i: (i, 0)),
        out_shape=jax.ShapeDtypeStruct((n, nsel), F32),
        scratch_shapes=[pltpu.VMEM((tq, wide), F32)] + _gather_scratch(nsel, tq, GATHER_STAGES),
        compiler_params=_params(("arbitrary",), GATHER_VMEM_LIMIT),
        name="peer_u",
    )(idx_t, h2, gate_t, _group_matrix(nsel), tab)


def _lane_aligned(count):
    return -(-count // LANES) * LANES


def _streamed_rows_bf16(words):
    x = pltpu.bitcast(words, BF16)
    return jnp.concatenate([x[:, 0:LANES], x[:, LANES:2 * LANES]], axis=0)


def _streamed_copy_matrix(nsel):
    r = jnp.arange(nsel * SUBLANES)
    return ((r[None, :] % (2 * nsel)) // 2 == jnp.arange(nsel)[:, None]).astype(BF16)


def _streamed_diag(nsel):
    shape = (SUBLANES, nsel * SUBLANES)
    r = lax.broadcasted_iota(jnp.int32, shape, 1)
    sub = 4 * ((r % (4 * nsel)) // (2 * nsel)) + 2 * (r // (4 * nsel)) + r % 2
    return lax.broadcasted_iota(jnp.int32, shape, 0) == sub


def _peer_v_kernel(idx_hbm, act_ref, rep_ref, rep2_ref, x1_ref, g2_ref, gpf_ref, tab_ref, rows_hbm, o_ref,
                   wide_ref, y_ref, rowbuf, rsem, *scratch, nvld, nstages):
    tq, nsel = act_ref.shape
    stages, sems, bufs = scratch[:nstages], scratch[nstages], scratch[nstages + 1:]
    trips = nvld // nstages
    per_trip = (tq - nvld) // trips
    rows_tok = nsel * STREAM_HALVES
    step = pl.program_id(0)
    chunks = pl.num_programs(0) * trips
    diag = _diag_mask(nsel)
    diag2 = _streamed_diag(nsel)
    a_hi, a_lo = _split_bf16(act_ref[...])
    for lo, hi, rep in ((0, nvld, rep_ref[...]), (nvld, tq, rep2_ref[...])):
        wide_ref[lo:hi, :] = (jnp.dot(a_hi[lo:hi], rep, preferred_element_type=F32)
                              + jnp.dot(a_lo[lo:hi], rep, preferred_element_type=F32))

    def finish(t, mask, rows_bf16):
        w = jnp.where(mask, jnp.broadcast_to(wide_ref[pl.ds(t, 1), :], mask.shape), 0.0)
        w_hi, w_lo = _split_bf16(w)
        both = jnp.dot(jnp.concatenate([w_hi, w_lo], axis=0), rows_bf16, preferred_element_type=F32)
        tile = both[0:SUBLANES] + both[SUBLANES:2 * SUBLANES]
        y_ref[pl.ds(t, 1), :] = jnp.concatenate([tile[s:s + 1, :] for s in range(SUBLANES)], axis=1)

    def compute(t, stage_ref):
        finish(t, diag, _staged_bf16(stage_ref))

    def chunk_copy(g, slot):
        return pltpu.make_async_copy(rows_hbm.at[pl.ds(g * (per_trip * rows_tok), per_trip * rows_tok), :],
                                     rowbuf.at[slot], rsem.at[slot])

    ring = STREAM_RING

    @pl.when(step == 0)
    def _():
        for g0 in range(ring - 1):
            chunk_copy(g0, g0).start()

    def on_trip(i):
        g = step * trips + i
        chunk_copy(g, i % ring).wait()

        @pl.when(g + ring - 1 < chunks)
        def _():
            chunk_copy(g + ring - 1, (i + ring - 1) % ring).start()

    places = [(d * nstages) // per_trip for d in range(per_trip)]

    def after(i, k):
        for d in range(per_trip):
            if places[d] == k:
                rows = rowbuf.at[i % ring, pl.ds(d * rows_tok, rows_tok), :]
                finish(nvld + i * per_trip + d, diag2, _streamed_rows_bf16(rows[...]))

    def run(slots):
        _pipelined_tokens(nvld, functools.partial(_gather_rows, slots, tab_ref), compute, stages, on_trip, after)

    _with_slot_indices(idx_hbm, sems, bufs, tq, _lane_aligned(nvld), run)
    o_ref[...] = x1_ref[...] + g2_ref[0] * _rms(y_ref[...], gpf_ref[...])


def _peer_v(idx_t, act, x1, gate2, g_post_ffn, tab, rows, seq, tq, nvld, nstages):
    nsel, n = idx_t.shape
    d = x1.shape[1]
    wide = nsel * SUBLANES
    per_b = seq // tq
    row = lambda i: (i, 0)
    const = lambda i: (0, 0)
    trips = nvld // nstages
    per_trip = (tq - nvld) // trips
    assert trips % STREAM_RING == 0 and trips * nstages == nvld and trips * per_trip == tq - nvld
    return pl.pallas_call(
        functools.partial(_peer_v_kernel, nvld=nvld, nstages=nstages),
        grid=(n // tq,),
        in_specs=[pl.BlockSpec(memory_space=pl.ANY),
                  pl.BlockSpec((tq, nsel), row),
                  pl.BlockSpec((nsel, wide), const),
                  pl.BlockSpec((nsel, wide), const),
                  pl.BlockSpec((tq, d), row),
                  pl.BlockSpec((1, 1, d), lambda i: (i // per_b, 0, 0)),
                  pl.BlockSpec((1, d), const),
                  pl.BlockSpec(memory_space=pltpu.VMEM),
                  pl.BlockSpec(memory_space=pl.ANY)],
        out_specs=pl.BlockSpec((tq, d), row),
        out_shape=jax.ShapeDtypeStruct((n, d), F32),
        scratch_shapes=[pltpu.VMEM((tq, wide), F32), pltpu.VMEM((tq, d), F32),
                        pltpu.VMEM((STREAM_RING, per_trip * nsel * STREAM_HALVES, STREAM_ROW), jnp.int32),
                        pltpu.SemaphoreType.DMA((STREAM_RING,))]
        + _gather_scratch(nsel, _lane_aligned(nvld), nstages),
        compiler_params=_params(("arbitrary",), GATHER_VMEM_LIMIT),
        name="peer_v",
    )(idx_t, act, _group_matrix(nsel).T, _streamed_copy_matrix(nsel), x1, gate2, g_post_ffn, tab, rows)


def _dup_heads(w, heads, dh):
    d = w.shape[0]
    return jnp.repeat(w.reshape(d, heads, 1, dh), 2, axis=2).reshape(d, heads * 2 * dh)


def _layer(x2, c, pos_col, bsz, seq, w_mod, b_mod, g_pre_mix, g_post_mix, w_in, conv_w, b_igate, b_fgate,
           mlstm_norm_g, att_sinks, w_out, g_pre_ffn, g_post_ffn, peer_wq, peer_keys1, peer_keys2, peer_u, peer_v):
    n, d = x2.shape
    tm = min(seq, 512)
    mod = _mod(c, w_mod, b_mod)
    shift1, scale1, gate1, shift2, scale2, gate2 = [m.reshape(bsz, 1, d) for m in jnp.split(mod, 6, axis=-1)]

    aw = ATT_HEADS * ATT_HEAD_DIM
    kvw = ATT_KV_HEADS * ATT_HEAD_DIM
    qkw = MLSTM_HEADS * MLSTM_QK_DIM
    mw = MLSTM_HEADS * MLSTM_V_DIM
    o = 0
    wq_a = w_in[:, o:o + aw]; o += aw
    wk_a = w_in[:, o:o + kvw]; o += kvw
    wv_a = w_in[:, o:o + kvw]; o += kvw
    w_mqk = w_in[:, o:o + 2 * qkw]; o += 2 * qkw
    w_mv = w_in[:, o:o + mw]; o += mw
    w_g = w_in[:, o:o + 2 * MLSTM_HEADS]; o += 2 * MLSTM_HEADS
    w_mo = w_in[:, o:o + mw]
    w_gp = jnp.pad(w_g, ((0, 0), (0, LANES - 2 * MLSTM_HEADS)))
    w_all = jnp.concatenate([wq_a, _dup_heads(wk_a, ATT_KV_HEADS, ATT_HEAD_DIM),
                             _dup_heads(wv_a, ATT_KV_HEADS, ATT_HEAD_DIM), w_mqk, w_mv, w_mo, w_gp],
                            axis=1).astype(BF16)

    cos, sin = _rope_tab(pos_col)
    q, kd, vd, mqk, mv, mo, gts = _in_proj(x2, scale1, shift1, g_pre_mix.reshape(1, d), cos, sin, w_all, seq, tm)
    att = _swa(att_sinks, q, kd, vd, bsz, seq)
    gate_bias = jnp.pad(jnp.concatenate([b_igate, b_fgate]), (0, LANES - 2 * MLSTM_HEADS)).reshape(1, LANES)
    chunks = min(seq // MLSTM_CHUNK, 8)
    mh = _mlstm(mqk, mv, mo, gts, conv_w, gate_bias, mlstm_norm_g.reshape(1, mw), bsz, seq, chunks,
                MLSTM_GROUP if bsz % MLSTM_GROUP == 0 else 1)
    wo = w_out.astype(BF16)
    x1, h2 = _out_proj(att, mh, x2, gate1, scale2, shift2, g_post_mix.reshape(1, d), g_pre_ffn.reshape(1, d),
                       wo[:aw], wo[aw:], seq, tm)

    tq = min(n, 256)
    eid_t, gate_t, row_ids = _peer_sel(h2, peer_wq.astype(BF16), peer_keys1.astype(BF16),
                                       peer_keys2.astype(BF16), tq)
    tg = min(seq, GATHER_TOKENS)
    tab_v = _pack_table(peer_v)
    nvld = tg * STREAM_SPLIT[0] // sum(STREAM_SPLIT)
    streamed = row_ids.reshape(n // tg, tg, row_ids.shape[1])[:, nvld:, :]
    rows_v = _sc_gather_rows(tab_v.reshape(-1, STREAM_ROW), streamed.reshape(-1))
    act = _peer_u(eid_t, h2, gate_t, _pack_table(peer_u), tg)
    return _peer_v(eid_t, act, x1, gate2, g_post_ffn.reshape(1, d), tab_v, rows_v, seq, tg, nvld, STREAM_STAGES)


def kernel(x, c, positions, w_mod, b_mod, g_pre_mix, g_post_mix, w_in, conv_w, b_igate, b_fgate, mlstm_norm_g, att_sinks, w_out, g_pre_ffn, g_post_ffn, peer_wq, peer_keys1, peer_keys2, peer_u, peer_v):
    bsz, seq, d = x.shape
    n = bsz * seq
    x2 = x.reshape(n, d)
    pos_col = positions.reshape(n, 1)
    for l in range(w_mod.shape[0]):
        x2 = _layer(x2, c, pos_col, bsz, seq, w_mod[l], b_mod[l], g_pre_mix[l], g_post_mix[l], w_in[l], conv_w[l],
                    b_igate[l], b_fgate[l], mlstm_norm_g[l], att_sinks[l], w_out[l], g_pre_ffn[l], g_post_ffn[l],
                    peer_wq[l], peer_keys1[l], peer_keys2[l], peer_u[l], peer_v[l])
    return x2.reshape(bsz, seq, d)
```

```python
import functools

import jax
import jax.numpy as jnp
from jax import lax
from jax.experimental import pallas as pl
from jax.experimental.pallas import tpu as pltpu
from jax.experimental.pallas import tpu_sc as plsc

F32 = jnp.float32
BF16 = jnp.bfloat16

ATT_HEADS = 8
ATT_KV_HEADS = 2
ATT_HEAD_DIM = 64
ATT_BLOCK = 128
ROPE_THETA = 10000.0
MLSTM_HEADS = 4
MLSTM_V_DIM = 128
MLSTM_QK_DIM = 64
MLSTM_CHUNK = 64
CONV_WIDTH = 4
PEER_HEADS = 8
PEER_KEYS = 128
PEER_HALF = 128
PEER_TOPK = 16
NORM_EPS = 1e-6

LANES = 128
SUBLANES = 8
VMEM_LIMIT = 52 * 1024 * 1024
GATHER_STAGES = 32
ROW_WORDS = SUBLANES // 2
MLSTM_GROUP = 4
STREAM_SPLIT = (4, 4)
STREAM_RING = 4
STREAM_STAGES = 8
STREAM_HALVES = 2
STREAM_ROW = 2 * LANES
GATHER_TOKENS = 512
GATHER_VMEM_LIMIT = 58 * 1024 * 1024

NEG_INF = float("-inf")
NT_DIMS = (((1,), (1,)), ((), ()))
TN_DIMS = (((0,), (0,)), ((), ()))


def _params(sem, vmem=None):
    return pltpu.CompilerParams(dimension_semantics=sem, vmem_limit_bytes=vmem)


def _rms(x, g):
    return x * lax.rsqrt(jnp.mean(x * x, axis=-1, keepdims=True) + NORM_EPS) * g


def _mod_kernel(c_ref, w_ref, b_ref, o_ref):
    o_ref[...] = jnp.dot(c_ref[...], w_ref[...], preferred_element_type=F32,
                         precision=lax.Precision.HIGHEST) + b_ref[...]


def _mod(c, w, b):
    bsz, d = c.shape
    nout = w.shape[1]
    return pl.pallas_call(
        _mod_kernel,
        grid=(nout // d,),
        in_specs=[pl.BlockSpec((bsz, d), lambda i: (0, 0)),
                  pl.BlockSpec((d, d), lambda i: (0, i)),
                  pl.BlockSpec((1, d), lambda i: (0, i))],
        out_specs=pl.BlockSpec((bsz, d), lambda i: (0, i)),
        out_shape=jax.ShapeDtypeStruct((bsz, nout), F32),
        compiler_params=_params(("arbitrary",)),
        name="mod",
    )(c, w, b.reshape(1, nout))


def _rope_tab_kernel(pos_ref, inv_ref, sign_ref, cos_ref, sin_ref):
    ang = pos_ref[...].astype(F32) * inv_ref[...]
    cos_ref[...] = jnp.cos(ang)
    sin_ref[...] = jnp.sin(ang) * sign_ref[...]


def _rope_tab(pos_col):
    n = pos_col.shape[0]
    tr = min(n, 1024)
    half = ATT_HEAD_DIM // 2
    inv = ROPE_THETA ** (-jnp.arange(0, ATT_HEAD_DIM, 2, dtype=F32) / ATT_HEAD_DIM)
    inv_row = jnp.tile(inv, LANES // half).reshape(1, LANES)
    lane = jnp.arange(LANES)
    sign_row = jnp.where((lane % ATT_HEAD_DIM) < half, -1.0, 1.0).astype(F32).reshape(1, LANES)
    return pl.pallas_call(
        _rope_tab_kernel,
        grid=(n // tr,),
        in_specs=[pl.BlockSpec((tr, 1), lambda i: (i, 0)),
                  pl.BlockSpec((1, LANES), lambda i: (0, 0)),
                  pl.BlockSpec((1, LANES), lambda i: (0, 0))],
        out_specs=[pl.BlockSpec((tr, LANES), lambda i: (i, 0))] * 2,
        out_shape=[jax.ShapeDtypeStruct((n, LANES), F32)] * 2,
        compiler_params=_params(("arbitrary",)),
        name="rope_tab",
    )(pos_col, inv_row, sign_row)


def _rope(v, cos, sin):
    half = ATT_HEAD_DIM // 2
    lane = lax.broadcasted_iota(jnp.int32, cos.shape, 1)
    first = (lane % ATT_HEAD_DIM) < half
    outs = []
    for j in range(v.shape[1] // LANES):
        c = v[:, j * LANES:(j + 1) * LANES]
        rot = jnp.where(first, pltpu.roll(c, LANES - half, 1), pltpu.roll(c, half, 1))
        outs.append(c * cos + rot * sin)
    return jnp.concatenate(outs, axis=1)


_C_Q, _C_K, _C_V, _C_MQK, _C_MV, _C_MO, _C_G, _C_END = 0, 512, 768, 1024, 1536, 2048, 2560, 2688


def _in_proj_kernel(x_ref, sc_ref, sh_ref, g_ref, cos_ref, sin_ref, w_ref,
                    q_ref, k_ref, v_ref, mqk_ref, mv_ref, mo_ref, gt_ref):
    x = x_ref[...]
    h = _rms(x, g_ref[...]) * (1.0 + sc_ref[0]) + sh_ref[0]
    hb = h.astype(BF16)

    def mm(a, b):
        return jnp.dot(hb, w_ref[:, a:b], preferred_element_type=F32)

    cos = cos_ref[...]
    sin = sin_ref[...]
    q_ref[...] = (_rope(mm(_C_Q, _C_K), cos, sin) * (ATT_HEAD_DIM ** -0.5)).astype(BF16)
    k_ref[...] = _rope(mm(_C_K, _C_V), cos, sin).astype(BF16)
    v_ref[...] = mm(_C_V, _C_MQK).astype(BF16)
    mqk_ref[...] = mm(_C_MQK, _C_MV)
    mv_ref[...] = mm(_C_MV, _C_MO).astype(BF16)
    mo_ref[...] = mm(_C_MO, _C_G)
    gt_ref[...] = mm(_C_G, _C_END)


def _in_proj(x2, scale1, shift1, g_pre, cos, sin, w_all, seq, tm):
    n, d = x2.shape
    per_b = seq // tm
    row = lambda i: (i, 0)
    bsel = lambda i: (i // per_b, 0, 0)
    widths = (512, 256, 256, 512, 512, 512, 128)
    dtypes = (BF16, BF16, BF16, F32, BF16, F32, F32)
    return pl.pallas_call(
        _in_proj_kernel,
        grid=(n // tm,),
        in_specs=[pl.BlockSpec((tm, d), row),
                  pl.BlockSpec((1, 1, d), bsel),
                  pl.BlockSpec((1, 1, d), bsel),
                  pl.BlockSpec((1, d), lambda i: (0, 0)),
                  pl.BlockSpec((tm, LANES), row),
                  pl.BlockSpec((tm, LANES), row),
                  pl.BlockSpec((d, _C_END), lambda i: (0, 0))],
        out_specs=[pl.BlockSpec((tm, w), row) for w in widths],
        out_shape=[jax.ShapeDtypeStruct((n, w), dt) for w, dt in zip(widths, dtypes)],
        compiler_params=_params(("arbitrary",), VMEM_LIMIT),
        name="in_proj",
    )(x2, scale1, shift1, g_pre, cos, sin, w_all)


def _swa_kernel(sink_ref, q_ref, kp_ref, kc_ref, vp_ref, vc_ref, o_ref):
    blk = ATT_BLOCK
    n = pl.program_id(1)
    qi = lax.broadcasted_iota(jnp.int32, (blk, 2 * blk), 0)
    si = lax.broadcasted_iota(jnp.int32, (blk, 2 * blk), 1)
    delta = qi + blk - si
    valid = (delta >= 0) & (delta < blk) & ((si >= blk) | (n > 0))
    lo = lax.broadcasted_iota(jnp.int32, (2 * blk, LANES), 1) < ATT_HEAD_DIM
    group = ATT_HEADS // ATT_KV_HEADS
    kv = []
    for g in range(ATT_KV_HEADS):
        cs = slice(g * LANES, (g + 1) * LANES)
        k = jnp.concatenate([kp_ref[:, cs], kc_ref[:, cs]], axis=0)
        v = jnp.concatenate([vp_ref[:, cs], vc_ref[:, cs]], axis=0)
        zero = jnp.zeros_like(k)
        kv.append(((jnp.where(lo, k, zero), jnp.where(lo, v, zero)),
                   (jnp.where(lo, zero, k), jnp.where(lo, zero, v))))
    heads = [(h, kv[h // group][h % 2]) for h in range(ATT_HEADS)]
    sinks = [sink_ref[h] for h in range(ATT_HEADS)]
    scores = [jnp.where(valid, lax.dot_general(q_ref[:, (h // 2) * LANES:(h // 2 + 1) * LANES], kh, NT_DIMS,
                                               preferred_element_type=F32), NEG_INF) for h, (kh, _) in heads]
    tops = [jnp.maximum(jnp.max(s, axis=-1, keepdims=True), sinks[h]) for h, s in enumerate(scores)]
    exps = [jnp.exp(s - m) for s, m in zip(scores, tops)]
    dens = [jnp.sum(e, axis=-1, keepdims=True) + jnp.exp(sinks[h] - tops[h]) for h, e in enumerate(exps)]
    outs = [jnp.dot((e / d).astype(BF16), vh, preferred_element_type=F32)
            for e, d, (_, (_, vh)) in zip(exps, dens, heads)]
    for p in range(ATT_HEADS // 2):
        o_ref[:, p * LANES:(p + 1) * LANES] = (outs[2 * p] + outs[2 * p + 1]).astype(BF16)


def _swa(sinks, q, kd, vd, bsz, seq):
    n = q.shape[0]
    nb = seq // ATT_BLOCK
    cur = lambda b, i: (b * nb + i, 0)
    prev = lambda b, i: (b * nb + jnp.maximum(i - 1, 0), 0)
    return pl.pallas_call(
        _swa_kernel,
        grid=(bsz, nb),
        in_specs=[pl.BlockSpec(memory_space=pltpu.SMEM),
                  pl.BlockSpec((ATT_BLOCK, 512), cur),
                  pl.BlockSpec((ATT_BLOCK, 256), prev),
                  pl.BlockSpec((ATT_BLOCK, 256), cur),
                  pl.BlockSpec((ATT_BLOCK, 256), prev),
                  pl.BlockSpec((ATT_BLOCK, 256), cur)],
        out_specs=pl.BlockSpec((ATT_BLOCK, 512), cur),
        out_shape=jax.ShapeDtypeStruct((n, 512), BF16),
        compiler_params=_params(("arbitrary", "arbitrary")),
        name="swa",
    )(sinks, q, kd, kd, vd, vd)


def _mlstm_kernel(mqk_all, mv_all, mo_all, gt_all, cw_ref, gb_ref, ng_ref, o_all,
                  tail_all, qk_all, xs_all, ct_all, n_all, m_all, *, chunks, group):
    @pl.when(pl.program_id(1) == 0)
    def _():
        for ref in (tail_all, ct_all, n_all, m_all):
            ref[...] = jnp.zeros_like(ref)

    for g in range(group):
        _mlstm_prepare(mqk_all.at[g], gt_all.at[g], tail_all.at[g], qk_all.at[g], xs_all.at[g], cw_ref, gb_ref,
                       chunks)
    seqs = [tuple(r.at[g] for r in (mv_all, mo_all, o_all, qk_all, xs_all, ct_all, n_all, m_all))
            for g in range(group)]
    lax.fori_loop(0, chunks, lambda c, carry: _mlstm_chunk(c, seqs, ng_ref, carry), 0)


def _mlstm_prepare(mqk_ref, gt_ref, tail_ref, qk_ref, xs_ref, cw_ref, gb_ref, chunks):
    tm = chunks * MLSTM_CHUNK
    nqk = MLSTM_HEADS * MLSTM_QK_DIM

    cur = mqk_ref[...]
    full = jnp.concatenate([tail_ref[...], cur], axis=0)
    off = SUBLANES - (CONV_WIDTH - 1)
    acc = full[off:off + tm] * cw_ref[0:1, :]
    for j in range(1, CONV_WIDTH):
        acc = acc + full[off + j:off + j + tm] * cw_ref[j:j + 1, :]
    act = acc * jax.nn.sigmoid(acc)
    col = lax.broadcasted_iota(jnp.int32, (1, 2 * nqk), 1)
    act = act * jnp.where(col < nqk, MLSTM_QK_DIM ** -0.5, 1.0)
    qk_ref[...] = act.astype(BF16)
    tail_ref[...] = cur[tm - SUBLANES:tm]

    lane = lax.broadcasted_iota(jnp.int32, (tm, LANES), 1)
    gts = gt_ref[...] + gb_ref[...]
    logsig = jnp.minimum(gts, 0.0) - jnp.log(1.0 + jnp.exp(-jnp.abs(gts)))
    xs_ref[...] = jnp.where(lane < MLSTM_HEADS, gts, jnp.where(lane < 2 * MLSTM_HEADS, logsig, 0.0))


def _mlstm_chunk(c, seqs, ng_ref, carry):
    L = MLSTM_CHUNK
    nqk = MLSTM_HEADS * MLSTM_QK_DIM
    ri = lax.broadcasted_iota(jnp.int32, (L, L), 0)
    ci = lax.broadcasted_iota(jnp.int32, (L, L), 1)
    causal = ci <= ri
    tril = causal.astype(F32)
    lane_l = lax.broadcasted_iota(jnp.int32, (L, LANES), 1)
    lo_l = lane_l < MLSTM_QK_DIM
    row_c = lax.broadcasted_iota(jnp.int32, (LANES, 1), 0) < MLSTM_QK_DIM
    lane_1 = lax.broadcasted_iota(jnp.int32, (1, LANES), 1) < MLSTM_QK_DIM
    rows = pl.ds(pl.multiple_of(c * L, L), L)
    G = range(len(seqs))
    U = [(g, h) for g in G for h in range(MLSTM_HEADS)]
    P = [(g, p) for g in G for p in range(MLSTM_HEADS // 2)]
    mv_r, mo_r, o_r, qk_r, xs_r, ct_r, n_r, m_r = (dict(enumerate(col)) for col in zip(*seqs))
    pair = lambda u: (u[0], u[1] // 2)

    xc = {g: xs_r[g][rows, :] for g in G}
    bc = {g: jnp.dot(tril, xc[g], preferred_element_type=F32, precision=lax.Precision.HIGHEST) for g in G}
    x2 = {g: jnp.where(lane_l < MLSTM_HEADS, xc[g], bc[g]) for g in G}
    xt = {g: x2[g].T for g in G}
    q2 = {(g, p): qk_r[g][rows, p * LANES:(p + 1) * LANES] for g, p in P}
    k2 = {(g, p): qk_r[g][rows, nqk + p * LANES:nqk + (p + 1) * LANES] for g, p in P}
    ct_old = {(g, p): ct_r[g][p] for g, p in P}
    ctb = {gp: ct_old[gp].astype(BF16) for gp in P}
    n2 = {(g, p): n_r[g][p:p + 1, :] for g, p in P}
    hm = {u: lo_l if u[1] % 2 == 0 else jnp.logical_not(lo_l) for u in U}
    qm = {u: jnp.where(hm[u], q2[pair(u)], jnp.zeros_like(q2[pair(u)])) for u in U}
    km = {u: jnp.where(hm[u], k2[pair(u)], jnp.zeros_like(k2[pair(u)])) for u in U}
    v = {(g, h): mv_r[g][rows, h * LANES:(h + 1) * LANES] for g, h in U}
    b_col = {(g, h): x2[g][:, MLSTM_HEADS + h:MLSTM_HEADS + h + 1] for g, h in U}
    ig_col = {(g, h): x2[g][:, h:h + 1] for g, h in U}
    b_row = {(g, h): xt[g][MLSTM_HEADS + h:MLSTM_HEADS + h + 1, :] for g, h in U}
    ig_row = {(g, h): xt[g][h:h + 1, :] for g, h in U}
    m_prev = {(g, h): m_r[g][h:h + 1, 0:1] for g, h in U}
    dlog = {u: jnp.where(causal, b_col[u] - b_row[u] + ig_row[u], NEG_INF) for u in U}
    m_inter = {u: b_col[u] + m_prev[u] for u in U}
    m_t = {u: jnp.maximum(m_inter[u], jnp.max(dlog[u], axis=-1, keepdims=True)) for u in U}
    w_intra = {u: jnp.exp(dlog[u] - m_t[u]) for u in U}
    a_inter = {u: jnp.exp(m_inter[u] - m_t[u]) for u in U}
    s = {u: lax.dot_general(q2[pair(u)], km[u], NT_DIMS, preferred_element_type=F32) * w_intra[u] for u in U}
    num = {u: jnp.dot(s[u].astype(BF16), v[u], preferred_element_type=F32)
           + a_inter[u] * jnp.dot(qm[u], ctb[pair(u)], preferred_element_type=F32) for u in U}
    den = {u: jnp.sum(s[u], axis=-1, keepdims=True)
           + a_inter[u] * jnp.sum(qm[u].astype(F32) * n2[pair(u)], axis=-1, keepdims=True) for u in U}
    hh = {u: num[u] / jnp.maximum(jnp.abs(den[u]), jnp.exp(-m_t[u])) for u in U}
    y = {(g, h): _rms(hh[g, h], ng_ref[:, h * LANES:(h + 1) * LANES]) for g, h in U}
    for g, h in U:
        o_r[g][rows, h * LANES:(h + 1) * LANES] = (
            y[g, h] * jax.nn.sigmoid(mo_r[g][rows, h * LANES:(h + 1) * LANES])).astype(BF16)
    b_last = {(g, h): xt[g][MLSTM_HEADS + h:MLSTM_HEADS + h + 1, L - 1:L] for g, h in U}
    g_col = {u: b_last[u] - b_col[u] + ig_col[u] for u in U}
    m_new = {u: jnp.maximum(b_last[u] + m_prev[u], jnp.max(g_col[u], axis=0, keepdims=True)) for u in U}
    kw = {u: km[u].astype(F32) * jnp.exp(g_col[u] - m_new[u]) for u in U}
    dec = {u: jnp.exp(b_last[u] + m_prev[u] - m_new[u]) for u in U}
    upd = {u: lax.dot_general(kw[u].astype(BF16), v[u], TN_DIMS, preferred_element_type=F32) for u in U}
    for g, h in U:
        m_r[g][h:h + 1, :] = jnp.broadcast_to(m_new[g, h], (1, LANES))
    for g, p in P:
        e, o = (g, 2 * p), (g, 2 * p + 1)
        ct_r[g][p] = ct_old[g, p] * jnp.where(row_c, dec[e], dec[o]) + upd[e] + upd[o]
        n_r[g][p:p + 1, :] = (n2[g, p] * jnp.where(lane_1, dec[e], dec[o])
                              + jnp.sum(kw[e] + kw[o], axis=0, keepdims=True))
    return carry


def _mlstm(mqk, mv, mo, gts, conv_w, gate_bias, norm_g, bsz, seq, chunks, group):
    n = mqk.shape[0]
    tm = chunks * MLSTM_CHUNK
    steps = seq // tm
    row = lambda b, i: (b, i, 0)
    const = lambda b, i: (0, 0)
    width = MLSTM_HEADS * MLSTM_V_DIM
    per_seq = lambda a: a.reshape(bsz, seq, a.shape[1])
    out = pl.pallas_call(
        functools.partial(_mlstm_kernel, chunks=chunks, group=group),
        grid=(bsz // group, steps),
        in_specs=[pl.BlockSpec((group, tm, width), row),
                  pl.BlockSpec((group, tm, width), row),
                  pl.BlockSpec((group, tm, width), row),
                  pl.BlockSpec((group, tm, LANES), row),
                  pl.BlockSpec((CONV_WIDTH, width), const),
                  pl.BlockSpec((1, LANES), const),
                  pl.BlockSpec((1, width), const)],
        out_specs=pl.BlockSpec((group, tm, width), row),
        out_shape=jax.ShapeDtypeStruct((bsz, seq, width), BF16),
        scratch_shapes=[pltpu.VMEM((group, SUBLANES, width), F32),
                        pltpu.VMEM((group, tm, width), BF16),
                        pltpu.VMEM((group, tm, LANES), F32),
                        pltpu.VMEM((group, MLSTM_HEADS // 2, LANES, LANES), F32),
                        pltpu.VMEM((group, SUBLANES, LANES), F32),
                        pltpu.VMEM((group, SUBLANES, LANES), F32)],
        compiler_params=_params(("arbitrary", "arbitrary")),
        name="mlstm",
    )(per_seq(mqk), per_seq(mv), per_seq(mo), per_seq(gts), conv_w, gate_bias, norm_g)
    return out.reshape(n, width)


def _out_proj_kernel(att_ref, mh_ref, x_ref, g1_ref, sc_ref, sh_ref, gpm_ref, gpf_ref, wa_ref, wb_ref,
                     x1_ref, h2_ref):
    mix = (jnp.dot(att_ref[...], wa_ref[...], preferred_element_type=F32)
           + jnp.dot(mh_ref[...], wb_ref[...], preferred_element_type=F32))
    x1 = x_ref[...] + g1_ref[0] * _rms(mix, gpm_ref[...])
    x1_ref[...] = x1
    h2_ref[...] = _rms(x1, gpf_ref[...]) * (1.0 + sc_ref[0]) + sh_ref[0]


def _out_proj(att, mh, x2, gate1, scale2, shift2, g_post_mix, g_pre_ffn, wa, wb, seq, tm):
    n, d = x2.shape
    per_b = seq // tm
    row = lambda i: (i, 0)
    bsel = lambda i: (i // per_b, 0, 0)
    const = lambda i: (0, 0)
    half = att.shape[1]
    return pl.pallas_call(
        _out_proj_kernel,
        grid=(n // tm,),
        in_specs=[pl.BlockSpec((tm, half), row), pl.BlockSpec((tm, half), row), pl.BlockSpec((tm, d), row),
                  pl.BlockSpec((1, 1, d), bsel), pl.BlockSpec((1, 1, d), bsel), pl.BlockSpec((1, 1, d), bsel),
                  pl.BlockSpec((1, d), const), pl.BlockSpec((1, d), const),
                  pl.BlockSpec((half, d), const), pl.BlockSpec((half, d), const)],
        out_specs=[pl.BlockSpec((tm, d), row)] * 2,
        out_shape=[jax.ShapeDtypeStruct((n, d), F32)] * 2,
        compiler_params=_params(("arbitrary",), VMEM_LIMIT),
        name="out_proj",
    )(att, mh, x2, gate1, scale2, shift2, g_post_mix, g_pre_ffn, wa, wb)


_BIG_ID = float(2 ** 30)
SORT_LEVELS = 8


def _top_scores(s, k):
    rows, t = s.shape
    span = SORT_LEVELS * SUBLANES
    r = lax.broadcasted_iota(jnp.int32, (rows // SORT_LEVELS, t), 0)
    col_id = ((r // SUBLANES) * span + r % SUBLANES).astype(F32)
    lev = [jnp.concatenate([s[g * span + l * SUBLANES:g * span + (l + 1) * SUBLANES] for g in range(rows // span)],
                           axis=0) for l in range(SORT_LEVELS)]
    ids = [col_id + float(l * SUBLANES) for l in range(SORT_LEVELS)]
    for rnd in range(SORT_LEVELS):
        for a in range(rnd % 2, SORT_LEVELS - 1, 2):
            swap = lev[a + 1] > lev[a]
            lev[a], lev[a + 1] = jnp.where(swap, lev[a + 1], lev[a]), jnp.where(swap, lev[a], lev[a + 1])
            ids[a], ids[a + 1] = jnp.where(swap, ids[a + 1], ids[a]), jnp.where(swap, ids[a], ids[a + 1])
    vals, sel = [], []
    for it in range(k):
        m = jnp.max(lev[0], axis=0, keepdims=True)
        i = jnp.min(jnp.where(lev[0] == m, ids[0], _BIG_ID), axis=0, keepdims=True)
        vals.append(m)
        sel.append(i)
        hit = ids[0] == i
        live = min(SORT_LEVELS, k - 1 - it)
        for l in range(min(live, SORT_LEVELS - 1)):
            lev[l] = jnp.where(hit, lev[l + 1], lev[l])
            ids[l] = jnp.where(hit, ids[l + 1], ids[l])
        if live == SORT_LEVELS:
            lev[-1] = jnp.where(hit, NEG_INF, lev[-1])
    return jnp.concatenate(vals, axis=0), jnp.concatenate(sel, axis=0).astype(jnp.int32)


def _top_pair_sums(v1, v2):
    k, t = v1.shape
    half = SUBLANES
    lev = [v1[0:half] + v2[b:b + 1, :] for b in range(k)]
    side = v1[half:k] + v2[0:1, :]
    a_low = lax.broadcasted_iota(jnp.int32, (half, t), 0).astype(F32) * float(k)
    side_id = a_low + float(half * k)
    depth = jnp.zeros((half, t), F32)
    vals, sel = [], []
    for it in range(k):
        top_id = a_low + depth
        m = jnp.max(jnp.maximum(lev[0], side), axis=0, keepdims=True)
        i = jnp.min(jnp.minimum(jnp.where(lev[0] == m, top_id, _BIG_ID), jnp.where(side == m, side_id, _BIG_ID)),
                    axis=0, keepdims=True)
        vals.append(m)
        sel.append(i)
        hit = top_id == i
        for l in range(k - 1 - it):
            lev[l] = jnp.where(hit, lev[l + 1], lev[l])
        side = jnp.where(side_id == i, NEG_INF, side)
        depth = depth + jnp.where(hit, 1.0, 0.0)
    return jnp.concatenate(vals, axis=0), jnp.concatenate(sel, axis=0).astype(jnp.int32)


def _pick_rows(table, which):
    r = lax.broadcasted_iota(jnp.int32, table.shape, 0)
    rows = []
    for k in range(which.shape[0]):
        rows.append(jnp.sum(jnp.where(r == which[k:k + 1, :], table, 0), axis=0, keepdims=True))
    return jnp.concatenate(rows, axis=0)


def _peer_sel_kernel(h_ref, wq_ref, k1_ref, k2_ref, e_ref, g_ref, r_ref):
    tq = h_ref.shape[0]
    K = PEER_TOPK
    q = jnp.dot(h_ref[...].astype(BF16), wq_ref[...], preferred_element_type=F32).astype(BF16)
    for hd in range(PEER_HEADS):
        base = hd * 2 * PEER_HALF
        s1 = lax.dot_general(k1_ref[hd], q[:, base:base + PEER_HALF], NT_DIMS, preferred_element_type=F32)
        s2 = lax.dot_general(k2_ref[hd], q[:, base + PEER_HALF:base + 2 * PEER_HALF], NT_DIMS,
                             preferred_element_type=F32)
        v1, i1 = _top_scores(s1, K)
        v2, i2 = _top_scores(s2, K)
        top, pos = _top_pair_sums(v1, v2)
        eid = _pick_rows(i1, pos >> 4) * PEER_KEYS + _pick_rows(i2, pos & (K - 1))
        ex = jnp.exp(top - top[0:1, :])
        e_ref[hd * K:(hd + 1) * K, :] = eid * ROW_WORDS
        g_ref[hd * K:(hd + 1) * K, :] = ex / jnp.sum(ex, axis=0, keepdims=True)
    pair = (e_ref[...] >> 1).T
    nsel = pair.shape[1]
    for h in range(STREAM_HALVES):
        r_ref[:, h * nsel:(h + 1) * nsel] = pair + h


def _peer_sel(h2, wq, k1, k2, tq):
    n, d = h2.shape
    rows = PEER_HEADS * PEER_TOPK
    return pl.pallas_call(
        _peer_sel_kernel,
        grid=(n // tq,),
        in_specs=[pl.BlockSpec((tq, d), lambda i: (i, 0)),
                  pl.BlockSpec(wq.shape, lambda i: (0, 0)),
                  pl.BlockSpec(k1.shape, lambda i: (0, 0, 0)),
                  pl.BlockSpec(k2.shape, lambda i: (0, 0, 0))],
        out_specs=[pl.BlockSpec((rows, tq), lambda i: (0, i))] * 2
        + [pl.BlockSpec((tq, rows * STREAM_HALVES), lambda i: (i, 0))],
        out_shape=[jax.ShapeDtypeStruct((rows, n), jnp.int32), jax.ShapeDtypeStruct((rows, n), F32),
                   jax.ShapeDtypeStruct((n, rows * STREAM_HALVES), jnp.int32)],
        compiler_params=_params(("arbitrary",), VMEM_LIMIT),
        name="peer_sel",
    )(h2, wq, k1, k2)


def _split_bf16(x):
    hi = x.astype(BF16)
    return hi, (x - hi.astype(F32)).astype(BF16)


def _pack_kernel(w_ref, o_ref):
    x = w_ref[...]
    eb = x.shape[0]
    for r in range(ROW_WORDS):
        lo = x[:, 2 * r * LANES:(2 * r + 1) * LANES].astype(BF16).astype(F32)
        hi = x[:, (2 * r + 1) * LANES:(2 * r + 2) * LANES].astype(BF16).astype(F32)
        word = (lax.shift_right_logical(pltpu.bitcast(lo, jnp.int32), 16)
                | (pltpu.bitcast(hi, jnp.int32) & jnp.int32(-65536)))
        o_ref[pl.ds(r, eb, stride=ROW_WORDS), :] = word


def _pack_table(w):
    e, d = w.shape
    eb = 512
    return pl.pallas_call(
        _pack_kernel,
        grid=(e // eb,),
        in_specs=[pl.BlockSpec((eb, d), lambda i: (i, 0))],
        out_specs=pl.BlockSpec((eb * ROW_WORDS, LANES), lambda i: (i, 0)),
        out_shape=jax.ShapeDtypeStruct((e * ROW_WORDS, LANES), jnp.int32),
        compiler_params=_params(("arbitrary",)),
        name="pack_table",
    )(w)


def _gather_rows(slots, tab_ref, t, stage_ref):
    for j, slot in enumerate(slots):
        src = pl.ds(pl.multiple_of(slot[t], ROW_WORDS), ROW_WORDS)
        stage_ref[j * ROW_WORDS:(j + 1) * ROW_WORDS, :] = tab_ref[src, :]


def _staged_bf16(stage_ref):
    return pltpu.bitcast(stage_ref[...], BF16)


def _pipelined_tokens(ntok, gather, compute, stages, on_trip=None, after=None):
    nb = len(stages)
    for k in range(nb):
        gather(k, stages[k])

    def trip(i, carry):
        t = nb * i
        if on_trip is not None:
            on_trip(i)
        for k in range(nb):
            compute(t + k, stages[k])
            ahead = t + k + nb
            gather(jnp.minimum(ahead, ntok - 1), stages[k])
            if after is not None:
                after(i, k)
        return carry

    lax.fori_loop(0, ntok // nb, trip, 0)


def _with_slot_indices(idx_hbm, sems, bufs, stride, count, run):
    nsel = len(bufs) // 2
    step = pl.program_id(0)
    last = pl.num_programs(0) - 1

    def copies(block, which):
        return [pltpu.make_async_copy(idx_hbm.at[j, pl.ds(block * stride, count)], bufs[which * nsel + j],
                                      sems.at[which])
                for j in range(nsel)]

    @pl.when(step == 0)
    def _():
        for cp in copies(0, 0):
            cp.start()

    def phase(which):
        for cp in copies(step, which):
            cp.wait()

        @pl.when(step < last)
        def _():
            for cp in copies(step + 1, 1 - which):
                cp.start()

        run(bufs[which * nsel:(which + 1) * nsel])

    for which in range(2):
        pl.when(step % 2 == which)(functools.partial(phase, which))


SC_WINDOW = 128


def _sc_gather_rows(tab, row_ids):
    n = row_ids.shape[0]
    mesh = plsc.VectorSubcoreMesh(core_axis_name="core", subcore_axis_name="subcore")

    @pl.kernel(out_type=jax.ShapeDtypeStruct((n, tab.shape[1]), tab.dtype), mesh=mesh)
    def gather(tab_hbm, ids_hbm, out_hbm):
        def body(ids_vmem, out_vmem):
            pltpu.sync_copy(tab_hbm.at[ids_vmem.at[0]], out_vmem)

        pltpu.emit_pipeline(
            body,
            grid=(n // SC_WINDOW,),
            in_specs=[pl.BlockSpec((1, SC_WINDOW), index_map=lambda i: (0, i))],
            out_specs=[pl.BlockSpec((SC_WINDOW, tab.shape[1]), index_map=lambda i: (i, 0))],
            core_axis_name=("core", "subcore"),
            dimension_semantics=(pltpu.PARALLEL,),
            trace_scopes=False,
        )(ids_hbm, out_hbm)

    return gather(tab, row_ids.reshape(1, n))


def _diag_mask(nsel):
    shape = (SUBLANES, nsel * SUBLANES)
    return (lax.broadcasted_iota(jnp.int32, shape, 1) % SUBLANES) == lax.broadcasted_iota(jnp.int32, shape, 0)


def _token_tile(ref, t):
    row = ref[pl.ds(t, 1), :]
    return jnp.concatenate([row[:, s * LANES:(s + 1) * LANES] for s in range(SUBLANES)], axis=0)


def _peer_u_kernel(idx_hbm, h_ref, gate_ref, grp_ref, tab_ref, act_ref, part_ref, *scratch):
    nsel, tq = gate_ref.shape
    stages, sems, bufs = scratch[:GATHER_STAGES], scratch[GATHER_STAGES], scratch[GATHER_STAGES + 1:]
    diag = _diag_mask(nsel)

    def compute(t, stage_ref):
        h_hi, h_lo = _split_bf16(_token_tile(h_ref, t))
        both = lax.dot_general(jnp.concatenate([h_hi, h_lo], axis=0), _staged_bf16(stage_ref), NT_DIMS,
                               preferred_element_type=F32)
        prod = both[0:SUBLANES] + both[SUBLANES:2 * SUBLANES]
        part_ref[pl.ds(t, 1), :] = jnp.sum(jnp.where(diag, prod, 0.0), axis=0, keepdims=True)

    def run(slots):
        _pipelined_tokens(tq, functools.partial(_gather_rows, slots, tab_ref), compute, stages)

    _with_slot_indices(idx_hbm, sems, bufs, tq, tq, run)
    p_hi, p_lo = _split_bf16(part_ref[...])
    grp = grp_ref[...]
    pre = jnp.dot(p_hi, grp, preferred_element_type=F32) + jnp.dot(p_lo, grp, preferred_element_type=F32)
    act_ref[...] = 0.5 * pre * (1.0 + lax.erf(pre * (2.0 ** -0.5))) * gate_ref[...].T


def _group_matrix(nsel):
    r = jnp.arange(nsel * SUBLANES)[:, None] // SUBLANES
    return (r == jnp.arange(nsel)[None, :]).astype(BF16)


def _gather_scratch(nsel, ntok, nstages):
    return ([pltpu.VMEM((nsel * ROW_WORDS, LANES), jnp.int32)] * nstages
            + [pltpu.SemaphoreType.DMA((2,))] + [pltpu.SMEM((ntok,), jnp.int32)] * (2 * nsel))


def _peer_u(idx_t, h2, gate_t, tab, tq):
    nsel, n = idx_t.shape
    d = h2.shape[1]
    wide = nsel * SUBLANES
    return pl.pallas_call(
        _peer_u_kernel,
        grid=(n // tq,),
        in_specs=[pl.BlockSpec(memory_space=pl.ANY),
                  pl.BlockSpec((tq, d), lambda i: (i, 0)),
                  pl.BlockSpec((nsel, tq), lambda i: (0, i)),
                  pl.BlockSpec((wide, nsel), lambda i: (0, 0)),
                  pl.BlockSpec(memory_space=pltpu.VMEM)],
        out_specs=pl.BlockSpec((tq, nsel), lambda i: (i, 0)),
        out_shape=jax.ShapeDtypeStruct((n, nsel), F32),
        scratch_shapes=[pltpu.VMEM((tq, wide), F32)] + _gather_scratch(nsel, tq, GATHER_STAGES),
        compiler_params=_params(("arbitrary",), GATHER_VMEM_LIMIT),
        name="peer_u",
    )(idx_t, h2, gate_t, _group_matrix(nsel), tab)


def _lane_aligned(count):
    return -(-count // LANES) * LANES


def _streamed_rows_bf16(words):
    x = pltpu.bitcast(words, BF16)
    return jnp.concatenate([x[:, 0:LANES], x[:, LANES:2 * LANES]], axis=0)


def _streamed_copy_matrix(nsel):
    r = jnp.arange(nsel * SUBLANES)
    return ((r[None, :] % (2 * nsel)) // 2 == jnp.arange(nsel)[:, None]).astype(BF16)


def _streamed_diag(nsel):
    shape = (SUBLANES, nsel * SUBLANES)
    r = lax.broadcasted_iota(jnp.int32, shape, 1)
    sub = 4 * ((r % (4 * nsel)) // (2 * nsel)) + 2 * (r // (4 * nsel)) + r % 2
    return lax.broadcasted_iota(jnp.int32, shape, 0) == sub


def _peer_v_kernel(idx_hbm, act_ref, rep_ref, rep2_ref, x1_ref, g2_ref, gpf_ref, tab_ref, rows_hbm, o_ref,
                   wide_ref, y_ref, rowbuf, rsem, *scratch, nvld, nstages):
    tq, nsel = act_ref.shape
    stages, sems, bufs = scratch[:nstages], scratch[nstages], scratch[nstages + 1:]
    trips = nvld // nstages
    per_trip = (tq - nvld) // trips
    rows_tok = nsel * STREAM_HALVES
    step = pl.program_id(0)
    chunks = pl.num_programs(0) * trips
    diag = _diag_mask(nsel)
    diag2 = _streamed_diag(nsel)
    a_hi, a_lo = _split_bf16(act_ref[...])
    for lo, hi, rep in ((0, nvld, rep_ref[...]), (nvld, tq, rep2_ref[...])):
        wide_ref[lo:hi, :] = (jnp.dot(a_hi[lo:hi], rep, preferred_element_type=F32)
                              + jnp.dot(a_lo[lo:hi], rep, preferred_element_type=F32))

    def finish(t, mask, rows_bf16):
        w = jnp.where(mask, jnp.broadcast_to(wide_ref[pl.ds(t, 1), :], mask.shape), 0.0)
        w_hi, w_lo = _split_bf16(w)
        both = jnp.dot(jnp.concatenate([w_hi, w_lo], axis=0), rows_bf16, preferred_element_type=F32)
        tile = both[0:SUBLANES] + both[SUBLANES:2 * SUBLANES]
        y_ref[pl.ds(t, 1), :] = jnp.concatenate([tile[s:s + 1, :] for s in range(SUBLANES)], axis=1)

    def compute(t, stage_ref):
        finish(t, diag, _staged_bf16(stage_ref))

    def chunk_copy(g, slot):
        return pltpu.make_async_copy(rows_hbm.at[pl.ds(g * (per_trip * rows_tok), per_trip * rows_tok), :],
                                     rowbuf.at[slot], rsem.at[slot])

    ring = STREAM_RING

    @pl.when(step == 0)
    def _():
        for g0 in range(ring - 1):
            chunk_copy(g0, g0).start()

    def on_trip(i):
        g = step * trips + i
        chunk_copy(g, i % ring).wait()

        @pl.when(g + ring - 1 < chunks)
        def _():
            chunk_copy(g + ring - 1, (i + ring - 1) % ring).start()

    places = [(d * nstages) // per_trip for d in range(per_trip)]

    def after(i, k):
        for d in range(per_trip):
            if places[d] == k:
                rows = rowbuf.at[i % ring, pl.ds(d * rows_tok, rows_tok), :]
                finish(nvld + i * per_trip + d, diag2, _streamed_rows_bf16(rows[...]))

    def run(slots):
        _pipelined_tokens(nvld, functools.partial(_gather_rows, slots, tab_ref), compute, stages, on_trip, after)

    _with_slot_indices(idx_hbm, sems, bufs, tq, _lane_aligned(nvld), run)
    o_ref[...] = x1_ref[...] + g2_ref[0] * _rms(y_ref[...], gpf_ref[...])


def _peer_v(idx_t, act, x1, gate2, g_post_ffn, tab, rows, seq, tq, nvld, nstages):
    nsel, n = idx_t.shape
    d = x1.shape[1]
    wide = nsel * SUBLANES
    per_b = seq // tq
    row = lambda i: (i, 0)
    const = lambda i: (0, 0)
    trips = nvld // nstages
    per_trip = (tq - nvld) // trips
    assert trips % STREAM_RING == 0 and trips * nstages == nvld and trips * per_trip == tq - nvld
    return pl.pallas_call(
        functools.partial(_peer_v_kernel, nvld=nvld, nstages=nstages),
        grid=(n // tq,),
        in_specs=[pl.BlockSpec(memory_space=pl.ANY),
                  pl.BlockSpec((tq, nsel), row),
                  pl.BlockSpec((nsel, wide), const),
                  pl.BlockSpec((nsel, wide), const),
                  pl.BlockSpec((tq, d), row),
                  pl.BlockSpec((1, 1, d), lambda i: (i // per_b, 0, 0)),
                  pl.BlockSpec((1, d), const),
                  pl.BlockSpec(memory_space=pltpu.VMEM),
                  pl.BlockSpec(memory_space=pl.ANY)],
        out_specs=pl.BlockSpec((tq, d), row),
        out_shape=jax.ShapeDtypeStruct((n, d), F32),
        scratch_shapes=[pltpu.VMEM((tq, wide), F32), pltpu.VMEM((tq, d), F32),
                        pltpu.VMEM((STREAM_RING, per_trip * nsel * STREAM_HALVES, STREAM_ROW), jnp.int32),
                        pltpu.SemaphoreType.DMA((STREAM_RING,))]
        + _gather_scratch(nsel, _lane_aligned(nvld), nstages),
        compiler_params=_params(("arbitrary",), GATHER_VMEM_LIMIT),
        name="peer_v",
    )(idx_t, act, _group_matrix(nsel).T, _streamed_copy_matrix(nsel), x1, gate2, g_post_ffn, tab, rows)


def _dup_heads(w, heads, dh):
    d = w.shape[0]
    return jnp.repeat(w.reshape(d, heads, 1, dh), 2, axis=2).reshape(d, heads * 2 * dh)


def _layer(x2, c, pos_col, bsz, seq, w_mod, b_mod, g_pre_mix, g_post_mix, w_in, conv_w, b_igate, b_fgate,
           mlstm_norm_g, att_sinks, w_out, g_pre_ffn, g_post_ffn, peer_wq, peer_keys1, peer_keys2, peer_u, peer_v):
    n, d = x2.shape
    tm = min(seq, 512)
    mod = _mod(c, w_mod, b_mod)
    shift1, scale1, gate1, shift2, scale2, gate2 = [m.reshape(bsz, 1, d) for m in jnp.split(mod, 6, axis=-1)]

    aw = ATT_HEADS * ATT_HEAD_DIM
    kvw = ATT_KV_HEADS * ATT_HEAD_DIM
    qkw = MLSTM_HEADS * MLSTM_QK_DIM
    mw = MLSTM_HEADS * MLSTM_V_DIM
    o = 0
    wq_a = w_in[:, o:o + aw]; o += aw
    wk_a = w_in[:, o:o + kvw]; o += kvw
    wv_a = w_in[:, o:o + kvw]; o += kvw
    w_mqk = w_in[:, o:o + 2 * qkw]; o += 2 * qkw
    w_mv = w_in[:, o:o + mw]; o += mw
    w_g = w_in[:, o:o + 2 * MLSTM_HEADS]; o += 2 * MLSTM_HEADS
    w_mo = w_in[:, o:o + mw]
    w_gp = jnp.pad(w_g, ((0, 0), (0, LANES - 2 * MLSTM_HEADS)))
    w_all = jnp.concatenate([wq_a, _dup_heads(wk_a, ATT_KV_HEADS, ATT_HEAD_DIM),
                             _dup_heads(wv_a, ATT_KV_HEADS, ATT_HEAD_DIM), w_mqk, w_mv, w_mo, w_gp],
                            axis=1).astype(BF16)

    cos, sin = _rope_tab(pos_col)
    q, kd, vd, mqk, mv, mo, gts = _in_proj(x2, scale1, shift1, g_pre_mix.reshape(1, d), cos, sin, w_all, seq, tm)
    att = _swa(att_sinks, q, kd, vd, bsz, seq)
    gate_bias = jnp.pad(jnp.concatenate([b_igate, b_fgate]), (0, LANES - 2 * MLSTM_HEADS)).reshape(1, LANES)
    chunks = min(seq // MLSTM_CHUNK, 8)
    mh = _mlstm(mqk, mv, mo, gts, conv_w, gate_bias, mlstm_norm_g.reshape(1, mw), bsz, seq, chunks,
                MLSTM_GROUP if bsz % MLSTM_GROUP == 0 else 1)
    wo = w_out.astype(BF16)
    x1, h2 = _out_proj(att, mh, x2, gate1, scale2, shift2, g_post_mix.reshape(1, d), g_pre_ffn.reshape(1, d),
                       wo[:aw], wo[aw:], seq, tm)

    tq = min(n, 256)
    eid_t, gate_t, row_ids = _peer_sel(h2, peer_wq.astype(BF16), peer_keys1.astype(BF16),
                                       peer_keys2.astype(BF16), tq)
    tg = min(seq, GATHER_TOKENS)
    tab_v = _pack_table(peer_v)
    nvld = tg * STREAM_SPLIT[0] // sum(STREAM_SPLIT)
    streamed = row_ids.reshape(n // tg, tg, row_ids.shape[1])[:, nvld:, :]
    rows_v = _sc_gather_rows(tab_v.reshape(-1, STREAM_ROW), streamed.reshape(-1))
    act = _peer_u(eid_t, h2, gate_t, _pack_table(peer_u), tg)
    return _peer_v(eid_t, act, x1, gate2, g_post_ffn.reshape(1, d), tab_v, rows_v, seq, tg, nvld, STREAM_STAGES)


def kernel(x, c, positions, w_mod, b_mod, g_pre_mix, g_post_mix, w_in, conv_w, b_igate, b_fgate, mlstm_norm_g, att_sinks, w_out, g_pre_ffn, g_post_ffn, peer_wq, peer_keys1, peer_keys2, peer_u, peer_v):
    bsz, seq, d = x.shape
    n = bsz * seq
    x2 = x.reshape(n, d)
    pos_col = positions.reshape(n, 1)
    for l in range(w_mod.shape[0]):
        x2 = _layer(x2, c, pos_col, bsz, seq, w_mod[l], b_mod[l], g_pre_mix[l], g_post_mix[l], w_in[l], conv_w[l],
                    b_igate[l], b_fgate[l], mlstm_norm_g[l], att_sinks[l], w_out[l], g_pre_ffn[l], g_post_ffn[l],
                    peer_wq[l], peer_keys1[l], peer_keys2[l], peer_u[l], peer_v[l])
    return x2.reshape(bsz, seq, d)
```

```python
import functools

import jax
import jax.numpy as jnp
from jax import lax
from jax.experimental import pallas as pl
from jax.experimental.pallas import tpu as pltpu
from jax.experimental.pallas import tpu_sc as plsc

F32 = jnp.float32
BF16 = jnp.bfloat16

ATT_HEADS = 8
ATT_KV_HEADS = 2
ATT_HEAD_DIM = 64
ATT_BLOCK = 128
ROPE_THETA = 10000.0
MLSTM_HEADS = 4
MLSTM_V_DIM = 128
MLSTM_QK_DIM = 64
MLSTM_CHUNK = 64
CONV_WIDTH = 4
PEER_HEADS = 8
PEER_KEYS = 128
PEER_HALF = 128
PEER_TOPK = 16
NORM_EPS = 1e-6

LANES = 128
SUBLANES = 8
VMEM_LIMIT = 52 * 1024 * 1024
GATHER_STAGES = 32
ROW_WORDS = SUBLANES // 2
SWA_BLOCKS = 2
MLSTM_GROUP = 4
STREAM_SPLIT = (4, 4)
STREAM_RING = 4
STREAM_STAGES = 8
STREAM_HALVES = 2
STREAM_ROW = 2 * LANES
GATHER_TOKENS = 512
GATHER_VMEM_LIMIT = 58 * 1024 * 1024

NEG_INF = float("-inf")
NT_DIMS = (((1,), (1,)), ((), ()))
TN_DIMS = (((0,), (0,)), ((), ()))


def _params(sem, vmem=None):
    return pltpu.CompilerParams(dimension_semantics=sem, vmem_limit_bytes=vmem)


def _rms(x, g):
    return x * lax.rsqrt(jnp.mean(x * x, axis=-1, keepdims=True) + NORM_EPS) * g


def _mod_kernel(c_ref, w_ref, b_ref, o_ref):
    o_ref[...] = jnp.dot(c_ref[...], w_ref[...], preferred_element_type=F32,
                         precision=lax.Precision.HIGHEST) + b_ref[...]


def _mod(c, w, b):
    bsz, d = c.shape
    nout = w.shape[1]
    return pl.pallas_call(
        _mod_kernel,
        grid=(nout // d,),
        in_specs=[pl.BlockSpec((bsz, d), lambda i: (0, 0)),
                  pl.BlockSpec((d, d), lambda i: (0, i)),
                  pl.BlockSpec((1, d), lambda i: (0, i))],
        out_specs=pl.BlockSpec((bsz, d), lambda i: (0, i)),
        out_shape=jax.ShapeDtypeStruct((bsz, nout), F32),
        compiler_params=_params(("arbitrary",)),
        name="mod",
    )(c, w, b.reshape(1, nout))


def _rope_tab_kernel(pos_ref, inv_ref, sign_ref, cos_ref, sin_ref):
    ang = pos_ref[...].astype(F32) * inv_ref[...]
    cos_ref[...] = jnp.cos(ang)
    sin_ref[...] = jnp.sin(ang) * sign_ref[...]


def _rope_tab(pos_col):
    n = pos_col.shape[0]
    tr = min(n, 1024)
    half = ATT_HEAD_DIM // 2
    inv = ROPE_THETA ** (-jnp.arange(0, ATT_HEAD_DIM, 2, dtype=F32) / ATT_HEAD_DIM)
    inv_row = jnp.tile(inv, LANES // half).reshape(1, LANES)
    lane = jnp.arange(LANES)
    sign_row = jnp.where((lane % ATT_HEAD_DIM) < half, -1.0, 1.0).astype(F32).reshape(1, LANES)
    return pl.pallas_call(
        _rope_tab_kernel,
        grid=(n // tr,),
        in_specs=[pl.BlockSpec((tr, 1), lambda i: (i, 0)),
                  pl.BlockSpec((1, LANES), lambda i: (0, 0)),
                  pl.BlockSpec((1, LANES), lambda i: (0, 0))],
        out_specs=[pl.BlockSpec((tr, LANES), lambda i: (i, 0))] * 2,
        out_shape=[jax.ShapeDtypeStruct((n, LANES), F32)] * 2,
        compiler_params=_params(("arbitrary",)),
        name="rope_tab",
    )(pos_col, inv_row, sign_row)


def _rope(v, cos, sin):
    half = ATT_HEAD_DIM // 2
    lane = lax.broadcasted_iota(jnp.int32, cos.shape, 1)
    first = (lane % ATT_HEAD_DIM) < half
    outs = []
    for j in range(v.shape[1] // LANES):
        c = v[:, j * LANES:(j + 1) * LANES]
        rot = jnp.where(first, pltpu.roll(c, LANES - half, 1), pltpu.roll(c, half, 1))
        outs.append(c * cos + rot * sin)
    return jnp.concatenate(outs, axis=1)


_C_Q, _C_K, _C_V, _C_MQK, _C_MV, _C_MO, _C_G, _C_END = 0, 512, 768, 1024, 1536, 2048, 2560, 2688


def _in_proj_kernel(x_ref, sc_ref, sh_ref, g_ref, cos_ref, sin_ref, w_ref,
                    q_ref, k_ref, v_ref, mqk_ref, mv_ref, mo_ref, gt_ref):
    x = x_ref[...]
    h = _rms(x, g_ref[...]) * (1.0 + sc_ref[0]) + sh_ref[0]
    hb = h.astype(BF16)

    def mm(a, b):
        return jnp.dot(hb, w_ref[:, a:b], preferred_element_type=F32)

    cos = cos_ref[...]
    sin = sin_ref[...]
    q_ref[...] = (_rope(mm(_C_Q, _C_K), cos, sin) * (ATT_HEAD_DIM ** -0.5)).astype(BF16)
    k_ref[...] = _rope(mm(_C_K, _C_V), cos, sin).astype(BF16)
    v_ref[...] = mm(_C_V, _C_MQK).astype(BF16)
    mqk_ref[...] = mm(_C_MQK, _C_MV)
    mv_ref[...] = mm(_C_MV, _C_MO).astype(BF16)
    mo_ref[...] = mm(_C_MO, _C_G)
    gt_ref[...] = mm(_C_G, _C_END)


def _in_proj(x2, scale1, shift1, g_pre, cos, sin, w_all, seq, tm):
    n, d = x2.shape
    per_b = seq // tm
    row = lambda i: (i, 0)
    bsel = lambda i: (i // per_b, 0, 0)
    widths = (512, 256, 256, 512, 512, 512, 128)
    dtypes = (BF16, BF16, BF16, F32, BF16, F32, F32)
    return pl.pallas_call(
        _in_proj_kernel,
        grid=(n // tm,),
        in_specs=[pl.BlockSpec((tm, d), row),
                  pl.BlockSpec((1, 1, d), bsel),
                  pl.BlockSpec((1, 1, d), bsel),
                  pl.BlockSpec((1, d), lambda i: (0, 0)),
                  pl.BlockSpec((tm, LANES), row),
                  pl.BlockSpec((tm, LANES), row),
                  pl.BlockSpec((d, _C_END), lambda i: (0, 0))],
        out_specs=[pl.BlockSpec((tm, w), row) for w in widths],
        out_shape=[jax.ShapeDtypeStruct((n, w), dt) for w, dt in zip(widths, dtypes)],
        compiler_params=_params(("arbitrary",), VMEM_LIMIT),
        name="in_proj",
    )(x2, scale1, shift1, g_pre, cos, sin, w_all)


def _swa_kernel(sink_ref, q_ref, kp_ref, kc_ref, vp_ref, vc_ref, o_ref):
    blk = ATT_BLOCK
    n = pl.program_id(1)
    qi = lax.broadcasted_iota(jnp.int32, (blk, 2 * blk), 0)
    si = lax.broadcasted_iota(jnp.int32, (blk, 2 * blk), 1)
    delta = qi + blk - si
    window = (delta >= 0) & (delta < blk)
    lo = lax.broadcasted_iota(jnp.int32, (2 * blk, LANES), 1) < ATT_HEAD_DIM
    group = ATT_HEADS // ATT_KV_HEADS
    kv = {}
    for j in range(SWA_BLOCKS):
        rows = slice(j * blk, (j + 1) * blk)
        before = slice((j - 1) * blk, j * blk)
        for g in range(ATT_KV_HEADS):
            cs = slice(g * LANES, (g + 1) * LANES)
            k = jnp.concatenate([kp_ref[:, cs] if j == 0 else kc_ref[before, cs], kc_ref[rows, cs]], axis=0)
            v = jnp.concatenate([vp_ref[:, cs] if j == 0 else vc_ref[before, cs], vc_ref[rows, cs]], axis=0)
            zero = jnp.zeros_like(k)
            kv[j, g] = ((jnp.where(lo, k, zero), jnp.where(lo, v, zero)),
                        (jnp.where(lo, zero, k), jnp.where(lo, zero, v)))
    units = [(j, h) for j in range(SWA_BLOCKS) for h in range(ATT_HEADS)]
    valid = [window & ((si >= blk) | (n > 0)) if j == 0 else window for j in range(SWA_BLOCKS)]
    sinks = [sink_ref[h] for h in range(ATT_HEADS)]
    q = {(j, h): q_ref[j * blk:(j + 1) * blk, (h // 2) * LANES:(h // 2 + 1) * LANES] for j, h in units}
    scores = {(j, h): jnp.where(valid[j], lax.dot_general(q[j, h], kv[j, h // group][h % 2][0], NT_DIMS,
                                                          preferred_element_type=F32), NEG_INF) for j, h in units}
    tops = {(j, h): jnp.maximum(jnp.max(scores[j, h], axis=-1, keepdims=True), sinks[h]) for j, h in units}
    exps = {u: jnp.exp(scores[u] - tops[u]) for u in units}
    dens = {(j, h): jnp.sum(exps[j, h], axis=-1, keepdims=True) + jnp.exp(sinks[h] - tops[j, h]) for j, h in units}
    outs = {(j, h): jnp.dot((exps[j, h] / dens[j, h]).astype(BF16), kv[j, h // group][h % 2][1],
                            preferred_element_type=F32) for j, h in units}
    for j in range(SWA_BLOCKS):
        for p in range(ATT_HEADS // 2):
            o_ref[j * blk:(j + 1) * blk, p * LANES:(p + 1) * LANES] = (outs[j, 2 * p] + outs[j, 2 * p + 1]).astype(BF16)


def _swa(sinks, q, kd, vd, bsz, seq):
    n = q.shape[0]
    nb = seq // ATT_BLOCK
    steps, partial_step = divmod(nb, SWA_BLOCKS)
    assert partial_step == 0
    rows = SWA_BLOCKS * ATT_BLOCK
    cur = lambda b, i: (b * steps + i, 0)
    prev = lambda b, i: (b * nb + jnp.maximum(i * SWA_BLOCKS - 1, 0), 0)
    return pl.pallas_call(
        _swa_kernel,
        grid=(bsz, steps),
        in_specs=[pl.BlockSpec(memory_space=pltpu.SMEM),
                  pl.BlockSpec((rows, 512), cur),
                  pl.BlockSpec((ATT_BLOCK, 256), prev),
                  pl.BlockSpec((rows, 256), cur),
                  pl.BlockSpec((ATT_BLOCK, 256), prev),
                  pl.BlockSpec((rows, 256), cur)],
        out_specs=pl.BlockSpec((rows, 512), cur),
        out_shape=jax.ShapeDtypeStruct((n, 512), BF16),
        compiler_params=_params(("arbitrary", "arbitrary")),
        name="swa",
    )(sinks, q, kd, kd, vd, vd)


def _mlstm_kernel(mqk_all, mv_all, mo_all, gt_all, cw_ref, gb_ref, ng_ref, o_all,
                  tail_all, qk_all, xs_all, ct_all, n_all, m_all, *, chunks, group):
    @pl.when(pl.program_id(1) == 0)
    def _():
        for ref in (tail_all, ct_all, n_all, m_all):
            ref[...] = jnp.zeros_like(ref)

    for g in range(group):
        _mlstm_prepare(mqk_all.at[g], gt_all.at[g], tail_all.at[g], qk_all.at[g], xs_all.at[g], cw_ref, gb_ref,
                       chunks)
    seqs = [tuple(r.at[g] for r in (mv_all, mo_all, o_all, qk_all, xs_all, ct_all, n_all, m_all))
            for g in range(group)]
    lax.fori_loop(0, chunks, lambda c, carry: _mlstm_chunk(c, seqs, ng_ref, carry), 0)


def _mlstm_prepare(mqk_ref, gt_ref, tail_ref, qk_ref, xs_ref, cw_ref, gb_ref, chunks):
    tm = chunks * MLSTM_CHUNK
    nqk = MLSTM_HEADS * MLSTM_QK_DIM

    cur = mqk_ref[...]
    full = jnp.concatenate([tail_ref[...], cur], axis=0)
    off = SUBLANES - (CONV_WIDTH - 1)
    acc = full[off:off + tm] * cw_ref[0:1, :]
    for j in range(1, CONV_WIDTH):
        acc = acc + full[off + j:off + j + tm] * cw_ref[j:j + 1, :]
    act = acc * jax.nn.sigmoid(acc)
    col = lax.broadcasted_iota(jnp.int32, (1, 2 * nqk), 1)
    act = act * jnp.where(col < nqk, MLSTM_QK_DIM ** -0.5, 1.0)
    qk_ref[...] = act.astype(BF16)
    tail_ref[...] = cur[tm - SUBLANES:tm]

    lane = lax.broadcasted_iota(jnp.int32, (tm, LANES), 1)
    gts = gt_ref[...] + gb_ref[...]
    logsig = jnp.minimum(gts, 0.0) - jnp.log(1.0 + jnp.exp(-jnp.abs(gts)))
    xs_ref[...] = jnp.where(lane < MLSTM_HEADS, gts, jnp.where(lane < 2 * MLSTM_HEADS, logsig, 0.0))


def _mlstm_chunk(c, seqs, ng_ref, carry):
    L = MLSTM_CHUNK
    nqk = MLSTM_HEADS * MLSTM_QK_DIM
    ri = lax.broadcasted_iota(jnp.int32, (L, L), 0)
    ci = lax.broadcasted_iota(jnp.int32, (L, L), 1)
    causal = ci <= ri
    tril = causal.astype(F32)
    lane_l = lax.broadcasted_iota(jnp.int32, (L, LANES), 1)
    lo_l = lane_l < MLSTM_QK_DIM
    row_c = lax.broadcasted_iota(jnp.int32, (LANES, 1), 0) < MLSTM_QK_DIM
    lane_1 = lax.broadcasted_iota(jnp.int32, (1, LANES), 1) < MLSTM_QK_DIM
    rows = pl.ds(pl.multiple_of(c * L, L), L)
    G = range(len(seqs))
    U = [(g, h) for g in G for h in range(MLSTM_HEADS)]
    P = [(g, p) for g in G for p in range(MLSTM_HEADS // 2)]
    mv_r, mo_r, o_r, qk_r, xs_r, ct_r, n_r, m_r = (dict(enumerate(col)) for col in zip(*seqs))
    pair = lambda u: (u[0], u[1] // 2)

    xc = {g: xs_r[g][rows, :] for g in G}
    bc = {g: jnp.dot(tril, xc[g], preferred_element_type=F32, precision=lax.Precision.HIGHEST) for g in G}
    x2 = {g: jnp.where(lane_l < MLSTM_HEADS, xc[g], bc[g]) for g in G}
    xt = {g: x2[g].T for g in G}
    q2 = {(g, p): qk_r[g][rows, p * LANES:(p + 1) * LANES] for g, p in P}
    k2 = {(g, p): qk_r[g][rows, nqk + p * LANES:nqk + (p + 1) * LANES] for g, p in P}
    ct_old = {(g, p): ct_r[g][p] for g, p in P}
    ctb = {gp: ct_old[gp].astype(BF16) for gp in P}
    n2 = {(g, p): n_r[g][p:p + 1, :] for g, p in P}
    hm = {u: lo_l if u[1] % 2 == 0 else jnp.logical_not(lo_l) for u in U}
    qm = {u: jnp.where(hm[u], q2[pair(u)], jnp.zeros_like(q2[pair(u)])) for u in U}
    km = {u: jnp.where(hm[u], k2[pair(u)], jnp.zeros_like(k2[pair(u)])) for u in U}
    v = {(g, h): mv_r[g][rows, h * LANES:(h + 1) * LANES] for g, h in U}
    b_col = {(g, h): x2[g][:, MLSTM_HEADS + h:MLSTM_HEADS + h + 1] for g, h in U}
    ig_col = {(g, h): x2[g][:, h:h + 1] for g, h in U}
    b_row = {(g, h): xt[g][MLSTM_HEADS + h:MLSTM_HEADS + h + 1, :] for g, h in U}
    ig_row = {(g, h): xt[g][h:h + 1, :] for g, h in U}
    m_prev = {(g, h): m_r[g][h:h + 1, 0:1] for g, h in U}
    dlog = {u: jnp.where(causal, b_col[u] - b_row[u] + ig_row[u], NEG_INF) for u in U}
    m_inter = {u: b_col[u] + m_prev[u] for u in U}
    m_t = {u: jnp.maximum(m_inter[u], jnp.max(dlog[u], axis=-1, keepdims=True)) for u in U}
    w_intra = {u: jnp.exp(dlog[u] - m_t[u]) for u in U}
    a_inter = {u: jnp.exp(m_inter[u] - m_t[u]) for u in U}
    s = {u: lax.dot_general(q2[pair(u)], km[u], NT_DIMS, preferred_element_type=F32) * w_intra[u] for u in U}
    num = {u: jnp.dot(s[u].astype(BF16), v[u], preferred_element_type=F32)
           + a_inter[u] * jnp.dot(qm[u], ctb[pair(u)], preferred_element_type=F32) for u in U}
    den = {u: jnp.sum(s[u], axis=-1, keepdims=True)
           + a_inter[u] * jnp.sum(qm[u].astype(F32) * n2[pair(u)], axis=-1, keepdims=True) for u in U}
    hh = {u: num[u] / jnp.maximum(jnp.abs(den[u]), jnp.exp(-m_t[u])) for u in U}
    y = {(g, h): _rms(hh[g, h], ng_ref[:, h * LANES:(h + 1) * LANES]) for g, h in U}
    for g, h in U:
        o_r[g][rows, h * LANES:(h + 1) * LANES] = (
            y[g, h] * jax.nn.sigmoid(mo_r[g][rows, h * LANES:(h + 1) * LANES])).astype(BF16)
    b_last = {(g, h): xt[g][MLSTM_HEADS + h:MLSTM_HEADS + h + 1, L - 1:L] for g, h in U}
    g_col = {u: b_last[u] - b_col[u] + ig_col[u] for u in U}
    m_new = {u: jnp.maximum(b_last[u] + m_prev[u], jnp.max(g_col[u], axis=0, keepdims=True)) for u in U}
    kw = {u: km[u].astype(F32) * jnp.exp(g_col[u] - m_new[u]) for u in U}
    dec = {u: jnp.exp(b_last[u] + m_prev[u] - m_new[u]) for u in U}
    upd = {u: lax.dot_general(kw[u].astype(BF16), v[u], TN_DIMS, preferred_element_type=F32) for u in U}
    for g, h in U:
        m_r[g][h:h + 1, :] = jnp.broadcast_to(m_new[g, h], (1, LANES))
    for g, p in P:
        e, o = (g, 2 * p), (g, 2 * p + 1)
        ct_r[g][p] = ct_old[g, p] * jnp.where(row_c, dec[e], dec[o]) + upd[e] + upd[o]
        n_r[g][p:p + 1, :] = (n2[g, p] * jnp.where(lane_1, dec[e], dec[o])
                              + jnp.sum(kw[e] + kw[o], axis=0, keepdims=True))
    return carry


def _mlstm(mqk, mv, mo, gts, conv_w, gate_bias, norm_g, bsz, seq, chunks, group):
    n = mqk.shape[0]
    tm = chunks * MLSTM_CHUNK
    steps = seq // tm
    row = lambda b, i: (b, i, 0)
    const = lambda b, i: (0, 0)
    width = MLSTM_HEADS * MLSTM_V_DIM
    per_seq = lambda a: a.reshape(bsz, seq, a.shape[1])
    out = pl.pallas_call(
        functools.partial(_mlstm_kernel, chunks=chunks, group=group),
        grid=(bsz // group, steps),
        in_specs=[pl.BlockSpec((group, tm, width), row),
                  pl.BlockSpec((group, tm, width), row),
                  pl.BlockSpec((group, tm, width), row),
                  pl.BlockSpec((group, tm, LANES), row),
                  pl.BlockSpec((CONV_WIDTH, width), const),
                  pl.BlockSpec((1, LANES), const),
                  pl.BlockSpec((1, width), const)],
        out_specs=pl.BlockSpec((group, tm, width), row),
        out_shape=jax.ShapeDtypeStruct((bsz, seq, width), BF16),
        scratch_shapes=[pltpu.VMEM((group, SUBLANES, width), F32),
                        pltpu.VMEM((group, tm, width), BF16),
                        pltpu.VMEM((group, tm, LANES), F32),
                        pltpu.VMEM((group, MLSTM_HEADS // 2, LANES, LANES), F32),
                        pltpu.VMEM((group, SUBLANES, LANES), F32),
                        pltpu.VMEM((group, SUBLANES, LANES), F32)],
        compiler_params=_params(("arbitrary", "arbitrary")),
        name="mlstm",
    )(per_seq(mqk), per_seq(mv), per_seq(mo), per_seq(gts), conv_w, gate_bias, norm_g)
    return out.reshape(n, width)


def _out_proj_kernel(att_ref, mh_ref, x_ref, g1_ref, sc_ref, sh_ref, gpm_ref, gpf_ref, wa_ref, wb_ref,
                     x1_ref, h2_ref):
    mix = (jnp.dot(att_ref[...], wa_ref[...], preferred_element_type=F32)
           + jnp.dot(mh_ref[...], wb_ref[...], preferred_element_type=F32))
    x1 = x_ref[...] + g1_ref[0] * _rms(mix, gpm_ref[...])
    x1_ref[...] = x1
    h2_ref[...] = _rms(x1, gpf_ref[...]) * (1.0 + sc_ref[0]) + sh_ref[0]


def _out_proj(att, mh, x2, gate1, scale2, shift2, g_post_mix, g_pre_ffn, wa, wb, seq, tm):
    n, d = x2.shape
    per_b = seq // tm
    row = lambda i: (i, 0)
    bsel = lambda i: (i // per_b, 0, 0)
    const = lambda i: (0, 0)
    half = att.shape[1]
    return pl.pallas_call(
        _out_proj_kernel,
        grid=(n // tm,),
        in_specs=[pl.BlockSpec((tm, half), row), pl.BlockSpec((tm, half), row), pl.BlockSpec((tm, d), row),
                  pl.BlockSpec((1, 1, d), bsel), pl.BlockSpec((1, 1, d), bsel), pl.BlockSpec((1, 1, d), bsel),
                  pl.BlockSpec((1, d), const), pl.BlockSpec((1, d), const),
                  pl.BlockSpec((half, d), const), pl.BlockSpec((half, d), const)],
        out_specs=[pl.BlockSpec((tm, d), row)] * 2,
        out_shape=[jax.ShapeDtypeStruct((n, d), F32)] * 2,
        compiler_params=_params(("arbitrary",), VMEM_LIMIT),
        name="out_proj",
    )(att, mh, x2, gate1, scale2, shift2, g_post_mix, g_pre_ffn, wa, wb)


_BIG_ID = float(2 ** 30)
SORT_LEVELS = 8


def _top_scores(s, k):
    rows, t = s.shape
    span = SORT_LEVELS * SUBLANES
    r = lax.broadcasted_iota(jnp.int32, (rows // SORT_LEVELS, t), 0)
    col_id = ((r // SUBLANES) * span + r % SUBLANES).astype(F32)
    lev = [jnp.concatenate([s[g * span + l * SUBLANES:g * span + (l + 1) * SUBLANES] for g in range(rows // span)],
                           axis=0) for l in range(SORT_LEVELS)]
    ids = [col_id + float(l * SUBLANES) for l in range(SORT_LEVELS)]
    for rnd in range(SORT_LEVELS):
        for a in range(rnd % 2, SORT_LEVELS - 1, 2):
            swap = lev[a + 1] > lev[a]
            lev[a], lev[a + 1] = jnp.where(swap, lev[a + 1], lev[a]), jnp.where(swap, lev[a], lev[a + 1])
            ids[a], ids[a + 1] = jnp.where(swap, ids[a + 1], ids[a]), jnp.where(swap, ids[a], ids[a + 1])
    vals, sel = [], []
    for it in range(k):
        m = jnp.max(lev[0], axis=0, keepdims=True)
        i = jnp.min(jnp.where(lev[0] == m, ids[0], _BIG_ID), axis=0, keepdims=True)
        vals.append(m)
        sel.append(i)
        hit = ids[0] == i
        live = min(SORT_LEVELS, k - 1 - it)
        for l in range(min(live, SORT_LEVELS - 1)):
            lev[l] = jnp.where(hit, lev[l + 1], lev[l])
            ids[l] = jnp.where(hit, ids[l + 1], ids[l])
        if live == SORT_LEVELS:
            lev[-1] = jnp.where(hit, NEG_INF, lev[-1])
    return jnp.concatenate(vals, axis=0), jnp.concatenate(sel, axis=0).astype(jnp.int32)


def _top_pair_sums(v1, v2):
    k, t = v1.shape
    half = SUBLANES
    lev = [v1[0:half] + v2[b:b + 1, :] for b in range(k)]
    side = v1[half:k] + v2[0:1, :]
    a_low = lax.broadcasted_iota(jnp.int32, (half, t), 0).astype(F32) * float(k)
    side_id = a_low + float(half * k)
    depth = jnp.zeros((half, t), F32)
    vals, sel = [], []
    for it in range(k):
        top_id = a_low + depth
        m = jnp.max(jnp.maximum(lev[0], side), axis=0, keepdims=True)
        i = jnp.min(jnp.minimum(jnp.where(lev[0] == m, top_id, _BIG_ID), jnp.where(side == m, side_id, _BIG_ID)),
                    axis=0, keepdims=True)
        vals.append(m)
        sel.append(i)
        hit = top_id == i
        for l in range(k - 1 - it):
            lev[l] = jnp.where(hit, lev[l + 1], lev[l])
        side = jnp.where(side_id == i, NEG_INF, side)
        depth = depth + jnp.where(hit, 1.0, 0.0)
    return jnp.concatenate(vals, axis=0), jnp.concatenate(sel, axis=0).astype(jnp.int32)


def _pick_rows(table, which):
    r = lax.broadcasted_iota(jnp.int32, table.shape, 0)
    rows = []
    for k in range(which.shape[0]):
        rows.append(jnp.sum(jnp.where(r == which[k:k + 1, :], table, 0), axis=0, keepdims=True))
    return jnp.concatenate(rows, axis=0)


def _peer_sel_kernel(h_ref, wq_ref, k1_ref, k2_ref, e_ref, g_ref, r_ref):
    tq = h_ref.shape[0]
    K = PEER_TOPK
    q = jnp.dot(h_ref[...].astype(BF16), wq_ref[...], preferred_element_type=F32).astype(BF16)
    for hd in range(PEER_HEADS):
        base = hd * 2 * PEER_HALF
        s1 = lax.dot_general(k1_ref[hd], q[:, base:base + PEER_HALF], NT_DIMS, preferred_element_type=F32)
        s2 = lax.dot_general(k2_ref[hd], q[:, base + PEER_HALF:base + 2 * PEER_HALF], NT_DIMS,
                             preferred_element_type=F32)
        v1, i1 = _top_scores(s1, K)
        v2, i2 = _top_scores(s2, K)
        top, pos = _top_pair_sums(v1, v2)
        eid = _pick_rows(i1, pos >> 4) * PEER_KEYS + _pick_rows(i2, pos & (K - 1))
        ex = jnp.exp(top - top[0:1, :])
        e_ref[hd * K:(hd + 1) * K, :] = eid * ROW_WORDS
        g_ref[hd * K:(hd + 1) * K, :] = ex / jnp.sum(ex, axis=0, keepdims=True)
    pair = (e_ref[...] >> 1).T
    nsel = pair.shape[1]
    for h in range(STREAM_HALVES):
        r_ref[:, h * nsel:(h + 1) * nsel] = pair + h


def _peer_sel(h2, wq, k1, k2, tq):
    n, d = h2.shape
    rows = PEER_HEADS * PEER_TOPK
    return pl.pallas_call(
        _peer_sel_kernel,
        grid=(n // tq,),
        in_specs=[pl.BlockSpec((tq, d), lambda i: (i, 0)),
                  pl.BlockSpec(wq.shape, lambda i: (0, 0)),
                  pl.BlockSpec(k1.shape, lambda i: (0, 0, 0)),
                  pl.BlockSpec(k2.shape, lambda i: (0, 0, 0))],
        out_specs=[pl.BlockSpec((rows, tq), lambda i: (0, i))] * 2
        + [pl.BlockSpec((tq, rows * STREAM_HALVES), lambda i: (i, 0))],
        out_shape=[jax.ShapeDtypeStruct((rows, n), jnp.int32), jax.ShapeDtypeStruct((rows, n), F32),
                   jax.ShapeDtypeStruct((n, rows * STREAM_HALVES), jnp.int32)],
        compiler_params=_params(("arbitrary",), VMEM_LIMIT),
        name="peer_sel",
    )(h2, wq, k1, k2)


def _split_bf16(x):
    hi = x.astype(BF16)
    return hi, (x - hi.astype(F32)).astype(BF16)


def _pack_kernel(w_ref, o_ref):
    x = w_ref[...]
    eb = x.shape[0]
    for r in range(ROW_WORDS):
        lo = x[:, 2 * r * LANES:(2 * r + 1) * LANES].astype(BF16).astype(F32)
        hi = x[:, (2 * r + 1) * LANES:(2 * r + 2) * LANES].astype(BF16).astype(F32)
        word = (lax.shift_right_logical(pltpu.bitcast(lo, jnp.int32), 16)
                | (pltpu.bitcast(hi, jnp.int32) & jnp.int32(-65536)))
        o_ref[pl.ds(r, eb, stride=ROW_WORDS), :] = word


def _pack_table(w):
    e, d = w.shape
    eb = 512
    return pl.pallas_call(
        _pack_kernel,
        grid=(e // eb,),
        in_specs=[pl.BlockSpec((eb, d), lambda i: (i, 0))],
        out_specs=pl.BlockSpec((eb * ROW_WORDS, LANES), lambda i: (i, 0)),
        out_shape=jax.ShapeDtypeStruct((e * ROW_WORDS, LANES), jnp.int32),
        compiler_params=_params(("arbitrary",)),
        name="pack_table",
    )(w)


def _gather_rows(slots, tab_ref, t, stage_ref):
    for j, slot in enumerate(slots):
        src = pl.ds(pl.multiple_of(slot[t], ROW_WORDS), ROW_WORDS)
        stage_ref[j * ROW_WORDS:(j + 1) * ROW_WORDS, :] = tab_ref[src, :]


def _staged_bf16(stage_ref):
    return pltpu.bitcast(stage_ref[...], BF16)


def _pipelined_tokens(ntok, gather, compute, stages, on_trip=None, after=None):
    nb = len(stages)
    for k in range(nb):
        gather(k, stages[k])

    def trip(i, carry):
        t = nb * i
        if on_trip is not None:
            on_trip(i)
        for k in range(nb):
            compute(t + k, stages[k])
            ahead = t + k + nb
            gather(jnp.minimum(ahead, ntok - 1), stages[k])
            if after is not None:
                after(i, k)
        return carry

    lax.fori_loop(0, ntok // nb, trip, 0)


def _with_slot_indices(idx_hbm, sems, bufs, stride, count, run):
    nsel = len(bufs) // 2
    step = pl.program_id(0)
    last = pl.num_programs(0) - 1

    def copies(block, which):
        return [pltpu.make_async_copy(idx_hbm.at[j, pl.ds(block * stride, count)], bufs[which * nsel + j],
                                      sems.at[which])
                for j in range(nsel)]

    @pl.when(step == 0)
    def _():
        for cp in copies(0, 0):
            cp.start()

    def phase(which):
        for cp in copies(step, which):
            cp.wait()

        @pl.when(step < last)
        def _():
            for cp in copies(step + 1, 1 - which):
                cp.start()

        run(bufs[which * nsel:(which + 1) * nsel])

    for which in range(2):
        pl.when(step % 2 == which)(functools.partial(phase, which))


SC_WINDOW = 128


def _sc_gather_rows(tab, row_ids):
    n = row_ids.shape[0]
    mesh = plsc.VectorSubcoreMesh(core_axis_name="core", subcore_axis_name="subcore")

    @pl.kernel(out_type=jax.ShapeDtypeStruct((n, tab.shape[1]), tab.dtype), mesh=mesh)
    def gather(tab_hbm, ids_hbm, out_hbm):
        def body(ids_vmem, out_vmem):
            pltpu.sync_copy(tab_hbm.at[ids_vmem.at[0]], out_vmem)

        pltpu.emit_pipeline(
            body,
            grid=(n // SC_WINDOW,),
            in_specs=[pl.BlockSpec((1, SC_WINDOW), index_map=lambda i: (0, i))],
            out_specs=[pl.BlockSpec((SC_WINDOW, tab.shape[1]), index_map=lambda i: (i, 0))],
            core_axis_name=("core", "subcore"),
            dimension_semantics=(pltpu.PARALLEL,),
            trace_scopes=False,
        )(ids_hbm, out_hbm)

    return gather(tab, row_ids.reshape(1, n))


def _diag_mask(nsel):
    shape = (SUBLANES, nsel * SUBLANES)
    return (lax.broadcasted_iota(jnp.int32, shape, 1) % SUBLANES) == lax.broadcasted_iota(jnp.int32, shape, 0)


def _token_tile(ref, t):
    row = ref[pl.ds(t, 1), :]
    return jnp.concatenate([row[:, s * LANES:(s + 1) * LANES] for s in range(SUBLANES)], axis=0)


def _peer_u_kernel(idx_hbm, h_ref, gate_ref, grp_ref, tab_ref, act_ref, part_ref, *scratch):
    nsel, tq = gate_ref.shape
    stages, sems, bufs = scratch[:GATHER_STAGES], scratch[GATHER_STAGES], scratch[GATHER_STAGES + 1:]
    diag = _diag_mask(nsel)

    def compute(t, stage_ref):
        h_hi, h_lo = _split_bf16(_token_tile(h_ref, t))
        both = lax.dot_general(jnp.concatenate([h_hi, h_lo], axis=0), _staged_bf16(stage_ref), NT_DIMS,
                               preferred_element_type=F32)
        prod = both[0:SUBLANES] + both[SUBLANES:2 * SUBLANES]
        part_ref[pl.ds(t, 1), :] = jnp.sum(jnp.where(diag, prod, 0.0), axis=0, keepdims=True)

    def run(slots):
        _pipelined_tokens(tq, functools.partial(_gather_rows, slots, tab_ref), compute, stages)

    _with_slot_indices(idx_hbm, sems, bufs, tq, tq, run)
    p_hi, p_lo = _split_bf16(part_ref[...])
    grp = grp_ref[...]
    pre = jnp.dot(p_hi, grp, preferred_element_type=F32) + jnp.dot(p_lo, grp, preferred_element_type=F32)
    act_ref[...] = 0.5 * pre * (1.0 + lax.erf(pre * (2.0 ** -0.5))) * gate_ref[...].T


def _group_matrix(nsel):
    r = jnp.arange(nsel * SUBLANES)[:, None] // SUBLANES
    return (r == jnp.arange(nsel)[None, :]).astype(BF16)


def _gather_scratch(nsel, ntok, nstages):
    return ([pltpu.VMEM((nsel * ROW_WORDS, LANES), jnp.int32)] * nstages
            + [pltpu.SemaphoreType.DMA((2,))] + [pltpu.SMEM((ntok,), jnp.int32)] * (2 * nsel))


def _peer_u(idx_t, h2, gate_t, tab, tq):
    nsel, n = idx_t.shape
    d = h2.shape[1]
    wide = nsel * SUBLANES
    return pl.pallas_call(
        _peer_u_kernel,
        grid=(n // tq,),
        in_specs=[pl.BlockSpec(memory_space=pl.ANY),
                  pl.BlockSpec((tq, d), lambda i: (i, 0)),
                  pl.BlockSpec((nsel, tq), lambda i: (0, i)),
                  pl.BlockSpec((wide, nsel), lambda i: (0, 0)),
                  pl.BlockSpec(memory_space=pltpu.VMEM)],
        out_specs=pl.BlockSpec((tq, nsel), lambda i: (i, 0)),
        out_shape=jax.ShapeDtypeStruct((n, nsel), F32),
        scratch_shapes=[pltpu.VMEM((tq, wide), F32)] + _gather_scratch(nsel, tq, GATHER_STAGES),
        compiler_params=_params(("arbitrary",), GATHER_VMEM_LIMIT),
        name="peer_u",
    )(idx_t, h2, gate_t, _group_matrix(nsel), tab)


def _lane_aligned(count):
    return -(-count // LANES) * LANES


def _streamed_rows_bf16(words):
    x = pltpu.bitcast(words, BF16)
    return jnp.concatenate([x[:, 0:LANES], x[:, LANES:2 * LANES]], axis=0)


def _streamed_copy_matrix(nsel):
    r = jnp.arange(nsel * SUBLANES)
    return ((r[None, :] % (2 * nsel)) // 2 == jnp.arange(nsel)[:, None]).astype(BF16)


def _streamed_diag(nsel):
    shape = (SUBLANES, nsel * SUBLANES)
    r = lax.broadcasted_iota(jnp.int32, shape, 1)
    sub = 4 * ((r % (4 * nsel)) // (2 * nsel)) + 2 * (r // (4 * nsel)) + r % 2
    return lax.broadcasted_iota(jnp.int32, shape, 0) == sub


def _peer_v_kernel(idx_hbm, act_ref, rep_ref, rep2_ref, x1_ref, g2_ref, gpf_ref, tab_ref, rows_hbm, o_ref,
                   wide_ref, y_ref, rowbuf, rsem, *scratch, nvld, nstages):
    tq, nsel = act_ref.shape
    stages, sems, bufs = scratch[:nstages], scratch[nstages], scratch[nstages + 1:]
    trips = nvld // nstages
    per_trip = (tq - nvld) // trips
    rows_tok = nsel * STREAM_HALVES
    step = pl.program_id(0)
    chunks = pl.num_programs(0) * trips
    diag = _diag_mask(nsel)
    diag2 = _streamed_diag(nsel)
    a_hi, a_lo = _split_bf16(act_ref[...])
    for lo, hi, rep in ((0, nvld, rep_ref[...]), (nvld, tq, rep2_ref[...])):
        wide_ref[lo:hi, :] = (jnp.dot(a_hi[lo:hi], rep, preferred_element_type=F32)
                              + jnp.dot(a_lo[lo:hi], rep, preferred_element_type=F32))

    def finish(t, mask, rows_bf16):
        w = jnp.where(mask, jnp.broadcast_to(wide_ref[pl.ds(t, 1), :], mask.shape), 0.0)
        w_hi, w_lo = _split_bf16(w)
        both = jnp.dot(jnp.concatenate([w_hi, w_lo], axis=0), rows_bf16, preferred_element_type=F32)
        tile = both[0:SUBLANES] + both[SUBLANES:2 * SUBLANES]
        y_ref[pl.ds(t, 1), :] = jnp.concatenate([tile[s:s + 1, :] for s in range(SUBLANES)], axis=1)

    def compute(t, stage_ref):
        finish(t, diag, _staged_bf16(stage_ref))

    def chunk_copy(g, slot):
        return pltpu.make_async_copy(rows_hbm.at[pl.ds(g * (per_trip * rows_tok), per_trip * rows_tok), :],
                                     rowbuf.at[slot], rsem.at[slot])

    ring = STREAM_RING

    @pl.when(step == 0)
    def _():
        for g0 in range(ring - 1):
            chunk_copy(g0, g0).start()

    def on_trip(i):
        g = step * trips + i
        chunk_copy(g, i % ring).wait()

        @pl.when(g + ring - 1 < chunks)
        def _():
            chunk_copy(g + ring - 1, (i + ring - 1) % ring).start()

    places = [(d * nstages) // per_trip for d in range(per_trip)]

    def after(i, k):
        for d in range(per_trip):
            if places[d] == k:
                rows = rowbuf.at[i % ring, pl.ds(d * rows_tok, rows_tok), :]
                finish(nvld + i * per_trip + d, diag2, _streamed_rows_bf16(rows[...]))

    def run(slots):
        _pipelined_tokens(nvld, functools.partial(_gather_rows, slots, tab_ref), compute, stages, on_trip, after)

    _with_slot_indices(idx_hbm, sems, bufs, tq, _lane_aligned(nvld), run)
    o_ref[...] = x1_ref[...] + g2_ref[0] * _rms(y_ref[...], gpf_ref[...])


def _peer_v(idx_t, act, x1, gate2, g_post_ffn, tab, rows, seq, tq, nvld, nstages):
    nsel, n = idx_t.shape
    d = x1.shape[1]
    wide = nsel * SUBLANES
    per_b = seq // tq
    row = lambda i: (i, 0)
    const = lambda i: (0, 0)
    trips = nvld // nstages
    per_trip = (tq - nvld) // trips
    assert trips % STREAM_RING == 0 and trips * nstages == nvld and trips * per_trip == tq - nvld
    return pl.pallas_call(
        functools.partial(_peer_v_kernel, nvld=nvld, nstages=nstages),
        grid=(n // tq,),
        in_specs=[pl.BlockSpec(memory_space=pl.ANY),
                  pl.BlockSpec((tq, nsel), row),
                  pl.BlockSpec((nsel, wide), const),
                  pl.BlockSpec((nsel, wide), const),
                  pl.BlockSpec((tq, d), row),
                  pl.BlockSpec((1, 1, d), lambda i: (i // per_b, 0, 0)),
                  pl.BlockSpec((1, d), const),
                  pl.BlockSpec(memory_space=pltpu.VMEM),
                  pl.BlockSpec(memory_space=pl.ANY)],
        out_specs=pl.BlockSpec((tq, d), row),
        out_shape=jax.ShapeDtypeStruct((n, d), F32),
        scratch_shapes=[pltpu.VMEM((tq, wide), F32), pltpu.VMEM((tq, d), F32),
                        pltpu.VMEM((STREAM_RING, per_trip * nsel * STREAM_HALVES, STREAM_ROW), jnp.int32),
                        pltpu.SemaphoreType.DMA((STREAM_RING,))]
        + _gather_scratch(nsel, _lane_aligned(nvld), nstages),
        compiler_params=_params(("arbitrary",), GATHER_VMEM_LIMIT),
        name="peer_v",
    )(idx_t, act, _group_matrix(nsel).T, _streamed_copy_matrix(nsel), x1, gate2, g_post_ffn, tab, rows)


def _dup_heads(w, heads, dh):
    d = w.shape[0]
    return jnp.repeat(w.reshape(d, heads, 1, dh), 2, axis=2).reshape(d, heads * 2 * dh)


def _layer(x2, c, pos_col, bsz, seq, w_mod, b_mod, g_pre_mix, g_post_mix, w_in, conv_w, b_igate, b_fgate,
           mlstm_norm_g, att_sinks, w_out, g_pre_ffn, g_post_ffn, peer_wq, peer_keys1, peer_keys2, peer_u, peer_v):
    n, d = x2.shape
    tm = min(seq, 512)
    mod = _mod(c, w_mod, b_mod)
    shift1, scale1, gate1, shift2, scale2, gate2 = [m.reshape(bsz, 1, d) for m in jnp.split(mod, 6, axis=-1)]

    aw = ATT_HEADS * ATT_HEAD_DIM
    kvw = ATT_KV_HEADS * ATT_HEAD_DIM
    qkw = MLSTM_HEADS * MLSTM_QK_DIM
    mw = MLSTM_HEADS * MLSTM_V_DIM
    o = 0
    wq_a = w_in[:, o:o + aw]; o += aw
    wk_a = w_in[:, o:o + kvw]; o += kvw
    wv_a = w_in[:, o:o + kvw]; o += kvw
    w_mqk = w_in[:, o:o + 2 * qkw]; o += 2 * qkw
    w_mv = w_in[:, o:o + mw]; o += mw
    w_g = w_in[:, o:o + 2 * MLSTM_HEADS]; o += 2 * MLSTM_HEADS
    w_mo = w_in[:, o:o + mw]
    w_gp = jnp.pad(w_g, ((0, 0), (0, LANES - 2 * MLSTM_HEADS)))
    w_all = jnp.concatenate([wq_a, _dup_heads(wk_a, ATT_KV_HEADS, ATT_HEAD_DIM),
                             _dup_heads(wv_a, ATT_KV_HEADS, ATT_HEAD_DIM), w_mqk, w_mv, w_mo, w_gp],
                            axis=1).astype(BF16)

    cos, sin = _rope_tab(pos_col)
    q, kd, vd, mqk, mv, mo, gts = _in_proj(x2, scale1, shift1, g_pre_mix.reshape(1, d), cos, sin, w_all, seq, tm)
    att = _swa(att_sinks, q, kd, vd, bsz, seq)
    gate_bias = jnp.pad(jnp.concatenate([b_igate, b_fgate]), (0, LANES - 2 * MLSTM_HEADS)).reshape(1, LANES)
    chunks = min(seq // MLSTM_CHUNK, 8)
    mh = _mlstm(mqk, mv, mo, gts, conv_w, gate_bias, mlstm_norm_g.reshape(1, mw), bsz, seq, chunks,
                MLSTM_GROUP if bsz % MLSTM_GROUP == 0 else 1)
    wo = w_out.astype(BF16)
    x1, h2 = _out_proj(att, mh, x2, gate1, scale2, shift2, g_post_mix.reshape(1, d), g_pre_ffn.reshape(1, d),
                       wo[:aw], wo[aw:], seq, tm)

    tq = min(n, 256)
    eid_t, gate_t, row_ids = _peer_sel(h2, peer_wq.astype(BF16), peer_keys1.astype(BF16),
                                       peer_keys2.astype(BF16), tq)
    tg = min(seq, GATHER_TOKENS)
    tab_v = _pack_table(peer_v)
    nvld = tg * STREAM_SPLIT[0] // sum(STREAM_SPLIT)
    streamed = row_ids.reshape(n // tg, tg, row_ids.shape[1])[:, nvld:, :]
    rows_v = _sc_gather_rows(tab_v.reshape(-1, STREAM_ROW), streamed.reshape(-1))
    act = _peer_u(eid_t, h2, gate_t, _pack_table(peer_u), tg)
    return _peer_v(eid_t, act, x1, gate2, g_post_ffn.reshape(1, d), tab_v, rows_v, seq, tg, nvld, STREAM_STAGES)


def kernel(x, c, positions, w_mod, b_mod, g_pre_mix, g_post_mix, w_in, conv_w, b_igate, b_fgate, mlstm_norm_g, att_sinks, w_out, g_pre_ffn, g_post_ffn, peer_wq, peer_keys1, peer_keys2, peer_u, peer_v):
    bsz, seq, d = x.shape
    n = bsz * seq
    x2 = x.reshape(n, d)
    pos_col = positions.reshape(n, 1)
    for l in range(w_mod.shape[0]):
        x2 = _layer(x2, c, pos_col, bsz, seq, w_mod[l], b_mod[l], g_pre_mix[l], g_post_mix[l], w_in[l], conv_w[l],
                    b_igate[l], b_fgate[l], mlstm_norm_g[l], att_sinks[l], w_out[l], g_pre_ffn[l], g_post_ffn[l],
                    peer_wq[l], peer_keys1[l], peer_keys2[l], peer_u[l], peer_v[l])
    return x2.reshape(bsz, seq, d)
```

```python
import functools

import jax
import jax.numpy as jnp
from jax import lax
from jax.experimental import pallas as pl
from jax.experimental.pallas import tpu as pltpu
from jax.experimental.pallas import tpu_sc as plsc

F32 = jnp.float32
BF16 = jnp.bfloat16

ATT_HEADS = 8
ATT_KV_HEADS = 2
ATT_HEAD_DIM = 64
ATT_BLOCK = 128
ROPE_THETA = 10000.0
MLSTM_HEADS = 4
MLSTM_V_DIM = 128
MLSTM_QK_DIM = 64
MLSTM_CHUNK = 64
CONV_WIDTH = 4
PEER_HEADS = 8
PEER_KEYS = 128
PEER_HALF = 128
PEER_TOPK = 16
NORM_EPS = 1e-6

LANES = 128
SUBLANES = 8
VMEM_LIMIT = 52 * 1024 * 1024
GATHER_STAGES = 32
ROW_WORDS = SUBLANES // 2
SWA_BLOCKS = 2
MLSTM_GROUP = 4
STREAM_SPLIT = (4, 4)
STREAM_RING = 4
STREAM_STAGES = 8
STREAM_HALVES = 2
STREAM_ROW = 2 * LANES
GATHER_TOKENS = 512
GATHER_VMEM_LIMIT = 58 * 1024 * 1024

NEG_INF = float("-inf")
NT_DIMS = (((1,), (1,)), ((), ()))
TN_DIMS = (((0,), (0,)), ((), ()))


def _params(sem, vmem=None):
    return pltpu.CompilerParams(dimension_semantics=sem, vmem_limit_bytes=vmem)


def _rms(x, g):
    return x * lax.rsqrt(jnp.mean(x * x, axis=-1, keepdims=True) + NORM_EPS) * g


def _mod_kernel(c_ref, w_ref, b_ref, o_ref):
    o_ref[...] = jnp.dot(c_ref[...], w_ref[...], preferred_element_type=F32,
                         precision=lax.Precision.HIGHEST) + b_ref[...]


def _mod(c, w, b):
    bsz, d = c.shape
    nout = w.shape[1]
    return pl.pallas_call(
        _mod_kernel,
        grid=(nout // d,),
        in_specs=[pl.BlockSpec((bsz, d), lambda i: (0, 0)),
                  pl.BlockSpec((d, d), lambda i: (0, i)),
                  pl.BlockSpec((1, d), lambda i: (0, i))],
        out_specs=pl.BlockSpec((bsz, d), lambda i: (0, i)),
        out_shape=jax.ShapeDtypeStruct((bsz, nout), F32),
        compiler_params=_params(("arbitrary",)),
        name="mod",
    )(c, w, b.reshape(1, nout))


def _rope_tab_kernel(pos_ref, inv_ref, sign_ref, cos_ref, sin_ref):
    ang = pos_ref[...].astype(F32) * inv_ref[...]
    cos_ref[...] = jnp.cos(ang)
    sin_ref[...] = jnp.sin(ang) * sign_ref[...]


def _rope_tab(pos_col):
    n = pos_col.shape[0]
    tr = min(n, 1024)
    half = ATT_HEAD_DIM // 2
    inv = ROPE_THETA ** (-jnp.arange(0, ATT_HEAD_DIM, 2, dtype=F32) / ATT_HEAD_DIM)
    inv_row = jnp.tile(inv, LANES // half).reshape(1, LANES)
    lane = jnp.arange(LANES)
    sign_row = jnp.where((lane % ATT_HEAD_DIM) < half, -1.0, 1.0).astype(F32).reshape(1, LANES)
    return pl.pallas_call(
        _rope_tab_kernel,
        grid=(n // tr,),
        in_specs=[pl.BlockSpec((tr, 1), lambda i: (i, 0)),
                  pl.BlockSpec((1, LANES), lambda i: (0, 0)),
                  pl.BlockSpec((1, LANES), lambda i: (0, 0))],
        out_specs=[pl.BlockSpec((tr, LANES), lambda i: (i, 0))] * 2,
        out_shape=[jax.ShapeDtypeStruct((n, LANES), F32)] * 2,
        compiler_params=_params(("arbitrary",)),
        name="rope_tab",
    )(pos_col, inv_row, sign_row)


def _rope(v, cos, sin):
    half = ATT_HEAD_DIM // 2
    lane = lax.broadcasted_iota(jnp.int32, cos.shape, 1)
    first = (lane % ATT_HEAD_DIM) < half
    outs = []
    for j in range(v.shape[1] // LANES):
        c = v[:, j * LANES:(j + 1) * LANES]
        rot = jnp.where(first, pltpu.roll(c, LANES - half, 1), pltpu.roll(c, half, 1))
        outs.append(c * cos + rot * sin)
    return jnp.concatenate(outs, axis=1)


_C_Q, _C_K, _C_V, _C_MQK, _C_MV, _C_MO, _C_G, _C_END = 0, 512, 768, 1024, 1536, 2048, 2560, 2688


def _in_proj_kernel(x_ref, sc_ref, sh_ref, g_ref, cos_ref, sin_ref, w_ref,
                    q_ref, k_ref, v_ref, mqk_ref, mv_ref, mo_ref, gt_ref):
    x = x_ref[...]
    h = _rms(x, g_ref[...]) * (1.0 + sc_ref[0]) + sh_ref[0]
    hb = h.astype(BF16)

    def mm(a, b):
        return jnp.dot(hb, w_ref[:, a:b], preferred_element_type=F32)

    cos = cos_ref[...]
    sin = sin_ref[...]
    q_ref[...] = (_rope(mm(_C_Q, _C_K), cos, sin) * (ATT_HEAD_DIM ** -0.5)).astype(BF16)
    k_ref[...] = _rope(mm(_C_K, _C_V), cos, sin).astype(BF16)
    v_ref[...] = mm(_C_V, _C_MQK).astype(BF16)
    mqk_ref[...] = mm(_C_MQK, _C_MV)
    mv_ref[...] = mm(_C_MV, _C_MO).astype(BF16)
    mo_ref[...] = mm(_C_MO, _C_G)
    gt_ref[...] = mm(_C_G, _C_END)


def _in_proj(x2, scale1, shift1, g_pre, cos, sin, w_all, seq, tm):
    n, d = x2.shape
    per_b = seq // tm
    row = lambda i: (i, 0)
    bsel = lambda i: (i // per_b, 0, 0)
    widths = (512, 256, 256, 512, 512, 512, 128)
    dtypes = (BF16, BF16, BF16, F32, BF16, F32, F32)
    return pl.pallas_call(
        _in_proj_kernel,
        grid=(n // tm,),
        in_specs=[pl.BlockSpec((tm, d), row),
                  pl.BlockSpec((1, 1, d), bsel),
                  pl.BlockSpec((1, 1, d), bsel),
                  pl.BlockSpec((1, d), lambda i: (0, 0)),
                  pl.BlockSpec((tm, LANES), row),
                  pl.BlockSpec((tm, LANES), row),
                  pl.BlockSpec((d, _C_END), lambda i: (0, 0))],
        out_specs=[pl.BlockSpec((tm, w), row) for w in widths],
        out_shape=[jax.ShapeDtypeStruct((n, w), dt) for w, dt in zip(widths, dtypes)],
        compiler_params=_params(("arbitrary",), VMEM_LIMIT),
        name="in_proj",
    )(x2, scale1, shift1, g_pre, cos, sin, w_all)


def _swa_kernel(sink_ref, q_ref, kp_ref, kc_ref, vp_ref, vc_ref, o_ref):
    blk = ATT_BLOCK
    n = pl.program_id(1)
    qi = lax.broadcasted_iota(jnp.int32, (blk, 2 * blk), 0)
    si = lax.broadcasted_iota(jnp.int32, (blk, 2 * blk), 1)
    delta = qi + blk - si
    window = (delta >= 0) & (delta < blk)
    lo = lax.broadcasted_iota(jnp.int32, (2 * blk, LANES), 1) < ATT_HEAD_DIM
    group = ATT_HEADS // ATT_KV_HEADS
    kv = {}
    for j in range(SWA_BLOCKS):
        rows = slice(j * blk, (j + 1) * blk)
        before = slice((j - 1) * blk, j * blk)
        for g in range(ATT_KV_HEADS):
            cs = slice(g * LANES, (g + 1) * LANES)
            k = jnp.concatenate([kp_ref[:, cs] if j == 0 else kc_ref[before, cs], kc_ref[rows, cs]], axis=0)
            v = jnp.concatenate([vp_ref[:, cs] if j == 0 else vc_ref[before, cs], vc_ref[rows, cs]], axis=0)
            zero = jnp.zeros_like(k)
            kv[j, g] = ((jnp.where(lo, k, zero), jnp.where(lo, v, zero)),
                        (jnp.where(lo, zero, k), jnp.where(lo, zero, v)))
    units = [(j, h) for j in range(SWA_BLOCKS) for h in range(ATT_HEADS)]
    valid = [window & ((si >= blk) | (n > 0)) if j == 0 else window for j in range(SWA_BLOCKS)]
    sinks = [sink_ref[h] for h in range(ATT_HEADS)]
    q = {(j, h): q_ref[j * blk:(j + 1) * blk, (h // 2) * LANES:(h // 2 + 1) * LANES] for j, h in units}
    scores = {(j, h): jnp.where(valid[j], lax.dot_general(q[j, h], kv[j, h // group][h % 2][0], NT_DIMS,
                                                          preferred_element_type=F32), NEG_INF) for j, h in units}
    tops = {(j, h): jnp.maximum(jnp.max(scores[j, h], axis=-1, keepdims=True), sinks[h]) for j, h in units}
    exps = {u: jnp.exp(scores[u] - tops[u]) for u in units}
    dens = {(j, h): jnp.sum(exps[j, h], axis=-1, keepdims=True) + jnp.exp(sinks[h] - tops[j, h]) for j, h in units}
    outs = {(j, h): jnp.dot((exps[j, h] / dens[j, h]).astype(BF16), kv[j, h // group][h % 2][1],
                            preferred_element_type=F32) for j, h in units}
    for j in range(SWA_BLOCKS):
        for p in range(ATT_HEADS // 2):
            o_ref[j * blk:(j + 1) * blk, p * LANES:(p + 1) * LANES] = (outs[j, 2 * p] + outs[j, 2 * p + 1]).astype(BF16)


def _swa(sinks, q, kd, vd, bsz, seq):
    n = q.shape[0]
    nb = seq // ATT_BLOCK
    steps, partial_step = divmod(nb, SWA_BLOCKS)
    assert partial_step == 0
    rows = SWA_BLOCKS * ATT_BLOCK
    cur = lambda b, i: (b * steps + i, 0)
    prev = lambda b, i: (b * nb + jnp.maximum(i * SWA_BLOCKS - 1, 0), 0)
    return pl.pallas_call(
        _swa_kernel,
        grid=(bsz, steps),
        in_specs=[pl.BlockSpec(memory_space=pltpu.SMEM),
                  pl.BlockSpec((rows, 512), cur),
                  pl.BlockSpec((ATT_BLOCK, 256), prev),
                  pl.BlockSpec((rows, 256), cur),
                  pl.BlockSpec((ATT_BLOCK, 256), prev),
                  pl.BlockSpec((rows, 256), cur)],
        out_specs=pl.BlockSpec((rows, 512), cur),
        out_shape=jax.ShapeDtypeStruct((n, 512), BF16),
        compiler_params=_params(("arbitrary", "arbitrary")),
        name="swa",
    )(sinks, q, kd, kd, vd, vd)


def _mlstm_kernel(mqk_all, mv_all, mo_all, gt_all, cw_ref, gb_ref, ng_ref, o_all,
                  tail_all, qk_all, xs_all, ct_all, n_all, m_all, *, chunks, group):
    @pl.when(pl.program_id(1) == 0)
    def _():
        for ref in (tail_all, ct_all, n_all, m_all):
            ref[...] = jnp.zeros_like(ref)

    for g in range(group):
        _mlstm_prepare(mqk_all.at[g], gt_all.at[g], tail_all.at[g], qk_all.at[g], xs_all.at[g], cw_ref, gb_ref,
                       chunks)
    seqs = [tuple(r.at[g] for r in (mv_all, mo_all, o_all, qk_all, xs_all, ct_all, n_all, m_all))
            for g in range(group)]
    lax.fori_loop(0, chunks, lambda c, carry: _mlstm_chunk(c, seqs, ng_ref, carry), 0)


def _mlstm_prepare(mqk_ref, gt_ref, tail_ref, qk_ref, xs_ref, cw_ref, gb_ref, chunks):
    tm = chunks * MLSTM_CHUNK
    nqk = MLSTM_HEADS * MLSTM_QK_DIM

    cur = mqk_ref[...]
    full = jnp.concatenate([tail_ref[...], cur], axis=0)
    off = SUBLANES - (CONV_WIDTH - 1)
    acc = full[off:off + tm] * cw_ref[0:1, :]
    for j in range(1, CONV_WIDTH):
        acc = acc + full[off + j:off + j + tm] * cw_ref[j:j + 1, :]
    act = acc * jax.nn.sigmoid(acc)
    col = lax.broadcasted_iota(jnp.int32, (1, 2 * nqk), 1)
    act = act * jnp.where(col < nqk, MLSTM_QK_DIM ** -0.5, 1.0)
    qk_ref[...] = act.astype(BF16)
    tail_ref[...] = cur[tm - SUBLANES:tm]

    lane = lax.broadcasted_iota(jnp.int32, (tm, LANES), 1)
    gts = gt_ref[...] + gb_ref[...]
    logsig = jnp.minimum(gts, 0.0) - jnp.log(1.0 + jnp.exp(-jnp.abs(gts)))
    xs_ref[...] = jnp.where(lane < MLSTM_HEADS, gts, jnp.where(lane < 2 * MLSTM_HEADS, logsig, 0.0))


def _mlstm_chunk(c, seqs, ng_ref, carry):
    L = MLSTM_CHUNK
    nqk = MLSTM_HEADS * MLSTM_QK_DIM
    ri = lax.broadcasted_iota(jnp.int32, (L, L), 0)
    ci = lax.broadcasted_iota(jnp.int32, (L, L), 1)
    causal = ci <= ri
    tril = causal.astype(F32)
    lane_l = lax.broadcasted_iota(jnp.int32, (L, LANES), 1)
    lo_l = lane_l < MLSTM_QK_DIM
    row_c = lax.broadcasted_iota(jnp.int32, (LANES, 1), 0) < MLSTM_QK_DIM
    lane_1 = lax.broadcasted_iota(jnp.int32, (1, LANES), 1) < MLSTM_QK_DIM
    rows = pl.ds(pl.multiple_of(c * L, L), L)
    G = range(len(seqs))
    U = [(g, h) for g in G for h in range(MLSTM_HEADS)]
    P = [(g, p) for g in G for p in range(MLSTM_HEADS // 2)]
    mv_r, mo_r, o_r, qk_r, xs_r, ct_r, n_r, m_r = (dict(enumerate(col)) for col in zip(*seqs))
    pair = lambda u: (u[0], u[1] // 2)

    xc = {g: xs_r[g][rows, :] for g in G}
    bc = {g: jnp.dot(tril, xc[g], preferred_element_type=F32, precision=lax.Precision.HIGHEST) for g in G}
    x2 = {g: jnp.where(lane_l < MLSTM_HEADS, xc[g], bc[g]) for g in G}
    xt = {g: x2[g].T for g in G}
    q2 = {(g, p): qk_r[g][rows, p * LANES:(p + 1) * LANES] for g, p in P}
    k2 = {(g, p): qk_r[g][rows, nqk + p * LANES:nqk + (p + 1) * LANES] for g, p in P}
    ct_old = {(g, p): ct_r[g][p] for g, p in P}
    ctb = {gp: ct_old[gp].astype(BF16) for gp in P}
    n2 = {(g, p): n_r[g][p:p + 1, :] for g, p in P}
    hm = {u: lo_l if u[1] % 2 == 0 else jnp.logical_not(lo_l) for u in U}
    qm = {u: jnp.where(hm[u], q2[pair(u)], jnp.zeros_like(q2[pair(u)])) for u in U}
    km = {u: jnp.where(hm[u], k2[pair(u)], jnp.zeros_like(k2[pair(u)])) for u in U}
    v = {(g, h): mv_r[g][rows, h * LANES:(h + 1) * LANES] for g, h in U}
    b_col = {(g, h): x2[g][:, MLSTM_HEADS + h:MLSTM_HEADS + h + 1] for g, h in U}
    ig_col = {(g, h): x2[g][:, h:h + 1] for g, h in U}
    b_row = {(g, h): xt[g][MLSTM_HEADS + h:MLSTM_HEADS + h + 1, :] for g, h in U}
    ig_row = {(g, h): xt[g][h:h + 1, :] for g, h in U}
    m_prev = {(g, h): m_r[g][h:h + 1, 0:1] for g, h in U}
    dlog = {u: jnp.where(causal, b_col[u] - b_row[u] + ig_row[u], NEG_INF) for u in U}
    m_inter = {u: b_col[u] + m_prev[u] for u in U}
    m_t = {u: jnp.maximum(m_inter[u], jnp.max(dlog[u], axis=-1, keepdims=True)) for u in U}
    w_intra = {u: jnp.exp(dlog[u] - m_t[u]) for u in U}
    a_inter = {u: jnp.exp(m_inter[u] - m_t[u]) for u in U}
    s = {u: lax.dot_general(q2[pair(u)], km[u], NT_DIMS, preferred_element_type=F32) * w_intra[u] for u in U}
    num = {u: jnp.dot(s[u].astype(BF16), v[u], preferred_element_type=F32)
           + a_inter[u] * jnp.dot(qm[u], ctb[pair(u)], preferred_element_type=F32) for u in U}
    den = {u: jnp.sum(s[u], axis=-1, keepdims=True)
           + a_inter[u] * jnp.sum(qm[u].astype(F32) * n2[pair(u)], axis=-1, keepdims=True) for u in U}
    hh = {u: num[u] / jnp.maximum(jnp.abs(den[u]), jnp.exp(-m_t[u])) for u in U}
    y = {(g, h): _rms(hh[g, h], ng_ref[:, h * LANES:(h + 1) * LANES]) for g, h in U}
    for g, h in U:
        o_r[g][rows, h * LANES:(h + 1) * LANES] = (
            y[g, h] * jax.nn.sigmoid(mo_r[g][rows, h * LANES:(h + 1) * LANES])).astype(BF16)
    b_last = {(g, h): xt[g][MLSTM_HEADS + h:MLSTM_HEADS + h + 1, L - 1:L] for g, h in U}
    g_col = {u: b_last[u] - b_col[u] + ig_col[u] for u in U}
    m_new = {u: jnp.maximum(b_last[u] + m_prev[u], jnp.max(g_col[u], axis=0, keepdims=True)) for u in U}
    kw = {u: km[u].astype(F32) * jnp.exp(g_col[u] - m_new[u]) for u in U}
    dec = {u: jnp.exp(b_last[u] + m_prev[u] - m_new[u]) for u in U}
    upd = {u: lax.dot_general(kw[u].astype(BF16), v[u], TN_DIMS, preferred_element_type=F32) for u in U}
    for g, h in U:
        m_r[g][h:h + 1, :] = jnp.broadcast_to(m_new[g, h], (1, LANES))
    for g, p in P:
        e, o = (g, 2 * p), (g, 2 * p + 1)
        ct_r[g][p] = ct_old[g, p] * jnp.where(row_c, dec[e], dec[o]) + upd[e] + upd[o]
        n_r[g][p:p + 1, :] = (n2[g, p] * jnp.where(lane_1, dec[e], dec[o])
                              + jnp.sum(kw[e] + kw[o], axis=0, keepdims=True))
    return carry


def _mlstm(mqk, mv, mo, gts, conv_w, gate_bias, norm_g, bsz, seq, chunks, group):
    n = mqk.shape[0]
    tm = chunks * MLSTM_CHUNK
    steps = seq // tm
    row = lambda b, i: (b, i, 0)
    const = lambda b, i: (0, 0)
    width = MLSTM_HEADS * MLSTM_V_DIM
    per_seq = lambda a: a.reshape(bsz, seq, a.shape[1])
    out = pl.pallas_call(
        functools.partial(_mlstm_kernel, chunks=chunks, group=group),
        grid=(bsz // group, steps),
        in_specs=[pl.BlockSpec((group, tm, width), row),
                  pl.BlockSpec((group, tm, width), row),
                  pl.BlockSpec((group, tm, width), row),
                  pl.BlockSpec((group, tm, LANES), row),
                  pl.BlockSpec((CONV_WIDTH, width), const),
                  pl.BlockSpec((1, LANES), const),
                  pl.BlockSpec((1, width), const)],
        out_specs=pl.BlockSpec((group, tm, width), row),
        out_shape=jax.ShapeDtypeStruct((bsz, seq, width), BF16),
        scratch_shapes=[pltpu.VMEM((group, SUBLANES, width), F32),
                        pltpu.VMEM((group, tm, width), BF16),
                        pltpu.VMEM((group, tm, LANES), F32),
                        pltpu.VMEM((group, MLSTM_HEADS // 2, LANES, LANES), F32),
                        pltpu.VMEM((group, SUBLANES, LANES), F32),
                        pltpu.VMEM((group, SUBLANES, LANES), F32)],
        compiler_params=_params(("arbitrary", "arbitrary")),
        name="mlstm",
    )(per_seq(mqk), per_seq(mv), per_seq(mo), per_seq(gts), conv_w, gate_bias, norm_g)
    return out.reshape(n, width)


def _out_proj_kernel(att_ref, mh_ref, x_ref, g1_ref, sc_ref, sh_ref, gpm_ref, gpf_ref, wa_ref, wb_ref,
                     x1_ref, h2_ref):
    mix = (jnp.dot(att_ref[...], wa_ref[...], preferred_element_type=F32)
           + jnp.dot(mh_ref[...], wb_ref[...], preferred_element_type=F32))
    x1 = x_ref[...] + g1_ref[0] * _rms(mix, gpm_ref[...])
    x1_ref[...] = x1
    h2_ref[...] = _rms(x1, gpf_ref[...]) * (1.0 + sc_ref[0]) + sh_ref[0]


def _out_proj(att, mh, x2, gate1, scale2, shift2, g_post_mix, g_pre_ffn, wa, wb, seq, tm):
    n, d = x2.shape
    per_b = seq // tm
    row = lambda i: (i, 0)
    bsel = lambda i: (i // per_b, 0, 0)
    const = lambda i: (0, 0)
    half = att.shape[1]
    return pl.pallas_call(
        _out_proj_kernel,
        grid=(n // tm,),
        in_specs=[pl.BlockSpec((tm, half), row), pl.BlockSpec((tm, half), row), pl.BlockSpec((tm, d), row),
                  pl.BlockSpec((1, 1, d), bsel), pl.BlockSpec((1, 1, d), bsel), pl.BlockSpec((1, 1, d), bsel),
                  pl.BlockSpec((1, d), const), pl.BlockSpec((1, d), const),
                  pl.BlockSpec((half, d), const), pl.BlockSpec((half, d), const)],
        out_specs=[pl.BlockSpec((tm, d), row)] * 2,
        out_shape=[jax.ShapeDtypeStruct((n, d), F32)] * 2,
        compiler_params=_params(("arbitrary",), VMEM_LIMIT),
        name="out_proj",
    )(att, mh, x2, gate1, scale2, shift2, g_post_mix, g_pre_ffn, wa, wb)


_BIG_ID = float(2 ** 30)
SORT_LEVELS = 8


def _top_scores(s, k):
    rows, t = s.shape
    span = SORT_LEVELS * SUBLANES
    r = lax.broadcasted_iota(jnp.int32, (rows // SORT_LEVELS, t), 0)
    col_id = ((r // SUBLANES) * span + r % SUBLANES).astype(F32)
    lev = [jnp.concatenate([s[g * span + l * SUBLANES:g * span + (l + 1) * SUBLANES] for g in range(rows // span)],
                           axis=0) for l in range(SORT_LEVELS)]
    ids = [col_id + float(l * SUBLANES) for l in range(SORT_LEVELS)]
    for rnd in range(SORT_LEVELS):
        for a in range(rnd % 2, SORT_LEVELS - 1, 2):
            swap = lev[a + 1] > lev[a]
            lev[a], lev[a + 1] = jnp.where(swap, lev[a + 1], lev[a]), jnp.where(swap, lev[a], lev[a + 1])
            ids[a], ids[a + 1] = jnp.where(swap, ids[a + 1], ids[a]), jnp.where(swap, ids[a], ids[a + 1])
    vals, sel = [], []
    for it in range(k):
        m = jnp.max(lev[0], axis=0, keepdims=True)
        i = jnp.min(jnp.where(lev[0] == m, ids[0], _BIG_ID), axis=0, keepdims=True)
        vals.append(m)
        sel.append(i)
        hit = ids[0] == i
        live = min(SORT_LEVELS, k - 1 - it)
        for l in range(min(live, SORT_LEVELS - 1)):
            lev[l] = jnp.where(hit, lev[l + 1], lev[l])
            ids[l] = jnp.where(hit, ids[l + 1], ids[l])
        if live == SORT_LEVELS:
            lev[-1] = jnp.where(hit, NEG_INF, lev[-1])
    return jnp.concatenate(vals, axis=0), jnp.concatenate(sel, axis=0).astype(jnp.int32)


def _top_pair_sums(v1, v2):
    k, t = v1.shape
    half = SUBLANES
    lev = [v1[0:half] + v2[b:b + 1, :] for b in range(k)]
    side = v1[half:k] + v2[0:1, :]
    a_low = lax.broadcasted_iota(jnp.int32, (half, t), 0).astype(F32) * float(k)
    side_id = a_low + float(half * k)
    depth = jnp.zeros((half, t), F32)
    vals, sel = [], []
    for it in range(k):
        top_id = a_low + depth
        m = jnp.max(jnp.maximum(lev[0], side), axis=0, keepdims=True)
        i = jnp.min(jnp.minimum(jnp.where(lev[0] == m, top_id, _BIG_ID), jnp.where(side == m, side_id, _BIG_ID)),
                    axis=0, keepdims=True)
        vals.append(m)
        sel.append(i)
        hit = top_id == i
        for l in range(k - 1 - it):
            lev[l] = jnp.where(hit, lev[l + 1], lev[l])
        side = jnp.where(side_id == i, NEG_INF, side)
        depth = depth + jnp.where(hit, 1.0, 0.0)
    return jnp.concatenate(vals, axis=0), jnp.concatenate(sel, axis=0).astype(jnp.int32)


def _pick_rows(table, which):
    r = lax.broadcasted_iota(jnp.int32, table.shape, 0)
    rows = []
    for k in range(which.shape[0]):
        rows.append(jnp.sum(jnp.where(r == which[k:k + 1, :], table, 0), axis=0, keepdims=True))
    return jnp.concatenate(rows, axis=0)


def _peer_sel_kernel(h_ref, wq_ref, k1_ref, k2_ref, e_ref, g_ref, r_ref):
    tq = h_ref.shape[0]
    K = PEER_TOPK
    q = jnp.dot(h_ref[...].astype(BF16), wq_ref[...], preferred_element_type=F32).astype(BF16)
    for hd in range(PEER_HEADS):
        base = hd * 2 * PEER_HALF
        s1 = lax.dot_general(k1_ref[hd], q[:, base:base + PEER_HALF], NT_DIMS, preferred_element_type=F32)
        s2 = lax.dot_general(k2_ref[hd], q[:, base + PEER_HALF:base + 2 * PEER_HALF], NT_DIMS,
                             preferred_element_type=F32)
        v1, i1 = _top_scores(s1, K)
        v2, i2 = _top_scores(s2, K)
        top, pos = _top_pair_sums(v1, v2)
        eid = _pick_rows(i1, pos >> 4) * PEER_KEYS + _pick_rows(i2, pos & (K - 1))
        ex = jnp.exp(top - top[0:1, :])
        e_ref[hd * K:(hd + 1) * K, :] = eid * ROW_WORDS
        g_ref[hd * K:(hd + 1) * K, :] = ex / jnp.sum(ex, axis=0, keepdims=True)
    pair = (e_ref[...] >> 1).T
    for h in range(STREAM_HALVES):
        r_ref[pl.ds(h, tq, stride=STREAM_HALVES), :] = pair + h


def _peer_sel(h2, wq, k1, k2, tq, tg):
    n, d = h2.shape
    rows = PEER_HEADS * PEER_TOPK
    assert rows == LANES and tg % tq == 0
    return pl.pallas_call(
        _peer_sel_kernel,
        grid=(n // tq,),
        in_specs=[pl.BlockSpec((tq, d), lambda i: (i, 0)),
                  pl.BlockSpec(wq.shape, lambda i: (0, 0)),
                  pl.BlockSpec(k1.shape, lambda i: (0, 0, 0)),
                  pl.BlockSpec(k2.shape, lambda i: (0, 0, 0))],
        out_specs=[pl.BlockSpec((rows, tq), lambda i: (0, i))] * 2
        + [pl.BlockSpec((tq * STREAM_HALVES, rows), lambda i: (i * tq // tg, 0))],
        out_shape=[jax.ShapeDtypeStruct((rows, n), jnp.int32), jax.ShapeDtypeStruct((rows, n), F32),
                   jax.ShapeDtypeStruct((n // tg * tq * STREAM_HALVES, rows), jnp.int32)],
        compiler_params=_params(("arbitrary",), VMEM_LIMIT),
        name="peer_sel",
    )(h2, wq, k1, k2)


def _split_bf16(x):
    hi = x.astype(BF16)
    return hi, (x - hi.astype(F32)).astype(BF16)


def _pack_kernel(w_ref, o_ref, *wide_ref):
    x = w_ref[...]
    eb = x.shape[0]
    for r in range(ROW_WORDS):
        lo = x[:, 2 * r * LANES:(2 * r + 1) * LANES].astype(BF16).astype(F32)
        hi = x[:, (2 * r + 1) * LANES:(2 * r + 2) * LANES].astype(BF16).astype(F32)
        word = (lax.shift_right_logical(pltpu.bitcast(lo, jnp.int32), 16)
                | (pltpu.bitcast(hi, jnp.int32) & jnp.int32(-65536)))
        o_ref[pl.ds(r, eb, stride=ROW_WORDS), :] = word
    per_row = STREAM_ROW // LANES
    for ref in wide_ref:
        for c in range(per_row):
            ref[:, c * LANES:(c + 1) * LANES] = o_ref[pl.ds(c, eb * STREAM_HALVES, stride=per_row), :]


def _pack_table(w, wide_copy=False):
    e, d = w.shape
    eb = 512
    shapes = [(ROW_WORDS, LANES)] + [(STREAM_HALVES, STREAM_ROW)] * wide_copy
    return pl.pallas_call(
        _pack_kernel,
        grid=(e // eb,),
        in_specs=[pl.BlockSpec((eb, d), lambda i: (i, 0))],
        out_specs=[pl.BlockSpec((eb * rows, width), lambda i: (i, 0)) for rows, width in shapes],
        out_shape=[jax.ShapeDtypeStruct((e * rows, width), jnp.int32) for rows, width in shapes],
        compiler_params=_params(("arbitrary",)),
        name="pack_table",
    )(w)


def _gather_rows(slots, tab_ref, t, stage_ref):
    for j, slot in enumerate(slots):
        src = pl.ds(pl.multiple_of(slot[t], ROW_WORDS), ROW_WORDS)
        stage_ref[j * ROW_WORDS:(j + 1) * ROW_WORDS, :] = tab_ref[src, :]


def _staged_bf16(stage_ref):
    return pltpu.bitcast(stage_ref[...], BF16)


def _pipelined_tokens(ntok, gather, compute, stages, on_trip=None, after=None):
    nb = len(stages)
    for k in range(nb):
        gather(k, stages[k])

    def trip(i, carry):
        t = nb * i
        if on_trip is not None:
            on_trip(i)
        for k in range(nb):
            compute(t + k, stages[k])
            ahead = t + k + nb
            gather(jnp.minimum(ahead, ntok - 1), stages[k])
            if after is not None:
                after(i, k)
        return carry

    lax.fori_loop(0, ntok // nb, trip, 0)


def _with_slot_indices(idx_hbm, sems, bufs, stride, count, run):
    nsel = len(bufs) // 2
    step = pl.program_id(0)
    last = pl.num_programs(0) - 1

    def copies(block, which):
        return [pltpu.make_async_copy(idx_hbm.at[j, pl.ds(block * stride, count)], bufs[which * nsel + j],
                                      sems.at[which])
                for j in range(nsel)]

    @pl.when(step == 0)
    def _():
        for cp in copies(0, 0):
            cp.start()

    def phase(which):
        for cp in copies(step, which):
            cp.wait()

        @pl.when(step < last)
        def _():
            for cp in copies(step + 1, 1 - which):
                cp.start()

        run(bufs[which * nsel:(which + 1) * nsel])

    for which in range(2):
        pl.when(step % 2 == which)(functools.partial(phase, which))


SC_WINDOW = 128


def _sc_gather_rows(tab, row_ids):
    windows, width = row_ids.shape
    assert width == SC_WINDOW
    n = windows * SC_WINDOW
    mesh = plsc.VectorSubcoreMesh(core_axis_name="core", subcore_axis_name="subcore")

    @pl.kernel(out_type=jax.ShapeDtypeStruct((n, tab.shape[1]), tab.dtype), mesh=mesh)
    def gather(tab_hbm, ids_hbm, out_hbm):
        def body(ids_vmem, out_vmem):
            pltpu.sync_copy(tab_hbm.at[ids_vmem.at[0]], out_vmem)

        pltpu.emit_pipeline(
            body,
            grid=(windows,),
            in_specs=[pl.BlockSpec((1, SC_WINDOW), index_map=lambda i: (i, 0))],
            out_specs=[pl.BlockSpec((SC_WINDOW, tab.shape[1]), index_map=lambda i: (i, 0))],
            core_axis_name=("core", "subcore"),
            dimension_semantics=(pltpu.PARALLEL,),
            trace_scopes=False,
        )(ids_hbm, out_hbm)

    return gather(tab, row_ids)


def _diag_mask(nsel):
    shape = (SUBLANES, nsel * SUBLANES)
    return (lax.broadcasted_iota(jnp.int32, shape, 1) % SUBLANES) == lax.broadcasted_iota(jnp.int32, shape, 0)


def _token_tile(ref, t):
    row = ref[pl.ds(t, 1), :]
    return jnp.concatenate([row[:, s * LANES:(s + 1) * LANES] for s in range(SUBLANES)], axis=0)


def _peer_u_kernel(idx_hbm, h_ref, gate_ref, grp_ref, tab_ref, act_ref, part_ref, *scratch):
    nsel, tq = gate_ref.shape
    stages, sems, bufs = scratch[:GATHER_STAGES], scratch[GATHER_STAGES], scratch[GATHER_STAGES + 1:]
    diag = _diag_mask(nsel)

    def compute(t, stage_ref):
        h_hi, h_lo = _split_bf16(_token_tile(h_ref, t))
        both = lax.dot_general(jnp.concatenate([h_hi, h_lo], axis=0), _staged_bf16(stage_ref), NT_DIMS,
                               preferred_element_type=F32)
        prod = both[0:SUBLANES] + both[SUBLANES:2 * SUBLANES]
        part_ref[pl.ds(t, 1), :] = jnp.sum(jnp.where(diag, prod, 0.0), axis=0, keepdims=True)

    def run(slots):
        _pipelined_tokens(tq, functools.partial(_gather_rows, slots, tab_ref), compute, stages)

    _with_slot_indices(idx_hbm, sems, bufs, tq, tq, run)
    p_hi, p_lo = _split_bf16(part_ref[...])
    grp = grp_ref[...]
    pre = jnp.dot(p_hi, grp, preferred_element_type=F32) + jnp.dot(p_lo, grp, preferred_element_type=F32)
    act_ref[...] = 0.5 * pre * (1.0 + lax.erf(pre * (2.0 ** -0.5))) * gate_ref[...].T


def _group_matrix(nsel):
    r = jnp.arange(nsel * SUBLANES)[:, None] // SUBLANES
    return (r == jnp.arange(nsel)[None, :]).astype(BF16)


def _gather_scratch(nsel, ntok, nstages):
    return ([pltpu.VMEM((nsel * ROW_WORDS, LANES), jnp.int32)] * nstages
            + [pltpu.SemaphoreType.DMA((2,))] + [pltpu.SMEM((ntok,), jnp.int32)] * (2 * nsel))


def _peer_u(idx_t, h2, gate_t, tab, tq):
    nsel, n = idx_t.shape
    d = h2.shape[1]
    wide = nsel * SUBLANES
    return pl.pallas_call(
        _peer_u_kernel,
        grid=(n // tq,),
        in_specs=[pl.BlockSpec(memory_space=pl.ANY),
                  pl.BlockSpec((tq, d), lambda i: (i, 0)),
                  pl.BlockSpec((nsel, tq), lambda i: (0, i)),
                  pl.BlockSpec((wide, nsel), lambda i: (0, 0)),
                  pl.BlockSpec(memory_space=pltpu.VMEM)],
        out_specs=pl.BlockSpec((tq, nsel), lambda i: (i, 0)),
        out_shape=jax.ShapeDtypeStruct((n, nsel), F32),
        scratch_shapes=[pltpu.VMEM((tq, wide), F32)] + _gather_scratch(nsel, tq, GATHER_STAGES),
        compiler_params=_params(("arbitrary",), GATHER_VMEM_LIMIT),
        name="peer_u",
    )(idx_t, h2, gate_t, _group_matrix(nsel), tab)


def _lane_aligned(count):
    return -(-count // LANES) * LANES


def _streamed_rows_bf16(words):
    x = pltpu.bitcast(words, BF16)
    return jnp.concatenate([x[:, 0:LANES], x[:, LANES:2 * LANES]], axis=0)


def _streamed_copy_matrix(nsel):
    r = jnp.arange(nsel * SUBLANES)
    return ((r[None, :] % (2 * nsel)) // 2 == jnp.arange(nsel)[:, None]).astype(BF16)


def _streamed_diag(nsel):
    shape = (SUBLANES, nsel * SUBLANES)
    r = lax.broadcasted_iota(jnp.int32, shape, 1)
    sub = 4 * ((r % (4 * nsel)) // (2 * nsel)) + 2 * (r // (4 * nsel)) + r % 2
    return lax.broadcasted_iota(jnp.int32, shape, 0) == sub


def _peer_v_kernel(idx_hbm, act_ref, rep_ref, rep2_ref, x1_ref, g2_ref, gpf_ref, tab_ref, rows_hbm, o_ref,
                   wide_ref, y_ref, rowbuf, rsem, *scratch, nvld, nstages):
    tq, nsel = act_ref.shape
    stages, sems, bufs = scratch[:nstages], scratch[nstages], scratch[nstages + 1:]
    trips = nvld // nstages
    per_trip = (tq - nvld) // trips
    rows_tok = nsel * STREAM_HALVES
    step = pl.program_id(0)
    chunks = pl.num_programs(0) * trips
    diag = _diag_mask(nsel)
    diag2 = _streamed_diag(nsel)
    a_hi, a_lo = _split_bf16(act_ref[...])
    for lo, hi, rep in ((0, nvld, rep_ref[...]), (nvld, tq, rep2_ref[...])):
        wide_ref[lo:hi, :] = (jnp.dot(a_hi[lo:hi], rep, preferred_element_type=F32)
                              + jnp.dot(a_lo[lo:hi], rep, preferred_element_type=F32))

    def finish(t, mask, rows_bf16):
        w = jnp.where(mask, jnp.broadcast_to(wide_ref[pl.ds(t, 1), :], mask.shape), 0.0)
        w_hi, w_lo = _split_bf16(w)
        both = jnp.dot(jnp.concatenate([w_hi, w_lo], axis=0), rows_bf16, preferred_element_type=F32)
        tile = both[0:SUBLANES] + both[SUBLANES:2 * SUBLANES]
        y_ref[pl.ds(t, 1), :] = jnp.concatenate([tile[s:s + 1, :] for s in range(SUBLANES)], axis=1)

    def compute(t, stage_ref):
        finish(t, diag, _staged_bf16(stage_ref))

    def chunk_copy(g, slot):
        return pltpu.make_async_copy(rows_hbm.at[pl.ds(g * (per_trip * rows_tok), per_trip * rows_tok), :],
                                     rowbuf.at[slot], rsem.at[slot])

    ring = STREAM_RING

    @pl.when(step == 0)
    def _():
        for g0 in range(ring - 1):
            chunk_copy(g0, g0).start()

    def on_trip(i):
        g = step * trips + i
        chunk_copy(g, i % ring).wait()

        @pl.when(g + ring - 1 < chunks)
        def _():
            chunk_copy(g + ring - 1, (i + ring - 1) % ring).start()

    places = [(d * nstages) // per_trip for d in range(per_trip)]

    def after(i, k):
        for d in range(per_trip):
            if places[d] == k:
                rows = rowbuf.at[i % ring, pl.ds(d * rows_tok, rows_tok), :]
                finish(nvld + i * per_trip + d, diag2, _streamed_rows_bf16(rows[...]))

    def run(slots):
        _pipelined_tokens(nvld, functools.partial(_gather_rows, slots, tab_ref), compute, stages, on_trip, after)

    _with_slot_indices(idx_hbm, sems, bufs, tq, _lane_aligned(nvld), run)
    o_ref[...] = x1_ref[...] + g2_ref[0] * _rms(y_ref[...], gpf_ref[...])


def _peer_v(idx_t, act, x1, gate2, g_post_ffn, tab, rows, seq, tq, nvld, nstages):
    nsel, n = idx_t.shape
    d = x1.shape[1]
    wide = nsel * SUBLANES
    per_b = seq // tq
    row = lambda i: (i, 0)
    const = lambda i: (0, 0)
    trips = nvld // nstages
    per_trip = (tq - nvld) // trips
    assert trips % STREAM_RING == 0 and trips * nstages == nvld and trips * per_trip == tq - nvld
    return pl.pallas_call(
        functools.partial(_peer_v_kernel, nvld=nvld, nstages=nstages),
        grid=(n // tq,),
        in_specs=[pl.BlockSpec(memory_space=pl.ANY),
                  pl.BlockSpec((tq, nsel), row),
                  pl.BlockSpec((nsel, wide), const),
                  pl.BlockSpec((nsel, wide), const),
                  pl.BlockSpec((tq, d), row),
                  pl.BlockSpec((1, 1, d), lambda i: (i // per_b, 0, 0)),
                  pl.BlockSpec((1, d), const),
                  pl.BlockSpec(memory_space=pltpu.VMEM),
                  pl.BlockSpec(memory_space=pl.ANY)],
        out_specs=pl.BlockSpec((tq, d), row),
        out_shape=jax.ShapeDtypeStruct((n, d), F32),
        scratch_shapes=[pltpu.VMEM((tq, wide), F32), pltpu.VMEM((tq, d), F32),
                        pltpu.VMEM((STREAM_RING, per_trip * nsel * STREAM_HALVES, STREAM_ROW), jnp.int32),
                        pltpu.SemaphoreType.DMA((STREAM_RING,))]
        + _gather_scratch(nsel, _lane_aligned(nvld), nstages),
        compiler_params=_params(("arbitrary",), GATHER_VMEM_LIMIT),
        name="peer_v",
    )(idx_t, act, _group_matrix(nsel).T, _streamed_copy_matrix(nsel), x1, gate2, g_post_ffn, tab, rows)


def _dup_heads(w, heads, dh):
    d = w.shape[0]
    return jnp.repeat(w.reshape(d, heads, 1, dh), 2, axis=2).reshape(d, heads * 2 * dh)


def _layer(x2, c, pos_col, bsz, seq, w_mod, b_mod, g_pre_mix, g_post_mix, w_in, conv_w, b_igate, b_fgate,
           mlstm_norm_g, att_sinks, w_out, g_pre_ffn, g_post_ffn, peer_wq, peer_keys1, peer_keys2, peer_u, peer_v):
    n, d = x2.shape
    tm = min(seq, 512)
    mod = _mod(c, w_mod, b_mod)
    shift1, scale1, gate1, shift2, scale2, gate2 = [m.reshape(bsz, 1, d) for m in jnp.split(mod, 6, axis=-1)]

    aw = ATT_HEADS * ATT_HEAD_DIM
    kvw = ATT_KV_HEADS * ATT_HEAD_DIM
    qkw = MLSTM_HEADS * MLSTM_QK_DIM
    mw = MLSTM_HEADS * MLSTM_V_DIM
    o = 0
    wq_a = w_in[:, o:o + aw]; o += aw
    wk_a = w_in[:, o:o + kvw]; o += kvw
    wv_a = w_in[:, o:o + kvw]; o += kvw
    w_mqk = w_in[:, o:o + 2 * qkw]; o += 2 * qkw
    w_mv = w_in[:, o:o + mw]; o += mw
    w_g = w_in[:, o:o + 2 * MLSTM_HEADS]; o += 2 * MLSTM_HEADS
    w_mo = w_in[:, o:o + mw]
    w_gp = jnp.pad(w_g, ((0, 0), (0, LANES - 2 * MLSTM_HEADS)))
    w_all = jnp.concatenate([wq_a, _dup_heads(wk_a, ATT_KV_HEADS, ATT_HEAD_DIM),
                             _dup_heads(wv_a, ATT_KV_HEADS, ATT_HEAD_DIM), w_mqk, w_mv, w_mo, w_gp],
                            axis=1).astype(BF16)

    cos, sin = _rope_tab(pos_col)
    q, kd, vd, mqk, mv, mo, gts = _in_proj(x2, scale1, shift1, g_pre_mix.reshape(1, d), cos, sin, w_all, seq, tm)
    att = _swa(att_sinks, q, kd, vd, bsz, seq)
    gate_bias = jnp.pad(jnp.concatenate([b_igate, b_fgate]), (0, LANES - 2 * MLSTM_HEADS)).reshape(1, LANES)
    chunks = min(seq // MLSTM_CHUNK, 8)
    mh = _mlstm(mqk, mv, mo, gts, conv_w, gate_bias, mlstm_norm_g.reshape(1, mw), bsz, seq, chunks,
                MLSTM_GROUP if bsz % MLSTM_GROUP == 0 else 1)
    wo = w_out.astype(BF16)
    x1, h2 = _out_proj(att, mh, x2, gate1, scale2, shift2, g_post_mix.reshape(1, d), g_pre_ffn.reshape(1, d),
                       wo[:aw], wo[aw:], seq, tm)

    tab_u, = _pack_table(peer_u)
    tab_v, tab_v_wide = _pack_table(peer_v, wide_copy=True)
    tg = min(seq, GATHER_TOKENS)
    nvld = tg * STREAM_SPLIT[0] // sum(STREAM_SPLIT)
    eid_t, gate_t, row_ids = _peer_sel(h2, peer_wq.astype(BF16), peer_keys1.astype(BF16),
                                       peer_keys2.astype(BF16), tg - nvld, tg)
    rows_v = _sc_gather_rows(tab_v_wide, row_ids)
    act = _peer_u(eid_t, h2, gate_t, tab_u, tg)
    return _peer_v(eid_t, act, x1, gate2, g_post_ffn.reshape(1, d), tab_v, rows_v, seq, tg, nvld, STREAM_STAGES)


def kernel(x, c, positions, w_mod, b_mod, g_pre_mix, g_post_mix, w_in, conv_w, b_igate, b_fgate, mlstm_norm_g, att_sinks, w_out, g_pre_ffn, g_post_ffn, peer_wq, peer_keys1, peer_keys2, peer_u, peer_v):
    bsz, seq, d = x.shape
    n = bsz * seq
    x2 = x.reshape(n, d)
    pos_col = positions.reshape(n, 1)
    for l in range(w_mod.shape[0]):
        x2 = _layer(x2, c, pos_col, bsz, seq, w_mod[l], b_mod[l], g_pre_mix[l], g_post_mix[l], w_in[l], conv_w[l],
                    b_igate[l], b_fgate[l], mlstm_norm_g[l], att_sinks[l], w_out[l], g_pre_ffn[l], g_post_ffn[l],
                    peer_wq[l], peer_keys1[l], peer_keys2[l], peer_u[l], peer_v[l])
    return x2.reshape(bsz, seq, d)
```

```python
import functools

import jax
import jax.numpy as jnp
from jax import lax
from jax.experimental import pallas as pl
from jax.experimental.pallas import tpu as pltpu
from jax.experimental.pallas import tpu_sc as plsc

F32 = jnp.float32
BF16 = jnp.bfloat16

ATT_HEADS = 8
ATT_KV_HEADS = 2
ATT_HEAD_DIM = 64
ATT_BLOCK = 128
ROPE_THETA = 10000.0
MLSTM_HEADS = 4
MLSTM_V_DIM = 128
MLSTM_QK_DIM = 64
MLSTM_CHUNK = 64
CONV_WIDTH = 4
PEER_HEADS = 8
PEER_KEYS = 128
PEER_HALF = 128
PEER_TOPK = 16
NORM_EPS = 1e-6

LANES = 128
SUBLANES = 8
VMEM_LIMIT = 52 * 1024 * 1024
GATHER_STAGES = 32
ROW_WORDS = SUBLANES // 2
SWA_BLOCKS = 4
MLSTM_GROUP = 4
STREAM_SPLIT = (4, 4)
STREAM_RING = 4
STREAM_STAGES = 8
STREAM_HALVES = 2
STREAM_ROW = 2 * LANES
GATHER_TOKENS = 512
GATHER_VMEM_LIMIT = 58 * 1024 * 1024

NEG_INF = float("-inf")
NT_DIMS = (((1,), (1,)), ((), ()))
TN_DIMS = (((0,), (0,)), ((), ()))


def _params(sem, vmem=None):
    return pltpu.CompilerParams(dimension_semantics=sem, vmem_limit_bytes=vmem)


def _rms(x, g):
    return x * lax.rsqrt(jnp.mean(x * x, axis=-1, keepdims=True) + NORM_EPS) * g


def _mod_kernel(c_ref, w_ref, b_ref, o_ref):
    o_ref[...] = jnp.dot(c_ref[...], w_ref[...], preferred_element_type=F32,
                         precision=lax.Precision.HIGHEST) + b_ref[...]


def _mod(c, w, b):
    bsz, d = c.shape
    nout = w.shape[1]
    return pl.pallas_call(
        _mod_kernel,
        grid=(nout // d,),
        in_specs=[pl.BlockSpec((bsz, d), lambda i: (0, 0)),
                  pl.BlockSpec((d, d), lambda i: (0, i)),
                  pl.BlockSpec((1, d), lambda i: (0, i))],
        out_specs=pl.BlockSpec((bsz, d), lambda i: (0, i)),
        out_shape=jax.ShapeDtypeStruct((bsz, nout), F32),
        compiler_params=_params(("arbitrary",)),
        name="mod",
    )(c, w, b.reshape(1, nout))


def _rope_tab_kernel(pos_ref, inv_ref, sign_ref, cos_ref, sin_ref):
    ang = pos_ref[...].astype(F32) * inv_ref[...]
    cos_ref[...] = jnp.cos(ang)
    sin_ref[...] = jnp.sin(ang) * sign_ref[...]


def _rope_tab(pos_col):
    n = pos_col.shape[0]
    tr = min(n, 1024)
    half = ATT_HEAD_DIM // 2
    inv = ROPE_THETA ** (-jnp.arange(0, ATT_HEAD_DIM, 2, dtype=F32) / ATT_HEAD_DIM)
    inv_row = jnp.tile(inv, LANES // half).reshape(1, LANES)
    lane = jnp.arange(LANES)
    sign_row = jnp.where((lane % ATT_HEAD_DIM) < half, -1.0, 1.0).astype(F32).reshape(1, LANES)
    return pl.pallas_call(
        _rope_tab_kernel,
        grid=(n // tr,),
        in_specs=[pl.BlockSpec((tr, 1), lambda i: (i, 0)),
                  pl.BlockSpec((1, LANES), lambda i: (0, 0)),
                  pl.BlockSpec((1, LANES), lambda i: (0, 0))],
        out_specs=[pl.BlockSpec((tr, LANES), lambda i: (i, 0))] * 2,
        out_shape=[jax.ShapeDtypeStruct((n, LANES), F32)] * 2,
        compiler_params=_params(("arbitrary",)),
        name="rope_tab",
    )(pos_col, inv_row, sign_row)


def _rope(v, cos, sin):
    half = ATT_HEAD_DIM // 2
    lane = lax.broadcasted_iota(jnp.int32, cos.shape, 1)
    first = (lane % ATT_HEAD_DIM) < half
    outs = []
    for j in range(v.shape[1] // LANES):
        c = v[:, j * LANES:(j + 1) * LANES]
        rot = jnp.where(first, pltpu.roll(c, LANES - half, 1), pltpu.roll(c, half, 1))
        outs.append(c * cos + rot * sin)
    return jnp.concatenate(outs, axis=1)


_C_Q, _C_K, _C_V, _C_MQK, _C_MV, _C_MO, _C_G, _C_END = 0, 512, 768, 1024, 1536, 2048, 2560, 2688


def _in_proj_kernel(x_ref, sc_ref, sh_ref, g_ref, cos_ref, sin_ref, w_ref,
                    q_ref, k_ref, v_ref, mqk_ref, mv_ref, mo_ref, gt_ref):
    x = x_ref[...]
    h = _rms(x, g_ref[...]) * (1.0 + sc_ref[0]) + sh_ref[0]
    hb = h.astype(BF16)

    def mm(a, b):
        return jnp.dot(hb, w_ref[:, a:b], preferred_element_type=F32)

    cos = cos_ref[...]
    sin = sin_ref[...]
    q_ref[...] = (_rope(mm(_C_Q, _C_K), cos, sin) * (ATT_HEAD_DIM ** -0.5)).astype(BF16)
    k_ref[...] = _rope(mm(_C_K, _C_V), cos, sin).astype(BF16)
    v_ref[...] = mm(_C_V, _C_MQK).astype(BF16)
    mqk_ref[...] = mm(_C_MQK, _C_MV)
    mv_ref[...] = mm(_C_MV, _C_MO).astype(BF16)
    mo_ref[...] = mm(_C_MO, _C_G)
    gt_ref[...] = mm(_C_G, _C_END)


def _in_proj(x2, scale1, shift1, g_pre, cos, sin, w_all, seq, tm):
    n, d = x2.shape
    per_b = seq // tm
    row = lambda i: (i, 0)
    bsel = lambda i: (i // per_b, 0, 0)
    widths = (512, 256, 256, 512, 512, 512, 128)
    dtypes = (BF16, BF16, BF16, F32, BF16, F32, F32)
    return pl.pallas_call(
        _in_proj_kernel,
        grid=(n // tm,),
        in_specs=[pl.BlockSpec((tm, d), row),
                  pl.BlockSpec((1, 1, d), bsel),
                  pl.BlockSpec((1, 1, d), bsel),
                  pl.BlockSpec((1, d), lambda i: (0, 0)),
                  pl.BlockSpec((tm, LANES), row),
                  pl.BlockSpec((tm, LANES), row),
                  pl.BlockSpec((d, _C_END), lambda i: (0, 0))],
        out_specs=[pl.BlockSpec((tm, w), row) for w in widths],
        out_shape=[jax.ShapeDtypeStruct((n, w), dt) for w, dt in zip(widths, dtypes)],
        compiler_params=_params(("arbitrary",), VMEM_LIMIT),
        name="in_proj",
    )(x2, scale1, shift1, g_pre, cos, sin, w_all)


def _swa_kernel(sink_ref, q_ref, kp_ref, kc_ref, vp_ref, vc_ref, o_ref):
    blk = ATT_BLOCK
    n = pl.program_id(1)
    qi = lax.broadcasted_iota(jnp.int32, (blk, 2 * blk), 0)
    si = lax.broadcasted_iota(jnp.int32, (blk, 2 * blk), 1)
    delta = qi + blk - si
    window = (delta >= 0) & (delta < blk)
    lo = lax.broadcasted_iota(jnp.int32, (2 * blk, LANES), 1) < ATT_HEAD_DIM
    group = ATT_HEADS // ATT_KV_HEADS
    kv = {}
    for j in range(SWA_BLOCKS):
        rows = slice(j * blk, (j + 1) * blk)
        before = slice((j - 1) * blk, j * blk)
        for g in range(ATT_KV_HEADS):
            cs = slice(g * LANES, (g + 1) * LANES)
            k = jnp.concatenate([kp_ref[:, cs] if j == 0 else kc_ref[before, cs], kc_ref[rows, cs]], axis=0)
            v = jnp.concatenate([vp_ref[:, cs] if j == 0 else vc_ref[before, cs], vc_ref[rows, cs]], axis=0)
            zero = jnp.zeros_like(k)
            kv[j, g] = ((jnp.where(lo, k, zero), jnp.where(lo, v, zero)),
                        (jnp.where(lo, zero, k), jnp.where(lo, zero, v)))
    units = [(j, h) for j in range(SWA_BLOCKS) for h in range(ATT_HEADS)]
    valid = [window & ((si >= blk) | (n > 0)) if j == 0 else window for j in range(SWA_BLOCKS)]
    sinks = [sink_ref[h] for h in range(ATT_HEADS)]
    q = {(j, h): q_ref[j * blk:(j + 1) * blk, (h // 2) * LANES:(h // 2 + 1) * LANES] for j, h in units}
    scores = {(j, h): jnp.where(valid[j], lax.dot_general(q[j, h], kv[j, h // group][h % 2][0], NT_DIMS,
                                                          preferred_element_type=F32), NEG_INF) for j, h in units}
    tops = {(j, h): jnp.maximum(jnp.max(scores[j, h], axis=-1, keepdims=True), sinks[h]) for j, h in units}
    exps = {u: jnp.exp(scores[u] - tops[u]) for u in units}
    dens = {(j, h): jnp.sum(exps[j, h], axis=-1, keepdims=True) + jnp.exp(sinks[h] - tops[j, h]) for j, h in units}
    outs = {(j, h): jnp.dot((exps[j, h] / dens[j, h]).astype(BF16), kv[j, h // group][h % 2][1],
                            preferred_element_type=F32) for j, h in units}
    for j in range(SWA_BLOCKS):
        for p in range(ATT_HEADS // 2):
            o_ref[j * blk:(j + 1) * blk, p * LANES:(p + 1) * LANES] = (outs[j, 2 * p] + outs[j, 2 * p + 1]).astype(BF16)


def _swa(sinks, q, kd, vd, bsz, seq):
    n = q.shape[0]
    nb = seq // ATT_BLOCK
    steps, partial_step = divmod(nb, SWA_BLOCKS)
    assert partial_step == 0
    rows = SWA_BLOCKS * ATT_BLOCK
    cur = lambda b, i: (b * steps + i, 0)
    prev = lambda b, i: (b * nb + jnp.maximum(i * SWA_BLOCKS - 1, 0), 0)
    return pl.pallas_call(
        _swa_kernel,
        grid=(bsz, steps),
        in_specs=[pl.BlockSpec(memory_space=pltpu.SMEM),
                  pl.BlockSpec((rows, 512), cur),
                  pl.BlockSpec((ATT_BLOCK, 256), prev),
                  pl.BlockSpec((rows, 256), cur),
                  pl.BlockSpec((ATT_BLOCK, 256), prev),
                  pl.BlockSpec((rows, 256), cur)],
        out_specs=pl.BlockSpec((rows, 512), cur),
        out_shape=jax.ShapeDtypeStruct((n, 512), BF16),
        compiler_params=_params(("arbitrary", "arbitrary")),
        name="swa",
    )(sinks, q, kd, kd, vd, vd)


def _mlstm_kernel(mqk_all, mv_all, mo_all, gt_all, cw_ref, gb_ref, ng_ref, o_all,
                  tail_all, qk_all, xs_all, ct_all, n_all, m_all, *, chunks, group):
    @pl.when(pl.program_id(1) == 0)
    def _():
        for ref in (tail_all, ct_all, n_all, m_all):
            ref[...] = jnp.zeros_like(ref)

    for g in range(group):
        _mlstm_prepare(mqk_all.at[g], gt_all.at[g], tail_all.at[g], qk_all.at[g], xs_all.at[g], cw_ref, gb_ref,
                       chunks)
    seqs = [tuple(r.at[g] for r in (mv_all, mo_all, o_all, qk_all, xs_all, ct_all, n_all, m_all))
            for g in range(group)]
    lax.fori_loop(0, chunks, lambda c, carry: _mlstm_chunk(c, seqs, ng_ref, carry), 0)


def _mlstm_prepare(mqk_ref, gt_ref, tail_ref, qk_ref, xs_ref, cw_ref, gb_ref, chunks):
    tm = chunks * MLSTM_CHUNK
    nqk = MLSTM_HEADS * MLSTM_QK_DIM

    cur = mqk_ref[...]
    full = jnp.concatenate([tail_ref[...], cur], axis=0)
    off = SUBLANES - (CONV_WIDTH - 1)
    acc = full[off:off + tm] * cw_ref[0:1, :]
    for j in range(1, CONV_WIDTH):
        acc = acc + full[off + j:off + j + tm] * cw_ref[j:j + 1, :]
    act = acc * jax.nn.sigmoid(acc)
    col = lax.broadcasted_iota(jnp.int32, (1, 2 * nqk), 1)
    act = act * jnp.where(col < nqk, MLSTM_QK_DIM ** -0.5, 1.0)
    qk_ref[...] = act.astype(BF16)
    tail_ref[...] = cur[tm - SUBLANES:tm]

    lane = lax.broadcasted_iota(jnp.int32, (tm, LANES), 1)
    gts = gt_ref[...] + gb_ref[...]
    logsig = jnp.minimum(gts, 0.0) - jnp.log(1.0 + jnp.exp(-jnp.abs(gts)))
    xs_ref[...] = jnp.where(lane < MLSTM_HEADS, gts, jnp.where(lane < 2 * MLSTM_HEADS, logsig, 0.0))


def _mlstm_chunk(c, seqs, ng_ref, carry):
    L = MLSTM_CHUNK
    nqk = MLSTM_HEADS * MLSTM_QK_DIM
    ri = lax.broadcasted_iota(jnp.int32, (L, L), 0)
    ci = lax.broadcasted_iota(jnp.int32, (L, L), 1)
    causal = ci <= ri
    tril = causal.astype(F32)
    lane_l = lax.broadcasted_iota(jnp.int32, (L, LANES), 1)
    lo_l = lane_l < MLSTM_QK_DIM
    row_c = lax.broadcasted_iota(jnp.int32, (LANES, 1), 0) < MLSTM_QK_DIM
    lane_1 = lax.broadcasted_iota(jnp.int32, (1, LANES), 1) < MLSTM_QK_DIM
    rows = pl.ds(pl.multiple_of(c * L, L), L)
    G = range(len(seqs))
    U = [(g, h) for g in G for h in range(MLSTM_HEADS)]
    P = [(g, p) for g in G for p in range(MLSTM_HEADS // 2)]
    mv_r, mo_r, o_r, qk_r, xs_r, ct_r, n_r, m_r = (dict(enumerate(col)) for col in zip(*seqs))
    pair = lambda u: (u[0], u[1] // 2)

    xc = {g: xs_r[g][rows, :] for g in G}
    bc = {g: jnp.dot(tril, xc[g], preferred_element_type=F32, precision=lax.Precision.HIGHEST) for g in G}
    x2 = {g: jnp.where(lane_l < MLSTM_HEADS, xc[g], bc[g]) for g in G}
    xt = {g: x2[g].T for g in G}
    q2 = {(g, p): qk_r[g][rows, p * LANES:(p + 1) * LANES] for g, p in P}
    k2 = {(g, p): qk_r[g][rows, nqk + p * LANES:nqk + (p + 1) * LANES] for g, p in P}
    ct_old = {(g, p): ct_r[g][p] for g, p in P}
    ctb = {gp: ct_old[gp].astype(BF16) for gp in P}
    n2 = {(g, p): n_r[g][p:p + 1, :] for g, p in P}
    hm = {u: lo_l if u[1] % 2 == 0 else jnp.logical_not(lo_l) for u in U}
    qm = {u: jnp.where(hm[u], q2[pair(u)], jnp.zeros_like(q2[pair(u)])) for u in U}
    km = {u: jnp.where(hm[u], k2[pair(u)], jnp.zeros_like(k2[pair(u)])) for u in U}
    v = {(g, h): mv_r[g][rows, h * LANES:(h + 1) * LANES] for g, h in U}
    b_col = {(g, h): x2[g][:, MLSTM_HEADS + h:MLSTM_HEADS + h + 1] for g, h in U}
    ig_col = {(g, h): x2[g][:, h:h + 1] for g, h in U}
    b_row = {(g, h): xt[g][MLSTM_HEADS + h:MLSTM_HEADS + h + 1, :] for g, h in U}
    ig_row = {(g, h): xt[g][h:h + 1, :] for g, h in U}
    m_prev = {(g, h): m_r[g][h:h + 1, 0:1] for g, h in U}
    dlog = {u: jnp.where(causal, b_col[u] - b_row[u] + ig_row[u], NEG_INF) for u in U}
    m_inter = {u: b_col[u] + m_prev[u] for u in U}
    m_t = {u: jnp.maximum(m_inter[u], jnp.max(dlog[u], axis=-1, keepdims=True)) for u in U}
    w_intra = {u: jnp.exp(dlog[u] - m_t[u]) for u in U}
    a_inter = {u: jnp.exp(m_inter[u] - m_t[u]) for u in U}
    s = {u: lax.dot_general(q2[pair(u)], km[u], NT_DIMS, preferred_element_type=F32) * w_intra[u] for u in U}
    num = {u: jnp.dot(s[u].astype(BF16), v[u], preferred_element_type=F32)
           + a_inter[u] * jnp.dot(qm[u], ctb[pair(u)], preferred_element_type=F32) for u in U}
    den = {u: jnp.sum(s[u], axis=-1, keepdims=True)
           + a_inter[u] * jnp.sum(qm[u].astype(F32) * n2[pair(u)], axis=-1, keepdims=True) for u in U}
    hh = {u: num[u] / jnp.maximum(jnp.abs(den[u]), jnp.exp(-m_t[u])) for u in U}
    y = {(g, h): _rms(hh[g, h], ng_ref[:, h * LANES:(h + 1) * LANES]) for g, h in U}
    for g, h in U:
        o_r[g][rows, h * LANES:(h + 1) * LANES] = (
            y[g, h] * jax.nn.sigmoid(mo_r[g][rows, h * LANES:(h + 1) * LANES])).astype(BF16)
    b_last = {(g, h): xt[g][MLSTM_HEADS + h:MLSTM_HEADS + h + 1, L - 1:L] for g, h in U}
    g_col = {u: b_last[u] - b_col[u] + ig_col[u] for u in U}
    m_new = {u: jnp.maximum(b_last[u] + m_prev[u], jnp.max(g_col[u], axis=0, keepdims=True)) for u in U}
    kw = {u: km[u].astype(F32) * jnp.exp(g_col[u] - m_new[u]) for u in U}
    dec = {u: jnp.exp(b_last[u] + m_prev[u] - m_new[u]) for u in U}
    upd = {u: lax.dot_general(kw[u].astype(BF16), v[u], TN_DIMS, preferred_element_type=F32) for u in U}
    for g, h in U:
        m_r[g][h:h + 1, :] = jnp.broadcast_to(m_new[g, h], (1, LANES))
    for g, p in P:
        e, o = (g, 2 * p), (g, 2 * p + 1)
        ct_r[g][p] = ct_old[g, p] * jnp.where(row_c, dec[e], dec[o]) + upd[e] + upd[o]
        n_r[g][p:p + 1, :] = (n2[g, p] * jnp.where(lane_1, dec[e], dec[o])
                              + jnp.sum(kw[e] + kw[o], axis=0, keepdims=True))
    return carry


def _mlstm(mqk, mv, mo, gts, conv_w, gate_bias, norm_g, bsz, seq, chunks, group):
    n = mqk.shape[0]
    tm = chunks * MLSTM_CHUNK
    steps = seq // tm
    row = lambda b, i: (b, i, 0)
    const = lambda b, i: (0, 0)
    width = MLSTM_HEADS * MLSTM_V_DIM
    per_seq = lambda a: a.reshape(bsz, seq, a.shape[1])
    out = pl.pallas_call(
        functools.partial(_mlstm_kernel, chunks=chunks, group=group),
        grid=(bsz // group, steps),
        in_specs=[pl.BlockSpec((group, tm, width), row),
                  pl.BlockSpec((group, tm, width), row),
                  pl.BlockSpec((group, tm, width), row),
                  pl.BlockSpec((group, tm, LANES), row),
                  pl.BlockSpec((CONV_WIDTH, width), const),
                  pl.BlockSpec((1, LANES), const),
                  pl.BlockSpec((1, width), const)],
        out_specs=pl.BlockSpec((group, tm, width), row),
        out_shape=jax.ShapeDtypeStruct((bsz, seq, width), BF16),
        scratch_shapes=[pltpu.VMEM((group, SUBLANES, width), F32),
                        pltpu.VMEM((group, tm, width), BF16),
                        pltpu.VMEM((group, tm, LANES), F32),
                        pltpu.VMEM((group, MLSTM_HEADS // 2, LANES, LANES), F32),
                        pltpu.VMEM((group, SUBLANES, LANES), F32),
                        pltpu.VMEM((group, SUBLANES, LANES), F32)],
        compiler_params=_params(("arbitrary", "arbitrary")),
        name="mlstm",
    )(per_seq(mqk), per_seq(mv), per_seq(mo), per_seq(gts), conv_w, gate_bias, norm_g)
    return out.reshape(n, width)


def _out_proj_kernel(att_ref, mh_ref, x_ref, g1_ref, sc_ref, sh_ref, gpm_ref, gpf_ref, wa_ref, wb_ref,
                     x1_ref, h2_ref):
    mix = (jnp.dot(att_ref[...], wa_ref[...], preferred_element_type=F32)
           + jnp.dot(mh_ref[...], wb_ref[...], preferred_element_type=F32))
    x1 = x_ref[...] + g1_ref[0] * _rms(mix, gpm_ref[...])
    x1_ref[...] = x1
    h2_ref[...] = _rms(x1, gpf_ref[...]) * (1.0 + sc_ref[0]) + sh_ref[0]


def _out_proj(att, mh, x2, gate1, scale2, shift2, g_post_mix, g_pre_ffn, wa, wb, seq, tm):
    n, d = x2.shape
    per_b = seq // tm
    row = lambda i: (i, 0)
    bsel = lambda i: (i // per_b, 0, 0)
    const = lambda i: (0, 0)
    half = att.shape[1]
    return pl.pallas_call(
        _out_proj_kernel,
        grid=(n // tm,),
        in_specs=[pl.BlockSpec((tm, half), row), pl.BlockSpec((tm, half), row), pl.BlockSpec((tm, d), row),
                  pl.BlockSpec((1, 1, d), bsel), pl.BlockSpec((1, 1, d), bsel), pl.BlockSpec((1, 1, d), bsel),
                  pl.BlockSpec((1, d), const), pl.BlockSpec((1, d), const),
                  pl.BlockSpec((half, d), const), pl.BlockSpec((half, d), const)],
        out_specs=[pl.BlockSpec((tm, d), row)] * 2,
        out_shape=[jax.ShapeDtypeStruct((n, d), F32)] * 2,
        compiler_params=_params(("arbitrary",), VMEM_LIMIT),
        name="out_proj",
    )(att, mh, x2, gate1, scale2, shift2, g_post_mix, g_pre_ffn, wa, wb)


_BIG_ID = float(2 ** 30)
SORT_LEVELS = 8


def _top_scores(s, k):
    rows, t = s.shape
    span = SORT_LEVELS * SUBLANES
    r = lax.broadcasted_iota(jnp.int32, (rows // SORT_LEVELS, t), 0)
    col_id = ((r // SUBLANES) * span + r % SUBLANES).astype(F32)
    lev = [jnp.concatenate([s[g * span + l * SUBLANES:g * span + (l + 1) * SUBLANES] for g in range(rows // span)],
                           axis=0) for l in range(SORT_LEVELS)]
    ids = [col_id + float(l * SUBLANES) for l in range(SORT_LEVELS)]
    for rnd in range(SORT_LEVELS):
        for a in range(rnd % 2, SORT_LEVELS - 1, 2):
            swap = lev[a + 1] > lev[a]
            lev[a], lev[a + 1] = jnp.where(swap, lev[a + 1], lev[a]), jnp.where(swap, lev[a], lev[a + 1])
            ids[a], ids[a + 1] = jnp.where(swap, ids[a + 1], ids[a]), jnp.where(swap, ids[a], ids[a + 1])
    vals, sel = [], []
    for it in range(k):
        m = jnp.max(lev[0], axis=0, keepdims=True)
        i = jnp.min(jnp.where(lev[0] == m, ids[0], _BIG_ID), axis=0, keepdims=True)
        vals.append(m)
        sel.append(i)
        hit = ids[0] == i
        live = min(SORT_LEVELS, k - 1 - it)
        for l in range(min(live, SORT_LEVELS - 1)):
            lev[l] = jnp.where(hit, lev[l + 1], lev[l])
            ids[l] = jnp.where(hit, ids[l + 1], ids[l])
        if live == SORT_LEVELS:
            lev[-1] = jnp.where(hit, NEG_INF, lev[-1])
    return jnp.concatenate(vals, axis=0), jnp.concatenate(sel, axis=0).astype(jnp.int32)


def _top_pair_sums(v1, v2):
    k, t = v1.shape
    half = SUBLANES
    lev = [v1[0:half] + v2[b:b + 1, :] for b in range(k)]
    side = v1[half:k] + v2[0:1, :]
    a_low = lax.broadcasted_iota(jnp.int32, (half, t), 0).astype(F32) * float(k)
    side_id = a_low + float(half * k)
    depth = jnp.zeros((half, t), F32)
    vals, sel = [], []
    for it in range(k):
        top_id = a_low + depth
        m = jnp.max(jnp.maximum(lev[0], side), axis=0, keepdims=True)
        i = jnp.min(jnp.minimum(jnp.where(lev[0] == m, top_id, _BIG_ID), jnp.where(side == m, side_id, _BIG_ID)),
                    axis=0, keepdims=True)
        vals.append(m)
        sel.append(i)
        hit = top_id == i
        for l in range(k - 1 - it):
            lev[l] = jnp.where(hit, lev[l + 1], lev[l])
        side = jnp.where(side_id == i, NEG_INF, side)
        depth = depth + jnp.where(hit, 1.0, 0.0)
    return jnp.concatenate(vals, axis=0), jnp.concatenate(sel, axis=0).astype(jnp.int32)


def _pick_rows(table, which):
    r = lax.broadcasted_iota(jnp.int32, table.shape, 0)
    rows = []
    for k in range(which.shape[0]):
        rows.append(jnp.sum(jnp.where(r == which[k:k + 1, :], table, 0), axis=0, keepdims=True))
    return jnp.concatenate(rows, axis=0)


def _peer_sel_kernel(h_ref, wq_ref, k1_ref, k2_ref, e_ref, g_ref, r_ref):
    tq = h_ref.shape[0]
    K = PEER_TOPK
    q = jnp.dot(h_ref[...].astype(BF16), wq_ref[...], preferred_element_type=F32).astype(BF16)
    for hd in range(PEER_HEADS):
        base = hd * 2 * PEER_HALF
        s1 = lax.dot_general(k1_ref[hd], q[:, base:base + PEER_HALF], NT_DIMS, preferred_element_type=F32)
        s2 = lax.dot_general(k2_ref[hd], q[:, base + PEER_HALF:base + 2 * PEER_HALF], NT_DIMS,
                             preferred_element_type=F32)
        v1, i1 = _top_scores(s1, K)
        v2, i2 = _top_scores(s2, K)
        top, pos = _top_pair_sums(v1, v2)
        eid = _pick_rows(i1, pos >> 4) * PEER_KEYS + _pick_rows(i2, pos & (K - 1))
        ex = jnp.exp(top - top[0:1, :])
        e_ref[hd * K:(hd + 1) * K, :] = eid * ROW_WORDS
        g_ref[hd * K:(hd + 1) * K, :] = ex / jnp.sum(ex, axis=0, keepdims=True)
    pair = (e_ref[...] >> 1).T
    for h in range(STREAM_HALVES):
        r_ref[pl.ds(h, tq, stride=STREAM_HALVES), :] = pair + h


def _peer_sel(h2, wq, k1, k2, tq, tg):
    n, d = h2.shape
    rows = PEER_HEADS * PEER_TOPK
    assert rows == LANES and tg % tq == 0
    return pl.pallas_call(
        _peer_sel_kernel,
        grid=(n // tq,),
        in_specs=[pl.BlockSpec((tq, d), lambda i: (i, 0)),
                  pl.BlockSpec(wq.shape, lambda i: (0, 0)),
                  pl.BlockSpec(k1.shape, lambda i: (0, 0, 0)),
                  pl.BlockSpec(k2.shape, lambda i: (0, 0, 0))],
        out_specs=[pl.BlockSpec((rows, tq), lambda i: (0, i))] * 2
        + [pl.BlockSpec((tq * STREAM_HALVES, rows), lambda i: (i * tq // tg, 0))],
        out_shape=[jax.ShapeDtypeStruct((rows, n), jnp.int32), jax.ShapeDtypeStruct((rows, n), F32),
                   jax.ShapeDtypeStruct((n // tg * tq * STREAM_HALVES, rows), jnp.int32)],
        compiler_params=_params(("arbitrary",), VMEM_LIMIT),
        name="peer_sel",
    )(h2, wq, k1, k2)


def _split_bf16(x):
    hi = x.astype(BF16)
    return hi, (x - hi.astype(F32)).astype(BF16)


def _pack_kernel(w_ref, o_ref, *wide_ref):
    x = w_ref[...]
    eb = x.shape[0]
    for r in range(ROW_WORDS):
        lo = x[:, 2 * r * LANES:(2 * r + 1) * LANES].astype(BF16).astype(F32)
        hi = x[:, (2 * r + 1) * LANES:(2 * r + 2) * LANES].astype(BF16).astype(F32)
        word = (lax.shift_right_logical(pltpu.bitcast(lo, jnp.int32), 16)
                | (pltpu.bitcast(hi, jnp.int32) & jnp.int32(-65536)))
        o_ref[pl.ds(r, eb, stride=ROW_WORDS), :] = word
    per_row = STREAM_ROW // LANES
    for ref in wide_ref:
        for c in range(per_row):
            ref[:, c * LANES:(c + 1) * LANES] = o_ref[pl.ds(c, eb * STREAM_HALVES, stride=per_row), :]


def _pack_table(w, wide_copy=False):
    e, d = w.shape
    eb = 512
    shapes = [(ROW_WORDS, LANES)] + [(STREAM_HALVES, STREAM_ROW)] * wide_copy
    return pl.pallas_call(
        _pack_kernel,
        grid=(e // eb,),
        in_specs=[pl.BlockSpec((eb, d), lambda i: (i, 0))],
        out_specs=[pl.BlockSpec((eb * rows, width), lambda i: (i, 0)) for rows, width in shapes],
        out_shape=[jax.ShapeDtypeStruct((e * rows, width), jnp.int32) for rows, width in shapes],
        compiler_params=_params(("arbitrary",)),
        name="pack_table",
    )(w)


def _gather_rows(slots, tab_ref, t, stage_ref):
    for j, slot in enumerate(slots):
        src = pl.ds(pl.multiple_of(slot[t], ROW_WORDS), ROW_WORDS)
        stage_ref[j * ROW_WORDS:(j + 1) * ROW_WORDS, :] = tab_ref[src, :]


def _staged_bf16(stage_ref):
    return pltpu.bitcast(stage_ref[...], BF16)


def _pipelined_tokens(ntok, gather, compute, stages, on_trip=None, after=None):
    nb = len(stages)
    for k in range(nb):
        gather(k, stages[k])

    def trip(i, carry):
        t = nb * i
        if on_trip is not None:
            on_trip(i)
        for k in range(nb):
            compute(t + k, stages[k])
            ahead = t + k + nb
            gather(jnp.minimum(ahead, ntok - 1), stages[k])
            if after is not None:
                after(i, k)
        return carry

    lax.fori_loop(0, ntok // nb, trip, 0)


def _with_slot_indices(idx_hbm, sems, bufs, stride, count, run):
    nsel = len(bufs) // 2
    step = pl.program_id(0)
    last = pl.num_programs(0) - 1

    def copies(block, which):
        return [pltpu.make_async_copy(idx_hbm.at[j, pl.ds(block * stride, count)], bufs[which * nsel + j],
                                      sems.at[which])
                for j in range(nsel)]

    @pl.when(step == 0)
    def _():
        for cp in copies(0, 0):
            cp.start()

    def phase(which):
        for cp in copies(step, which):
            cp.wait()

        @pl.when(step < last)
        def _():
            for cp in copies(step + 1, 1 - which):
                cp.start()

        run(bufs[which * nsel:(which + 1) * nsel])

    for which in range(2):
        pl.when(step % 2 == which)(functools.partial(phase, which))


SC_WINDOW = 128


def _sc_gather_rows(tab, row_ids):
    windows, width = row_ids.shape
    assert width == SC_WINDOW
    n = windows * SC_WINDOW
    mesh = plsc.VectorSubcoreMesh(core_axis_name="core", subcore_axis_name="subcore")

    @pl.kernel(out_type=jax.ShapeDtypeStruct((n, tab.shape[1]), tab.dtype), mesh=mesh)
    def gather(tab_hbm, ids_hbm, out_hbm):
        def body(ids_vmem, out_vmem):
            pltpu.sync_copy(tab_hbm.at[ids_vmem.at[0]], out_vmem)

        pltpu.emit_pipeline(
            body,
            grid=(windows,),
            in_specs=[pl.BlockSpec((1, SC_WINDOW), index_map=lambda i: (i, 0))],
            out_specs=[pl.BlockSpec((SC_WINDOW, tab.shape[1]), index_map=lambda i: (i, 0))],
            core_axis_name=("core", "subcore"),
            dimension_semantics=(pltpu.PARALLEL,),
            trace_scopes=False,
        )(ids_hbm, out_hbm)

    return gather(tab, row_ids)


def _diag_mask(nsel):
    shape = (SUBLANES, nsel * SUBLANES)
    return (lax.broadcasted_iota(jnp.int32, shape, 1) % SUBLANES) == lax.broadcasted_iota(jnp.int32, shape, 0)


def _token_tile(ref, t):
    row = ref[pl.ds(t, 1), :]
    return jnp.concatenate([row[:, s * LANES:(s + 1) * LANES] for s in range(SUBLANES)], axis=0)


def _peer_u_kernel(idx_hbm, h_ref, gate_ref, grp_ref, tab_ref, act_ref, part_ref, *scratch):
    nsel, tq = gate_ref.shape
    stages, sems, bufs = scratch[:GATHER_STAGES], scratch[GATHER_STAGES], scratch[GATHER_STAGES + 1:]
    diag = _diag_mask(nsel)

    def compute(t, stage_ref):
        h_hi, h_lo = _split_bf16(_token_tile(h_ref, t))
        both = lax.dot_general(jnp.concatenate([h_hi, h_lo], axis=0), _staged_bf16(stage_ref), NT_DIMS,
                               preferred_element_type=F32)
        prod = both[0:SUBLANES] + both[SUBLANES:2 * SUBLANES]
        part_ref[pl.ds(t, 1), :] = jnp.sum(jnp.where(diag, prod, 0.0), axis=0, keepdims=True)

    def run(slots):
        _pipelined_tokens(tq, functools.partial(_gather_rows, slots, tab_ref), compute, stages)

    _with_slot_indices(idx_hbm, sems, bufs, tq, tq, run)
    p_hi, p_lo = _split_bf16(part_ref[...])
    grp = grp_ref[...]
    pre = jnp.dot(p_hi, grp, preferred_element_type=F32) + jnp.dot(p_lo, grp, preferred_element_type=F32)
    act_ref[...] = 0.5 * pre * (1.0 + lax.erf(pre * (2.0 ** -0.5))) * gate_ref[...].T


def _group_matrix(nsel):
    r = jnp.arange(nsel * SUBLANES)[:, None] // SUBLANES
    return (r == jnp.arange(nsel)[None, :]).astype(BF16)


def _gather_scratch(nsel, ntok, nstages):
    return ([pltpu.VMEM((nsel * ROW_WORDS, LANES), jnp.int32)] * nstages
            + [pltpu.SemaphoreType.DMA((2,))] + [pltpu.SMEM((ntok,), jnp.int32)] * (2 * nsel))


def _peer_u(idx_t, h2, gate_t, tab, tq):
    nsel, n = idx_t.shape
    d = h2.shape[1]
    wide = nsel * SUBLANES
    return pl.pallas_call(
        _peer_u_kernel,
        grid=(n // tq,),
        in_specs=[pl.BlockSpec(memory_space=pl.ANY),
                  pl.BlockSpec((tq, d), lambda i: (i, 0)),
                  pl.BlockSpec((nsel, tq), lambda i: (0, i)),
                  pl.BlockSpec((wide, nsel), lambda i: (0, 0)),
                  pl.BlockSpec(memory_space=pltpu.VMEM)],
        out_specs=pl.BlockSpec((tq, nsel), lambda i: (i, 0)),
        out_shape=jax.ShapeDtypeStruct((n, nsel), F32),
        scratch_shapes=[pltpu.VMEM((tq, wide), F32)] + _gather_scratch(nsel, tq, GATHER_STAGES),
        compiler_params=_params(("arbitrary",), GATHER_VMEM_LIMIT),
        name="peer_u",
    )(idx_t, h2, gate_t, _group_matrix(nsel), tab)


def _lane_aligned(count):
    return -(-count // LANES) * LANES


def _streamed_rows_bf16(words):
    x = pltpu.bitcast(words, BF16)
    return jnp.concatenate([x[:, 0:LANES], x[:, LANES:2 * LANES]], axis=0)


def _streamed_copy_matrix(nsel):
    r = jnp.arange(nsel * SUBLANES)
    return ((r[None, :] % (2 * nsel)) // 2 == jnp.arange(nsel)[:, None]).astype(BF16)


def _streamed_diag(nsel):
    shape = (SUBLANES, nsel * SUBLANES)
    r = lax.broadcasted_iota(jnp.int32, shape, 1)
    sub = 4 * ((r % (4 * nsel)) // (2 * nsel)) + 2 * (r // (4 * nsel)) + r % 2
    return lax.broadcasted_iota(jnp.int32, shape, 0) == sub


def _peer_v_kernel(idx_hbm, act_ref, rep_ref, rep2_ref, x1_ref, g2_ref, gpf_ref, tab_ref, rows_hbm, o_ref,
                   wide_ref, y_ref, rowbuf, rsem, *scratch, nvld, nstages):
    tq, nsel = act_ref.shape
    stages, sems, bufs = scratch[:nstages], scratch[nstages], scratch[nstages + 1:]
    trips = nvld // nstages
    per_trip = (tq - nvld) // trips
    rows_tok = nsel * STREAM_HALVES
    step = pl.program_id(0)
    chunks = pl.num_programs(0) * trips
    diag = _diag_mask(nsel)
    diag2 = _streamed_diag(nsel)
    a_hi, a_lo = _split_bf16(act_ref[...])
    for lo, hi, rep in ((0, nvld, rep_ref[...]), (nvld, tq, rep2_ref[...])):
        wide_ref[lo:hi, :] = (jnp.dot(a_hi[lo:hi], rep, preferred_element_type=F32)
                              + jnp.dot(a_lo[lo:hi], rep, preferred_element_type=F32))

    def finish(t, mask, rows_bf16):
        w = jnp.where(mask, jnp.broadcast_to(wide_ref[pl.ds(t, 1), :], mask.shape), 0.0)
        w_hi, w_lo = _split_bf16(w)
        both = jnp.dot(jnp.concatenate([w_hi, w_lo], axis=0), rows_bf16, preferred_element_type=F32)
        tile = both[0:SUBLANES] + both[SUBLANES:2 * SUBLANES]
        y_ref[pl.ds(t, 1), :] = jnp.concatenate([tile[s:s + 1, :] for s in range(SUBLANES)], axis=1)

    def compute(t, stage_ref):
        finish(t, diag, _staged_bf16(stage_ref))

    def chunk_copy(g, slot):
        return pltpu.make_async_copy(rows_hbm.at[pl.ds(g * (per_trip * rows_tok), per_trip * rows_tok), :],
                                     rowbuf.at[slot], rsem.at[slot])

    ring = STREAM_RING

    @pl.when(step == 0)
    def _():
        for g0 in range(ring - 1):
            chunk_copy(g0, g0).start()

    def on_trip(i):
        g = step * trips + i
        chunk_copy(g, i % ring).wait()

        @pl.when(g + ring - 1 < chunks)
        def _():
            chunk_copy(g + ring - 1, (i + ring - 1) % ring).start()

    places = [(d * nstages) // per_trip for d in range(per_trip)]

    def after(i, k):
        for d in range(per_trip):
            if places[d] == k:
                rows = rowbuf.at[i % ring, pl.ds(d * rows_tok, rows_tok), :]
                finish(nvld + i * per_trip + d, diag2, _streamed_rows_bf16(rows[...]))

    def run(slots):
        _pipelined_tokens(nvld, functools.partial(_gather_rows, slots, tab_ref), compute, stages, on_trip, after)

    _with_slot_indices(idx_hbm, sems, bufs, tq, _lane_aligned(nvld), run)
    o_ref[...] = x1_ref[...] + g2_ref[0] * _rms(y_ref[...], gpf_ref[...])


def _peer_v(idx_t, act, x1, gate2, g_post_ffn, tab, rows, seq, tq, nvld, nstages):
    nsel, n = idx_t.shape
    d = x1.shape[1]
    wide = nsel * SUBLANES
    per_b = seq // tq
    row = lambda i: (i, 0)
    const = lambda i: (0, 0)
    trips = nvld // nstages
    per_trip = (tq - nvld) // trips
    assert trips % STREAM_RING == 0 and trips * nstages == nvld and trips * per_trip == tq - nvld
    return pl.pallas_call(
        functools.partial(_peer_v_kernel, nvld=nvld, nstages=nstages),
        grid=(n // tq,),
        in_specs=[pl.BlockSpec(memory_space=pl.ANY),
                  pl.BlockSpec((tq, nsel), row),
                  pl.BlockSpec((nsel, wide), const),
                  pl.BlockSpec((nsel, wide), const),
                  pl.BlockSpec((tq, d), row),
                  pl.BlockSpec((1, 1, d), lambda i: (i // per_b, 0, 0)),
                  pl.BlockSpec((1, d), const),
                  pl.BlockSpec(memory_space=pltpu.VMEM),
                  pl.BlockSpec(memory_space=pl.ANY)],
        out_specs=pl.BlockSpec((tq, d), row),
        out_shape=jax.ShapeDtypeStruct((n, d), F32),
        scratch_shapes=[pltpu.VMEM((tq, wide), F32), pltpu.VMEM((tq, d), F32),
                        pltpu.VMEM((STREAM_RING, per_trip * nsel * STREAM_HALVES, STREAM_ROW), jnp.int32),
                        pltpu.SemaphoreType.DMA((STREAM_RING,))]
        + _gather_scratch(nsel, _lane_aligned(nvld), nstages),
        compiler_params=_params(("arbitrary",), GATHER_VMEM_LIMIT),
        name="peer_v",
    )(idx_t, act, _group_matrix(nsel).T, _streamed_copy_matrix(nsel), x1, gate2, g_post_ffn, tab, rows)


def _dup_heads(w, heads, dh):
    d = w.shape[0]
    return jnp.repeat(w.reshape(d, heads, 1, dh), 2, axis=2).reshape(d, heads * 2 * dh)


def _layer(x2, c, pos_col, bsz, seq, w_mod, b_mod, g_pre_mix, g_post_mix, w_in, conv_w, b_igate, b_fgate,
           mlstm_norm_g, att_sinks, w_out, g_pre_ffn, g_post_ffn, peer_wq, peer_keys1, peer_keys2, peer_u, peer_v):
    n, d = x2.shape
    tm = min(seq, 512)
    mod = _mod(c, w_mod, b_mod)
    shift1, scale1, gate1, shift2, scale2, gate2 = [m.reshape(bsz, 1, d) for m in jnp.split(mod, 6, axis=-1)]

    aw = ATT_HEADS * ATT_HEAD_DIM
    kvw = ATT_KV_HEADS * ATT_HEAD_DIM
    qkw = MLSTM_HEADS * MLSTM_QK_DIM
    mw = MLSTM_HEADS * MLSTM_V_DIM
    o = 0
    wq_a = w_in[:, o:o + aw]; o += aw
    wk_a = w_in[:, o:o + kvw]; o += kvw
    wv_a = w_in[:, o:o + kvw]; o += kvw
    w_mqk = w_in[:, o:o + 2 * qkw]; o += 2 * qkw
    w_mv = w_in[:, o:o + mw]; o += mw
    w_g = w_in[:, o:o + 2 * MLSTM_HEADS]; o += 2 * MLSTM_HEADS
    w_mo = w_in[:, o:o + mw]
    w_gp = jnp.pad(w_g, ((0, 0), (0, LANES - 2 * MLSTM_HEADS)))
    w_all = jnp.concatenate([wq_a, _dup_heads(wk_a, ATT_KV_HEADS, ATT_HEAD_DIM),
                             _dup_heads(wv_a, ATT_KV_HEADS, ATT_HEAD_DIM), w_mqk, w_mv, w_mo, w_gp],
                            axis=1).astype(BF16)

    cos, sin = _rope_tab(pos_col)
    q, kd, vd, mqk, mv, mo, gts = _in_proj(x2, scale1, shift1, g_pre_mix.reshape(1, d), cos, sin, w_all, seq, tm)
    att = _swa(att_sinks, q, kd, vd, bsz, seq)
    gate_bias = jnp.pad(jnp.concatenate([b_igate, b_fgate]), (0, LANES - 2 * MLSTM_HEADS)).reshape(1, LANES)
    chunks = min(seq // MLSTM_CHUNK, 8)
    mh = _mlstm(mqk, mv, mo, gts, conv_w, gate_bias, mlstm_norm_g.reshape(1, mw), bsz, seq, chunks,
                MLSTM_GROUP if bsz % MLSTM_GROUP == 0 else 1)
    wo = w_out.astype(BF16)
    x1, h2 = _out_proj(att, mh, x2, gate1, scale2, shift2, g_post_mix.reshape(1, d), g_pre_ffn.reshape(1, d),
                       wo[:aw], wo[aw:], seq, tm)

    tg = min(seq, GATHER_TOKENS)
    nvld = tg * STREAM_SPLIT[0] // sum(STREAM_SPLIT)
    eid_t, gate_t, row_ids = _peer_sel(h2, peer_wq.astype(BF16), peer_keys1.astype(BF16),
                                       peer_keys2.astype(BF16), tg - nvld, tg)
    tab_v, tab_v_wide = _pack_table(peer_v, wide_copy=True)
    rows_v = _sc_gather_rows(tab_v_wide, row_ids)
    tab_u, = _pack_table(peer_u)
    act = _peer_u(eid_t, h2, gate_t, tab_u, tg)
    return _peer_v(eid_t, act, x1, gate2, g_post_ffn.reshape(1, d), tab_v, rows_v, seq, tg, nvld, STREAM_STAGES)


def kernel(x, c, positions, w_mod, b_mod, g_pre_mix, g_post_mix, w_in, conv_w, b_igate, b_fgate, mlstm_norm_g, att_sinks, w_out, g_pre_ffn, g_post_ffn, peer_wq, peer_keys1, peer_keys2, peer_u, peer_v):
    bsz, seq, d = x.shape
    n = bsz * seq
    x2 = x.reshape(n, d)
    pos_col = positions.reshape(n, 1)
    for l in range(w_mod.shape[0]):
        x2 = _layer(x2, c, pos_col, bsz, seq, w_mod[l], b_mod[l], g_pre_mix[l], g_post_mix[l], w_in[l], conv_w[l],
                    b_igate[l], b_fgate[l], mlstm_norm_g[l], att_sinks[l], w_out[l], g_pre_ffn[l], g_post_ffn[l],
                    peer_wq[l], peer_keys1[l], peer_keys2[l], peer_u[l], peer_v[l])
    return x2.reshape(bsz, seq, d)
```

```python
import functools

import jax
import jax.numpy as jnp
from jax import lax
from jax.experimental import pallas as pl
from jax.experimental.pallas import tpu as pltpu
from jax.experimental.pallas import tpu_sc as plsc

F32 = jnp.float32
BF16 = jnp.bfloat16

ATT_HEADS = 8
ATT_KV_HEADS = 2
ATT_HEAD_DIM = 64
ATT_BLOCK = 128
ROPE_THETA = 10000.0
MLSTM_HEADS = 4
MLSTM_V_DIM = 128
MLSTM_QK_DIM = 64
MLSTM_CHUNK = 64
CONV_WIDTH = 4
PEER_HEADS = 8
PEER_KEYS = 128
PEER_HALF = 128
PEER_TOPK = 16
NORM_EPS = 1e-6

LANES = 128
SUBLANES = 8
VMEM_LIMIT = 52 * 1024 * 1024
GATHER_STAGES = 32
ROW_WORDS = SUBLANES // 2
SWA_BLOCKS = 8
MLSTM_GROUP = 4
STREAM_SPLIT = (4, 4)
STREAM_RING = 4
STREAM_STAGES = 8
STREAM_HALVES = 2
STREAM_ROW = 2 * LANES
GATHER_TOKENS = 512
GATHER_VMEM_LIMIT = 58 * 1024 * 1024

NEG_INF = float("-inf")
NT_DIMS = (((1,), (1,)), ((), ()))
TN_DIMS = (((0,), (0,)), ((), ()))


def _params(sem, vmem=None):
    return pltpu.CompilerParams(dimension_semantics=sem, vmem_limit_bytes=vmem)


def _rms(x, g):
    return x * lax.rsqrt(jnp.mean(x * x, axis=-1, keepdims=True) + NORM_EPS) * g


def _mod_kernel(c_ref, w_ref, b_ref, o_ref):
    o_ref[...] = jnp.dot(c_ref[...], w_ref[...], preferred_element_type=F32,
                         precision=lax.Precision.HIGHEST) + b_ref[...]


def _mod(c, w, b):
    bsz, d = c.shape
    nout = w.shape[1]
    return pl.pallas_call(
        _mod_kernel,
        grid=(nout // d,),
        in_specs=[pl.BlockSpec((bsz, d), lambda i: (0, 0)),
                  pl.BlockSpec((d, d), lambda i: (0, i)),
                  pl.BlockSpec((1, d), lambda i: (0, i))],
        out_specs=pl.BlockSpec((bsz, d), lambda i: (0, i)),
        out_shape=jax.ShapeDtypeStruct((bsz, nout), F32),
        compiler_params=_params(("arbitrary",)),
        name="mod",
    )(c, w, b.reshape(1, nout))


def _rope_tab_kernel(pos_ref, inv_ref, sign_ref, cos_ref, sin_ref):
    ang = pos_ref[...].astype(F32) * inv_ref[...]
    cos_ref[...] = jnp.cos(ang)
    sin_ref[...] = jnp.sin(ang) * sign_ref[...]


def _rope_tab(pos_col):
    n = pos_col.shape[0]
    tr = min(n, 1024)
    half = ATT_HEAD_DIM // 2
    inv = ROPE_THETA ** (-jnp.arange(0, ATT_HEAD_DIM, 2, dtype=F32) / ATT_HEAD_DIM)
    inv_row = jnp.tile(inv, LANES // half).reshape(1, LANES)
    lane = jnp.arange(LANES)
    sign_row = jnp.where((lane % ATT_HEAD_DIM) < half, -1.0, 1.0).astype(F32).reshape(1, LANES)
    return pl.pallas_call(
        _rope_tab_kernel,
        grid=(n // tr,),
        in_specs=[pl.BlockSpec((tr, 1), lambda i: (i, 0)),
                  pl.BlockSpec((1, LANES), lambda i: (0, 0)),
                  pl.BlockSpec((1, LANES), lambda i: (0, 0))],
        out_specs=[pl.BlockSpec((tr, LANES), lambda i: (i, 0))] * 2,
        out_shape=[jax.ShapeDtypeStruct((n, LANES), F32)] * 2,
        compiler_params=_params(("arbitrary",)),
        name="rope_tab",
    )(pos_col, inv_row, sign_row)


def _rope(v, cos, sin):
    half = ATT_HEAD_DIM // 2
    lane = lax.broadcasted_iota(jnp.int32, cos.shape, 1)
    first = (lane % ATT_HEAD_DIM) < half
    outs = []
    for j in range(v.shape[1] // LANES):
        c = v[:, j * LANES:(j + 1) * LANES]
        rot = jnp.where(first, pltpu.roll(c, LANES - half, 1), pltpu.roll(c, half, 1))
        outs.append(c * cos + rot * sin)
    return jnp.concatenate(outs, axis=1)


_C_Q, _C_K, _C_V, _C_MQK, _C_MV, _C_MO, _C_G, _C_END = 0, 512, 768, 1024, 1536, 2048, 2560, 2688


def _in_proj_kernel(x_ref, sc_ref, sh_ref, g_ref, cos_ref, sin_ref, w_ref,
                    q_ref, k_ref, v_ref, mqk_ref, mv_ref, mo_ref, gt_ref):
    x = x_ref[...]
    h = _rms(x, g_ref[...]) * (1.0 + sc_ref[0]) + sh_ref[0]
    hb = h.astype(BF16)

    def mm(a, b):
        return jnp.dot(hb, w_ref[:, a:b], preferred_element_type=F32)

    cos = cos_ref[...]
    sin = sin_ref[...]
    q_ref[...] = (_rope(mm(_C_Q, _C_K), cos, sin) * (ATT_HEAD_DIM ** -0.5)).astype(BF16)
    k_ref[...] = _rope(mm(_C_K, _C_V), cos, sin).astype(BF16)
    v_ref[...] = mm(_C_V, _C_MQK).astype(BF16)
    mqk_ref[...] = mm(_C_MQK, _C_MV)
    mv_ref[...] = mm(_C_MV, _C_MO).astype(BF16)
    mo_ref[...] = mm(_C_MO, _C_G)
    gt_ref[...] = mm(_C_G, _C_END)


def _in_proj(x2, scale1, shift1, g_pre, cos, sin, w_all, seq, tm):
    n, d = x2.shape
    per_b = seq // tm
    row = lambda i: (i, 0)
    bsel = lambda i: (i // per_b, 0, 0)
    widths = (512, 256, 256, 512, 512, 512, 128)
    dtypes = (BF16, BF16, BF16, F32, BF16, F32, F32)
    return pl.pallas_call(
        _in_proj_kernel,
        grid=(n // tm,),
        in_specs=[pl.BlockSpec((tm, d), row),
                  pl.BlockSpec((1, 1, d), bsel),
                  pl.BlockSpec((1, 1, d), bsel),
                  pl.BlockSpec((1, d), lambda i: (0, 0)),
                  pl.BlockSpec((tm, LANES), row),
                  pl.BlockSpec((tm, LANES), row),
                  pl.BlockSpec((d, _C_END), lambda i: (0, 0))],
        out_specs=[pl.BlockSpec((tm, w), row) for w in widths],
        out_shape=[jax.ShapeDtypeStruct((n, w), dt) for w, dt in zip(widths, dtypes)],
        compiler_params=_params(("arbitrary",), VMEM_LIMIT),
        name="in_proj",
    )(x2, scale1, shift1, g_pre, cos, sin, w_all)


def _swa_kernel(sink_ref, q_ref, kp_ref, kc_ref, vp_ref, vc_ref, o_ref):
    blk = ATT_BLOCK
    n = pl.program_id(1)
    qi = lax.broadcasted_iota(jnp.int32, (blk, 2 * blk), 0)
    si = lax.broadcasted_iota(jnp.int32, (blk, 2 * blk), 1)
    delta = qi + blk - si
    window = (delta >= 0) & (delta < blk)
    lo = lax.broadcasted_iota(jnp.int32, (2 * blk, LANES), 1) < ATT_HEAD_DIM
    group = ATT_HEADS // ATT_KV_HEADS
    kv = {}
    for j in range(SWA_BLOCKS):
        rows = slice(j * blk, (j + 1) * blk)
        before = slice((j - 1) * blk, j * blk)
        for g in range(ATT_KV_HEADS):
            cs = slice(g * LANES, (g + 1) * LANES)
            k = jnp.concatenate([kp_ref[:, cs] if j == 0 else kc_ref[before, cs], kc_ref[rows, cs]], axis=0)
            v = jnp.concatenate([vp_ref[:, cs] if j == 0 else vc_ref[before, cs], vc_ref[rows, cs]], axis=0)
            zero = jnp.zeros_like(k)
            kv[j, g] = ((jnp.where(lo, k, zero), jnp.where(lo, v, zero)),
                        (jnp.where(lo, zero, k), jnp.where(lo, zero, v)))
    units = [(j, h) for j in range(SWA_BLOCKS) for h in range(ATT_HEADS)]
    valid = [window & ((si >= blk) | (n > 0)) if j == 0 else window for j in range(SWA_BLOCKS)]
    sinks = [sink_ref[h] for h in range(ATT_HEADS)]
    q = {(j, h): q_ref[j * blk:(j + 1) * blk, (h // 2) * LANES:(h // 2 + 1) * LANES] for j, h in units}
    scores = {(j, h): jnp.where(valid[j], lax.dot_general(q[j, h], kv[j, h // group][h % 2][0], NT_DIMS,
                                                          preferred_element_type=F32), NEG_INF) for j, h in units}
    tops = {(j, h): jnp.maximum(jnp.max(scores[j, h], axis=-1, keepdims=True), sinks[h]) for j, h in units}
    exps = {u: jnp.exp(scores[u] - tops[u]) for u in units}
    dens = {(j, h): jnp.sum(exps[j, h], axis=-1, keepdims=True) + jnp.exp(sinks[h] - tops[j, h]) for j, h in units}
    outs = {(j, h): jnp.dot((exps[j, h] / dens[j, h]).astype(BF16), kv[j, h // group][h % 2][1],
                            preferred_element_type=F32) for j, h in units}
    for j in range(SWA_BLOCKS):
        for p in range(ATT_HEADS // 2):
            o_ref[j * blk:(j + 1) * blk, p * LANES:(p + 1) * LANES] = (outs[j, 2 * p] + outs[j, 2 * p + 1]).astype(BF16)


def _swa(sinks, q, kd, vd, bsz, seq):
    n = q.shape[0]
    nb = seq // ATT_BLOCK
    steps, partial_step = divmod(nb, SWA_BLOCKS)
    assert partial_step == 0
    rows = SWA_BLOCKS * ATT_BLOCK
    cur = lambda b, i: (b * steps + i, 0)
    prev = lambda b, i: (b * nb + jnp.maximum(i * SWA_BLOCKS - 1, 0), 0)
    return pl.pallas_call(
        _swa_kernel,
        grid=(bsz, steps),
        in_specs=[pl.BlockSpec(memory_space=pltpu.SMEM),
                  pl.BlockSpec((rows, 512), cur),
                  pl.BlockSpec((ATT_BLOCK, 256), prev),
                  pl.BlockSpec((rows, 256), cur),
                  pl.BlockSpec((ATT_BLOCK, 256), prev),
                  pl.BlockSpec((rows, 256), cur)],
        out_specs=pl.BlockSpec((rows, 512), cur),
        out_shape=jax.ShapeDtypeStruct((n, 512), BF16),
        compiler_params=_params(("arbitrary", "arbitrary")),
        name="swa",
    )(sinks, q, kd, kd, vd, vd)


def _mlstm_kernel(mqk_all, mv_all, mo_all, gt_all, cw_ref, gb_ref, ng_ref, o_all,
                  tail_all, qk_all, xs_all, ct_all, n_all, m_all, *, chunks, group):
    @pl.when(pl.program_id(1) == 0)
    def _():
        for ref in (tail_all, ct_all, n_all, m_all):
            ref[...] = jnp.zeros_like(ref)

    for g in range(group):
        _mlstm_prepare(mqk_all.at[g], gt_all.at[g], tail_all.at[g], qk_all.at[g], xs_all.at[g], cw_ref, gb_ref,
                       chunks)
    seqs = [tuple(r.at[g] for r in (mv_all, mo_all, o_all, qk_all, xs_all, ct_all, n_all, m_all))
            for g in range(group)]
    lax.fori_loop(0, chunks, lambda c, carry: _mlstm_chunk(c, seqs, ng_ref, carry), 0)


def _mlstm_prepare(mqk_ref, gt_ref, tail_ref, qk_ref, xs_ref, cw_ref, gb_ref, chunks):
    tm = chunks * MLSTM_CHUNK
    nqk = MLSTM_HEADS * MLSTM_QK_DIM

    cur = mqk_ref[...]
    full = jnp.concatenate([tail_ref[...], cur], axis=0)
    off = SUBLANES - (CONV_WIDTH - 1)
    acc = full[off:off + tm] * cw_ref[0:1, :]
    for j in range(1, CONV_WIDTH):
        acc = acc + full[off + j:off + j + tm] * cw_ref[j:j + 1, :]
    act = acc * jax.nn.sigmoid(acc)
    col = lax.broadcasted_iota(jnp.int32, (1, 2 * nqk), 1)
    act = act * jnp.where(col < nqk, MLSTM_QK_DIM ** -0.5, 1.0)
    qk_ref[...] = act.astype(BF16)
    tail_ref[...] = cur[tm - SUBLANES:tm]

    lane = lax.broadcasted_iota(jnp.int32, (tm, LANES), 1)
    gts = gt_ref[...] + gb_ref[...]
    logsig = jnp.minimum(gts, 0.0) - jnp.log(1.0 + jnp.exp(-jnp.abs(gts)))
    xs_ref[...] = jnp.where(lane < MLSTM_HEADS, gts, jnp.where(lane < 2 * MLSTM_HEADS, logsig, 0.0))


def _mlstm_chunk(c, seqs, ng_ref, carry):
    L = MLSTM_CHUNK
    nqk = MLSTM_HEADS * MLSTM_QK_DIM
    ri = lax.broadcasted_iota(jnp.int32, (L, L), 0)
    ci = lax.broadcasted_iota(jnp.int32, (L, L), 1)
    causal = ci <= ri
    tril = causal.astype(F32)
    lane_l = lax.broadcasted_iota(jnp.int32, (L, LANES), 1)
    lo_l = lane_l < MLSTM_QK_DIM
    row_c = lax.broadcasted_iota(jnp.int32, (LANES, 1), 0) < MLSTM_QK_DIM
    lane_1 = lax.broadcasted_iota(jnp.int32, (1, LANES), 1) < MLSTM_QK_DIM
    rows = pl.ds(pl.multiple_of(c * L, L), L)
    G = range(len(seqs))
    U = [(g, h) for g in G for h in range(MLSTM_HEADS)]
    P = [(g, p) for g in G for p in range(MLSTM_HEADS // 2)]
    mv_r, mo_r, o_r, qk_r, xs_r, ct_r, n_r, m_r = (dict(enumerate(col)) for col in zip(*seqs))
    pair = lambda u: (u[0], u[1] // 2)

    xc = {g: xs_r[g][rows, :] for g in G}
    bc = {g: jnp.dot(tril, xc[g], preferred_element_type=F32, precision=lax.Precision.HIGHEST) for g in G}
    x2 = {g: jnp.where(lane_l < MLSTM_HEADS, xc[g], bc[g]) for g in G}
    xt = {g: x2[g].T for g in G}
    q2 = {(g, p): qk_r[g][rows, p * LANES:(p + 1) * LANES] for g, p in P}
    k2 = {(g, p): qk_r[g][rows, nqk + p * LANES:nqk + (p + 1) * LANES] for g, p in P}
    ct_old = {(g, p): ct_r[g][p] for g, p in P}
    ctb = {gp: ct_old[gp].astype(BF16) for gp in P}
    n2 = {(g, p): n_r[g][p:p + 1, :] for g, p in P}
    hm = {u: lo_l if u[1] % 2 == 0 else jnp.logical_not(lo_l) for u in U}
    qm = {u: jnp.where(hm[u], q2[pair(u)], jnp.zeros_like(q2[pair(u)])) for u in U}
    km = {u: jnp.where(hm[u], k2[pair(u)], jnp.zeros_like(k2[pair(u)])) for u in U}
    v = {(g, h): mv_r[g][rows, h * LANES:(h + 1) * LANES] for g, h in U}
    b_col = {(g, h): x2[g][:, MLSTM_HEADS + h:MLSTM_HEADS + h + 1] for g, h in U}
    ig_col = {(g, h): x2[g][:, h:h + 1] for g, h in U}
    b_row = {(g, h): xt[g][MLSTM_HEADS + h:MLSTM_HEADS + h + 1, :] for g, h in U}
    ig_row = {(g, h): xt[g][h:h + 1, :] for g, h in U}
    m_prev = {(g, h): m_r[g][h:h + 1, 0:1] for g, h in U}
    dlog = {u: jnp.where(causal, b_col[u] - b_row[u] + ig_row[u], NEG_INF) for u in U}
    m_inter = {u: b_col[u] + m_prev[u] for u in U}
    m_t = {u: jnp.maximum(m_inter[u], jnp.max(dlog[u], axis=-1, keepdims=True)) for u in U}
    w_intra = {u: jnp.exp(dlog[u] - m_t[u]) for u in U}
    a_inter = {u: jnp.exp(m_inter[u] - m_t[u]) for u in U}
    s = {u: lax.dot_general(q2[pair(u)], km[u], NT_DIMS, preferred_element_type=F32) * w_intra[u] for u in U}
    num = {u: jnp.dot(s[u].astype(BF16), v[u], preferred_element_type=F32)
           + a_inter[u] * jnp.dot(qm[u], ctb[pair(u)], preferred_element_type=F32) for u in U}
    den = {u: jnp.sum(s[u], axis=-1, keepdims=True)
           + a_inter[u] * jnp.sum(qm[u].astype(F32) * n2[pair(u)], axis=-1, keepdims=True) for u in U}
    hh = {u: num[u] / jnp.maximum(jnp.abs(den[u]), jnp.exp(-m_t[u])) for u in U}
    y = {(g, h): _rms(hh[g, h], ng_ref[:, h * LANES:(h + 1) * LANES]) for g, h in U}
    for g, h in U:
        o_r[g][rows, h * LANES:(h + 1) * LANES] = (
            y[g, h] * jax.nn.sigmoid(mo_r[g][rows, h * LANES:(h + 1) * LANES])).astype(BF16)
    b_last = {(g, h): xt[g][MLSTM_HEADS + h:MLSTM_HEADS + h + 1, L - 1:L] for g, h in U}
    g_col = {u: b_last[u] - b_col[u] + ig_col[u] for u in U}
    m_new = {u: jnp.maximum(b_last[u] + m_prev[u], jnp.max(g_col[u], axis=0, keepdims=True)) for u in U}
    kw = {u: km[u].astype(F32) * jnp.exp(g_col[u] - m_new[u]) for u in U}
    dec = {u: jnp.exp(b_last[u] + m_prev[u] - m_new[u]) for u in U}
    upd = {u: lax.dot_general(kw[u].astype(BF16), v[u], TN_DIMS, preferred_element_type=F32) for u in U}
    for g, h in U:
        m_r[g][h:h + 1, :] = jnp.broadcast_to(m_new[g, h], (1, LANES))
    for g, p in P:
        e, o = (g, 2 * p), (g, 2 * p + 1)
        ct_r[g][p] = ct_old[g, p] * jnp.where(row_c, dec[e], dec[o]) + upd[e] + upd[o]
        n_r[g][p:p + 1, :] = (n2[g, p] * jnp.where(lane_1, dec[e], dec[o])
                              + jnp.sum(kw[e] + kw[o], axis=0, keepdims=True))
    return carry


def _mlstm(mqk, mv, mo, gts, conv_w, gate_bias, norm_g, bsz, seq, chunks, group):
    n = mqk.shape[0]
    tm = chunks * MLSTM_CHUNK
    steps = seq // tm
    row = lambda b, i: (b, i, 0)
    const = lambda b, i: (0, 0)
    width = MLSTM_HEADS * MLSTM_V_DIM
    per_seq = lambda a: a.reshape(bsz, seq, a.shape[1])
    out = pl.pallas_call(
        functools.partial(_mlstm_kernel, chunks=chunks, group=group),
        grid=(bsz // group, steps),
        in_specs=[pl.BlockSpec((group, tm, width), row),
                  pl.BlockSpec((group, tm, width), row),
                  pl.BlockSpec((group, tm, width), row),
                  pl.BlockSpec((group, tm, LANES), row),
                  pl.BlockSpec((CONV_WIDTH, width), const),
                  pl.BlockSpec((1, LANES), const),
                  pl.BlockSpec((1, width), const)],
        out_specs=pl.BlockSpec((group, tm, width), row),
        out_shape=jax.ShapeDtypeStruct((bsz, seq, width), BF16),
        scratch_shapes=[pltpu.VMEM((group, SUBLANES, width), F32),
                        pltpu.VMEM((group, tm, width), BF16),
                        pltpu.VMEM((group, tm, LANES), F32),
                        pltpu.VMEM((group, MLSTM_HEADS // 2, LANES, LANES), F32),
                        pltpu.VMEM((group, SUBLANES, LANES), F32),
                        pltpu.VMEM((group, SUBLANES, LANES), F32)],
        compiler_params=_params(("arbitrary", "arbitrary")),
        name="mlstm",
    )(per_seq(mqk), per_seq(mv), per_seq(mo), per_seq(gts), conv_w, gate_bias, norm_g)
    return out.reshape(n, width)


def _out_proj_kernel(att_ref, mh_ref, x_ref, g1_ref, sc_ref, sh_ref, gpm_ref, gpf_ref, wa_ref, wb_ref,
                     x1_ref, h2_ref):
    mix = (jnp.dot(att_ref[...], wa_ref[...], preferred_element_type=F32)
           + jnp.dot(mh_ref[...], wb_ref[...], preferred_element_type=F32))
    x1 = x_ref[...] + g1_ref[0] * _rms(mix, gpm_ref[...])
    x1_ref[...] = x1
    h2_ref[...] = _rms(x1, gpf_ref[...]) * (1.0 + sc_ref[0]) + sh_ref[0]


def _out_proj(att, mh, x2, gate1, scale2, shift2, g_post_mix, g_pre_ffn, wa, wb, seq, tm):
    n, d = x2.shape
    per_b = seq // tm
    row = lambda i: (i, 0)
    bsel = lambda i: (i // per_b, 0, 0)
    const = lambda i: (0, 0)
    half = att.shape[1]
    return pl.pallas_call(
        _out_proj_kernel,
        grid=(n // tm,),
        in_specs=[pl.BlockSpec((tm, half), row), pl.BlockSpec((tm, half), row), pl.BlockSpec((tm, d), row),
                  pl.BlockSpec((1, 1, d), bsel), pl.BlockSpec((1, 1, d), bsel), pl.BlockSpec((1, 1, d), bsel),
                  pl.BlockSpec((1, d), const), pl.BlockSpec((1, d), const),
                  pl.BlockSpec((half, d), const), pl.BlockSpec((half, d), const)],
        out_specs=[pl.BlockSpec((tm, d), row)] * 2,
        out_shape=[jax.ShapeDtypeStruct((n, d), F32)] * 2,
        compiler_params=_params(("arbitrary",), VMEM_LIMIT),
        name="out_proj",
    )(att, mh, x2, gate1, scale2, shift2, g_post_mix, g_pre_ffn, wa, wb)


_BIG_ID = float(2 ** 30)
SORT_LEVELS = 8


def _top_scores(s, k):
    rows, t = s.shape
    span = SORT_LEVELS * SUBLANES
    r = lax.broadcasted_iota(jnp.int32, (rows // SORT_LEVELS, t), 0)
    col_id = ((r // SUBLANES) * span + r % SUBLANES).astype(F32)
    lev = [jnp.concatenate([s[g * span + l * SUBLANES:g * span + (l + 1) * SUBLANES] for g in range(rows // span)],
                           axis=0) for l in range(SORT_LEVELS)]
    ids = [col_id + float(l * SUBLANES) for l in range(SORT_LEVELS)]
    for rnd in range(SORT_LEVELS):
        for a in range(rnd % 2, SORT_LEVELS - 1, 2):
            swap = lev[a + 1] > lev[a]
            lev[a], lev[a + 1] = jnp.where(swap, lev[a + 1], lev[a]), jnp.where(swap, lev[a], lev[a + 1])
            ids[a], ids[a + 1] = jnp.where(swap, ids[a + 1], ids[a]), jnp.where(swap, ids[a], ids[a + 1])
    vals, sel = [], []
    for it in range(k):
        m = jnp.max(lev[0], axis=0, keepdims=True)
        i = jnp.min(jnp.where(lev[0] == m, ids[0], _BIG_ID), axis=0, keepdims=True)
        vals.append(m)
        sel.append(i)
        hit = ids[0] == i
        live = min(SORT_LEVELS, k - 1 - it)
        for l in range(min(live, SORT_LEVELS - 1)):
            lev[l] = jnp.where(hit, lev[l + 1], lev[l])
            ids[l] = jnp.where(hit, ids[l + 1], ids[l])
        if live == SORT_LEVELS:
            lev[-1] = jnp.where(hit, NEG_INF, lev[-1])
    return jnp.concatenate(vals, axis=0), jnp.concatenate(sel, axis=0).astype(jnp.int32)


def _top_pair_sums(v1, v2):
    k, t = v1.shape
    half = SUBLANES
    lev = [v1[0:half] + v2[b:b + 1, :] for b in range(k)]
    side = v1[half:k] + v2[0:1, :]
    a_low = lax.broadcasted_iota(jnp.int32, (half, t), 0).astype(F32) * float(k)
    side_id = a_low + float(half * k)
    depth = jnp.zeros((half, t), F32)
    vals, sel = [], []
    for it in range(k):
        top_id = a_low + depth
        m = jnp.max(jnp.maximum(lev[0], side), axis=0, keepdims=True)
        i = jnp.min(jnp.minimum(jnp.where(lev[0] == m, top_id, _BIG_ID), jnp.where(side == m, side_id, _BIG_ID)),
                    axis=0, keepdims=True)
        vals.append(m)
        sel.append(i)
        hit = top_id == i
        for l in range(k - 1 - it):
            lev[l] = jnp.where(hit, lev[l + 1], lev[l])
        side = jnp.where(side_id == i, NEG_INF, side)
        depth = depth + jnp.where(hit, 1.0, 0.0)
    return jnp.concatenate(vals, axis=0), jnp.concatenate(sel, axis=0).astype(jnp.int32)


def _pick_rows(table, which):
    r = lax.broadcasted_iota(jnp.int32, table.shape, 0)
    rows = []
    for k in range(which.shape[0]):
        rows.append(jnp.sum(jnp.where(r == which[k:k + 1, :], table, 0), axis=0, keepdims=True))
    return jnp.concatenate(rows, axis=0)


def _peer_sel_kernel(h_ref, wq_ref, k1_ref, k2_ref, e_ref, g_ref, r_ref):
    tq = h_ref.shape[0]
    K = PEER_TOPK
    q = jnp.dot(h_ref[...].astype(BF16), wq_ref[...], preferred_element_type=F32).astype(BF16)
    for hd in range(PEER_HEADS):
        base = hd * 2 * PEER_HALF
        s1 = lax.dot_general(k1_ref[hd], q[:, base:base + PEER_HALF], NT_DIMS, preferred_element_type=F32)
        s2 = lax.dot_general(k2_ref[hd], q[:, base + PEER_HALF:base + 2 * PEER_HALF], NT_DIMS,
                             preferred_element_type=F32)
        v1, i1 = _top_scores(s1, K)
        v2, i2 = _top_scores(s2, K)
        top, pos = _top_pair_sums(v1, v2)
        eid = _pick_rows(i1, pos >> 4) * PEER_KEYS + _pick_rows(i2, pos & (K - 1))
        ex = jnp.exp(top - top[0:1, :])
        e_ref[hd * K:(hd + 1) * K, :] = eid * ROW_WORDS
        g_ref[hd * K:(hd + 1) * K, :] = ex / jnp.sum(ex, axis=0, keepdims=True)
    pair = (e_ref[...] >> 1).T
    for h in range(STREAM_HALVES):
        r_ref[pl.ds(h, tq, stride=STREAM_HALVES), :] = pair + h


def _peer_sel(h2, wq, k1, k2, tq, tg):
    n, d = h2.shape
    rows = PEER_HEADS * PEER_TOPK
    assert rows == LANES and tg % tq == 0
    return pl.pallas_call(
        _peer_sel_kernel,
        grid=(n // tq,),
        in_specs=[pl.BlockSpec((tq, d), lambda i: (i, 0)),
                  pl.BlockSpec(wq.shape, lambda i: (0, 0)),
                  pl.BlockSpec(k1.shape, lambda i: (0, 0, 0)),
                  pl.BlockSpec(k2.shape, lambda i: (0, 0, 0))],
        out_specs=[pl.BlockSpec((rows, tq), lambda i: (0, i))] * 2
        + [pl.BlockSpec((tq * STREAM_HALVES, rows), lambda i: (i * tq // tg, 0))],
        out_shape=[jax.ShapeDtypeStruct((rows, n), jnp.int32), jax.ShapeDtypeStruct((rows, n), F32),
                   jax.ShapeDtypeStruct((n // tg * tq * STREAM_HALVES, rows), jnp.int32)],
        compiler_params=_params(("arbitrary",), VMEM_LIMIT),
        name="peer_sel",
    )(h2, wq, k1, k2)


def _split_bf16(x):
    hi = x.astype(BF16)
    return hi, (x - hi.astype(F32)).astype(BF16)


def _pack_kernel(w_ref, o_ref, *wide_ref):
    x = w_ref[...]
    eb = x.shape[0]
    for r in range(ROW_WORDS):
        lo = x[:, 2 * r * LANES:(2 * r + 1) * LANES].astype(BF16).astype(F32)
        hi = x[:, (2 * r + 1) * LANES:(2 * r + 2) * LANES].astype(BF16).astype(F32)
        word = (lax.shift_right_logical(pltpu.bitcast(lo, jnp.int32), 16)
                | (pltpu.bitcast(hi, jnp.int32) & jnp.int32(-65536)))
        o_ref[pl.ds(r, eb, stride=ROW_WORDS), :] = word
    per_row = STREAM_ROW // LANES
    for ref in wide_ref:
        for c in range(per_row):
            ref[:, c * LANES:(c + 1) * LANES] = o_ref[pl.ds(c, eb * STREAM_HALVES, stride=per_row), :]


def _pack_table(w, wide_copy=False):
    e, d = w.shape
    eb = 512
    shapes = [(ROW_WORDS, LANES)] + [(STREAM_HALVES, STREAM_ROW)] * wide_copy
    return pl.pallas_call(
        _pack_kernel,
        grid=(e // eb,),
        in_specs=[pl.BlockSpec((eb, d), lambda i: (i, 0))],
        out_specs=[pl.BlockSpec((eb * rows, width), lambda i: (i, 0)) for rows, width in shapes],
        out_shape=[jax.ShapeDtypeStruct((e * rows, width), jnp.int32) for rows, width in shapes],
        compiler_params=_params(("arbitrary",)),
        name="pack_table",
    )(w)


def _gather_rows(slots, tab_ref, t, stage_ref):
    for j, slot in enumerate(slots):
        src = pl.ds(pl.multiple_of(slot[t], ROW_WORDS), ROW_WORDS)
        stage_ref[j * ROW_WORDS:(j + 1) * ROW_WORDS, :] = tab_ref[src, :]


def _staged_bf16(stage_ref):
    return pltpu.bitcast(stage_ref[...], BF16)


def _pipelined_tokens(ntok, gather, compute, stages, on_trip=None, after=None):
    nb = len(stages)
    for k in range(nb):
        gather(k, stages[k])

    def trip(i, carry):
        t = nb * i
        if on_trip is not None:
            on_trip(i)
        for k in range(nb):
            compute(t + k, stages[k])
            ahead = t + k + nb
            gather(jnp.minimum(ahead, ntok - 1), stages[k])
            if after is not None:
                after(i, k)
        return carry

    lax.fori_loop(0, ntok // nb, trip, 0)


def _with_slot_indices(idx_hbm, sems, bufs, stride, count, run):
    nsel = len(bufs) // 2
    step = pl.program_id(0)
    last = pl.num_programs(0) - 1

    def copies(block, which):
        return [pltpu.make_async_copy(idx_hbm.at[j, pl.ds(block * stride, count)], bufs[which * nsel + j],
                                      sems.at[which])
                for j in range(nsel)]

    @pl.when(step == 0)
    def _():
        for cp in copies(0, 0):
            cp.start()

    def phase(which):
        for cp in copies(step, which):
            cp.wait()

        @pl.when(step < last)
        def _():
            for cp in copies(step + 1, 1 - which):
                cp.start()

        run(bufs[which * nsel:(which + 1) * nsel])

    for which in range(2):
        pl.when(step % 2 == which)(functools.partial(phase, which))


SC_WINDOW = 128


def _sc_gather_rows(tab, row_ids):
    windows, width = row_ids.shape
    assert width == SC_WINDOW
    n = windows * SC_WINDOW
    mesh = plsc.VectorSubcoreMesh(core_axis_name="core", subcore_axis_name="subcore")

    @pl.kernel(out_type=jax.ShapeDtypeStruct((n, tab.shape[1]), tab.dtype), mesh=mesh)
    def gather(tab_hbm, ids_hbm, out_hbm):
        def body(ids_vmem, out_vmem):
            pltpu.sync_copy(tab_hbm.at[ids_vmem.at[0]], out_vmem)

        pltpu.emit_pipeline(
            body,
            grid=(windows,),
            in_specs=[pl.BlockSpec((1, SC_WINDOW), index_map=lambda i: (i, 0))],
            out_specs=[pl.BlockSpec((SC_WINDOW, tab.shape[1]), index_map=lambda i: (i, 0))],
            core_axis_name=("core", "subcore"),
            dimension_semantics=(pltpu.PARALLEL,),
            trace_scopes=False,
        )(ids_hbm, out_hbm)

    return gather(tab, row_ids)


def _diag_mask(nsel):
    shape = (SUBLANES, nsel * SUBLANES)
    return (lax.broadcasted_iota(jnp.int32, shape, 1) % SUBLANES) == lax.broadcasted_iota(jnp.int32, shape, 0)


def _token_tile(ref, t):
    row = ref[pl.ds(t, 1), :]
    return jnp.concatenate([row[:, s * LANES:(s + 1) * LANES] for s in range(SUBLANES)], axis=0)


def _peer_u_kernel(idx_hbm, h_ref, gate_ref, grp_ref, tab_ref, act_ref, part_ref, *scratch):
    nsel, tq = gate_ref.shape
    stages, sems, bufs = scratch[:GATHER_STAGES], scratch[GATHER_STAGES], scratch[GATHER_STAGES + 1:]
    diag = _diag_mask(nsel)

    def compute(t, stage_ref):
        h_hi, h_lo = _split_bf16(_token_tile(h_ref, t))
        both = lax.dot_general(jnp.concatenate([h_hi, h_lo], axis=0), _staged_bf16(stage_ref), NT_DIMS,
                               preferred_element_type=F32)
        prod = both[0:SUBLANES] + both[SUBLANES:2 * SUBLANES]
        part_ref[pl.ds(t, 1), :] = jnp.sum(jnp.where(diag, prod, 0.0), axis=0, keepdims=True)

    def run(slots):
        _pipelined_tokens(tq, functools.partial(_gather_rows, slots, tab_ref), compute, stages)

    _with_slot_indices(idx_hbm, sems, bufs, tq, tq, run)
    p_hi, p_lo = _split_bf16(part_ref[...])
    grp = grp_ref[...]
    pre = jnp.dot(p_hi, grp, preferred_element_type=F32) + jnp.dot(p_lo, grp, preferred_element_type=F32)
    act_ref[...] = 0.5 * pre * (1.0 + lax.erf(pre * (2.0 ** -0.5))) * gate_ref[...].T


def _group_matrix(nsel):
    r = jnp.arange(nsel * SUBLANES)[:, None] // SUBLANES
    return (r == jnp.arange(nsel)[None, :]).astype(BF16)


def _gather_scratch(nsel, ntok, nstages):
    return ([pltpu.VMEM((nsel * ROW_WORDS, LANES), jnp.int32)] * nstages
            + [pltpu.SemaphoreType.DMA((2,))] + [pltpu.SMEM((ntok,), jnp.int32)] * (2 * nsel))


def _peer_u(idx_t, h2, gate_t, tab, tq):
    nsel, n = idx_t.shape
    d = h2.shape[1]
    wide = nsel * SUBLANES
    return pl.pallas_call(
        _peer_u_kernel,
        grid=(n // tq,),
        in_specs=[pl.BlockSpec(memory_space=pl.ANY),
                  pl.BlockSpec((tq, d), lambda i: (i, 0)),
                  pl.BlockSpec((nsel, tq), lambda i: (0, i)),
                  pl.BlockSpec((wide, nsel), lambda i: (0, 0)),
                  pl.BlockSpec(memory_space=pltpu.VMEM)],
        out_specs=pl.BlockSpec((tq, nsel), lambda i: (i, 0)),
        out_shape=jax.ShapeDtypeStruct((n, nsel), F32),
        scratch_shapes=[pltpu.VMEM((tq, wide), F32)] + _gather_scratch(nsel, tq, GATHER_STAGES),
        compiler_params=_params(("arbitrary",), GATHER_VMEM_LIMIT),
        name="peer_u",
    )(idx_t, h2, gate_t, _group_matrix(nsel), tab)


def _lane_aligned(count):
    return -(-count // LANES) * LANES


def _streamed_rows_bf16(words):
    x = pltpu.bitcast(words, BF16)
    return jnp.concatenate([x[:, 0:LANES], x[:, LANES:2 * LANES]], axis=0)


def _streamed_copy_matrix(nsel):
    r = jnp.arange(nsel * SUBLANES)
    return ((r[None, :] % (2 * nsel)) // 2 == jnp.arange(nsel)[:, None]).astype(BF16)


def _streamed_diag(nsel):
    shape = (SUBLANES, nsel * SUBLANES)
    r = lax.broadcasted_iota(jnp.int32, shape, 1)
    sub = 4 * ((r % (4 * nsel)) // (2 * nsel)) + 2 * (r // (4 * nsel)) + r % 2
    return lax.broadcasted_iota(jnp.int32, shape, 0) == sub


def _peer_v_kernel(idx_hbm, act_ref, rep_ref, rep2_ref, x1_ref, g2_ref, gpf_ref, tab_ref, rows_hbm, o_ref,
                   wide_ref, y_ref, rowbuf, rsem, *scratch, nvld, nstages):
    tq, nsel = act_ref.shape
    stages, sems, bufs = scratch[:nstages], scratch[nstages], scratch[nstages + 1:]
    trips = nvld // nstages
    per_trip = (tq - nvld) // trips
    rows_tok = nsel * STREAM_HALVES
    step = pl.program_id(0)
    chunks = pl.num_programs(0) * trips
    diag = _diag_mask(nsel)
    diag2 = _streamed_diag(nsel)
    a_hi, a_lo = _split_bf16(act_ref[...])
    for lo, hi, rep in ((0, nvld, rep_ref[...]), (nvld, tq, rep2_ref[...])):
        wide_ref[lo:hi, :] = (jnp.dot(a_hi[lo:hi], rep, preferred_element_type=F32)
                              + jnp.dot(a_lo[lo:hi], rep, preferred_element_type=F32))

    def finish(t, mask, rows_bf16):
        w = jnp.where(mask, jnp.broadcast_to(wide_ref[pl.ds(t, 1), :], mask.shape), 0.0)
        w_hi, w_lo = _split_bf16(w)
        both = jnp.dot(jnp.concatenate([w_hi, w_lo], axis=0), rows_bf16, preferred_element_type=F32)
        tile = both[0:SUBLANES] + both[SUBLANES:2 * SUBLANES]
        y_ref[pl.ds(t, 1), :] = jnp.concatenate([tile[s:s + 1, :] for s in range(SUBLANES)], axis=1)

    def compute(t, stage_ref):
        finish(t, diag, _staged_bf16(stage_ref))

    def chunk_copy(g, slot):
        return pltpu.make_async_copy(rows_hbm.at[pl.ds(g * (per_trip * rows_tok), per_trip * rows_tok), :],
                                     rowbuf.at[slot], rsem.at[slot])

    ring = STREAM_RING

    @pl.when(step == 0)
    def _():
        for g0 in range(ring - 1):
            chunk_copy(g0, g0).start()

    def on_trip(i):
        g = step * trips + i
        chunk_copy(g, i % ring).wait()

        @pl.when(g + ring - 1 < chunks)
        def _():
            chunk_copy(g + ring - 1, (i + ring - 1) % ring).start()

    places = [(d * nstages) // per_trip for d in range(per_trip)]

    def after(i, k):
        for d in range(per_trip):
            if places[d] == k:
                rows = rowbuf.at[i % ring, pl.ds(d * rows_tok, rows_tok), :]
                finish(nvld + i * per_trip + d, diag2, _streamed_rows_bf16(rows[...]))

    def run(slots):
        _pipelined_tokens(nvld, functools.partial(_gather_rows, slots, tab_ref), compute, stages, on_trip, after)

    _with_slot_indices(idx_hbm, sems, bufs, tq, _lane_aligned(nvld), run)
    o_ref[...] = x1_ref[...] + g2_ref[0] * _rms(y_ref[...], gpf_ref[...])


def _peer_v(idx_t, act, x1, gate2, g_post_ffn, tab, rows, seq, tq, nvld, nstages):
    nsel, n = idx_t.shape
    d = x1.shape[1]
    wide = nsel * SUBLANES
    per_b = seq // tq
    row = lambda i: (i, 0)
    const = lambda i: (0, 0)
    trips = nvld // nstages
    per_trip = (tq - nvld) // trips
    assert trips % STREAM_RING == 0 and trips * nstages == nvld and trips * per_trip == tq - nvld
    return pl.pallas_call(
        functools.partial(_peer_v_kernel, nvld=nvld, nstages=nstages),
        grid=(n // tq,),
        in_specs=[pl.BlockSpec(memory_space=pl.ANY),
                  pl.BlockSpec((tq, nsel), row),
                  pl.BlockSpec((nsel, wide), const),
                  pl.BlockSpec((nsel, wide), const),
                  pl.BlockSpec((tq, d), row),
                  pl.BlockSpec((1, 1, d), lambda i: (i // per_b, 0, 0)),
                  pl.BlockSpec((1, d), const),
                  pl.BlockSpec(memory_space=pltpu.VMEM),
                  pl.BlockSpec(memory_space=pl.ANY)],
        out_specs=pl.BlockSpec((tq, d), row),
        out_shape=jax.ShapeDtypeStruct((n, d), F32),
        scratch_shapes=[pltpu.VMEM((tq, wide), F32), pltpu.VMEM((tq, d), F32),
                        pltpu.VMEM((STREAM_RING, per_trip * nsel * STREAM_HALVES, STREAM_ROW), jnp.int32),
                        pltpu.SemaphoreType.DMA((STREAM_RING,))]
        + _gather_scratch(nsel, _lane_aligned(nvld), nstages),
        compiler_params=_params(("arbitrary",), GATHER_VMEM_LIMIT),
        name="peer_v",
    )(idx_t, act, _group_matrix(nsel).T, _streamed_copy_matrix(nsel), x1, gate2, g_post_ffn, tab, rows)


def _dup_heads(w, heads, dh):
    d = w.shape[0]
    return jnp.repeat(w.reshape(d, heads, 1, dh), 2, axis=2).reshape(d, heads * 2 * dh)


def _layer(x2, c, pos_col, bsz, seq, w_mod, b_mod, g_pre_mix, g_post_mix, w_in, conv_w, b_igate, b_fgate,
           mlstm_norm_g, att_sinks, w_out, g_pre_ffn, g_post_ffn, peer_wq, peer_keys1, peer_keys2, peer_u, peer_v):
    n, d = x2.shape
    tm = min(seq, 512)
    mod = _mod(c, w_mod, b_mod)
    shift1, scale1, gate1, shift2, scale2, gate2 = [m.reshape(bsz, 1, d) for m in jnp.split(mod, 6, axis=-1)]

    aw = ATT_HEADS * ATT_HEAD_DIM
    kvw = ATT_KV_HEADS * ATT_HEAD_DIM
    qkw = MLSTM_HEADS * MLSTM_QK_DIM
    mw = MLSTM_HEADS * MLSTM_V_DIM
    o = 0
    wq_a = w_in[:, o:o + aw]; o += aw
    wk_a = w_in[:, o:o + kvw]; o += kvw
    wv_a = w_in[:, o:o + kvw]; o += kvw
    w_mqk = w_in[:, o:o + 2 * qkw]; o += 2 * qkw
    w_mv = w_in[:, o:o + mw]; o += mw
    w_g = w_in[:, o:o + 2 * MLSTM_HEADS]; o += 2 * MLSTM_HEADS
    w_mo = w_in[:, o:o + mw]
    w_gp = jnp.pad(w_g, ((0, 0), (0, LANES - 2 * MLSTM_HEADS)))
    w_all = jnp.concatenate([wq_a, _dup_heads(wk_a, ATT_KV_HEADS, ATT_HEAD_DIM),
                             _dup_heads(wv_a, ATT_KV_HEADS, ATT_HEAD_DIM), w_mqk, w_mv, w_mo, w_gp],
                            axis=1).astype(BF16)

    cos, sin = _rope_tab(pos_col)
    q, kd, vd, mqk, mv, mo, gts = _in_proj(x2, scale1, shift1, g_pre_mix.reshape(1, d), cos, sin, w_all, seq, tm)
    att = _swa(att_sinks, q, kd, vd, bsz, seq)
    gate_bias = jnp.pad(jnp.concatenate([b_igate, b_fgate]), (0, LANES - 2 * MLSTM_HEADS)).reshape(1, LANES)
    chunks = min(seq // MLSTM_CHUNK, 8)
    mh = _mlstm(mqk, mv, mo, gts, conv_w, gate_bias, mlstm_norm_g.reshape(1, mw), bsz, seq, chunks,
                MLSTM_GROUP if bsz % MLSTM_GROUP == 0 else 1)
    wo = w_out.astype(BF16)
    x1, h2 = _out_proj(att, mh, x2, gate1, scale2, shift2, g_post_mix.reshape(1, d), g_pre_ffn.reshape(1, d),
                       wo[:aw], wo[aw:], seq, tm)

    tg = min(seq, GATHER_TOKENS)
    nvld = tg * STREAM_SPLIT[0] // sum(STREAM_SPLIT)
    eid_t, gate_t, row_ids = _peer_sel(h2, peer_wq.astype(BF16), peer_keys1.astype(BF16),
                                       peer_keys2.astype(BF16), tg - nvld, tg)
    tab_v, tab_v_wide = _pack_table(peer_v, wide_copy=True)
    rows_v = _sc_gather_rows(tab_v_wide, row_ids)
    tab_u, = _pack_table(peer_u)
    act = _peer_u(eid_t, h2, gate_t, tab_u, tg)
    return _peer_v(eid_t, act, x1, gate2, g_post_ffn.reshape(1, d), tab_v, rows_v, seq, tg, nvld, STREAM_STAGES)


def kernel(x, c, positions, w_mod, b_mod, g_pre_mix, g_post_mix, w_in, conv_w, b_igate, b_fgate, mlstm_norm_g, att_sinks, w_out, g_pre_ffn, g_post_ffn, peer_wq, peer_keys1, peer_keys2, peer_u, peer_v):
    bsz, seq, d = x.shape
    n = bsz * seq
    x2 = x.reshape(n, d)
    pos_col = positions.reshape(n, 1)
    for l in range(w_mod.shape[0]):
        x2 = _layer(x2, c, pos_col, bsz, seq, w_mod[l], b_mod[l], g_pre_mix[l], g_post_mix[l], w_in[l], conv_w[l],
                    b_igate[l], b_fgate[l], mlstm_norm_g[l], att_sinks[l], w_out[l], g_pre_ffn[l], g_post_ffn[l],
                    peer_wq[l], peer_keys1[l], peer_keys2[l], peer_u[l], peer_v[l])
    return x2.reshape(bsz, seq, d)
```
